```python
import math
import jax, jax.numpy as jnp
from jax import lax
import numpy as np

D_MODEL = 2048
BATCH = 4
SEQ = 4096
DEPTH = 1

NORM_EPS = 1e-6
ROPE_THETA = 500000.0
MAX_POS_OFFSET = 1024
ADALN_CHUNKS = 6

MLA_HEADS = 8
Q_LORA_RANK = 512
KV_LORA_RANK = 512
QK_NOPE_DIM = 128
QK_ROPE_DIM = 64
V_HEAD_DIM = 128
MLA_QK_DIM = QK_NOPE_DIM + QK_ROPE_DIM
Q_BLOCK = 128

DIL_PATTERNS = ((128, 1), (512, 4), (2048, 16))
DIL_GROUPS = len(DIL_PATTERNS)
DIL_HEADS_PER_GROUP = 4
DIL_HEADS = DIL_GROUPS * DIL_HEADS_PER_GROUP
DIL_HEAD_DIM = 128
DIL_ROT_DIM = DIL_HEAD_DIM // 4
DIL_BLOCK = max(w // d for (w, d) in DIL_PATTERNS)

IN_SIZES = (Q_LORA_RANK, KV_LORA_RANK + QK_ROPE_DIM, 3 * DIL_HEADS * DIL_HEAD_DIM, D_MODEL, D_MODEL)
IN_COLS = sum(IN_SIZES)
IN_SPLITS = [int(v) for v in np.cumsum(IN_SIZES)[:-1]]

N_EXPERTS = 64
N_EXPERT_GROUPS = 8
TOPK_GROUPS = 4
TOP_K = 6
EXPERT_DIM = 512
SHARED_DIM = 512
ROUTED_SCALE = 2.5
MOE_BLOCK = 256

kernel_name = "hybrid_mla_dilated_moe_block"


def rms_norm(x, g):
    xf = x.astype(jnp.float32)
    xf = xf * lax.rsqrt(jnp.mean(xf * xf, axis=-1, keepdims=True) + NORM_EPS)
    return (xf * g.astype(jnp.float32)).astype(x.dtype)


def rope_cos_sin(positions, dim):
    inv_freq = 1.0 / (ROPE_THETA ** (jnp.arange(0, dim, 2, dtype=jnp.float32) / dim))
    ang = positions.astype(jnp.float32)[..., None] * inv_freq
    return jnp.cos(ang)[:, :, None, :], jnp.sin(ang)[:, :, None, :]


def apply_rope(t, cos, sin):
    half = t.shape[-1] // 2
    t1 = t[..., :half].astype(jnp.float32)
    t2 = t[..., half:].astype(jnp.float32)
    return jnp.concatenate([t1 * cos - t2 * sin, t2 * cos + t1 * sin], axis=-1).astype(t.dtype)


def causal_block_attention(q, k, v, scale):
    B, S, H, Dq = q.shape
    Dv = v.shape[-1]
    nb = S // Q_BLOCK
    qb = q.reshape(B, nb, Q_BLOCK, H, Dq).swapaxes(0, 1)
    kpos = jnp.arange(S)

    def one_block(args):
        qi, i = args
        s = jnp.einsum('bqhd,bkhd->bhqk', qi, k, preferred_element_type=jnp.float32) * scale
        qpos = i * Q_BLOCK + jnp.arange(Q_BLOCK)
        s = jnp.where((kpos[None, :] <= qpos[:, None])[None, None], s, -jnp.inf)
        p = jax.nn.softmax(s, axis=-1).astype(v.dtype)
        return jnp.einsum('bhqk,bkhd->bqhd', p, v)

    o = lax.map(one_block, (qb, jnp.arange(nb)))
    return o.swapaxes(0, 1).reshape(B, S, H, Dv)


def mla_branch(q_a, kv_a, cos, sin, q_a_norm_g, w_q_up, kv_a_norm_g, w_kv_up, w_mla_o):
    B, S, _ = q_a.shape
    q = (rms_norm(q_a, q_a_norm_g) @ w_q_up).reshape(B, S, MLA_HEADS, MLA_QK_DIM)
    q_nope, q_rope = q[..., :QK_NOPE_DIM], q[..., QK_NOPE_DIM:]
    q = jnp.concatenate([q_nope, apply_rope(q_rope, cos, sin)], axis=-1)
    c_kv, k_rope = kv_a[..., :KV_LORA_RANK], kv_a[..., KV_LORA_RANK:]
    kv = (rms_norm(c_kv, kv_a_norm_g) @ w_kv_up).reshape(B, S, MLA_HEADS, QK_NOPE_DIM + V_HEAD_DIM)
    k_nope, v = kv[..., :QK_NOPE_DIM], kv[..., QK_NOPE_DIM:]
    k_rope = apply_rope(k_rope[:, :, None, :], cos, sin)
    k = jnp.concatenate([k_nope, jnp.broadcast_to(k_rope, (B, S, MLA_HEADS, QK_ROPE_DIM))], axis=-1)
    o = causal_block_attention(q, k, v, MLA_QK_DIM ** -0.5)
    return o.reshape(B, S, MLA_HEADS * V_HEAD_DIM) @ w_mla_o


def dilated_group_attention(q, k, v, window, dilation):
    B, S, H, Dh = q.shape
    L = S // dilation
    W = window // dilation
    blk = DIL_BLOCK
    nb = -(-L // blk)
    Lp = nb * blk

    def to_blocks(t):
        t = t.reshape(B, L, dilation, H, Dh).transpose(0, 2, 1, 3, 4).reshape(B * dilation, L, H, Dh)
        t = jnp.pad(t, ((0, 0), (0, Lp - L), (0, 0), (0, 0)))
        return t.reshape(B * dilation, nb, blk, H, Dh)

    def with_prev(t):
        prev = jnp.pad(t[:, :-1], ((0, 0), (1, 0), (0, 0), (0, 0), (0, 0)))
        return jnp.concatenate([prev, t], axis=2)

    qb = to_blocks(q)
    kw = with_prev(to_blocks(k))
    vw = with_prev(to_blocks(v))
    s = jnp.einsum('nbqhd,nbkhd->nbhqk', qb, kw, preferred_element_type=jnp.float32) * (Dh ** -0.5)
    qi = jnp.arange(blk)
    kj = jnp.arange(2 * blk)
    dist = qi[:, None] + blk - kj[None, :]
    band = (dist >= 0) & (dist <= W)
    key_exists = (jnp.arange(nb)[:, None] * blk + kj[None, :] - blk) >= 0
    mask = band[None] & key_exists[:, None, :]
    s = jnp.where(mask[None, :, None], s, -jnp.inf)
    lse = jax.nn.logsumexp(s, axis=-1)
    p = jnp.exp(s - lse[..., None]).astype(v.dtype)
    o = jnp.einsum('nbhqk,nbkhd->nbqhd', p, vw)
    o = o.reshape(B * dilation, Lp, H, Dh)[:, :L]
    o = o.reshape(B, dilation, L, H, Dh).transpose(0, 2, 1, 3, 4).reshape(B, S, H, Dh)
    lse = lse.transpose(0, 1, 3, 2).reshape(B * dilation, Lp, H)[:, :L]
    lse = lse.reshape(B, dilation, L, H).transpose(0, 2, 1, 3).reshape(B, S, H)
    return o, lse


def dilated_branch(dil, cos, sin, w_dil_o):
    B, S, _ = dil.shape
    q, k, v = [t.reshape(B, S, DIL_HEADS, DIL_HEAD_DIM) for t in jnp.split(dil, 3, axis=-1)]

    def partial_rope(t):
        return jnp.concatenate([apply_rope(t[..., :DIL_ROT_DIM], cos, sin), t[..., DIL_ROT_DIM:]], axis=-1)

    q, k = partial_rope(q), partial_rope(k)
    outs, lses = [], []
    for gi, (window, dilation) in enumerate(DIL_PATTERNS):
        sl = slice(gi * DIL_HEADS_PER_GROUP, (gi + 1) * DIL_HEADS_PER_GROUP)
        o, lse = dilated_group_attention(q[:, :, sl], k[:, :, sl], v[:, :, sl], window, dilation)
        outs.append(o)
        lses.append(lse)
    wts = jax.nn.softmax(jnp.stack(lses), axis=0).astype(dil.dtype)
    o = jnp.einsum('gbsh,gbshd->bshd', wts, jnp.stack(outs))
    return o.reshape(B, S, DIL_HEADS_PER_GROUP * DIL_HEAD_DIM) @ w_dil_o


def token_mixer(h, cos_m, sin_m, cos_d, sin_d, w_in, q_a_norm_g, w_q_up, kv_a_norm_g, w_kv_up,
                w_mla_o, w_dil_o, w_out):
    proj = h @ w_in
    q_a, kv_a, dil, g_a, g_b = jnp.split(proj, IN_SPLITS, axis=-1)
    y_a = mla_branch(q_a, kv_a, cos_m, sin_m, q_a_norm_g, w_q_up, kv_a_norm_g, w_kv_up, w_mla_o)
    y_b = dilated_branch(dil, cos_d, sin_d, w_dil_o)
    merged = jax.nn.sigmoid(g_a) * y_a + jax.nn.sigmoid(g_b) * y_b
    return merged @ w_out


def swiglu(t, wg, wu, wd):
    return (jax.nn.silu(t @ wg) * (t @ wu)) @ wd


def moe(h, w_router, router_bias, w_exp_gate, w_exp_up, w_exp_down, w_sh_gate, w_sh_up, w_sh_down):
    N, D = h.shape
    scores = jax.nn.sigmoid((h @ w_router).astype(jnp.float32))
    biased = scores + router_bias.astype(jnp.float32)
    per_group = N_EXPERTS // N_EXPERT_GROUPS
    grp_score = lax.top_k(biased.reshape(N, N_EXPERT_GROUPS, per_group), 2)[0].sum(-1)
    _, grp_idx = lax.top_k(grp_score, TOPK_GROUPS)
    grp_mask = jax.nn.one_hot(grp_idx, N_EXPERT_GROUPS, dtype=jnp.float32).sum(1) > 0
    sel = jnp.where(jnp.repeat(grp_mask, per_group, axis=1), biased, -jnp.inf)
    _, top_idx = lax.top_k(sel, TOP_K)
    top_w = jnp.take_along_axis(scores, top_idx, axis=-1)
    top_w = top_w / (top_w.sum(-1, keepdims=True) + 1e-20) * ROUTED_SCALE

    A = N * TOP_K
    blk = MOE_BLOCK
    flat_e = top_idx.reshape(-1).astype(jnp.int32)
    flat_tok = jnp.repeat(jnp.arange(N, dtype=jnp.int32), TOP_K)
    flat_w = top_w.reshape(-1)
    order = jnp.argsort(flat_e)
    se, st, sw = flat_e[order], flat_tok[order], flat_w[order]
    counts = jnp.bincount(flat_e, length=N_EXPERTS).astype(jnp.int32)
    starts = jnp.cumsum(counts) - counts
    pcounts = (counts + blk - 1) // blk * blk
    pends = jnp.cumsum(pcounts)
    pstarts = pends - pcounts
    dest = pstarts[se] + (jnp.arange(A, dtype=jnp.int32) - starts[se])
    nb = -(-(A + N_EXPERTS * (blk - 1)) // blk)
    P = nb * blk
    buf_tok = jnp.full((P,), N, jnp.int32).at[dest].set(st)
    buf_w = jnp.zeros((P,), jnp.float32).at[dest].set(sw)
    blk_e = jnp.minimum(jnp.searchsorted(pends, jnp.arange(nb, dtype=jnp.int32) * blk, side='right'),
                        N_EXPERTS - 1).astype(jnp.int32)
    h_pad = jnp.concatenate([h, jnp.zeros((1, D), h.dtype)], axis=0)

    def expert_block(args):
        tok, wt, e = args
        xi = h_pad[tok]
        y = swiglu(xi, w_exp_gate[e], w_exp_up[e], w_exp_down[e])
        return y * wt[:, None].astype(y.dtype)

    yb = lax.map(expert_block, (buf_tok.reshape(nb, blk), buf_w.reshape(nb, blk), blk_e))
    routed = jax.ops.segment_sum(yb.reshape(P, D), buf_tok, num_segments=N + 1)[:N]
    return routed + swiglu(h, w_sh_gate, w_sh_up, w_sh_down)


def setup_inputs(seed: int = 0) -> dict:
    key = jax.random.key(seed)
    ks = jax.random.split(key, 32)
    L = DEPTH

    def nrm(k, shape, scale):
        return jax.random.normal(k, shape, jnp.float32) * scale

    return {
        "x": nrm(ks[0], (BATCH, SEQ, D_MODEL), 1.0),
        "c": nrm(ks[1], (BATCH, D_MODEL), 1.0),
        "positions": (jnp.arange(SEQ, dtype=jnp.int32)[None, :]
                      + jax.random.randint(ks[2], (BATCH, 1), 0, MAX_POS_OFFSET, dtype=jnp.int32)),
        "w_ada": nrm(ks[3], (L, D_MODEL, ADALN_CHUNKS * D_MODEL), 0.5 * D_MODEL ** -0.5),
        "b_ada": nrm(ks[4], (L, ADALN_CHUNKS * D_MODEL), 0.02),
        "attn_pre_g": 1.0 + nrm(ks[5], (L, D_MODEL), 0.1),
        "w_in": nrm(ks[6], (L, D_MODEL, IN_COLS), D_MODEL ** -0.5),
        "q_a_norm_g": 1.0 + nrm(ks[7], (L, Q_LORA_RANK), 0.1),
        "w_q_up": nrm(ks[8], (L, Q_LORA_RANK, MLA_HEADS * MLA_QK_DIM), Q_LORA_RANK ** -0.5),
        "kv_a_norm_g": 1.0 + nrm(ks[9], (L, KV_LORA_RANK), 0.1),
        "w_kv_up": nrm(ks[10], (L, KV_LORA_RANK, MLA_HEADS * (QK_NOPE_DIM + V_HEAD_DIM)), KV_LORA_RANK ** -0.5),
        "w_mla_o": nrm(ks[11], (L, MLA_HEADS * V_HEAD_DIM, D_MODEL), (MLA_HEADS * V_HEAD_DIM) ** -0.5),
        "w_dil_o": nrm(ks[12], (L, DIL_HEADS_PER_GROUP * DIL_HEAD_DIM, D_MODEL),
                       (DIL_HEADS_PER_GROUP * DIL_HEAD_DIM) ** -0.5),
        "w_out": nrm(ks[13], (L, D_MODEL, D_MODEL), D_MODEL ** -0.5),
        "attn_post_g": 1.0 + nrm(ks[14], (L, D_MODEL), 0.1),
        "ffn_pre_g": 1.0 + nrm(ks[15], (L, D_MODEL), 0.1),
        "w_router": nrm(ks[16], (L, D_MODEL, N_EXPERTS), D_MODEL ** -0.5),
        "router_bias": nrm(ks[17], (L, N_EXPERTS), 0.01),
        "w_exp_gate": nrm(ks[18], (L, N_EXPERTS, D_MODEL, EXPERT_DIM), D_MODEL ** -0.5),
        "w_exp_up": nrm(ks[19], (L, N_EXPERTS, D_MODEL, EXPERT_DIM), D_MODEL ** -0.5),
        "w_exp_down": nrm(ks[20], (L, N_EXPERTS, EXPERT_DIM, D_MODEL), EXPERT_DIM ** -0.5),
        "w_sh_gate": nrm(ks[21], (L, D_MODEL, SHARED_DIM), D_MODEL ** -0.5),
        "w_sh_up": nrm(ks[22], (L, D_MODEL, SHARED_DIM), D_MODEL ** -0.5),
        "w_sh_down": nrm(ks[23], (L, SHARED_DIM, D_MODEL), SHARED_DIM ** -0.5),
        "ffn_post_g": 1.0 + nrm(ks[24], (L, D_MODEL), 0.1),
    }


def reference(x, c, positions, w_ada, b_ada, attn_pre_g, w_in, q_a_norm_g, w_q_up, kv_a_norm_g,
              w_kv_up, w_mla_o, w_dil_o, w_out, attn_post_g, ffn_pre_g, w_router, router_bias,
              w_exp_gate, w_exp_up, w_exp_down, w_sh_gate, w_sh_up, w_sh_down, ffn_post_g):
    B, S, D = x.shape
    cos_m, sin_m = rope_cos_sin(positions, QK_ROPE_DIM)
    cos_d, sin_d = rope_cos_sin(positions, DIL_ROT_DIM)
    for l in range(DEPTH):
        mod = (jax.nn.silu(c) @ w_ada[l] + b_ada[l])[:, None, :]
        shift_a, scale_a, gate_a, shift_f, scale_f, gate_f = jnp.split(mod, ADALN_CHUNKS, axis=-1)
        h = rms_norm(x, attn_pre_g[l]) * (1.0 + scale_a) + shift_a
        y = token_mixer(h, cos_m, sin_m, cos_d, sin_d, w_in[l], q_a_norm_g[l], w_q_up[l],
                        kv_a_norm_g[l], w_kv_up[l], w_mla_o[l], w_dil_o[l], w_out[l])
        x = x + gate_a * rms_norm(y, attn_post_g[l])
        h = rms_norm(x, ffn_pre_g[l]) * (1.0 + scale_f) + shift_f
        y = moe(h.reshape(B * S, D), w_router[l], router_bias[l], w_exp_gate[l], w_exp_up[l],
                w_exp_down[l], w_sh_gate[l], w_sh_up[l], w_sh_down[l]).reshape(B, S, D)
        x = x + gate_f * rms_norm(y, ffn_post_g[l])
    return x
```

```python
import functools

import jax
import jax.numpy as jnp
from jax import lax
from jax.experimental import pallas as pl
from jax.experimental.pallas import tpu as pltpu

F32 = jnp.float32
BF16 = jnp.bfloat16

D_MODEL = 2048
NORM_EPS = 1e-6
ROPE_THETA = 500000.0
ADALN_CHUNKS = 6

MLA_HEADS = 8
Q_LORA_RANK = 512
KV_LORA_RANK = 512
QK_NOPE_DIM = 128
QK_ROPE_DIM = 64
V_HEAD_DIM = 128
MLA_QK_DIM = QK_NOPE_DIM + QK_ROPE_DIM
MLA_QK_PAD = 256

DIL_PATTERNS = ((128, 1), (512, 4), (2048, 16))
DIL_GROUPS = len(DIL_PATTERNS)
DIL_HEADS_PER_GROUP = 4
DIL_HEADS = DIL_GROUPS * DIL_HEADS_PER_GROUP
DIL_HEAD_DIM = 128
DIL_ROT_DIM = DIL_HEAD_DIM // 4
DIL_SPAN = 128
DIL_GROUP_COLS = DIL_HEADS_PER_GROUP * DIL_HEAD_DIM

N_EXPERTS = 64
N_EXPERT_GROUPS = 8
TOPK_GROUPS = 4
TOP_K = 6
EXPERT_DIM = 512
SHARED_DIM = 512
ROUTED_SCALE = 2.5
MOE_BLOCK = 256

LANES = 128
NEG_BIG = -1e30
VMEM_LIMIT = 56 * 1024 * 1024


def _cparams(*sem):
    return pltpu.CompilerParams(dimension_semantics=sem, vmem_limit_bytes=VMEM_LIMIT)


def _sigmoid(v):
    return 1.0 / (1.0 + jnp.exp(-v))


def _rms(v, g):
    ms = jnp.mean(v * v, axis=-1, keepdims=True)
    return v * lax.rsqrt(ms + NORM_EPS) * g


def _ada_kernel(c_ref, w_ref, b_ref, o_ref):
    c = c_ref[...]
    a = (c * _sigmoid(c)).astype(BF16)
    o_ref[...] = jnp.dot(a, w_ref[...].astype(BF16), preferred_element_type=F32) + b_ref[...]


def _ada(c8, w_ada, b_ada, tn=1536):
    d, n = w_ada.shape
    return pl.pallas_call(
        _ada_kernel,
        grid=(n // tn,),
        in_specs=[pl.BlockSpec((8, d), lambda j: (0, 0)),
                  pl.BlockSpec((d, tn), lambda j: (0, j)),
                  pl.BlockSpec((1, tn), lambda j: (0, j))],
        out_specs=pl.BlockSpec((8, tn), lambda j: (0, j)),
        out_shape=jax.ShapeDtypeStruct((8, n), F32),
        compiler_params=_cparams("arbitrary"),
        name="ada_mod",
    )(c8, w_ada, b_ada)


def _prenorm_kernel(x_ref, g_ref, scale_ref, shift_ref, o_ref):
    xn = _rms(x_ref[...], g_ref[...])
    o_ref[...] = (xn * (1.0 + scale_ref[0]) + shift_ref[0]).astype(BF16)


def _prenorm(x2, g, mod3, seq, tm=512):
    n, d = x2.shape
    per_b = seq // tm
    return pl.pallas_call(
        _prenorm_kernel,
        grid=(n // tm,),
        in_specs=[pl.BlockSpec((tm, d), lambda i: (i, 0)),
                  pl.BlockSpec((1, d), lambda i: (0, 0)),
                  pl.BlockSpec((1, 1, d), lambda i: ((i // per_b) * ADALN_CHUNKS + 1, 0, 0)),
                  pl.BlockSpec((1, 1, d), lambda i: ((i // per_b) * ADALN_CHUNKS + 0, 0, 0))],
        out_specs=pl.BlockSpec((tm, d), lambda i: (i, 0)),
        out_shape=jax.ShapeDtypeStruct((n, d), BF16),
        compiler_params=_cparams("arbitrary"),
        name="prenorm_attn",
    )(x2, g, mod3, mod3)


def _mm_kernel(h_ref, w_ref, o_ref, *, act):
    y = jnp.dot(h_ref[...], w_ref[...], preferred_element_type=F32)
    if act == "sigmoid":
        y = _sigmoid(y)
    o_ref[...] = y.astype(o_ref.dtype)


def _mm(h, w, act=None, tm=1024, tn=1024):
    n, k = h.shape
    cols = w.shape[1]
    tn = min(tn, cols)
    return pl.pallas_call(
        functools.partial(_mm_kernel, act=act),
        grid=(cols // tn, n // tm),
        in_specs=[pl.BlockSpec((tm, k), lambda j, i: (i, 0)),
                  pl.BlockSpec((k, tn), lambda j, i: (0, j))],
        out_specs=pl.BlockSpec((tm, tn), lambda j, i: (i, j)),
        out_shape=jax.ShapeDtypeStruct((n, cols), BF16),
        compiler_params=_cparams("arbitrary", "arbitrary"),
        name="in_proj_" + (act or "plain"),
    )(h, w)


def _rope_lanes(t, c_tab, s_fwd, s_bwd, half):
    return t * c_tab + pltpu.roll(t, half, 1) * s_fwd + pltpu.roll(t, LANES - half, 1) * s_bwd


def _dilproj_kernel(h_ref, w_ref, c_ref, sf_ref, sb_ref, o_ref):
    y = jnp.dot(h_ref[...], w_ref[...], preferred_element_type=F32)
    c_tab, s_fwd, s_bwd = c_ref[...], sf_ref[...], sb_ref[...]
    n_rot = 2 * DIL_HEADS_PER_GROUP
    for hh in range(3 * DIL_HEADS_PER_GROUP):
        t = y[:, hh * LANES:(hh + 1) * LANES]
        if hh < n_rot:
            t = _rope_lanes(t, c_tab, s_fwd, s_bwd, DIL_ROT_DIM // 2)
        o_ref[:, hh * LANES:(hh + 1) * LANES] = t.astype(BF16)


def _dilproj(h, w, c_tab, s_fwd, s_bwd, tm=1024):
    n, k = h.shape
    cols = w.shape[1]
    tab = pl.BlockSpec((tm, LANES), lambda i: (i, 0))
    return pl.pallas_call(
        _dilproj_kernel,
        grid=(n // tm,),
        in_specs=[pl.BlockSpec((tm, k), lambda i: (i, 0)),
                  pl.BlockSpec((k, cols), lambda i: (0, 0)),
                  tab, tab, tab],
        out_specs=pl.BlockSpec((tm, cols), lambda i: (i, 0)),
        out_shape=jax.ShapeDtypeStruct((n, cols), BF16),
        compiler_params=_cparams("arbitrary"),
        name="dil_proj",
    )(h, w, c_tab, s_fwd, s_bwd)


def _mlaprep_kernel(a_ref, gq_ref, gkv_ref, wq_ref, wkv_ref, c_ref, sf_ref, sb_ref,
                    q_ref, k_ref, v_ref):
    a = a_ref[...].astype(F32)
    qa = a[:, :Q_LORA_RANK]
    ckv = a[:, Q_LORA_RANK:Q_LORA_RANK + KV_LORA_RANK]
    kr = a[:, Q_LORA_RANK + KV_LORA_RANK:]
    c_tab, s_fwd, s_bwd = c_ref[...], sf_ref[...], sb_ref[...]
    half = QK_ROPE_DIM // 2
    q = jnp.dot(_rms(qa, gq_ref[...]).astype(BF16), wq_ref[...], preferred_element_type=F32)
    q = q * (MLA_QK_DIM ** -0.5)
    kv = jnp.dot(_rms(ckv, gkv_ref[...]).astype(BF16), wkv_ref[...], preferred_element_type=F32)
    k_rot = _rope_lanes(kr, c_tab, s_fwd, s_bwd, half).astype(BF16)
    for hh in range(MLA_HEADS):
        base = hh * MLA_QK_PAD
        q_ref[0, hh, :, :LANES] = q[:, base:base + LANES].astype(BF16)
        q_ref[0, hh, :, LANES:] = _rope_lanes(q[:, base + LANES:base + 2 * LANES],
                                              c_tab, s_fwd, s_bwd, half).astype(BF16)
        k_ref[0, hh, :, :LANES] = kv[:, hh * LANES:(hh + 1) * LANES].astype(BF16)
        k_ref[0, hh, :, LANES:] = k_rot
        v_off = MLA_HEADS * LANES + hh * LANES
        v_ref[0, hh, :, :] = kv[:, v_off:v_off + LANES].astype(BF16)


def _mlaprep(a, gq, gkv, wq, wkv, c_tab, s_fwd, s_bwd, batch, seq, tm=512):
    n, cols = a.shape
    per_b = seq // tm
    tab = pl.BlockSpec((tm, LANES), lambda i: (i, 0))
    head_major = lambda w: pl.BlockSpec((1, MLA_HEADS, tm, w), lambda i: (i // per_b, 0, i % per_b, 0))
    return pl.pallas_call(
        _mlaprep_kernel,
        grid=(n // tm,),
        in_specs=[pl.BlockSpec((tm, cols), lambda i: (i, 0)),
                  pl.BlockSpec((1, Q_LORA_RANK), lambda i: (0, 0)),
                  pl.BlockSpec((1, KV_LORA_RANK), lambda i: (0, 0)),
                  pl.BlockSpec(wq.shape, lambda i: (0, 0)),
                  pl.BlockSpec(wkv.shape, lambda i: (0, 0)),
                  tab, tab, tab],
        out_specs=[head_major(MLA_QK_PAD), head_major(MLA_QK_PAD), head_major(V_HEAD_DIM)],
        out_shape=[jax.ShapeDtypeStruct((batch, MLA_HEADS, seq, MLA_QK_PAD), BF16),
                   jax.ShapeDtypeStruct((batch, MLA_HEADS, seq, MLA_QK_PAD), BF16),
                   jax.ShapeDtypeStruct((batch, MLA_HEADS, seq, V_HEAD_DIM), BF16)],
        compiler_params=_cparams("arbitrary"),
        name="mla_prep",
    )(a, gq, gkv, wq, wkv, c_tab, s_fwd, s_bwd)


def _mla_attn_kernel(q_ref, k_ref, v_ref, o_ref, *, tq):
    i = pl.program_id(2)
    q = q_ref[0, 0]

    def step(c, carry, masked):
        m, l, acc = carry
        k = k_ref[0, 0, pl.ds(pl.multiple_of(c * tq, tq), tq), :]
        v = v_ref[0, 0, pl.ds(pl.multiple_of(c * tq, tq), tq), :]
        s = lax.dot_general(q, k, (((1,), (1,)), ((), ())), preferred_element_type=F32)
        if masked:
            row = lax.broadcasted_iota(jnp.int32, (tq, tq), 0)
            col = lax.broadcasted_iota(jnp.int32, (tq, tq), 1)
            s = jnp.where(col <= row, s, NEG_BIG)
        m_new = jnp.maximum(m, jnp.max(s, axis=-1, keepdims=True))
        alpha = jnp.exp(m - m_new)
        p = jnp.exp(s - m_new)
        l = alpha * l + jnp.sum(p, axis=-1, keepdims=True)
        acc = alpha * acc + jnp.dot(p.astype(BF16), v, preferred_element_type=F32)
        return m_new, l, acc

    init = (jnp.full((tq, 1), NEG_BIG, F32), jnp.zeros((tq, 1), F32), jnp.zeros((tq, V_HEAD_DIM), F32))
    carry = lax.fori_loop(0, i, lambda c, cr: step(c, cr, False), init)
    m, l, acc = step(i, carry, True)
    o_ref[0] = (acc / l).astype(BF16)


def _mla_attn(q, k, v, tq=512):
    b, h, s, dk = q.shape
    dv = v.shape[-1]
    return pl.pallas_call(
        functools.partial(_mla_attn_kernel, tq=tq),
        grid=(b, h, s // tq),
        in_specs=[pl.BlockSpec((1, 1, tq, dk), lambda bi, hi, i: (bi, hi, i, 0)),
                  pl.BlockSpec((1, 1, s, dk), lambda bi, hi, i: (bi, hi, 0, 0)),
                  pl.BlockSpec((1, 1, s, dv), lambda bi, hi, i: (bi, hi, 0, 0))],
        out_specs=pl.BlockSpec((1, tq, dv), lambda bi, hi, i: (bi, i, hi)),
        out_shape=jax.ShapeDtypeStruct((b, s, h * dv), BF16),
        compiler_params=_cparams("arbitrary", "arbitrary", "arbitrary"),
        name="mla_attn",
    )(q, k, v)


def _dil_attn_kernel(q_ref, kc_ref, kp_ref, vc_ref, vp_ref, o_ref, lse_ref, *, tq):
    i = pl.program_id(2)
    sub = DIL_SPAN
    row = lax.broadcasted_iota(jnp.int32, (sub, sub), 0)
    col = lax.broadcasted_iota(jnp.int32, (sub, sub), 1)
    lane = lax.broadcasted_iota(jnp.int32, (sub, LANES), 1)
    scale = DIL_HEAD_DIM ** -0.5
    dn = (((1,), (1,)), ((), ()))
    for j in range(tq // sub):
        lse_blk = jnp.zeros((sub, LANES), F32)
        for hh in range(DIL_HEADS_PER_GROUP):
            cs = slice(hh * LANES, (hh + 1) * LANES)
            rs = slice(j * sub, (j + 1) * sub)
            q = q_ref[0, rs, cs]
            if j == 0:
                k_prev, v_prev = kp_ref[0, :, cs], vp_ref[0, :, cs]
                prev_off = jnp.where(i > 0, 0, sub)
            else:
                ps = slice((j - 1) * sub, j * sub)
                k_prev, v_prev = kc_ref[0, ps, cs], vc_ref[0, ps, cs]
                prev_off = 0
            k_cur, v_cur = kc_ref[0, rs, cs], vc_ref[0, rs, cs]
            s_prev = lax.dot_general(q, k_prev, dn, preferred_element_type=F32) * scale
            s_cur = lax.dot_general(q, k_cur, dn, preferred_element_type=F32) * scale
            s_prev = jnp.where(col >= row + prev_off, s_prev, NEG_BIG)
            s_cur = jnp.where(col <= row, s_cur, NEG_BIG)
            m = jnp.maximum(jnp.max(s_prev, axis=-1, keepdims=True), jnp.max(s_cur, axis=-1, keepdims=True))
            p_prev = jnp.exp(s_prev - m)
            p_cur = jnp.exp(s_cur - m)
            l = jnp.sum(p_prev, axis=-1, keepdims=True) + jnp.sum(p_cur, axis=-1, keepdims=True)
            inv = 1.0 / l
            acc = (jnp.dot((p_prev * inv).astype(BF16), v_prev, preferred_element_type=F32)
                   + jnp.dot((p_cur * inv).astype(BF16), v_cur, preferred_element_type=F32))
            o_ref[0, rs, cs] = acc.astype(BF16)
            lse_blk = jnp.where(lane == hh, m + jnp.log(l), lse_blk)
        lse_ref[0, j * sub:(j + 1) * sub, :] = lse_blk


def _dil_attn(qkv, batch, seq, dilation, tq=256):
    ln = seq // dilation
    gc = DIL_GROUP_COLS
    t = qkv.reshape(batch, ln, dilation * 3 * gc)
    ratio = tq // DIL_SPAN
    cur = lambda which: pl.BlockSpec((1, tq, gc), lambda b, r, i: (b, i, r * 3 + which))
    prev = lambda which: pl.BlockSpec(
        (1, DIL_SPAN, gc), lambda b, r, i: (b, jnp.maximum(i * ratio - 1, 0), r * 3 + which))
    o, lse = pl.pallas_call(
        functools.partial(_dil_attn_kernel, tq=tq),
        grid=(batch, dilation, ln // tq),
        in_specs=[cur(0), cur(1), prev(1), cur(2), prev(2)],
        out_specs=[pl.BlockSpec((1, tq, gc), lambda b, r, i: (b, i, r)),
                   pl.BlockSpec((1, tq, LANES), lambda b, r, i: (b, i, r))],
        out_shape=[jax.ShapeDtypeStruct((batch, ln, dilation * gc), BF16),
                   jax.ShapeDtypeStruct((batch, ln, dilation * LANES), F32)],
        compiler_params=_cparams("arbitrary", "arbitrary", "arbitrary"),
        name=f"dil_attn_d{dilation}",
    )(t, t, t, t, t)
    return o.reshape(batch * seq, gc), lse.reshape(batch * seq, LANES)


def _merge_kernel(oa_ref, o0_ref, o1_ref, o2_ref, l0_ref, l1_ref, l2_ref, ga_ref, gb_ref, x_ref,
                  gate_ref, shift_ref, scale_ref, gpost_ref, gpre_ref,
                  wa_ref, wb_ref, wo_ref, wrh_ref, wrl_ref,
                  x1_ref, h2_ref, h2f_ref, logit_ref):
    l0, l1, l2 = l0_ref[...], l1_ref[...], l2_ref[...]
    m = jnp.maximum(jnp.maximum(l0, l1), l2)
    e0, e1, e2 = jnp.exp(l0 - m), jnp.exp(l1 - m), jnp.exp(l2 - m)
    inv = 1.0 / (e0 + e1 + e2)
    w0, w1, w2 = e0 * inv, e1 * inv, e2 * inv
    parts = []
    for hh in range(DIL_HEADS_PER_GROUP):
        cs = slice(hh * LANES, (hh + 1) * LANES)
        parts.append(w0[:, hh:hh + 1] * o0_ref[:, cs].astype(F32)
                     + w1[:, hh:hh + 1] * o1_ref[:, cs].astype(F32)
                     + w2[:, hh:hh + 1] * o2_ref[:, cs].astype(F32))
    o_dil = jnp.concatenate(parts, axis=1).astype(BF16)
    y_a = jnp.dot(oa_ref[...], wa_ref[...], preferred_element_type=F32)
    y_b = jnp.dot(o_dil, wb_ref[...], preferred_element_type=F32)
    merged = ga_ref[...].astype(F32) * y_a + gb_ref[...].astype(F32) * y_b
    y = jnp.dot(merged.astype(BF16), wo_ref[...], preferred_element_type=F32)
    x1 = x_ref[...] + gate_ref[0] * _rms(y, gpost_ref[...])
    x1_ref[...] = x1
    h2 = _rms(x1, gpre_ref[...]) * (1.0 + scale_ref[0]) + shift_ref[0]
    h2f_ref[...] = h2
    h2_hi = h2.astype(BF16)
    h2_ref[...] = h2_hi
    h2_lo = (h2 - h2_hi.astype(F32)).astype(BF16)
    logit_ref[...] = (jnp.dot(h2_hi, wrh_ref[...], preferred_element_type=F32)
                      + jnp.dot(h2_lo, wrh_ref[...], preferred_element_type=F32)
                      + jnp.dot(h2_hi, wrl_ref[...], preferred_element_type=F32))


def _merge(oa, dil_o, dil_lse, gates, x2, mod3, gpost, gpre, wa, wb, wo, wr_hi, wr_lo, seq, tm=256):
    n, d = x2.shape
    per_b = seq // tm
    row = lambda w: pl.BlockSpec((tm, w), lambda i: (i, 0))
    const = lambda a: pl.BlockSpec(a.shape, lambda i: (0,) * a.ndim, pipeline_mode=pl.Buffered(1))
    modspec =lambda ch: pl.BlockSpec((1, 1, d), lambda i: ((i // per_b) * ADALN_CHUNKS + ch, 0, 0))
    return pl.pallas_call(
        _merge_kernel,
        grid=(n // tm,),
        in_specs=[row(oa.shape[1]),
                  row(DIL_GROUP_COLS), row(DIL_GROUP_COLS), row(DIL_GROUP_COLS),
                  row(LANES), row(LANES), row(LANES),
                  pl.BlockSpec((tm, d), lambda i: (i, 0)), pl.BlockSpec((tm, d), lambda i: (i, 1)),
                  row(d),
                  modspec(2), modspec(3), modspec(4),
                  const(gpost), const(gpre),
                  const(wa), const(wb), const(wo), const(wr_hi), const(wr_lo)],
        out_specs=[row(d), row(d), row(d), row(LANES)],
        out_shape=[jax.ShapeDtypeStruct((n, d), F32),
                   jax.ShapeDtypeStruct((n, d), BF16),
                   jax.ShapeDtypeStruct((n, d), F32),
                   jax.ShapeDtypeStruct((n, LANES), F32)],
        compiler_params=_cparams("arbitrary"),
        name="merge_outproj",
    )(oa, *dil_o, *dil_lse, gates, gates, x2, mod3, mod3, mod3, gpost, gpre, wa, wb, wo, wr_hi, wr_lo)


def _gmm_kernel(nact_ref, blke_ref, tok_ref, h_hbm, wg_ref, wu_ref, wd_ref, o_ref,
                xbuf, wgb, wub, wdb, sem):
    i = pl.program_id(0)
    nact = nact_ref[0]
    slot = i % 2

    def issue(blk, sl):
        def body(r, carry):
            t = tok_ref[blk * MOE_BLOCK + r]
            pltpu.make_async_copy(h_hbm.at[pl.ds(t, 1), :], xbuf.at[sl, pl.ds(r, 1), :], sem.at[sl]).start()
            return carry
        lax.fori_loop(0, MOE_BLOCK, body, 0)

    @pl.when(i == 0)
    def _():
        issue(0, 0)

    @pl.when(i + 1 < nact)
    def _():
        issue(i + 1, 1 - slot)

    @pl.when(i < nact)
    def _():
        pltpu.make_async_copy(h_hbm.at[pl.ds(0, MOE_BLOCK), :], xbuf.at[slot], sem.at[slot]).wait()
        changed = jnp.logical_or(i == 0, blke_ref[i] != blke_ref[jnp.maximum(i - 1, 0)])

        @pl.when(changed)
        def _():
            wgb[...] = wg_ref[0].astype(BF16)
            wub[...] = wu_ref[0].astype(BF16)
            wdb[...] = wd_ref[0].astype(BF16)

        xb = xbuf[slot].astype(BF16)
        g = jnp.dot(xb, wgb[...], preferred_element_type=F32)
        u = jnp.dot(xb, wub[...], preferred_element_type=F32)
        a = (g * _sigmoid(g) * u).astype(BF16)
        o_ref[...] = jnp.dot(a, wdb[...], preferred_element_type=F32)

    @pl.when(i >= nact)
    def _():
        o_ref[...] = jnp.zeros_like(o_ref)


def _gmm(nact, blk_e, buf_tok, h2f, w_gate, w_up, w_down, nb):
    n, d = h2f.shape
    f = w_gate.shape[-1]
    wspec = lambda shp: pl.BlockSpec((1,) + shp, lambda i, na, be, tk: (be[i], 0, 0))
    grid_spec = pltpu.PrefetchScalarGridSpec(
        num_scalar_prefetch=3,
        grid=(nb,),
        in_specs=[pl.BlockSpec(memory_space=pl.ANY), wspec((d, f)), wspec((d, f)), wspec((f, d))],
        out_specs=pl.BlockSpec((MOE_BLOCK, d), lambda i, na, be, tk: (i, 0)),
        scratch_shapes=[pltpu.VMEM((2, MOE_BLOCK, d), F32),
                        pltpu.VMEM((d, f), BF16), pltpu.VMEM((d, f), BF16), pltpu.VMEM((f, d), BF16),
                        pltpu.SemaphoreType.DMA((2,))],
    )
    return pl.pallas_call(
        _gmm_kernel,
        grid_spec=grid_spec,
        out_shape=jax.ShapeDtypeStruct((nb * MOE_BLOCK, d), F32),
        compiler_params=_cparams("arbitrary"),
        name="moe_experts",
    )(nact, blk_e, buf_tok, h2f, w_gate, w_up, w_down)


def _final_kernel(pos_ref, tw_ref, h2_ref, x1_ref, gate_ref, gpost_ref, wsg_ref, wsu_ref, wsd_ref, yb_hbm,
                  o_ref, rbuf, sem, *, tm):
    i = pl.program_id(0)

    def body(r, carry):
        for kk in range(TOP_K):
            p = pos_ref[(i * tm + r) * TOP_K + kk]
            pltpu.make_async_copy(yb_hbm.at[pl.ds(p, 1), :], rbuf.at[kk, pl.ds(r, 1), :], sem.at[0]).start()
        return carry
    lax.fori_loop(0, tm, body, 0)

    hb = h2_ref[...]
    g = jnp.dot(hb, wsg_ref[...], preferred_element_type=F32)
    u = jnp.dot(hb, wsu_ref[...], preferred_element_type=F32)
    y = jnp.dot((g * _sigmoid(g) * u).astype(BF16), wsd_ref[...], preferred_element_type=F32)

    for kk in range(TOP_K):
        pltpu.make_async_copy(yb_hbm.at[pl.ds(0, tm), :], rbuf.at[kk], sem.at[0]).wait()
    tw = tw_ref[...]
    for kk in range(TOP_K):
        y = y + tw[:, kk:kk + 1] * rbuf[kk]
    o_ref[...] = x1_ref[...] + gate_ref[0] * _rms(y, gpost_ref[...])


def _final(pos_flat, top_w8, h2, x1, mod3, gpost, wsg, wsu, wsd, yb, seq, tm=128):
    n, d = x1.shape
    per_b = seq // tm
    const = lambda a: pl.BlockSpec(a.shape, lambda i, ps: (0,) * a.ndim)
    row = lambda w: pl.BlockSpec((tm, w), lambda i, ps: (i, 0))
    grid_spec = pltpu.PrefetchScalarGridSpec(
        num_scalar_prefetch=1,
        grid=(n // tm,),
        in_specs=[row(top_w8.shape[1]), row(d), row(d),
                  pl.BlockSpec((1, 1, d), lambda i, ps: ((i // per_b) * ADALN_CHUNKS + 5, 0, 0)),
                  const(gpost), const(wsg), const(wsu), const(wsd),
                  pl.BlockSpec(memory_space=pl.ANY)],
        out_specs=row(d),
        scratch_shapes=[pltpu.VMEM((TOP_K, tm, d), F32), pltpu.SemaphoreType.DMA((1,))],
    )
    return pl.pallas_call(
        functools.partial(_final_kernel, tm=tm),
        grid_spec=grid_spec,
        out_shape=jax.ShapeDtypeStruct((n, d), F32),
        compiler_params=_cparams("arbitrary"),
        name="moe_combine_final",
    )(pos_flat, top_w8, h2, x1, mod3, gpost, wsg, wsu, wsd, yb)


def _route(logits, router_bias):
    n = logits.shape[0]
    scores = jax.nn.sigmoid(logits)
    biased = scores + router_bias.astype(F32)
    per_group = N_EXPERTS // N_EXPERT_GROUPS
    grp_score = lax.top_k(biased.reshape(n, N_EXPERT_GROUPS, per_group), 2)[0].sum(-1)
    _, grp_idx = lax.top_k(grp_score, TOPK_GROUPS)
    grp_mask = jax.nn.one_hot(grp_idx, N_EXPERT_GROUPS, dtype=F32).sum(1) > 0
    sel = jnp.where(jnp.repeat(grp_mask, per_group, axis=1), biased, -jnp.inf)
    _, top_idx = lax.top_k(sel, TOP_K)
    top_w = jnp.take_along_axis(scores, top_idx, axis=-1)
    top_w = top_w / (top_w.sum(-1, keepdims=True) + 1e-20) * ROUTED_SCALE

    onehot = (top_idx[:, :, None] == jnp.arange(N_EXPERTS, dtype=jnp.int32)[None, None, :]).any(1).astype(jnp.int32)
    counts = onehot.sum(0)
    rank = jnp.cumsum(onehot, axis=0) - onehot
    rank_k = jnp.take_along_axis(rank, top_idx, axis=1)
    pcounts = (counts + MOE_BLOCK - 1) // MOE_BLOCK * MOE_BLOCK
    pends = jnp.cumsum(pcounts)
    pstarts = pends - pcounts
    pos = (pstarts[top_idx] + rank_k).astype(jnp.int32)
    nb = -(-(n * TOP_K + N_EXPERTS * (MOE_BLOCK - 1)) // MOE_BLOCK)
    buf_tok = jnp.zeros((nb * MOE_BLOCK,), jnp.int32).at[pos.reshape(-1)].set(
        jnp.repeat(jnp.arange(n, dtype=jnp.int32), TOP_K))
    blk_e = jnp.minimum(jnp.searchsorted(pends, jnp.arange(nb, dtype=jnp.int32) * MOE_BLOCK, side='right'),
                        N_EXPERTS - 1).astype(jnp.int32)
    nact = (pends[-1:] // MOE_BLOCK).astype(jnp.int32)
    return top_w, pos, buf_tok, blk_e, nact, nb


def _rope_tables(positions, dim):
    half = dim // 2
    inv_freq = 1.0 / (ROPE_THETA ** (jnp.arange(0, dim, 2, dtype=F32) / dim))
    ang = positions.astype(F32).reshape(-1, 1) * inv_freq
    cos, sin = jnp.cos(ang), jnp.sin(ang)
    n = cos.shape[0]
    zeros = jnp.zeros((n, LANES - dim), F32)
    zh = jnp.zeros((n, half), F32)
    return cos, sin, zeros, zh


def kernel(x, c, positions, w_ada, b_ada, attn_pre_g, w_in, q_a_norm_g, w_q_up, kv_a_norm_g, w_kv_up, w_mla_o, w_dil_o, w_out, attn_post_g, ffn_pre_g, w_router, router_bias, w_exp_gate, w_exp_up, w_exp_down, w_sh_gate, w_sh_up, w_sh_down, ffn_post_g):
    batch, seq, d = x.shape
    n = batch * seq
    depth = w_ada.shape[0]

    cos, sin, zeros, zh = _rope_tables(positions, QK_ROPE_DIM)
    m_c = jnp.concatenate([cos, cos, zeros], axis=1)
    m_sf = jnp.concatenate([zh, sin, zeros], axis=1)
    m_sb = jnp.concatenate([-sin, zh, zeros], axis=1)
    cos, sin, zeros, zh = _rope_tables(positions, DIL_ROT_DIM)
    d_c = jnp.concatenate([cos, cos, jnp.ones_like(zeros)], axis=1)
    d_sf = jnp.concatenate([zh, sin, zeros], axis=1)
    d_sb = jnp.concatenate([-sin, zh, zeros], axis=1)

    x2 = x.reshape(n, d)
    c8 = jnp.pad(c, ((0, 8 - batch), (0, 0)))
    for l in range(depth):
        mod = _ada(c8, w_ada[l], b_ada[l].reshape(1, -1))
        mod3 = mod[:batch].reshape(batch * ADALN_CHUNKS, 1, d)

        wi = w_in[l]
        o_q, o_kv, o_dil, o_ga = Q_LORA_RANK, Q_LORA_RANK + KV_LORA_RANK + QK_ROPE_DIM, 0, 0
        o_dil = o_kv
        o_ga = o_dil + 3 * DIL_HEADS * DIL_HEAD_DIM
        w_a = jnp.concatenate([wi[:, :o_kv], jnp.zeros((d, LANES - QK_ROPE_DIM), F32)], axis=1).astype(BF16)
        wd3 = wi[:, o_dil:o_ga].reshape(d, 3, DIL_GROUPS, DIL_GROUP_COLS)
        w_dil = [wd3[:, :, g, :].reshape(d, 3 * DIL_GROUP_COLS).astype(BF16) for g in range(DIL_GROUPS)]
        w_g = wi[:, o_ga:].astype(BF16)
        wq3 = w_q_up[l].reshape(Q_LORA_RANK, MLA_HEADS, MLA_QK_DIM)
        wq = jnp.concatenate([wq3, jnp.zeros((Q_LORA_RANK, MLA_HEADS, MLA_QK_PAD - MLA_QK_DIM), F32)],
                             axis=2).reshape(Q_LORA_RANK, MLA_HEADS * MLA_QK_PAD).astype(BF16)
        wkv3 = w_kv_up[l].reshape(KV_LORA_RANK, MLA_HEADS, QK_NOPE_DIM + V_HEAD_DIM)
        wkv = jnp.concatenate([wkv3[:, :, :QK_NOPE_DIM].reshape(KV_LORA_RANK, -1),
                               wkv3[:, :, QK_NOPE_DIM:].reshape(KV_LORA_RANK, -1)], axis=1).astype(BF16)

        h = _prenorm(x2, attn_pre_g[l].reshape(1, d), mod3, seq)
        a = _mm(h, w_a, tn=w_a.shape[1])
        gates = _mm(h, w_g, act="sigmoid")
        q, k, v = _mlaprep(a, q_a_norm_g[l].reshape(1, -1), kv_a_norm_g[l].reshape(1, -1), wq, wkv,
                           m_c, m_sf, m_sb, batch, seq)
        o_mla = _mla_attn(q, k, v).reshape(n, MLA_HEADS * V_HEAD_DIM)
        dil_o, dil_lse = [], []
        for g, (_, dilation) in enumerate(DIL_PATTERNS):
            qkv = _dilproj(h, w_dil[g], d_c, d_sf, d_sb)
            o_g, lse_g = _dil_attn(qkv, batch, seq, dilation)
            dil_o.append(o_g)
            dil_lse.append(lse_g)

        wr = jnp.pad(w_router[l], ((0, 0), (0, LANES - N_EXPERTS)))
        wr_hi = wr.astype(BF16)
        wr_lo = (wr - wr_hi.astype(F32)).astype(BF16)
        x1, h2, h2f, logits = _merge(o_mla, dil_o, dil_lse, gates, x2, mod3,
                                     attn_post_g[l].reshape(1, d), ffn_pre_g[l].reshape(1, d),
                                     w_mla_o[l].astype(BF16), w_dil_o[l].astype(BF16), w_out[l].astype(BF16),
                                     wr_hi, wr_lo, seq)

        top_w, pos, buf_tok, blk_e, nact, nb = _route(logits[:, :N_EXPERTS], router_bias[l])
        yb = _gmm(nact, blk_e, buf_tok, h2f, w_exp_gate[l], w_exp_up[l], w_exp_down[l], nb)
        top_w8 = jnp.pad(top_w, ((0, 0), (0, 8 - TOP_K)))
        x2 = _final(pos.reshape(-1), top_w8, h2, x1, mod3, ffn_post_g[l].reshape(1, d),
                    w_sh_gate[l].astype(BF16), w_sh_up[l].astype(BF16), w_sh_down[l].astype(BF16), yb, seq)
    return x2.reshape(batch, seq, d)
```

```python
import functools

import jax
import jax.numpy as jnp
from jax import lax
from jax.experimental import pallas as pl
from jax.experimental.pallas import tpu as pltpu

F32 = jnp.float32
BF16 = jnp.bfloat16

D_MODEL = 2048
NORM_EPS = 1e-6
ROPE_THETA = 500000.0
ADALN_CHUNKS = 6

MLA_HEADS = 8
Q_LORA_RANK = 512
KV_LORA_RANK = 512
QK_NOPE_DIM = 128
QK_ROPE_DIM = 64
V_HEAD_DIM = 128
MLA_QK_DIM = QK_NOPE_DIM + QK_ROPE_DIM
MLA_QK_PAD = 256

DIL_PATTERNS = ((128, 1), (512, 4), (2048, 16))
DIL_GROUPS = len(DIL_PATTERNS)
DIL_HEADS_PER_GROUP = 4
DIL_HEADS = DIL_GROUPS * DIL_HEADS_PER_GROUP
DIL_HEAD_DIM = 128
DIL_ROT_DIM = DIL_HEAD_DIM // 4
DIL_SPAN = 128
DIL_GROUP_COLS = DIL_HEADS_PER_GROUP * DIL_HEAD_DIM

N_EXPERTS = 64
N_EXPERT_GROUPS = 8
TOPK_GROUPS = 4
TOP_K = 6
EXPERT_DIM = 512
SHARED_DIM = 512
ROUTED_SCALE = 2.5
MOE_BLOCK = 256

LANES = 128
NEG_BIG = -1e30
VMEM_LIMIT = 56 * 1024 * 1024


def _cparams(*sem):
    return pltpu.CompilerParams(dimension_semantics=sem, vmem_limit_bytes=VMEM_LIMIT)


def _sigmoid(v):
    return 1.0 / (1.0 + jnp.exp(-v))


def _rms(v, g):
    ms = jnp.mean(v * v, axis=-1, keepdims=True)
    return v * lax.rsqrt(ms + NORM_EPS) * g


def _ada_kernel(c_ref, w_ref, b_ref, o_ref):
    c = c_ref[...]
    a = (c * _sigmoid(c)).astype(BF16)
    o_ref[...] = jnp.dot(a, w_ref[...].astype(BF16), preferred_element_type=F32) + b_ref[...]


def _ada(c8, w_ada, b_ada, tn=1536):
    d, n = w_ada.shape
    return pl.pallas_call(
        _ada_kernel,
        grid=(n // tn,),
        in_specs=[pl.BlockSpec((8, d), lambda j: (0, 0)),
                  pl.BlockSpec((d, tn), lambda j: (0, j)),
                  pl.BlockSpec((1, tn), lambda j: (0, j))],
        out_specs=pl.BlockSpec((8, tn), lambda j: (0, j)),
        out_shape=jax.ShapeDtypeStruct((8, n), F32),
        compiler_params=_cparams("arbitrary"),
        name="ada_mod",
    )(c8, w_ada, b_ada)


def _prenorm_kernel(x_ref, g_ref, scale_ref, shift_ref, o_ref):
    xn = _rms(x_ref[...], g_ref[...])
    o_ref[...] = (xn * (1.0 + scale_ref[0]) + shift_ref[0]).astype(BF16)


def _prenorm(x2, g, mod3, seq, tm=512):
    n, d = x2.shape
    per_b = seq // tm
    return pl.pallas_call(
        _prenorm_kernel,
        grid=(n // tm,),
        in_specs=[pl.BlockSpec((tm, d), lambda i: (i, 0)),
                  pl.BlockSpec((1, d), lambda i: (0, 0)),
                  pl.BlockSpec((1, 1, d), lambda i: ((i // per_b) * ADALN_CHUNKS + 1, 0, 0)),
                  pl.BlockSpec((1, 1, d), lambda i: ((i // per_b) * ADALN_CHUNKS + 0, 0, 0))],
        out_specs=pl.BlockSpec((tm, d), lambda i: (i, 0)),
        out_shape=jax.ShapeDtypeStruct((n, d), BF16),
        compiler_params=_cparams("arbitrary"),
        name="prenorm_attn",
    )(x2, g, mod3, mod3)


def _mm_kernel(h_ref, w_ref, o_ref, *, act):
    y = jnp.dot(h_ref[...], w_ref[...], preferred_element_type=F32)
    if act == "sigmoid":
        y = _sigmoid(y)
    o_ref[...] = y.astype(o_ref.dtype)


def _mm(h, w, act=None, tm=1024, tn=1024):
    n, k = h.shape
    cols = w.shape[1]
    tn = min(tn, cols)
    return pl.pallas_call(
        functools.partial(_mm_kernel, act=act),
        grid=(cols // tn, n // tm),
        in_specs=[pl.BlockSpec((tm, k), lambda j, i: (i, 0)),
                  pl.BlockSpec((k, tn), lambda j, i: (0, j))],
        out_specs=pl.BlockSpec((tm, tn), lambda j, i: (i, j)),
        out_shape=jax.ShapeDtypeStruct((n, cols), BF16),
        compiler_params=_cparams("arbitrary", "arbitrary"),
        name="in_proj_" + (act or "plain"),
    )(h, w)


def _rope_lanes(t, c_tab, s_fwd, s_bwd, half):
    return t * c_tab + pltpu.roll(t, half, 1) * s_fwd + pltpu.roll(t, LANES - half, 1) * s_bwd


def _dilproj_kernel(h_ref, w_ref, c_ref, sf_ref, sb_ref, o_ref):
    y = jnp.dot(h_ref[...], w_ref[...], preferred_element_type=F32)
    c_tab, s_fwd, s_bwd = c_ref[...], sf_ref[...], sb_ref[...]
    n_rot = 2 * DIL_HEADS_PER_GROUP
    for hh in range(3 * DIL_HEADS_PER_GROUP):
        t = y[:, hh * LANES:(hh + 1) * LANES]
        if hh < n_rot:
            t = _rope_lanes(t, c_tab, s_fwd, s_bwd, DIL_ROT_DIM // 2)
        o_ref[:, hh * LANES:(hh + 1) * LANES] = t.astype(BF16)


def _dilproj(h, w, c_tab, s_fwd, s_bwd, tm=1024):
    n, k = h.shape
    cols = w.shape[1]
    tab = pl.BlockSpec((tm, LANES), lambda i: (i, 0))
    return pl.pallas_call(
        _dilproj_kernel,
        grid=(n // tm,),
        in_specs=[pl.BlockSpec((tm, k), lambda i: (i, 0)),
                  pl.BlockSpec((k, cols), lambda i: (0, 0)),
                  tab, tab, tab],
        out_specs=pl.BlockSpec((tm, cols), lambda i: (i, 0)),
        out_shape=jax.ShapeDtypeStruct((n, cols), BF16),
        compiler_params=_cparams("arbitrary"),
        name="dil_proj",
    )(h, w, c_tab, s_fwd, s_bwd)


def _mlaprep_kernel(a_ref, gq_ref, gkv_ref, wq_ref, wkv_ref, c_ref, sf_ref, sb_ref,
                    q_ref, k_ref, v_ref):
    a = a_ref[...].astype(F32)
    qa = a[:, :Q_LORA_RANK]
    ckv = a[:, Q_LORA_RANK:Q_LORA_RANK + KV_LORA_RANK]
    kr = a[:, Q_LORA_RANK + KV_LORA_RANK:]
    c_tab, s_fwd, s_bwd = c_ref[...], sf_ref[...], sb_ref[...]
    half = QK_ROPE_DIM // 2
    q = jnp.dot(_rms(qa, gq_ref[...]).astype(BF16), wq_ref[...], preferred_element_type=F32)
    q = q * (MLA_QK_DIM ** -0.5)
    kv = jnp.dot(_rms(ckv, gkv_ref[...]).astype(BF16), wkv_ref[...], preferred_element_type=F32)
    k_rot = _rope_lanes(kr, c_tab, s_fwd, s_bwd, half).astype(BF16)
    for hh in range(MLA_HEADS):
        base = hh * MLA_QK_PAD
        q_ref[0, hh, :, :LANES] = q[:, base:base + LANES].astype(BF16)
        q_ref[0, hh, :, LANES:] = _rope_lanes(q[:, base + LANES:base + 2 * LANES],
                                              c_tab, s_fwd, s_bwd, half).astype(BF16)
        k_ref[0, hh, :, :LANES] = kv[:, hh * LANES:(hh + 1) * LANES].astype(BF16)
        k_ref[0, hh, :, LANES:] = k_rot
        v_off = MLA_HEADS * LANES + hh * LANES
        v_ref[0, hh, :, :] = kv[:, v_off:v_off + LANES].astype(BF16)


def _mlaprep(a, gq, gkv, wq, wkv, c_tab, s_fwd, s_bwd, batch, seq, tm=512):
    n, cols = a.shape
    per_b = seq // tm
    tab = pl.BlockSpec((tm, LANES), lambda i: (i, 0))
    head_major = lambda w: pl.BlockSpec((1, MLA_HEADS, tm, w), lambda i: (i // per_b, 0, i % per_b, 0))
    return pl.pallas_call(
        _mlaprep_kernel,
        grid=(n // tm,),
        in_specs=[pl.BlockSpec((tm, cols), lambda i: (i, 0)),
                  pl.BlockSpec((1, Q_LORA_RANK), lambda i: (0, 0)),
                  pl.BlockSpec((1, KV_LORA_RANK), lambda i: (0, 0)),
                  pl.BlockSpec(wq.shape, lambda i: (0, 0)),
                  pl.BlockSpec(wkv.shape, lambda i: (0, 0)),
                  tab, tab, tab],
        out_specs=[head_major(MLA_QK_PAD), head_major(MLA_QK_PAD), head_major(V_HEAD_DIM)],
        out_shape=[jax.ShapeDtypeStruct((batch, MLA_HEADS, seq, MLA_QK_PAD), BF16),
                   jax.ShapeDtypeStruct((batch, MLA_HEADS, seq, MLA_QK_PAD), BF16),
                   jax.ShapeDtypeStruct((batch, MLA_HEADS, seq, V_HEAD_DIM), BF16)],
        compiler_params=_cparams("arbitrary"),
        name="mla_prep",
    )(a, gq, gkv, wq, wkv, c_tab, s_fwd, s_bwd)


def _mla_attn_kernel(q_ref, k_ref, v_ref, o_ref, *, tq):
    i = pl.program_id(2)
    q = q_ref[0, 0]

    def step(c, carry, masked):
        m, l, acc = carry
        k = k_ref[0, 0, pl.ds(pl.multiple_of(c * tq, tq), tq), :]
        v = v_ref[0, 0, pl.ds(pl.multiple_of(c * tq, tq), tq), :]
        s = lax.dot_general(q, k, (((1,), (1,)), ((), ())), preferred_element_type=F32)
        if masked:
            row = lax.broadcasted_iota(jnp.int32, (tq, tq), 0)
            col = lax.broadcasted_iota(jnp.int32, (tq, tq), 1)
            s = jnp.where(col <= row, s, NEG_BIG)
        m_new = jnp.maximum(m, jnp.max(s, axis=-1, keepdims=True))
        alpha = jnp.exp(m - m_new)
        p = jnp.exp(s - m_new)
        l = alpha * l + jnp.sum(p, axis=-1, keepdims=True)
        acc = alpha * acc + jnp.dot(p.astype(BF16), v, preferred_element_type=F32)
        return m_new, l, acc

    init = (jnp.full((tq, 1), NEG_BIG, F32), jnp.zeros((tq, 1), F32), jnp.zeros((tq, V_HEAD_DIM), F32))
    carry = lax.fori_loop(0, i, lambda c, cr: step(c, cr, False), init)
    m, l, acc = step(i, carry, True)
    o_ref[0] = (acc / l).astype(BF16)


def _mla_attn(q, k, v, tq=512):
    b, h, s, dk = q.shape
    dv = v.shape[-1]
    return pl.pallas_call(
        functools.partial(_mla_attn_kernel, tq=tq),
        grid=(b, h, s // tq),
        in_specs=[pl.BlockSpec((1, 1, tq, dk), lambda bi, hi, i: (bi, hi, i, 0)),
                  pl.BlockSpec((1, 1, s, dk), lambda bi, hi, i: (bi, hi, 0, 0)),
                  pl.BlockSpec((1, 1, s, dv), lambda bi, hi, i: (bi, hi, 0, 0))],
        out_specs=pl.BlockSpec((1, tq, dv), lambda bi, hi, i: (bi, i, hi)),
        out_shape=jax.ShapeDtypeStruct((b, s, h * dv), BF16),
        compiler_params=_cparams("arbitrary", "arbitrary", "arbitrary"),
        name="mla_attn",
    )(q, k, v)


def _dil_attn_kernel(q_ref, kc_ref, kp_ref, vc_ref, vp_ref, o_ref, lse_ref, *, tq):
    i = pl.program_id(2)
    sub = DIL_SPAN
    row = lax.broadcasted_iota(jnp.int32, (sub, sub), 0)
    col = lax.broadcasted_iota(jnp.int32, (sub, sub), 1)
    lane = lax.broadcasted_iota(jnp.int32, (sub, LANES), 1)
    scale = DIL_HEAD_DIM ** -0.5
    dn = (((1,), (1,)), ((), ()))
    for j in range(tq // sub):
        lse_blk = jnp.zeros((sub, LANES), F32)
        for hh in range(DIL_HEADS_PER_GROUP):
            cs = slice(hh * LANES, (hh + 1) * LANES)
            rs = slice(j * sub, (j + 1) * sub)
            q = q_ref[0, rs, cs]
            if j == 0:
                k_prev, v_prev = kp_ref[0, :, cs], vp_ref[0, :, cs]
                prev_off = jnp.where(i > 0, 0, sub)
            else:
                ps = slice((j - 1) * sub, j * sub)
                k_prev, v_prev = kc_ref[0, ps, cs], vc_ref[0, ps, cs]
                prev_off = 0
            k_cur, v_cur = kc_ref[0, rs, cs], vc_ref[0, rs, cs]
            s_prev = lax.dot_general(q, k_prev, dn, preferred_element_type=F32) * scale
            s_cur = lax.dot_general(q, k_cur, dn, preferred_element_type=F32) * scale
            s_prev = jnp.where(col >= row + prev_off, s_prev, NEG_BIG)
            s_cur = jnp.where(col <= row, s_cur, NEG_BIG)
            m = jnp.maximum(jnp.max(s_prev, axis=-1, keepdims=True), jnp.max(s_cur, axis=-1, keepdims=True))
            p_prev = jnp.exp(s_prev - m)
            p_cur = jnp.exp(s_cur - m)
            l = jnp.sum(p_prev, axis=-1, keepdims=True) + jnp.sum(p_cur, axis=-1, keepdims=True)
            inv = 1.0 / l
            acc = (jnp.dot((p_prev * inv).astype(BF16), v_prev, preferred_element_type=F32)
                   + jnp.dot((p_cur * inv).astype(BF16), v_cur, preferred_element_type=F32))
            o_ref[0, rs, cs] = acc.astype(BF16)
            lse_blk = jnp.where(lane == hh, m + jnp.log(l), lse_blk)
        lse_ref[0, j * sub:(j + 1) * sub, :] = lse_blk


def _dil_attn(qkv, batch, seq, dilation, tq=256):
    ln = seq // dilation
    gc = DIL_GROUP_COLS
    t = qkv.reshape(batch, ln, dilation * 3 * gc)
    ratio = tq // DIL_SPAN
    cur = lambda which: pl.BlockSpec((1, tq, gc), lambda b, r, i: (b, i, r * 3 + which))
    prev = lambda which: pl.BlockSpec(
        (1, DIL_SPAN, gc), lambda b, r, i: (b, jnp.maximum(i * ratio - 1, 0), r * 3 + which))
    o, lse = pl.pallas_call(
        functools.partial(_dil_attn_kernel, tq=tq),
        grid=(batch, dilation, ln // tq),
        in_specs=[cur(0), cur(1), prev(1), cur(2), prev(2)],
        out_specs=[pl.BlockSpec((1, tq, gc), lambda b, r, i: (b, i, r)),
                   pl.BlockSpec((1, tq, LANES), lambda b, r, i: (b, i, r))],
        out_shape=[jax.ShapeDtypeStruct((batch, ln, dilation * gc), BF16),
                   jax.ShapeDtypeStruct((batch, ln, dilation * LANES), F32)],
        compiler_params=_cparams("arbitrary", "arbitrary", "arbitrary"),
        name=f"dil_attn_d{dilation}",
    )(t, t, t, t, t)
    return o.reshape(batch * seq, gc), lse.reshape(batch * seq, LANES)


def _pack_halves(v):
    w = v.shape[1] // 2
    lo = lax.bitcast_convert_type(v[:, :w].astype(BF16).astype(F32), jnp.uint32)
    hi = lax.bitcast_convert_type(v[:, w:].astype(BF16).astype(F32), jnp.uint32)
    return (lo >> 16) | (hi & jnp.uint32(0xFFFF0000))


def _unpack_halves(pk):
    lo = lax.bitcast_convert_type(pk << 16, F32)
    hi = lax.bitcast_convert_type(pk & jnp.uint32(0xFFFF0000), F32)
    return lo, hi


def _merge_kernel(oa_ref, o0_ref, o1_ref, o2_ref, l0_ref, l1_ref, l2_ref, ga_ref, gb_ref, x_ref,
                  gate_ref, shift_ref, scale_ref, gpost_ref, gpre_ref,
                  wa_ref, wb_ref, wo_ref, wrh_ref, wrl_ref,
                  x1_ref, h2_ref, h2pk_ref, logit_ref):
    l0, l1, l2 = l0_ref[...], l1_ref[...], l2_ref[...]
    m = jnp.maximum(jnp.maximum(l0, l1), l2)
    e0, e1, e2 = jnp.exp(l0 - m), jnp.exp(l1 - m), jnp.exp(l2 - m)
    inv = 1.0 / (e0 + e1 + e2)
    w0, w1, w2 = e0 * inv, e1 * inv, e2 * inv
    parts = []
    for hh in range(DIL_HEADS_PER_GROUP):
        cs = slice(hh * LANES, (hh + 1) * LANES)
        parts.append(w0[:, hh:hh + 1] * o0_ref[:, cs].astype(F32)
                     + w1[:, hh:hh + 1] * o1_ref[:, cs].astype(F32)
                     + w2[:, hh:hh + 1] * o2_ref[:, cs].astype(F32))
    o_dil = jnp.concatenate(parts, axis=1).astype(BF16)
    y_a = jnp.dot(oa_ref[...], wa_ref[...], preferred_element_type=F32)
    y_b = jnp.dot(o_dil, wb_ref[...], preferred_element_type=F32)
    merged = ga_ref[...].astype(F32) * y_a + gb_ref[...].astype(F32) * y_b
    y = jnp.dot(merged.astype(BF16), wo_ref[...], preferred_element_type=F32)
    x1 = x_ref[...] + gate_ref[0] * _rms(y, gpost_ref[...])
    x1_ref[...] = x1
    h2 = _rms(x1, gpre_ref[...]) * (1.0 + scale_ref[0]) + shift_ref[0]
    h2pk_ref[...] = _pack_halves(h2)
    h2_hi = h2.astype(BF16)
    h2_ref[...] = h2_hi
    h2_lo = (h2 - h2_hi.astype(F32)).astype(BF16)
    dn = (((1,), (1,)), ((), ()))
    logit_ref[...] = (lax.dot_general(wrh_ref[...], h2_hi, dn, preferred_element_type=F32)
                      + lax.dot_general(wrh_ref[...], h2_lo, dn, preferred_element_type=F32)
                      + lax.dot_general(wrl_ref[...], h2_hi, dn, preferred_element_type=F32))


def _merge(oa, dil_o, dil_lse, gates, x2, mod3, gpost, gpre, wa, wb, wo, wr_hi, wr_lo, seq, tm=256):
    n, d = x2.shape
    per_b = seq // tm
    row = lambda w: pl.BlockSpec((tm, w), lambda i: (i, 0))
    const = lambda a: pl.BlockSpec(a.shape, lambda i: (0,) * a.ndim, pipeline_mode=pl.Buffered(1))
    modspec =lambda ch: pl.BlockSpec((1, 1, d), lambda i: ((i // per_b) * ADALN_CHUNKS + ch, 0, 0))
    return pl.pallas_call(
        _merge_kernel,
        grid=(n // tm,),
        in_specs=[row(oa.shape[1]),
                  row(DIL_GROUP_COLS), row(DIL_GROUP_COLS), row(DIL_GROUP_COLS),
                  row(LANES), row(LANES), row(LANES),
                  pl.BlockSpec((tm, d), lambda i: (i, 0)), pl.BlockSpec((tm, d), lambda i: (i, 1)),
                  row(d),
                  modspec(2), modspec(3), modspec(4),
                  const(gpost), const(gpre),
                  const(wa), const(wb), const(wo), const(wr_hi), const(wr_lo)],
        out_specs=[row(d), row(d), row(d // 2), pl.BlockSpec((LANES, tm), lambda i: (0, i))],
        out_shape=[jax.ShapeDtypeStruct((n, d), F32),
                   jax.ShapeDtypeStruct((n, d), BF16),
                   jax.ShapeDtypeStruct((n, d // 2), jnp.uint32),
                   jax.ShapeDtypeStruct((LANES, n), F32)],
        compiler_params=_cparams("arbitrary"),
        name="merge_outproj",
    )(oa, *dil_o, *dil_lse, gates, gates, x2, mod3, mod3, mod3, gpost, gpre, wa, wb, wo, wr_hi, wr_lo)


def _route_kernel(lg_ref, bias_ref, pos_ref, w_ref, meta_ref, blke_ref, cnt_sc, base_sc, *, tt):
    ps = pl.program_id(0)
    i = pl.program_id(1)
    per_group = N_EXPERTS // N_EXPERT_GROUPS
    neg_inf = -jnp.inf

    @pl.when(jnp.logical_and(ps == 0, i == 0))
    def _():
        cnt_sc[...] = jnp.zeros_like(cnt_sc)

    scores = _sigmoid(lg_ref[...])
    biased = scores + bias_ref[...]
    b3 = biased.reshape(N_EXPERT_GROUPS, per_group, tt)
    mem = lax.broadcasted_iota(jnp.int32, b3.shape, 1)
    m1 = jnp.max(b3, axis=1, keepdims=True)
    first = jnp.min(jnp.where(b3 == m1, mem, per_group), axis=1, keepdims=True)
    m2 = jnp.max(jnp.where(mem == first, neg_inf, b3), axis=1, keepdims=True)
    gs = m1 + m2
    gidx = lax.broadcasted_iota(jnp.int32, gs.shape, 0)
    grank = jnp.zeros(gs.shape, jnp.int32)
    for g2 in range(N_EXPERT_GROUPS):
        r = gs[g2:g2 + 1]
        beats = jnp.logical_or(r > gs, jnp.logical_and(r == gs, g2 < gidx))
        grank = grank + jnp.where(beats, 1, 0)
    sel = jnp.where(grank < TOPK_GROUPS, b3, neg_inf).reshape(N_EXPERTS, tt)
    eidx = lax.broadcasted_iota(jnp.int32, sel.shape, 0)
    erank = jnp.zeros(sel.shape, jnp.int32)
    for e2 in range(N_EXPERTS):
        r = sel[e2:e2 + 1, :]
        beats = jnp.logical_or(r > sel, jnp.logical_and(r == sel, e2 < eidx))
        erank = erank + jnp.where(beats, 1, 0)
    esel = erank < TOP_K
    mask_f = jnp.where(esel, 1.0, 0.0)
    tile_cnt = jnp.sum(mask_f, axis=1, keepdims=True).astype(jnp.int32)

    @pl.when(ps == 0)
    def _():
        cnt_sc[...] = cnt_sc[...] + tile_cnt

    @pl.when(jnp.logical_and(ps == 1, i == 0))
    def _():
        cnt = cnt_sc[...]
        pc = ((cnt + (MOE_BLOCK - 1)) // MOE_BLOCK) * MOE_BLOCK
        pcb = jnp.broadcast_to(pc, (N_EXPERTS, LANES))
        eid = lax.broadcasted_iota(jnp.int32, (N_EXPERTS, LANES), 0)
        pends = jnp.zeros((N_EXPERTS, LANES), jnp.int32)
        for e2 in range(N_EXPERTS):
            pends = pends + jnp.where(eid >= e2, pcb[e2:e2 + 1, :], 0)
        pst = pends - pcb
        base_sc[...] = pst[:, 0:1]
        meta_ref[0] = jnp.broadcast_to(cnt, (N_EXPERTS, LANES))
        meta_ref[1] = pst
        meta_ref[2] = pends
        nbl = blke_ref.shape[1]
        blk_start = lax.broadcasted_iota(jnp.int32, (N_EXPERTS, nbl), 1) * MOE_BLOCK
        pend_b = jnp.broadcast_to(pends[:, 0:1], (N_EXPERTS, nbl))
        be = jnp.sum(jnp.where(pend_b <= blk_start, 1, 0), axis=0, keepdims=True)
        blke_ref[...] = jnp.broadcast_to(jnp.minimum(be, N_EXPERTS - 1), blke_ref.shape)

    @pl.when(ps == 1)
    def _():
        rr = lax.broadcasted_iota(jnp.int32, (tt, tt), 0)
        cc = lax.broadcasted_iota(jnp.int32, (tt, tt), 1)
        upper = jnp.where(rr < cc, 1.0, 0.0).astype(BF16)
        prefix = jnp.dot(mask_f.astype(BF16), upper, preferred_element_type=F32)
        posd = base_sc[...] + prefix.astype(jnp.int32)
        base_sc[...] = base_sc[...] + tile_cnt
        wsel = jnp.where(esel, scores, 0.0)
        denom = jnp.sum(wsel, axis=0, keepdims=True)
        wn = wsel / (denom + 1e-20) * ROUTED_SCALE
        prow, wrow = [], []
        for kk in range(TOP_K):
            hit = erank == kk
            prow.append(jnp.sum(jnp.where(hit, posd, 0), axis=0, keepdims=True))
            wrow.append(jnp.sum(jnp.where(hit, wn, 0.0), axis=0, keepdims=True))
        pad = pos_ref.shape[0] - TOP_K
        pos_ref[...] = jnp.concatenate(prow + [jnp.zeros((pad, tt), jnp.int32)], axis=0)
        w_ref[...] = jnp.concatenate(wrow + [jnp.zeros((pad, tt), F32)], axis=0)


def _route(logits_t, bias_col, nb, tt=256):
    n = logits_t.shape[1]
    nbl = -(-nb // LANES) * LANES
    return pl.pallas_call(
        functools.partial(_route_kernel, tt=tt),
        grid=(2, n // tt),
        in_specs=[pl.BlockSpec((N_EXPERTS, tt), lambda ps, i: (0, i)),
                  pl.BlockSpec((N_EXPERTS, 1), lambda ps, i: (0, 0))],
        out_specs=[pl.BlockSpec((8, tt), lambda ps, i: (0, ps * i)),
                   pl.BlockSpec((8, tt), lambda ps, i: (0, ps * i)),
                   pl.BlockSpec((3, N_EXPERTS, LANES), lambda ps, i: (0, 0, 0)),
                   pl.BlockSpec((8, nbl), lambda ps, i: (0, 0))],
        out_shape=[jax.ShapeDtypeStruct((8, n), jnp.int32),
                   jax.ShapeDtypeStruct((8, n), F32),
                   jax.ShapeDtypeStruct((3, N_EXPERTS, LANES), jnp.int32),
                   jax.ShapeDtypeStruct((8, nbl), jnp.int32)],
        scratch_shapes=[pltpu.VMEM((N_EXPERTS, 1), jnp.int32), pltpu.VMEM((N_EXPERTS, 1), jnp.int32)],
        compiler_params=_cparams("arbitrary", "arbitrary"),
        name="moe_route",
    )(logits_t, bias_col)


def _dispatch_kernel(pos_ref, cnt_ref, pst_ref, h_ref, xs_hbm, zrow, sem, *, tm, n):
    i = pl.program_id(0)

    def body(r, carry):
        for kk in range(TOP_K):
            p = pos_ref[kk * n + i * tm + r]
            pltpu.make_async_copy(h_ref.at[pl.ds(r, 1), :], xs_hbm.at[pl.ds(p, 1), :], sem.at[0]).start()
        return carry
    lax.fori_loop(0, tm, body, 0)

    @pl.when(i == pl.num_programs(0) - 1)
    def _():
        zrow[...] = jnp.zeros_like(zrow)

        def per_expert(e, carry):
            cnt = cnt_ref[e]
            first = pst_ref[e] + cnt
            npad = ((cnt + (MOE_BLOCK - 1)) // MOE_BLOCK) * MOE_BLOCK - cnt

            def start(s, c2):
                pltpu.make_async_copy(zrow.at[pl.ds(0, 1), :], xs_hbm.at[pl.ds(first + s, 1), :], sem.at[1]).start()
                return c2

            def wait(s, c2):
                pltpu.make_async_copy(zrow.at[pl.ds(0, 1), :], xs_hbm.at[pl.ds(0, 1), :], sem.at[1]).wait()
                return c2
            lax.fori_loop(0, npad, start, 0)
            lax.fori_loop(0, npad, wait, 0)
            return carry
        lax.fori_loop(0, N_EXPERTS, per_expert, 0)

        last = N_EXPERTS - 1
        used = (pst_ref[last] + cnt_ref[last] + (MOE_BLOCK - 1)) // MOE_BLOCK

        def tail(b, carry):
            cp = pltpu.make_async_copy(zrow, xs_hbm.at[pl.ds(b * MOE_BLOCK, MOE_BLOCK), :], sem.at[1])
            cp.start()
            cp.wait()
            return carry
        lax.fori_loop(used, xs_hbm.shape[0] // MOE_BLOCK, tail, 0)

    for kk in range(TOP_K):
        pltpu.make_async_copy(h_ref, xs_hbm.at[pl.ds(0, tm), :], sem.at[0]).wait()


def _dispatch(pos_flat, cnt, pst, h2pk, nb, tm=256):
    n, w = h2pk.shape
    grid_spec = pltpu.PrefetchScalarGridSpec(
        num_scalar_prefetch=3,
        grid=(n // tm,),
        in_specs=[pl.BlockSpec((tm, w), lambda i, a, b, c: (i, 0))],
        out_specs=pl.BlockSpec(memory_space=pl.ANY),
        scratch_shapes=[pltpu.VMEM((MOE_BLOCK, w), jnp.uint32), pltpu.SemaphoreType.DMA((2,))],
    )
    return pl.pallas_call(
        functools.partial(_dispatch_kernel, tm=tm, n=n),
        grid_spec=grid_spec,
        out_shape=jax.ShapeDtypeStruct((nb * MOE_BLOCK, w), jnp.uint32),
        compiler_params=_cparams("arbitrary"),
        name="moe_dispatch",
    )(pos_flat, cnt, pst, h2pk)


def _gmm_kernel(nact_ref, blke_ref, xs_ref, wg_ref, wu_ref, wd_ref, o_ref, wgb, wub, wdb):
    i = pl.program_id(0)

    @pl.when(i < nact_ref[0])
    def _():
        changed = jnp.logical_or(i == 0, blke_ref[i] != blke_ref[jnp.maximum(i - 1, 0)])

        @pl.when(changed)
        def _():
            wgb[...] = wg_ref[0].astype(BF16)
            wub[...] = wu_ref[0].astype(BF16)
            wdb[...] = wd_ref[0].astype(BF16)

        lo, hi = _unpack_halves(xs_ref[...])
        lo, hi = lo.astype(BF16), hi.astype(BF16)
        half = lo.shape[1]
        g = (jnp.dot(lo, wgb[:half, :], preferred_element_type=F32)
             + jnp.dot(hi, wgb[half:, :], preferred_element_type=F32))
        u = (jnp.dot(lo, wub[:half, :], preferred_element_type=F32)
             + jnp.dot(hi, wub[half:, :], preferred_element_type=F32))
        a = (g * _sigmoid(g) * u).astype(BF16)
        o_ref[...] = _pack_halves(jnp.dot(a, wdb[...], preferred_element_type=F32))

    @pl.when(i >= nact_ref[0])
    def _():
        o_ref[...] = jnp.zeros_like(o_ref)


def _gmm(nact, blk_e, xs, w_gate, w_up, w_down, nb):
    d, f = w_gate.shape[1:]
    blk = lambda i, na: jnp.minimum(i, na[0] - 1)
    wspec = lambda shp: pl.BlockSpec((1,) + shp, lambda i, na, be: (be[blk(i, na)], 0, 0))
    grid_spec = pltpu.PrefetchScalarGridSpec(
        num_scalar_prefetch=2,
        grid=(nb,),
        in_specs=[pl.BlockSpec((MOE_BLOCK, d // 2), lambda i, na, be: (blk(i, na), 0)),
                  wspec((d, f)), wspec((d, f)), wspec((f, d))],
        out_specs=pl.BlockSpec((MOE_BLOCK, d // 2), lambda i, na, be: (i, 0)),
        scratch_shapes=[pltpu.VMEM((d, f), BF16), pltpu.VMEM((d, f), BF16), pltpu.VMEM((f, d), BF16)],
    )
    return pl.pallas_call(
        _gmm_kernel,
        grid_spec=grid_spec,
        out_shape=jax.ShapeDtypeStruct((nb * MOE_BLOCK, d // 2), jnp.uint32),
        compiler_params=_cparams("arbitrary"),
        name="moe_experts",
    )(nact, blk_e, xs, w_gate, w_up, w_down)


def _final_kernel(pos_ref, tw_ref, h2_ref, x1_ref, gate_ref, gpost_ref, wsg_ref, wsu_ref, wsd_ref, yb_hbm,
                  o_ref, rbuf, sem, *, tm, n):
    i = pl.program_id(0)

    def body(r, carry):
        for kk in range(TOP_K):
            p = pos_ref[kk * n + i * tm + r]
            pltpu.make_async_copy(yb_hbm.at[pl.ds(p, 1), :], rbuf.at[kk, pl.ds(r, 1), :], sem.at[0]).start()
        return carry
    lax.fori_loop(0, tm, body, 0)

    hb = h2_ref[...]
    g = jnp.dot(hb, wsg_ref[...], preferred_element_type=F32)
    u = jnp.dot(hb, wsu_ref[...], preferred_element_type=F32)
    y = jnp.dot((g * _sigmoid(g) * u).astype(BF16), wsd_ref[...], preferred_element_type=F32)

    for kk in range(TOP_K):
        pltpu.make_async_copy(yb_hbm.at[pl.ds(0, tm), :], rbuf.at[kk], sem.at[0]).wait()
    tw = tw_ref[...]
    r_lo = jnp.zeros((tm, rbuf.shape[2]), F32)
    r_hi = jnp.zeros((tm, rbuf.shape[2]), F32)
    for kk in range(TOP_K):
        lo, hi = _unpack_halves(rbuf[kk])
        wk = tw[:, kk:kk + 1]
        r_lo = r_lo + wk * lo
        r_hi = r_hi + wk * hi
    y = y + jnp.concatenate([r_lo, r_hi], axis=1)
    o_ref[...] = x1_ref[...] + gate_ref[0] * _rms(y, gpost_ref[...])


def _final(pos_flat, top_w8, h2, x1, mod3, gpost, wsg, wsu, wsd, yb, seq, tm=256):
    n, d = x1.shape
    per_b = seq // tm
    const = lambda a: pl.BlockSpec(a.shape, lambda i, ps: (0,) * a.ndim)
    row = lambda w: pl.BlockSpec((tm, w), lambda i, ps: (i, 0))
    grid_spec = pltpu.PrefetchScalarGridSpec(
        num_scalar_prefetch=1,
        grid=(n // tm,),
        in_specs=[row(top_w8.shape[1]), row(d), row(d),
                  pl.BlockSpec((1, 1, d), lambda i, ps: ((i // per_b) * ADALN_CHUNKS + 5, 0, 0)),
                  const(gpost), const(wsg), const(wsu), const(wsd),
                  pl.BlockSpec(memory_space=pl.ANY)],
        out_specs=row(d),
        scratch_shapes=[pltpu.VMEM((TOP_K, tm, d // 2), jnp.uint32), pltpu.SemaphoreType.DMA((1,))],
    )
    return pl.pallas_call(
        functools.partial(_final_kernel, tm=tm, n=n),
        grid_spec=grid_spec,
        out_shape=jax.ShapeDtypeStruct((n, d), F32),
        compiler_params=_cparams("arbitrary"),
        name="moe_combine_final",
    )(pos_flat, top_w8, h2, x1, mod3, gpost, wsg, wsu, wsd, yb)


def _rope_tables(positions, dim):
    half = dim // 2
    inv_freq = 1.0 / (ROPE_THETA ** (jnp.arange(0, dim, 2, dtype=F32) / dim))
    ang = positions.astype(F32).reshape(-1, 1) * inv_freq
    cos, sin = jnp.cos(ang), jnp.sin(ang)
    n = cos.shape[0]
    zeros = jnp.zeros((n, LANES - dim), F32)
    zh = jnp.zeros((n, half), F32)
    return cos, sin, zeros, zh


def kernel(x, c, positions, w_ada, b_ada, attn_pre_g, w_in, q_a_norm_g, w_q_up, kv_a_norm_g, w_kv_up, w_mla_o, w_dil_o, w_out, attn_post_g, ffn_pre_g, w_router, router_bias, w_exp_gate, w_exp_up, w_exp_down, w_sh_gate, w_sh_up, w_sh_down, ffn_post_g):
    batch, seq, d = x.shape
    n = batch * seq
    depth = w_ada.shape[0]

    cos, sin, zeros, zh = _rope_tables(positions, QK_ROPE_DIM)
    m_c = jnp.concatenate([cos, cos, zeros], axis=1)
    m_sf = jnp.concatenate([zh, sin, zeros], axis=1)
    m_sb = jnp.concatenate([-sin, zh, zeros], axis=1)
    cos, sin, zeros, zh = _rope_tables(positions, DIL_ROT_DIM)
    d_c = jnp.concatenate([cos, cos, jnp.ones_like(zeros)], axis=1)
    d_sf = jnp.concatenate([zh, sin, zeros], axis=1)
    d_sb = jnp.concatenate([-sin, zh, zeros], axis=1)

    x2 = x.reshape(n, d)
    c8 = jnp.pad(c, ((0, 8 - batch), (0, 0)))
    for l in range(depth):
        mod = _ada(c8, w_ada[l], b_ada[l].reshape(1, -1))
        mod3 = mod[:batch].reshape(batch * ADALN_CHUNKS, 1, d)

        wi = w_in[l]
        o_q, o_kv, o_dil, o_ga = Q_LORA_RANK, Q_LORA_RANK + KV_LORA_RANK + QK_ROPE_DIM, 0, 0
        o_dil = o_kv
        o_ga = o_dil + 3 * DIL_HEADS * DIL_HEAD_DIM
        w_a = jnp.concatenate([wi[:, :o_kv], jnp.zeros((d, LANES - QK_ROPE_DIM), F32)], axis=1).astype(BF16)
        wd3 = wi[:, o_dil:o_ga].reshape(d, 3, DIL_GROUPS, DIL_GROUP_COLS)
        w_dil = [wd3[:, :, g, :].reshape(d, 3 * DIL_GROUP_COLS).astype(BF16) for g in range(DIL_GROUPS)]
        w_g = wi[:, o_ga:].astype(BF16)
        wq3 = w_q_up[l].reshape(Q_LORA_RANK, MLA_HEADS, MLA_QK_DIM)
        wq = jnp.concatenate([wq3, jnp.zeros((Q_LORA_RANK, MLA_HEADS, MLA_QK_PAD - MLA_QK_DIM), F32)],
                             axis=2).reshape(Q_LORA_RANK, MLA_HEADS * MLA_QK_PAD).astype(BF16)
        wkv3 = w_kv_up[l].reshape(KV_LORA_RANK, MLA_HEADS, QK_NOPE_DIM + V_HEAD_DIM)
        wkv = jnp.concatenate([wkv3[:, :, :QK_NOPE_DIM].reshape(KV_LORA_RANK, -1),
                               wkv3[:, :, QK_NOPE_DIM:].reshape(KV_LORA_RANK, -1)], axis=1).astype(BF16)

        h = _prenorm(x2, attn_pre_g[l].reshape(1, d), mod3, seq)
        a = _mm(h, w_a, tn=w_a.shape[1])
        gates = _mm(h, w_g, act="sigmoid")
        q, k, v = _mlaprep(a, q_a_norm_g[l].reshape(1, -1), kv_a_norm_g[l].reshape(1, -1), wq, wkv,
                           m_c, m_sf, m_sb, batch, seq)
        o_mla = _mla_attn(q, k, v).reshape(n, MLA_HEADS * V_HEAD_DIM)
        dil_o, dil_lse = [], []
        for g, (_, dilation) in enumerate(DIL_PATTERNS):
            qkv = _dilproj(h, w_dil[g], d_c, d_sf, d_sb)
            o_g, lse_g = _dil_attn(qkv, batch, seq, dilation)
            dil_o.append(o_g)
            dil_lse.append(lse_g)

        wr = jnp.pad(w_router[l].T, ((0, LANES - N_EXPERTS), (0, 0)))
        wr_hi = wr.astype(BF16)
        wr_lo = (wr - wr_hi.astype(F32)).astype(BF16)
        x1, h2, h2pk, logits_t = _merge(o_mla, dil_o, dil_lse, gates, x2, mod3,
                                        attn_post_g[l].reshape(1, d), ffn_pre_g[l].reshape(1, d),
                                        w_mla_o[l].astype(BF16), w_dil_o[l].astype(BF16), w_out[l].astype(BF16),
                                        wr_hi, wr_lo, seq)

        nb = -(-(n * TOP_K + N_EXPERTS * (MOE_BLOCK - 1)) // MOE_BLOCK)
        pos_t, w_t, meta, blk_e = _route(logits_t, router_bias[l].astype(F32).reshape(N_EXPERTS, 1), nb)
        pos_flat = pos_t.reshape(-1)
        nact = meta[2, N_EXPERTS - 1, :1] // MOE_BLOCK
        xs = _dispatch(pos_flat, meta[0, :, 0], meta[1, :, 0], h2pk, nb)
        yb = _gmm(nact, blk_e[0], xs, w_exp_gate[l], w_exp_up[l], w_exp_down[l], nb)
        x2 = _final(pos_flat, w_t.T, h2, x1, mod3, ffn_post_g[l].reshape(1, d),
                    w_sh_gate[l].astype(BF16), w_sh_up[l].astype(BF16), w_sh_down[l].astype(BF16), yb, seq)
    return x2.reshape(batch, seq, d)
```

```python
import functools

import jax
import jax.numpy as jnp
from jax import lax
from jax.experimental import pallas as pl
from jax.experimental.pallas import tpu as pltpu

F32 = jnp.float32
BF16 = jnp.bfloat16

D_MODEL = 2048
NORM_EPS = 1e-6
ROPE_THETA = 500000.0
ADALN_CHUNKS = 6

MLA_HEADS = 8
Q_LORA_RANK = 512
KV_LORA_RANK = 512
QK_NOPE_DIM = 128
QK_ROPE_DIM = 64
V_HEAD_DIM = 128
MLA_QK_DIM = QK_NOPE_DIM + QK_ROPE_DIM
MLA_QK_PAD = 256

DIL_PATTERNS = ((128, 1), (512, 4), (2048, 16))
DIL_GROUPS = len(DIL_PATTERNS)
DIL_HEADS_PER_GROUP = 4
DIL_HEADS = DIL_GROUPS * DIL_HEADS_PER_GROUP
DIL_HEAD_DIM = 128
DIL_ROT_DIM = DIL_HEAD_DIM // 4
DIL_SPAN = 128
DIL_GROUP_COLS = DIL_HEADS_PER_GROUP * DIL_HEAD_DIM

N_EXPERTS = 64
N_EXPERT_GROUPS = 8
TOPK_GROUPS = 4
TOP_K = 6
EXPERT_DIM = 512
SHARED_DIM = 512
ROUTED_SCALE = 2.5
MOE_BLOCK = 256

LANES = 128
NEG_BIG = -1e30
LOG2_E = 1.4426950408889634
VMEM_LIMIT = 56 * 1024 * 1024


def _cparams(*sem):
    return pltpu.CompilerParams(dimension_semantics=sem, vmem_limit_bytes=VMEM_LIMIT)


def _sigmoid(v):
    return 1.0 / (1.0 + jnp.exp(-v))


def _rms(v, g):
    ms = jnp.mean(v * v, axis=-1, keepdims=True)
    return v * lax.rsqrt(ms + NORM_EPS) * g


def _ada_kernel(c_ref, w_ref, b_ref, o_ref):
    c = c_ref[...]
    a = (c * _sigmoid(c)).astype(BF16)
    o_ref[...] = jnp.dot(a, w_ref[...].astype(BF16), preferred_element_type=F32) + b_ref[...]


def _ada(c8, w_ada, b_ada, tn=1536):
    d, n = w_ada.shape
    return pl.pallas_call(
        _ada_kernel,
        grid=(n // tn,),
        in_specs=[pl.BlockSpec((8, d), lambda j: (0, 0)),
                  pl.BlockSpec((d, tn), lambda j: (0, j)),
                  pl.BlockSpec((1, tn), lambda j: (0, j))],
        out_specs=pl.BlockSpec((8, tn), lambda j: (0, j)),
        out_shape=jax.ShapeDtypeStruct((8, n), F32),
        compiler_params=_cparams("arbitrary"),
        name="ada_mod",
    )(c8, w_ada, b_ada)


def _prenorm_kernel(x_ref, g_ref, scale_ref, shift_ref, o_ref):
    xn = _rms(x_ref[...], g_ref[...])
    o_ref[...] = (xn * (1.0 + scale_ref[0]) + shift_ref[0]).astype(BF16)


def _prenorm(x2, g, mod3, seq, tm=512):
    n, d = x2.shape
    per_b = seq // tm
    return pl.pallas_call(
        _prenorm_kernel,
        grid=(n // tm,),
        in_specs=[pl.BlockSpec((tm, d), lambda i: (i, 0)),
                  pl.BlockSpec((1, d), lambda i: (0, 0)),
                  pl.BlockSpec((1, 1, d), lambda i: ((i // per_b) * ADALN_CHUNKS + 1, 0, 0)),
                  pl.BlockSpec((1, 1, d), lambda i: ((i // per_b) * ADALN_CHUNKS + 0, 0, 0))],
        out_specs=pl.BlockSpec((tm, d), lambda i: (i, 0)),
        out_shape=jax.ShapeDtypeStruct((n, d), BF16),
        compiler_params=_cparams("arbitrary"),
        name="prenorm_attn",
    )(x2, g, mod3, mod3)


def _mm_kernel(h_ref, w_ref, o_ref, *, act):
    y = jnp.dot(h_ref[...], w_ref[...], preferred_element_type=F32)
    if act == "sigmoid":
        y = _sigmoid(y)
    o_ref[...] = y.astype(o_ref.dtype)


def _mm(h, w, act=None, tm=1024, tn=1024):
    n, k = h.shape
    cols = w.shape[1]
    tn = min(tn, cols)
    return pl.pallas_call(
        functools.partial(_mm_kernel, act=act),
        grid=(cols // tn, n // tm),
        in_specs=[pl.BlockSpec((tm, k), lambda j, i: (i, 0)),
                  pl.BlockSpec((k, tn), lambda j, i: (0, j))],
        out_specs=pl.BlockSpec((tm, tn), lambda j, i: (i, j)),
        out_shape=jax.ShapeDtypeStruct((n, cols), BF16),
        compiler_params=_cparams("arbitrary", "arbitrary"),
        name="in_proj_" + (act or "plain"),
    )(h, w)


def _rope_lanes(t, c_tab, s_fwd, s_bwd, half):
    return t * c_tab + pltpu.roll(t, half, 1) * s_fwd + pltpu.roll(t, LANES - half, 1) * s_bwd


def _dilproj_kernel(h_ref, w_ref, c_ref, sf_ref, sb_ref, o_ref):
    y = jnp.dot(h_ref[...], w_ref[...], preferred_element_type=F32)
    c_tab, s_fwd, s_bwd = c_ref[...], sf_ref[...], sb_ref[...]
    n_rot = 2 * DIL_HEADS_PER_GROUP
    for hh in range(3 * DIL_HEADS_PER_GROUP):
        t = y[:, hh * LANES:(hh + 1) * LANES]
        if hh < n_rot:
            t = _rope_lanes(t, c_tab, s_fwd, s_bwd, DIL_ROT_DIM // 2)
        o_ref[:, hh * LANES:(hh + 1) * LANES] = t.astype(BF16)


def _dilproj(h, w, c_tab, s_fwd, s_bwd, tm=1024):
    n, k = h.shape
    cols = w.shape[1]
    tab = pl.BlockSpec((tm, LANES), lambda i: (i, 0))
    return pl.pallas_call(
        _dilproj_kernel,
        grid=(n // tm,),
        in_specs=[pl.BlockSpec((tm, k), lambda i: (i, 0)),
                  pl.BlockSpec((k, cols), lambda i: (0, 0)),
                  tab, tab, tab],
        out_specs=pl.BlockSpec((tm, cols), lambda i: (i, 0)),
        out_shape=jax.ShapeDtypeStruct((n, cols), BF16),
        compiler_params=_cparams("arbitrary"),
        name="dil_proj",
    )(h, w, c_tab, s_fwd, s_bwd)


def _mlaprep_kernel(a_ref, gq_ref, gkv_ref, wq_ref, wkv_ref, c_ref, sf_ref, sb_ref,
                    q_ref, k_ref, v_ref):
    a = a_ref[...].astype(F32)
    qa = a[:, :Q_LORA_RANK]
    ckv = a[:, Q_LORA_RANK:Q_LORA_RANK + KV_LORA_RANK]
    kr = a[:, Q_LORA_RANK + KV_LORA_RANK:]
    c_tab, s_fwd, s_bwd = c_ref[...], sf_ref[...], sb_ref[...]
    half = QK_ROPE_DIM // 2
    q = jnp.dot(_rms(qa, gq_ref[...]).astype(BF16), wq_ref[...], preferred_element_type=F32)
    q = q * (MLA_QK_DIM ** -0.5 * LOG2_E)
    kv = jnp.dot(_rms(ckv, gkv_ref[...]).astype(BF16), wkv_ref[...], preferred_element_type=F32)
    k_rot = _rope_lanes(kr, c_tab, s_fwd, s_bwd, half).astype(BF16)
    lane = lax.broadcasted_iota(jnp.int32, (a.shape[0], LANES), 1)
    ones_col = jnp.where(lane == 0, 1.0, 0.0).astype(BF16)
    for hh in range(MLA_HEADS):
        base = hh * MLA_QK_PAD
        q_ref[0, hh, :, :LANES] = q[:, base:base + LANES].astype(BF16)
        q_ref[0, hh, :, LANES:] = _rope_lanes(q[:, base + LANES:base + 2 * LANES],
                                              c_tab, s_fwd, s_bwd, half).astype(BF16)
        k_ref[0, hh, :, :LANES] = kv[:, hh * LANES:(hh + 1) * LANES].astype(BF16)
        k_ref[0, hh, :, LANES:] = k_rot
        v_off = MLA_HEADS * LANES + hh * LANES
        v_ref[0, hh, :, :LANES] = kv[:, v_off:v_off + LANES].astype(BF16)
        v_ref[0, hh, :, LANES:] = ones_col


def _mlaprep(a, gq, gkv, wq, wkv, c_tab, s_fwd, s_bwd, batch, seq, tm=512):
    n, cols = a.shape
    per_b = seq // tm
    tab = pl.BlockSpec((tm, LANES), lambda i: (i, 0))
    head_major = lambda w: pl.BlockSpec((1, MLA_HEADS, tm, w), lambda i: (i // per_b, 0, i % per_b, 0))
    return pl.pallas_call(
        _mlaprep_kernel,
        grid=(n // tm,),
        in_specs=[pl.BlockSpec((tm, cols), lambda i: (i, 0)),
                  pl.BlockSpec((1, Q_LORA_RANK), lambda i: (0, 0)),
                  pl.BlockSpec((1, KV_LORA_RANK), lambda i: (0, 0)),
                  pl.BlockSpec(wq.shape, lambda i: (0, 0)),
                  pl.BlockSpec(wkv.shape, lambda i: (0, 0)),
                  tab, tab, tab],
        out_specs=[head_major(MLA_QK_PAD), head_major(MLA_QK_PAD), head_major(2 * V_HEAD_DIM)],
        out_shape=[jax.ShapeDtypeStruct((batch, MLA_HEADS, seq, MLA_QK_PAD), BF16),
                   jax.ShapeDtypeStruct((batch, MLA_HEADS, seq, MLA_QK_PAD), BF16),
                   jax.ShapeDtypeStruct((batch, MLA_HEADS, seq, 2 * V_HEAD_DIM), BF16)],
        compiler_params=_cparams("arbitrary"),
        name="mla_prep",
    )(a, gq, gkv, wq, wkv, c_tab, s_fwd, s_bwd)


def _mla_attn_kernel(q_ref, k_ref, v_ref, o_ref, *, tq, nh):
    i = pl.program_id(2)
    qs = [q_ref[0, hh] for hh in range(nh)]

    def step(c, carry, masked):
        base = pl.multiple_of(c * tq, tq)
        ss = []
        for hh in range(nh):
            k = k_ref[0, hh, pl.ds(base, tq), :]
            s = lax.dot_general(qs[hh], k, (((1,), (1,)), ((), ())), preferred_element_type=F32)
            if masked:
                row = lax.broadcasted_iota(jnp.int32, (tq, tq), 0)
                col = lax.broadcasted_iota(jnp.int32, (tq, tq), 1)
                s = jnp.where(col <= row, s, NEG_BIG)
            ss.append(s)
        out = []
        for hh in range(nh):
            m, l, acc = carry[hh]
            v = v_ref[0, hh, pl.ds(base, tq), :]
            m_new = jnp.maximum(m, jnp.max(ss[hh], axis=-1, keepdims=True))
            alpha = jnp.exp2(m - m_new)
            pv = jnp.dot(jnp.exp2((ss[hh] - m_new).astype(BF16)), v, preferred_element_type=F32)
            out.append((m_new, alpha * l + pv[:, V_HEAD_DIM:V_HEAD_DIM + 1], alpha * acc + pv[:, :V_HEAD_DIM]))
        return tuple(out)

    init = tuple((jnp.full((tq, 1), NEG_BIG, F32), jnp.zeros((tq, 1), F32), jnp.zeros((tq, V_HEAD_DIM), F32))
                 for _ in range(nh))
    carry = lax.fori_loop(0, i, lambda c, cr: step(c, cr, False), init)
    carry = step(i, carry, True)
    for hh in range(nh):
        _, l, acc = carry[hh]
        o_ref[0, :, hh * V_HEAD_DIM:(hh + 1) * V_HEAD_DIM] = (acc / l).astype(BF16)


def _mla_attn(q, k, v, tq=512, nh=4):
    b, h, s, dk = q.shape
    dv = v.shape[-1]
    resident = pl.Buffered(1)
    return pl.pallas_call(
        functools.partial(_mla_attn_kernel, tq=tq, nh=nh),
        grid=(b, h // nh, s // tq),
        in_specs=[pl.BlockSpec((1, nh, tq, dk), lambda bi, hi, i: (bi, hi, i, 0)),
                  pl.BlockSpec((1, nh, s, dk), lambda bi, hi, i: (bi, hi, 0, 0), pipeline_mode=resident),
                  pl.BlockSpec((1, nh, s, dv), lambda bi, hi, i: (bi, hi, 0, 0), pipeline_mode=resident)],
        out_specs=pl.BlockSpec((1, tq, nh * V_HEAD_DIM), lambda bi, hi, i: (bi, i, hi)),
        out_shape=jax.ShapeDtypeStruct((b, s, h * V_HEAD_DIM), BF16),
        compiler_params=_cparams("arbitrary", "arbitrary", "arbitrary"),
        name="mla_attn",
    )(q, k, v)


def _dil_attn_kernel(q_ref, kc_ref, kp_ref, vc_ref, vp_ref, o_ref, lse_ref, *, tq):
    i = pl.program_id(2)
    sub = DIL_SPAN
    row = lax.broadcasted_iota(jnp.int32, (sub, 2 * sub), 0)
    col = lax.broadcasted_iota(jnp.int32, (sub, 2 * sub), 1)
    band = jnp.logical_and(col >= row, col <= row + sub)
    first = jnp.logical_and(band, col >= jnp.where(i > 0, 0, sub))
    lane = lax.broadcasted_iota(jnp.int32, (sub, LANES), 1)
    scale = DIL_HEAD_DIM ** -0.5
    dn = (((1,), (1,)), ((), ()))
    chains = [(j, hh) for j in range(tq // sub) for hh in range(DIL_HEADS_PER_GROUP)]

    def window(cur_ref, prev_ref, j, cs):
        if j == 0:
            return jnp.concatenate([prev_ref[0, :, cs], cur_ref[0, :sub, cs]], axis=0)
        return cur_ref[0, (j - 1) * sub:(j + 1) * sub, cs]

    scores = []
    for j, hh in chains:
        cs = slice(hh * LANES, (hh + 1) * LANES)
        s = lax.dot_general(q_ref[0, j * sub:(j + 1) * sub, cs], window(kc_ref, kp_ref, j, cs), dn,
                            preferred_element_type=F32) * scale
        scores.append(jnp.where(first if j == 0 else band, s, NEG_BIG))
    lse_blk = [jnp.zeros((sub, LANES), F32) for _ in range(tq // sub)]
    for (j, hh), s in zip(chains, scores):
        cs = slice(hh * LANES, (hh + 1) * LANES)
        m = jnp.max(s, axis=-1, keepdims=True)
        p = jnp.exp(s - m)
        l = jnp.sum(p, axis=-1, keepdims=True)
        acc = jnp.dot(p.astype(BF16), window(vc_ref, vp_ref, j, cs), preferred_element_type=F32)
        o_ref[0, j * sub:(j + 1) * sub, cs] = (acc * (1.0 / l)).astype(BF16)
        lse_blk[j] = jnp.where(lane == hh, m + jnp.log(l), lse_blk[j])
    for j in range(tq // sub):
        lse_ref[0, j * sub:(j + 1) * sub, :] = lse_blk[j]


def _dil_attn(qkv, batch, seq, dilation):
    ln = seq // dilation
    tq = min(ln, 4 * DIL_SPAN)
    gc = DIL_GROUP_COLS
    t = qkv.reshape(batch, ln, dilation * 3 * gc)
    ratio = tq // DIL_SPAN
    cur = lambda which: pl.BlockSpec((1, tq, gc), lambda b, r, i: (b, i, r * 3 + which))
    prev = lambda which: pl.BlockSpec(
        (1, DIL_SPAN, gc), lambda b, r, i: (b, jnp.maximum(i * ratio - 1, 0), r * 3 + which))
    o, lse = pl.pallas_call(
        functools.partial(_dil_attn_kernel, tq=tq),
        grid=(batch, dilation, ln // tq),
        in_specs=[cur(0), cur(1), prev(1), cur(2), prev(2)],
        out_specs=[pl.BlockSpec((1, tq, gc), lambda b, r, i: (b, i, r)),
                   pl.BlockSpec((1, tq, LANES), lambda b, r, i: (b, i, r))],
        out_shape=[jax.ShapeDtypeStruct((batch, ln, dilation * gc), BF16),
                   jax.ShapeDtypeStruct((batch, ln, dilation * LANES), F32)],
        compiler_params=_cparams("arbitrary", "arbitrary", "arbitrary"),
        name=f"dil_attn_d{dilation}",
    )(t, t, t, t, t)
    return o.reshape(batch * seq, gc), lse.reshape(batch * seq, LANES)


def _pack_halves(v):
    w = v.shape[1] // 2
    lo = lax.bitcast_convert_type(v[:, :w].astype(BF16).astype(F32), jnp.uint32)
    hi = lax.bitcast_convert_type(v[:, w:].astype(BF16).astype(F32), jnp.uint32)
    return (lo >> 16) | (hi & jnp.uint32(0xFFFF0000))


def _unpack_halves(pk):
    lo = lax.bitcast_convert_type(pk << 16, F32)
    hi = lax.bitcast_convert_type(pk & jnp.uint32(0xFFFF0000), F32)
    return lo, hi


def _merge_kernel(oa_ref, o0_ref, o1_ref, o2_ref, l0_ref, l1_ref, l2_ref, ga_ref, gb_ref, x_ref,
                  gate_ref, shift_ref, scale_ref, gpost_ref, gpre_ref,
                  wa_ref, wb_ref, wo_ref, wrh_ref, wrl_ref,
                  x1_ref, h2_ref, h2pk_ref, logit_ref):
    l0, l1, l2 = l0_ref[...], l1_ref[...], l2_ref[...]
    m = jnp.maximum(jnp.maximum(l0, l1), l2)
    e0, e1, e2 = jnp.exp(l0 - m), jnp.exp(l1 - m), jnp.exp(l2 - m)
    inv = 1.0 / (e0 + e1 + e2)
    w0, w1, w2 = e0 * inv, e1 * inv, e2 * inv
    parts = []
    for hh in range(DIL_HEADS_PER_GROUP):
        cs = slice(hh * LANES, (hh + 1) * LANES)
        parts.append(w0[:, hh:hh + 1] * o0_ref[:, cs].astype(F32)
                     + w1[:, hh:hh + 1] * o1_ref[:, cs].astype(F32)
                     + w2[:, hh:hh + 1] * o2_ref[:, cs].astype(F32))
    o_dil = jnp.concatenate(parts, axis=1).astype(BF16)
    y_a = jnp.dot(oa_ref[...], wa_ref[...], preferred_element_type=F32)
    y_b = jnp.dot(o_dil, wb_ref[...], preferred_element_type=F32)
    merged = ga_ref[...].astype(F32) * y_a + gb_ref[...].astype(F32) * y_b
    y = jnp.dot(merged.astype(BF16), wo_ref[...], preferred_element_type=F32)
    x1 = x_ref[...] + gate_ref[0] * _rms(y, gpost_ref[...])
    x1_ref[...] = x1
    h2 = _rms(x1, gpre_ref[...]) * (1.0 + scale_ref[0]) + shift_ref[0]
    h2pk_ref[...] = _pack_halves(h2)
    h2_hi = h2.astype(BF16)
    h2_ref[...] = h2_hi
    h2_lo = (h2 - h2_hi.astype(F32)).astype(BF16)
    dn = (((1,), (1,)), ((), ()))
    logit_ref[...] = (lax.dot_general(wrh_ref[...], h2_hi, dn, preferred_element_type=F32)
                      + lax.dot_general(wrh_ref[...], h2_lo, dn, preferred_element_type=F32)
                      + lax.dot_general(wrl_ref[...], h2_hi, dn, preferred_element_type=F32))


def _merge(oa, dil_o, dil_lse, gates, x2, mod3, gpost, gpre, wa, wb, wo, wr_hi, wr_lo, seq, tm=256):
    n, d = x2.shape
    per_b = seq // tm
    row = lambda w: pl.BlockSpec((tm, w), lambda i: (i, 0))
    const = lambda a: pl.BlockSpec(a.shape, lambda i: (0,) * a.ndim, pipeline_mode=pl.Buffered(1))
    modspec =lambda ch: pl.BlockSpec((1, 1, d), lambda i: ((i // per_b) * ADALN_CHUNKS + ch, 0, 0))
    return pl.pallas_call(
        _merge_kernel,
        grid=(n // tm,),
        in_specs=[row(oa.shape[1]),
                  row(DIL_GROUP_COLS), row(DIL_GROUP_COLS), row(DIL_GROUP_COLS),
                  row(LANES), row(LANES), row(LANES),
                  pl.BlockSpec((tm, d), lambda i: (i, 0)), pl.BlockSpec((tm, d), lambda i: (i, 1)),
                  row(d),
                  modspec(2), modspec(3), modspec(4),
                  const(gpost), const(gpre),
                  const(wa), const(wb), const(wo), const(wr_hi), const(wr_lo)],
        out_specs=[row(d), row(d), row(d // 2), pl.BlockSpec((LANES, tm), lambda i: (0, i))],
        out_shape=[jax.ShapeDtypeStruct((n, d), F32),
                   jax.ShapeDtypeStruct((n, d), BF16),
                   jax.ShapeDtypeStruct((n, d // 2), jnp.uint32),
                   jax.ShapeDtypeStruct((LANES, n), F32)],
        compiler_params=_cparams("arbitrary"),
        name="merge_outproj",
    )(oa, *dil_o, *dil_lse, gates, gates, x2, mod3, mod3, mod3, gpost, gpre, wa, wb, wo, wr_hi, wr_lo)


def _route_kernel(lg_ref, bias_ref, pos_ref, w_ref, meta_ref, blke_ref, cnt_sc, base_sc, *, tt):
    ps = pl.program_id(0)
    i = pl.program_id(1)
    per_group = N_EXPERTS // N_EXPERT_GROUPS
    neg_inf = -jnp.inf

    @pl.when(jnp.logical_and(ps == 0, i == 0))
    def _():
        cnt_sc[...] = jnp.zeros_like(cnt_sc)

    scores = _sigmoid(lg_ref[...])
    biased = scores + bias_ref[...]
    b3 = biased.reshape(N_EXPERT_GROUPS, per_group, tt)
    mem = lax.broadcasted_iota(jnp.int32, b3.shape, 1)
    m1 = jnp.max(b3, axis=1, keepdims=True)
    first = jnp.min(jnp.where(b3 == m1, mem, per_group), axis=1, keepdims=True)
    m2 = jnp.max(jnp.where(mem == first, neg_inf, b3), axis=1, keepdims=True)
    gs = m1 + m2
    gidx = lax.broadcasted_iota(jnp.int32, gs.shape, 0)
    grank = jnp.zeros(gs.shape, jnp.int32)
    for g2 in range(N_EXPERT_GROUPS):
        r = gs[g2:g2 + 1]
        beats = jnp.logical_or(r > gs, jnp.logical_and(r == gs, g2 < gidx))
        grank = grank + jnp.where(beats, 1, 0)
    sel = jnp.where(grank < TOPK_GROUPS, b3, neg_inf).reshape(N_EXPERTS, tt)
    eidx = lax.broadcasted_iota(jnp.int32, sel.shape, 0)
    erank = jnp.zeros(sel.shape, jnp.int32)
    for e2 in range(N_EXPERTS):
        r = sel[e2:e2 + 1, :]
        beats = jnp.logical_or(r > sel, jnp.logical_and(r == sel, e2 < eidx))
        erank = erank + jnp.where(beats, 1, 0)
    esel = erank < TOP_K
    mask_f = jnp.where(esel, 1.0, 0.0)
    tile_cnt = jnp.sum(mask_f, axis=1, keepdims=True).astype(jnp.int32)

    @pl.when(ps == 0)
    def _():
        cnt_sc[...] = cnt_sc[...] + tile_cnt

    @pl.when(jnp.logical_and(ps == 1, i == 0))
    def _():
        cnt = cnt_sc[...]
        pc = ((cnt + (MOE_BLOCK - 1)) // MOE_BLOCK) * MOE_BLOCK
        pcb = jnp.broadcast_to(pc, (N_EXPERTS, LANES))
        eid = lax.broadcasted_iota(jnp.int32, (N_EXPERTS, LANES), 0)
        pends = jnp.zeros((N_EXPERTS, LANES), jnp.int32)
        for e2 in range(N_EXPERTS):
            pends = pends + jnp.where(eid >= e2, pcb[e2:e2 + 1, :], 0)
        pst = pends - pcb
        base_sc[...] = pst[:, 0:1]
        meta_ref[0] = jnp.broadcast_to(cnt, (N_EXPERTS, LANES))
        meta_ref[1] = pst
        meta_ref[2] = pends
        nbl = blke_ref.shape[1]
        blk_start = lax.broadcasted_iota(jnp.int32, (N_EXPERTS, nbl), 1) * MOE_BLOCK
        pend_b = jnp.broadcast_to(pends[:, 0:1], (N_EXPERTS, nbl))
        be = jnp.sum(jnp.where(pend_b <= blk_start, 1, 0), axis=0, keepdims=True)
        blke_ref[...] = jnp.broadcast_to(jnp.minimum(be, N_EXPERTS - 1), blke_ref.shape)

    @pl.when(ps == 1)
    def _():
        rr = lax.broadcasted_iota(jnp.int32, (tt, tt), 0)
        cc = lax.broadcasted_iota(jnp.int32, (tt, tt), 1)
        upper = jnp.where(rr < cc, 1.0, 0.0).astype(BF16)
        prefix = jnp.dot(mask_f.astype(BF16), upper, preferred_element_type=F32)
        posd = base_sc[...] + prefix.astype(jnp.int32)
        base_sc[...] = base_sc[...] + tile_cnt
        wsel = jnp.where(esel, scores, 0.0)
        denom = jnp.sum(wsel, axis=0, keepdims=True)
        wn = wsel / (denom + 1e-20) * ROUTED_SCALE
        prow, wrow = [], []
        for kk in range(TOP_K):
            hit = erank == kk
            prow.append(jnp.sum(jnp.where(hit, posd, 0), axis=0, keepdims=True))
            wrow.append(jnp.sum(jnp.where(hit, wn, 0.0), axis=0, keepdims=True))
        pad = pos_ref.shape[0] - TOP_K
        pos_ref[...] = jnp.concatenate(prow + [jnp.zeros((pad, tt), jnp.int32)], axis=0)
        w_ref[...] = jnp.concatenate(wrow + [jnp.zeros((pad, tt), F32)], axis=0)


def _route(logits_t, bias_col, nb, tt=256):
    n = logits_t.shape[1]
    nbl = -(-nb // LANES) * LANES
    return pl.pallas_call(
        functools.partial(_route_kernel, tt=tt),
        grid=(2, n // tt),
        in_specs=[pl.BlockSpec((N_EXPERTS, tt), lambda ps, i: (0, i)),
                  pl.BlockSpec((N_EXPERTS, 1), lambda ps, i: (0, 0))],
        out_specs=[pl.BlockSpec((8, tt), lambda ps, i: (0, ps * i)),
                   pl.BlockSpec((8, tt), lambda ps, i: (0, ps * i)),
                   pl.BlockSpec((3, N_EXPERTS, LANES), lambda ps, i: (0, 0, 0)),
                   pl.BlockSpec((8, nbl), lambda ps, i: (0, 0))],
        out_shape=[jax.ShapeDtypeStruct((8, n), jnp.int32),
                   jax.ShapeDtypeStruct((8, n), F32),
                   jax.ShapeDtypeStruct((3, N_EXPERTS, LANES), jnp.int32),
                   jax.ShapeDtypeStruct((8, nbl), jnp.int32)],
        scratch_shapes=[pltpu.VMEM((N_EXPERTS, 1), jnp.int32), pltpu.VMEM((N_EXPERTS, 1), jnp.int32)],
        compiler_params=_cparams("arbitrary", "arbitrary"),
        name="moe_route",
    )(logits_t, bias_col)


def _dispatch_kernel(pos_ref, cnt_ref, pst_ref, h_ref, xs_hbm, zrow, sem, *, tm, n):
    i = pl.program_id(0)

    def body(r, carry):
        for kk in range(TOP_K):
            p = pos_ref[kk * n + i * tm + r]
            pltpu.make_async_copy(h_ref.at[pl.ds(r, 1), :], xs_hbm.at[pl.ds(p, 1), :], sem.at[0]).start()
        return carry
    lax.fori_loop(0, tm, body, 0)

    @pl.when(i == pl.num_programs(0) - 1)
    def _():
        zrow[...] = jnp.zeros_like(zrow)

        def per_expert(e, carry):
            cnt = cnt_ref[e]
            first = pst_ref[e] + cnt
            npad = ((cnt + (MOE_BLOCK - 1)) // MOE_BLOCK) * MOE_BLOCK - cnt

            def start(s, c2):
                pltpu.make_async_copy(zrow.at[pl.ds(0, 1), :], xs_hbm.at[pl.ds(first + s, 1), :], sem.at[1]).start()
                return c2

            def wait(s, c2):
                pltpu.make_async_copy(zrow.at[pl.ds(0, 1), :], xs_hbm.at[pl.ds(0, 1), :], sem.at[1]).wait()
                return c2
            lax.fori_loop(0, npad, start, 0)
            lax.fori_loop(0, npad, wait, 0)
            return carry
        lax.fori_loop(0, N_EXPERTS, per_expert, 0)

        last = N_EXPERTS - 1
        used = (pst_ref[last] + cnt_ref[last] + (MOE_BLOCK - 1)) // MOE_BLOCK

        def tail(b, carry):
            cp = pltpu.make_async_copy(zrow, xs_hbm.at[pl.ds(b * MOE_BLOCK, MOE_BLOCK), :], sem.at[1])
            cp.start()
            cp.wait()
            return carry
        lax.fori_loop(used, xs_hbm.shape[0] // MOE_BLOCK, tail, 0)

    for kk in range(TOP_K):
        pltpu.make_async_copy(h_ref, xs_hbm.at[pl.ds(0, tm), :], sem.at[0]).wait()


def _dispatch(pos_flat, cnt, pst, h2pk, nb, tm=256):
    n, w = h2pk.shape
    grid_spec = pltpu.PrefetchScalarGridSpec(
        num_scalar_prefetch=3,
        grid=(n // tm,),
        in_specs=[pl.BlockSpec((tm, w), lambda i, a, b, c: (i, 0))],
        out_specs=pl.BlockSpec(memory_space=pl.ANY),
        scratch_shapes=[pltpu.VMEM((MOE_BLOCK, w), jnp.uint32), pltpu.SemaphoreType.DMA((2,))],
    )
    return pl.pallas_call(
        functools.partial(_dispatch_kernel, tm=tm, n=n),
        grid_spec=grid_spec,
        out_shape=jax.ShapeDtypeStruct((nb * MOE_BLOCK, w), jnp.uint32),
        compiler_params=_cparams("arbitrary"),
        name="moe_dispatch",
    )(pos_flat, cnt, pst, h2pk)


def _gmm_kernel(nact_ref, blke_ref, xs_ref, wg_ref, wu_ref, wd_ref, o_ref, wgb, wub, wdb):
    i = pl.program_id(0)

    @pl.when(i < nact_ref[0])
    def _():
        changed = jnp.logical_or(i == 0, blke_ref[i] != blke_ref[jnp.maximum(i - 1, 0)])

        @pl.when(changed)
        def _():
            wgb[...] = wg_ref[0].astype(BF16)
            wub[...] = wu_ref[0].astype(BF16)
            wdb[...] = wd_ref[0].astype(BF16)

        lo, hi = _unpack_halves(xs_ref[...])
        lo, hi = lo.astype(BF16), hi.astype(BF16)
        half = lo.shape[1]
        g = (jnp.dot(lo, wgb[:half, :], preferred_element_type=F32)
             + jnp.dot(hi, wgb[half:, :], preferred_element_type=F32))
        u = (jnp.dot(lo, wub[:half, :], preferred_element_type=F32)
             + jnp.dot(hi, wub[half:, :], preferred_element_type=F32))
        a = (g * _sigmoid(g) * u).astype(BF16)
        o_ref[...] = _pack_halves(jnp.dot(a, wdb[...], preferred_element_type=F32))

    @pl.when(i >= nact_ref[0])
    def _():
        o_ref[...] = jnp.zeros_like(o_ref)


def _gmm(nact, blk_e, xs, w_gate, w_up, w_down, nb):
    d, f = w_gate.shape[1:]
    blk = lambda i, na: jnp.minimum(i, na[0] - 1)
    wspec = lambda shp: pl.BlockSpec((1,) + shp, lambda i, na, be: (be[blk(i, na)], 0, 0))
    grid_spec = pltpu.PrefetchScalarGridSpec(
        num_scalar_prefetch=2,
        grid=(nb,),
        in_specs=[pl.BlockSpec((MOE_BLOCK, d // 2), lambda i, na, be: (blk(i, na), 0)),
                  wspec((d, f)), wspec((d, f)), wspec((f, d))],
        out_specs=pl.BlockSpec((MOE_BLOCK, d // 2), lambda i, na, be: (i, 0)),
        scratch_shapes=[pltpu.VMEM((d, f), BF16), pltpu.VMEM((d, f), BF16), pltpu.VMEM((f, d), BF16)],
    )
    return pl.pallas_call(
        _gmm_kernel,
        grid_spec=grid_spec,
        out_shape=jax.ShapeDtypeStruct((nb * MOE_BLOCK, d // 2), jnp.uint32),
        compiler_params=_cparams("arbitrary"),
        name="moe_experts",
    )(nact, blk_e, xs, w_gate, w_up, w_down)


def _final_kernel(pos_ref, tw_ref, h2_ref, x1_ref, gate_ref, gpost_ref, wsg_ref, wsu_ref, wsd_ref, yb_hbm,
                  o_ref, rbuf, sem, *, tm, n):
    i = pl.program_id(0)

    def body(r, carry):
        for kk in range(TOP_K):
            p = pos_ref[kk * n + i * tm + r]
            pltpu.make_async_copy(yb_hbm.at[pl.ds(p, 1), :], rbuf.at[kk, pl.ds(r, 1), :], sem.at[0]).start()
        return carry
    lax.fori_loop(0, tm, body, 0)

    hb = h2_ref[...]
    g = jnp.dot(hb, wsg_ref[...], preferred_element_type=F32)
    u = jnp.dot(hb, wsu_ref[...], preferred_element_type=F32)
    y = jnp.dot((g * _sigmoid(g) * u).astype(BF16), wsd_ref[...], preferred_element_type=F32)

    for kk in range(TOP_K):
        pltpu.make_async_copy(yb_hbm.at[pl.ds(0, tm), :], rbuf.at[kk], sem.at[0]).wait()
    tw = tw_ref[...]
    r_lo = jnp.zeros((tm, rbuf.shape[2]), F32)
    r_hi = jnp.zeros((tm, rbuf.shape[2]), F32)
    for kk in range(TOP_K):
        lo, hi = _unpack_halves(rbuf[kk])
        wk = tw[:, kk:kk + 1]
        r_lo = r_lo + wk * lo
        r_hi = r_hi + wk * hi
    y = y + jnp.concatenate([r_lo, r_hi], axis=1)
    o_ref[...] = x1_ref[...] + gate_ref[0] * _rms(y, gpost_ref[...])


def _final(pos_flat, top_w8, h2, x1, mod3, gpost, wsg, wsu, wsd, yb, seq, tm=256):
    n, d = x1.shape
    per_b = seq // tm
    const = lambda a: pl.BlockSpec(a.shape, lambda i, ps: (0,) * a.ndim)
    row = lambda w: pl.BlockSpec((tm, w), lambda i, ps: (i, 0))
    grid_spec = pltpu.PrefetchScalarGridSpec(
        num_scalar_prefetch=1,
        grid=(n // tm,),
        in_specs=[row(top_w8.shape[1]), row(d), row(d),
                  pl.BlockSpec((1, 1, d), lambda i, ps: ((i // per_b) * ADALN_CHUNKS + 5, 0, 0)),
                  const(gpost), const(wsg), const(wsu), const(wsd),
                  pl.BlockSpec(memory_space=pl.ANY)],
        out_specs=row(d),
        scratch_shapes=[pltpu.VMEM((TOP_K, tm, d // 2), jnp.uint32), pltpu.SemaphoreType.DMA((1,))],
    )
    return pl.pallas_call(
        functools.partial(_final_kernel, tm=tm, n=n),
        grid_spec=grid_spec,
        out_shape=jax.ShapeDtypeStruct((n, d), F32),
        compiler_params=_cparams("arbitrary"),
        name="moe_combine_final",
    )(pos_flat, top_w8, h2, x1, mod3, gpost, wsg, wsu, wsd, yb)


def _rope_tables(positions, dim):
    half = dim // 2
    inv_freq = 1.0 / (ROPE_THETA ** (jnp.arange(0, dim, 2, dtype=F32) / dim))
    ang = positions.astype(F32).reshape(-1, 1) * inv_freq
    cos, sin = jnp.cos(ang), jnp.sin(ang)
    n = cos.shape[0]
    zeros = jnp.zeros((n, LANES - dim), F32)
    zh = jnp.zeros((n, half), F32)
    return cos, sin, zeros, zh


def kernel(x, c, positions, w_ada, b_ada, attn_pre_g, w_in, q_a_norm_g, w_q_up, kv_a_norm_g, w_kv_up, w_mla_o, w_dil_o, w_out, attn_post_g, ffn_pre_g, w_router, router_bias, w_exp_gate, w_exp_up, w_exp_down, w_sh_gate, w_sh_up, w_sh_down, ffn_post_g):
    batch, seq, d = x.shape
    n = batch * seq
    depth = w_ada.shape[0]

    cos, sin, zeros, zh = _rope_tables(positions, QK_ROPE_DIM)
    m_c = jnp.concatenate([cos, cos, zeros], axis=1)
    m_sf = jnp.concatenate([zh, sin, zeros], axis=1)
    m_sb = jnp.concatenate([-sin, zh, zeros], axis=1)
    cos, sin, zeros, zh = _rope_tables(positions, DIL_ROT_DIM)
    d_c = jnp.concatenate([cos, cos, jnp.ones_like(zeros)], axis=1)
    d_sf = jnp.concatenate([zh, sin, zeros], axis=1)
    d_sb = jnp.concatenate([-sin, zh, zeros], axis=1)

    x2 = x.reshape(n, d)
    c8 = jnp.pad(c, ((0, 8 - batch), (0, 0)))
    for l in range(depth):
        mod = _ada(c8, w_ada[l], b_ada[l].reshape(1, -1))
        mod3 = mod[:batch].reshape(batch * ADALN_CHUNKS, 1, d)

        wi = w_in[l]
        o_q, o_kv, o_dil, o_ga = Q_LORA_RANK, Q_LORA_RANK + KV_LORA_RANK + QK_ROPE_DIM, 0, 0
        o_dil = o_kv
        o_ga = o_dil + 3 * DIL_HEADS * DIL_HEAD_DIM
        w_a = jnp.concatenate([wi[:, :o_kv], jnp.zeros((d, LANES - QK_ROPE_DIM), F32)], axis=1).astype(BF16)
        wd3 = wi[:, o_dil:o_ga].reshape(d, 3, DIL_GROUPS, DIL_GROUP_COLS)
        w_dil = [wd3[:, :, g, :].reshape(d, 3 * DIL_GROUP_COLS).astype(BF16) for g in range(DIL_GROUPS)]
        w_g = wi[:, o_ga:].astype(BF16)
        wq3 = w_q_up[l].reshape(Q_LORA_RANK, MLA_HEADS, MLA_QK_DIM)
        wq = jnp.concatenate([wq3, jnp.zeros((Q_LORA_RANK, MLA_HEADS, MLA_QK_PAD - MLA_QK_DIM), F32)],
                             axis=2).reshape(Q_LORA_RANK, MLA_HEADS * MLA_QK_PAD).astype(BF16)
        wkv3 = w_kv_up[l].reshape(KV_LORA_RANK, MLA_HEADS, QK_NOPE_DIM + V_HEAD_DIM)
        wkv = jnp.concatenate([wkv3[:, :, :QK_NOPE_DIM].reshape(KV_LORA_RANK, -1),
                               wkv3[:, :, QK_NOPE_DIM:].reshape(KV_LORA_RANK, -1)], axis=1).astype(BF16)

        h = _prenorm(x2, attn_pre_g[l].reshape(1, d), mod3, seq)
        a = _mm(h, w_a, tn=w_a.shape[1])
        gates = _mm(h, w_g, act="sigmoid")
        q, k, v = _mlaprep(a, q_a_norm_g[l].reshape(1, -1), kv_a_norm_g[l].reshape(1, -1), wq, wkv,
                           m_c, m_sf, m_sb, batch, seq)
        o_mla = _mla_attn(q, k, v).reshape(n, MLA_HEADS * V_HEAD_DIM)
        dil_o, dil_lse = [], []
        for g, (_, dilation) in enumerate(DIL_PATTERNS):
            qkv = _dilproj(h, w_dil[g], d_c, d_sf, d_sb)
            o_g, lse_g = _dil_attn(qkv, batch, seq, dilation)
            dil_o.append(o_g)
            dil_lse.append(lse_g)

        wr = jnp.pad(w_router[l].T, ((0, LANES - N_EXPERTS), (0, 0)))
        wr_hi = wr.astype(BF16)
        wr_lo = (wr - wr_hi.astype(F32)).astype(BF16)
        x1, h2, h2pk, logits_t = _merge(o_mla, dil_o, dil_lse, gates, x2, mod3,
                                        attn_post_g[l].reshape(1, d), ffn_pre_g[l].reshape(1, d),
                                        w_mla_o[l].astype(BF16), w_dil_o[l].astype(BF16), w_out[l].astype(BF16),
                                        wr_hi, wr_lo, seq)

        nb = -(-(n * TOP_K + N_EXPERTS * (MOE_BLOCK - 1)) // MOE_BLOCK)
        pos_t, w_t, meta, blk_e = _route(logits_t, router_bias[l].astype(F32).reshape(N_EXPERTS, 1), nb)
        pos_flat = pos_t.reshape(-1)
        nact = meta[2, N_EXPERTS - 1, :1] // MOE_BLOCK
        xs = _dispatch(pos_flat, meta[0, :, 0], meta[1, :, 0], h2pk, nb)
        yb = _gmm(nact, blk_e[0], xs, w_exp_gate[l], w_exp_up[l], w_exp_down[l], nb)
        x2 = _final(pos_flat, w_t.T, h2, x1, mod3, ffn_post_g[l].reshape(1, d),
                    w_sh_gate[l].astype(BF16), w_sh_up[l].astype(BF16), w_sh_down[l].astype(BF16), yb, seq)
    return x2.reshape(batch, seq, d)
```

```python
import functools

import jax
import jax.numpy as jnp
from jax import lax
from jax.experimental import pallas as pl
from jax.experimental.pallas import tpu as pltpu

F32 = jnp.float32
BF16 = jnp.bfloat16

D_MODEL = 2048
NORM_EPS = 1e-6
ROPE_THETA = 500000.0
ADALN_CHUNKS = 6

MLA_HEADS = 8
Q_LORA_RANK = 512
KV_LORA_RANK = 512
QK_NOPE_DIM = 128
QK_ROPE_DIM = 64
V_HEAD_DIM = 128
MLA_QK_DIM = QK_NOPE_DIM + QK_ROPE_DIM
MLA_QK_PAD = 256

DIL_PATTERNS = ((128, 1), (512, 4), (2048, 16))
DIL_GROUPS = len(DIL_PATTERNS)
DIL_HEADS_PER_GROUP = 4
DIL_HEADS = DIL_GROUPS * DIL_HEADS_PER_GROUP
DIL_HEAD_DIM = 128
DIL_ROT_DIM = DIL_HEAD_DIM // 4
DIL_SPAN = 128
DIL_GROUP_COLS = DIL_HEADS_PER_GROUP * DIL_HEAD_DIM

N_EXPERTS = 64
N_EXPERT_GROUPS = 8
TOPK_GROUPS = 4
TOP_K = 6
EXPERT_DIM = 512
SHARED_DIM = 512
ROUTED_SCALE = 2.5
MOE_BLOCK = 256

LANES = 128
NEG_BIG = -1e30
LOG2_E = 1.4426950408889634
ROW_DMA_UNROLL = 4
VMEM_LIMIT = 56 * 1024 * 1024


def _cparams(*sem):
    return pltpu.CompilerParams(dimension_semantics=sem, vmem_limit_bytes=VMEM_LIMIT)


def _sigmoid(v):
    return 1.0 / (1.0 + jnp.exp(-v))


def _rms(v, g):
    ms = jnp.mean(v * v, axis=-1, keepdims=True)
    return v * lax.rsqrt(ms + NORM_EPS) * g


def _ada_kernel(c_ref, w_ref, b_ref, o_ref):
    c = c_ref[...]
    a = (c * _sigmoid(c)).astype(BF16)
    o_ref[...] = jnp.dot(a, w_ref[...].astype(BF16), preferred_element_type=F32) + b_ref[...]


def _ada(c8, w_ada, b_ada, tn=1536):
    d, n = w_ada.shape
    return pl.pallas_call(
        _ada_kernel,
        grid=(n // tn,),
        in_specs=[pl.BlockSpec((8, d), lambda j: (0, 0)),
                  pl.BlockSpec((d, tn), lambda j: (0, j)),
                  pl.BlockSpec((1, tn), lambda j: (0, j))],
        out_specs=pl.BlockSpec((8, tn), lambda j: (0, j)),
        out_shape=jax.ShapeDtypeStruct((8, n), F32),
        compiler_params=_cparams("arbitrary"),
        name="ada_mod",
    )(c8, w_ada, b_ada)


def _prenorm_kernel(x_ref, g_ref, scale_ref, shift_ref, o_ref):
    xn = _rms(x_ref[...], g_ref[...])
    o_ref[...] = (xn * (1.0 + scale_ref[0]) + shift_ref[0]).astype(BF16)


def _prenorm(x2, g, mod3, seq, tm=512):
    n, d = x2.shape
    per_b = seq // tm
    return pl.pallas_call(
        _prenorm_kernel,
        grid=(n // tm,),
        in_specs=[pl.BlockSpec((tm, d), lambda i: (i, 0)),
                  pl.BlockSpec((1, d), lambda i: (0, 0)),
                  pl.BlockSpec((1, 1, d), lambda i: ((i // per_b) * ADALN_CHUNKS + 1, 0, 0)),
                  pl.BlockSpec((1, 1, d), lambda i: ((i // per_b) * ADALN_CHUNKS + 0, 0, 0))],
        out_specs=pl.BlockSpec((tm, d), lambda i: (i, 0)),
        out_shape=jax.ShapeDtypeStruct((n, d), BF16),
        compiler_params=_cparams("arbitrary"),
        name="prenorm_attn",
    )(x2, g, mod3, mod3)


def _mm_kernel(h_ref, w_ref, o_ref, *, act):
    y = jnp.dot(h_ref[...], w_ref[...], preferred_element_type=F32)
    if act == "sigmoid":
        y = _sigmoid(y)
    o_ref[...] = y.astype(o_ref.dtype)


def _mm(h, w, act=None, tm=1024, tn=1024):
    n, k = h.shape
    cols = w.shape[1]
    tn = min(tn, cols)
    return pl.pallas_call(
        functools.partial(_mm_kernel, act=act),
        grid=(cols // tn, n // tm),
        in_specs=[pl.BlockSpec((tm, k), lambda j, i: (i, 0)),
                  pl.BlockSpec((k, tn), lambda j, i: (0, j))],
        out_specs=pl.BlockSpec((tm, tn), lambda j, i: (i, j)),
        out_shape=jax.ShapeDtypeStruct((n, cols), BF16),
        compiler_params=_cparams("arbitrary", "arbitrary"),
        name="in_proj_" + (act or "plain"),
    )(h, w)


def _rope_lanes(t, c_tab, s_fwd, s_bwd, half):
    return t * c_tab + pltpu.roll(t, half, 1) * s_fwd + pltpu.roll(t, LANES - half, 1) * s_bwd


def _dilproj_kernel(h_ref, w_ref, c_ref, sf_ref, sb_ref, o_ref):
    y = jnp.dot(h_ref[...], w_ref[...], preferred_element_type=F32)
    c_tab, s_fwd, s_bwd = c_ref[...], sf_ref[...], sb_ref[...]
    n_rot = 2 * DIL_HEADS_PER_GROUP
    for hh in range(3 * DIL_HEADS_PER_GROUP):
        t = y[:, hh * LANES:(hh + 1) * LANES]
        if hh < n_rot:
            t = _rope_lanes(t, c_tab, s_fwd, s_bwd, DIL_ROT_DIM // 2)
        o_ref[:, hh * LANES:(hh + 1) * LANES] = t.astype(BF16)


def _dilproj(h, w, c_tab, s_fwd, s_bwd, tm=1024):
    n, k = h.shape
    cols = w.shape[1]
    tab = pl.BlockSpec((tm, LANES), lambda i: (i, 0))
    return pl.pallas_call(
        _dilproj_kernel,
        grid=(n // tm,),
        in_specs=[pl.BlockSpec((tm, k), lambda i: (i, 0)),
                  pl.BlockSpec((k, cols), lambda i: (0, 0)),
                  tab, tab, tab],
        out_specs=pl.BlockSpec((tm, cols), lambda i: (i, 0)),
        out_shape=jax.ShapeDtypeStruct((n, cols), BF16),
        compiler_params=_cparams("arbitrary"),
        name="dil_proj",
    )(h, w, c_tab, s_fwd, s_bwd)


def _mlaprep_kernel(a_ref, gq_ref, gkv_ref, wq_ref, wkv_ref, c_ref, sf_ref, sb_ref,
                    q_ref, k_ref, v_ref):
    a = a_ref[...].astype(F32)
    qa = a[:, :Q_LORA_RANK]
    ckv = a[:, Q_LORA_RANK:Q_LORA_RANK + KV_LORA_RANK]
    kr = a[:, Q_LORA_RANK + KV_LORA_RANK:]
    c_tab, s_fwd, s_bwd = c_ref[...], sf_ref[...], sb_ref[...]
    half = QK_ROPE_DIM // 2
    q = jnp.dot(_rms(qa, gq_ref[...]).astype(BF16), wq_ref[...], preferred_element_type=F32)
    q = q * (MLA_QK_DIM ** -0.5 * LOG2_E)
    kv = jnp.dot(_rms(ckv, gkv_ref[...]).astype(BF16), wkv_ref[...], preferred_element_type=F32)
    k_rot = _rope_lanes(kr, c_tab, s_fwd, s_bwd, half).astype(BF16)
    lane = lax.broadcasted_iota(jnp.int32, (a.shape[0], LANES), 1)
    ones_col = jnp.where(lane == 0, 1.0, 0.0).astype(BF16)
    for hh in range(MLA_HEADS):
        base = hh * MLA_QK_PAD
        q_ref[0, hh, :, :LANES] = q[:, base:base + LANES].astype(BF16)
        q_ref[0, hh, :, LANES:] = _rope_lanes(q[:, base + LANES:base + 2 * LANES],
                                              c_tab, s_fwd, s_bwd, half).astype(BF16)
        k_ref[0, hh, :, :LANES] = kv[:, hh * LANES:(hh + 1) * LANES].astype(BF16)
        k_ref[0, hh, :, LANES:] = k_rot
        v_off = MLA_HEADS * LANES + hh * LANES
        v_ref[0, hh, :, :LANES] = kv[:, v_off:v_off + LANES].astype(BF16)
        v_ref[0, hh, :, LANES:] = ones_col


def _mlaprep(a, gq, gkv, wq, wkv, c_tab, s_fwd, s_bwd, batch, seq, tm=512):
    n, cols = a.shape
    per_b = seq // tm
    tab = pl.BlockSpec((tm, LANES), lambda i: (i, 0))
    head_major = lambda w: pl.BlockSpec((1, MLA_HEADS, tm, w), lambda i: (i // per_b, 0, i % per_b, 0))
    return pl.pallas_call(
        _mlaprep_kernel,
        grid=(n // tm,),
        in_specs=[pl.BlockSpec((tm, cols), lambda i: (i, 0)),
                  pl.BlockSpec((1, Q_LORA_RANK), lambda i: (0, 0)),
                  pl.BlockSpec((1, KV_LORA_RANK), lambda i: (0, 0)),
                  pl.BlockSpec(wq.shape, lambda i: (0, 0)),
                  pl.BlockSpec(wkv.shape, lambda i: (0, 0)),
                  tab, tab, tab],
        out_specs=[head_major(MLA_QK_PAD), head_major(MLA_QK_PAD), head_major(2 * V_HEAD_DIM)],
        out_shape=[jax.ShapeDtypeStruct((batch, MLA_HEADS, seq, MLA_QK_PAD), BF16),
                   jax.ShapeDtypeStruct((batch, MLA_HEADS, seq, MLA_QK_PAD), BF16),
                   jax.ShapeDtypeStruct((batch, MLA_HEADS, seq, 2 * V_HEAD_DIM), BF16)],
        compiler_params=_cparams("arbitrary"),
        name="mla_prep",
    )(a, gq, gkv, wq, wkv, c_tab, s_fwd, s_bwd)


def _mla_attn_kernel(q_ref, k_ref, v_ref, o_ref, *, tq, nh):
    i = pl.program_id(2)
    qs = [q_ref[0, hh] for hh in range(nh)]

    def step(c, carry, masked):
        base = pl.multiple_of(c * tq, tq)
        ss = []
        for hh in range(nh):
            k = k_ref[0, hh, pl.ds(base, tq), :]
            s = lax.dot_general(qs[hh], k, (((1,), (1,)), ((), ())), preferred_element_type=F32)
            if masked:
                row = lax.broadcasted_iota(jnp.int32, (tq, tq), 0)
                col = lax.broadcasted_iota(jnp.int32, (tq, tq), 1)
                s = jnp.where(col <= row, s, NEG_BIG)
            ss.append(s)
        out = []
        for hh in range(nh):
            m, l, acc = carry[hh]
            v = v_ref[0, hh, pl.ds(base, tq), :]
            m_new = jnp.maximum(m, jnp.max(ss[hh], axis=-1, keepdims=True))
            alpha = jnp.exp2(m - m_new)
            pv = jnp.dot(jnp.exp2((ss[hh] - m_new).astype(BF16)), v, preferred_element_type=F32)
            out.append((m_new, alpha * l + pv[:, V_HEAD_DIM:V_HEAD_DIM + 1], alpha * acc + pv[:, :V_HEAD_DIM]))
        return tuple(out)

    init = tuple((jnp.full((tq, 1), NEG_BIG, F32), jnp.zeros((tq, 1), F32), jnp.zeros((tq, V_HEAD_DIM), F32))
                 for _ in range(nh))
    carry = lax.fori_loop(0, i, lambda c, cr: step(c, cr, False), init)
    carry = step(i, carry, True)
    for hh in range(nh):
        _, l, acc = carry[hh]
        o_ref[0, :, hh * V_HEAD_DIM:(hh + 1) * V_HEAD_DIM] = (acc / l).astype(BF16)


def _mla_attn(q, k, v, tq=512, nh=4):
    b, h, s, dk = q.shape
    dv = v.shape[-1]
    resident = pl.Buffered(1)
    return pl.pallas_call(
        functools.partial(_mla_attn_kernel, tq=tq, nh=nh),
        grid=(b, h // nh, s // tq),
        in_specs=[pl.BlockSpec((1, nh, tq, dk), lambda bi, hi, i: (bi, hi, i, 0)),
                  pl.BlockSpec((1, nh, s, dk), lambda bi, hi, i: (bi, hi, 0, 0), pipeline_mode=resident),
                  pl.BlockSpec((1, nh, s, dv), lambda bi, hi, i: (bi, hi, 0, 0), pipeline_mode=resident)],
        out_specs=pl.BlockSpec((1, tq, nh * V_HEAD_DIM), lambda bi, hi, i: (bi, i, hi)),
        out_shape=jax.ShapeDtypeStruct((b, s, h * V_HEAD_DIM), BF16),
        compiler_params=_cparams("arbitrary", "arbitrary", "arbitrary"),
        name="mla_attn",
    )(q, k, v)


def _dil_attn_kernel(q_ref, kc_ref, kp_ref, vc_ref, vp_ref, o_ref, lse_ref, *, tq):
    i = pl.program_id(2)
    sub = DIL_SPAN
    row = lax.broadcasted_iota(jnp.int32, (sub, 2 * sub), 0)
    col = lax.broadcasted_iota(jnp.int32, (sub, 2 * sub), 1)
    band = jnp.logical_and(col >= row, col <= row + sub)
    first = jnp.logical_and(band, col >= jnp.where(i > 0, 0, sub))
    lane = lax.broadcasted_iota(jnp.int32, (sub, LANES), 1)
    scale = DIL_HEAD_DIM ** -0.5
    dn = (((1,), (1,)), ((), ()))
    chains = [(j, hh) for j in range(tq // sub) for hh in range(DIL_HEADS_PER_GROUP)]

    def window(cur_ref, prev_ref, j, cs):
        if j == 0:
            return jnp.concatenate([prev_ref[0, :, cs], cur_ref[0, :sub, cs]], axis=0)
        return cur_ref[0, (j - 1) * sub:(j + 1) * sub, cs]

    scores = []
    for j, hh in chains:
        cs = slice(hh * LANES, (hh + 1) * LANES)
        s = lax.dot_general(q_ref[0, j * sub:(j + 1) * sub, cs], window(kc_ref, kp_ref, j, cs), dn,
                            preferred_element_type=F32) * scale
        scores.append(jnp.where(first if j == 0 else band, s, NEG_BIG))
    lse_blk = [jnp.zeros((sub, LANES), F32) for _ in range(tq // sub)]
    for (j, hh), s in zip(chains, scores):
        cs = slice(hh * LANES, (hh + 1) * LANES)
        m = jnp.max(s, axis=-1, keepdims=True)
        p = jnp.exp(s - m)
        l = jnp.sum(p, axis=-1, keepdims=True)
        acc = jnp.dot(p.astype(BF16), window(vc_ref, vp_ref, j, cs), preferred_element_type=F32)
        o_ref[0, j * sub:(j + 1) * sub, cs] = (acc * (1.0 / l)).astype(BF16)
        lse_blk[j] = jnp.where(lane == hh, m + jnp.log(l), lse_blk[j])
    for j in range(tq // sub):
        lse_ref[0, j * sub:(j + 1) * sub, :] = lse_blk[j]


def _dil_attn(qkv, batch, seq, dilation):
    ln = seq // dilation
    tq = min(ln, 4 * DIL_SPAN)
    gc = DIL_GROUP_COLS
    t = qkv.reshape(batch, ln, dilation * 3 * gc)
    ratio = tq // DIL_SPAN
    cur = lambda which: pl.BlockSpec((1, tq, gc), lambda b, r, i: (b, i, r * 3 + which))
    prev = lambda which: pl.BlockSpec(
        (1, DIL_SPAN, gc), lambda b, r, i: (b, jnp.maximum(i * ratio - 1, 0), r * 3 + which))
    o, lse = pl.pallas_call(
        functools.partial(_dil_attn_kernel, tq=tq),
        grid=(batch, dilation, ln // tq),
        in_specs=[cur(0), cur(1), prev(1), cur(2), prev(2)],
        out_specs=[pl.BlockSpec((1, tq, gc), lambda b, r, i: (b, i, r)),
                   pl.BlockSpec((1, tq, LANES), lambda b, r, i: (b, i, r))],
        out_shape=[jax.ShapeDtypeStruct((batch, ln, dilation * gc), BF16),
                   jax.ShapeDtypeStruct((batch, ln, dilation * LANES), F32)],
        compiler_params=_cparams("arbitrary", "arbitrary", "arbitrary"),
        name=f"dil_attn_d{dilation}",
    )(t, t, t, t, t)
    return o.reshape(batch * seq, gc), lse.reshape(batch * seq, LANES)


def _pack_halves(v):
    w = v.shape[1] // 2
    lo = lax.bitcast_convert_type(v[:, :w].astype(BF16).astype(F32), jnp.uint32)
    hi = lax.bitcast_convert_type(v[:, w:].astype(BF16).astype(F32), jnp.uint32)
    return (lo >> 16) | (hi & jnp.uint32(0xFFFF0000))


def _unpack_halves(pk):
    lo = lax.bitcast_convert_type(pk << 16, F32)
    hi = lax.bitcast_convert_type(pk & jnp.uint32(0xFFFF0000), F32)
    return lo, hi


def _merge_kernel(oa_ref, o0_ref, o1_ref, o2_ref, l0_ref, l1_ref, l2_ref, ga_ref, gb_ref, x_ref,
                  gate_ref, shift_ref, scale_ref, gpost_ref, gpre_ref,
                  wa_ref, wb_ref, wo_ref, wrh_ref, wrl_ref,
                  x1_ref, h2_ref, h2pk_ref, logit_ref):
    l0, l1, l2 = l0_ref[...], l1_ref[...], l2_ref[...]
    m = jnp.maximum(jnp.maximum(l0, l1), l2)
    e0, e1, e2 = jnp.exp(l0 - m), jnp.exp(l1 - m), jnp.exp(l2 - m)
    inv = 1.0 / (e0 + e1 + e2)
    w0, w1, w2 = e0 * inv, e1 * inv, e2 * inv
    parts = []
    for hh in range(DIL_HEADS_PER_GROUP):
        cs = slice(hh * LANES, (hh + 1) * LANES)
        parts.append(w0[:, hh:hh + 1] * o0_ref[:, cs].astype(F32)
                     + w1[:, hh:hh + 1] * o1_ref[:, cs].astype(F32)
                     + w2[:, hh:hh + 1] * o2_ref[:, cs].astype(F32))
    o_dil = jnp.concatenate(parts, axis=1).astype(BF16)
    y_a = jnp.dot(oa_ref[...], wa_ref[...], preferred_element_type=F32)
    y_b = jnp.dot(o_dil, wb_ref[...], preferred_element_type=F32)
    merged = ga_ref[...].astype(F32) * y_a + gb_ref[...].astype(F32) * y_b
    y = jnp.dot(merged.astype(BF16), wo_ref[...], preferred_element_type=F32)
    x1 = x_ref[...] + gate_ref[0] * _rms(y, gpost_ref[...])
    x1_ref[...] = x1
    h2 = _rms(x1, gpre_ref[...]) * (1.0 + scale_ref[0]) + shift_ref[0]
    h2pk_ref[...] = _pack_halves(h2)
    h2_hi = h2.astype(BF16)
    h2_ref[...] = h2_hi
    h2_lo = (h2 - h2_hi.astype(F32)).astype(BF16)
    dn = (((1,), (1,)), ((), ()))
    logit_ref[...] = (lax.dot_general(wrh_ref[...], h2_hi, dn, preferred_element_type=F32)
                      + lax.dot_general(wrh_ref[...], h2_lo, dn, preferred_element_type=F32)
                      + lax.dot_general(wrl_ref[...], h2_hi, dn, preferred_element_type=F32))


def _merge(oa, dil_o, dil_lse, gates, x2, mod3, gpost, gpre, wa, wb, wo, wr_hi, wr_lo, seq, tm=256):
    n, d = x2.shape
    per_b = seq // tm
    row = lambda w: pl.BlockSpec((tm, w), lambda i: (i, 0))
    const = lambda a: pl.BlockSpec(a.shape, lambda i: (0,) * a.ndim, pipeline_mode=pl.Buffered(1))
    modspec =lambda ch: pl.BlockSpec((1, 1, d), lambda i: ((i // per_b) * ADALN_CHUNKS + ch, 0, 0))
    return pl.pallas_call(
        _merge_kernel,
        grid=(n // tm,),
        in_specs=[row(oa.shape[1]),
                  row(DIL_GROUP_COLS), row(DIL_GROUP_COLS), row(DIL_GROUP_COLS),
                  row(LANES), row(LANES), row(LANES),
                  pl.BlockSpec((tm, d), lambda i: (i, 0)), pl.BlockSpec((tm, d), lambda i: (i, 1)),
                  row(d),
                  modspec(2), modspec(3), modspec(4),
                  const(gpost), const(gpre),
                  const(wa), const(wb), const(wo), const(wr_hi), const(wr_lo)],
        out_specs=[row(d), row(d), row(d // 2), pl.BlockSpec((LANES, tm), lambda i: (0, i))],
        out_shape=[jax.ShapeDtypeStruct((n, d), F32),
                   jax.ShapeDtypeStruct((n, d), BF16),
                   jax.ShapeDtypeStruct((n, d // 2), jnp.uint32),
                   jax.ShapeDtypeStruct((LANES, n), F32)],
        compiler_params=_cparams("arbitrary"),
        name="merge_outproj",
    )(oa, *dil_o, *dil_lse, gates, gates, x2, mod3, mod3, mod3, gpost, gpre, wa, wb, wo, wr_hi, wr_lo)


def _route_kernel(lg_ref, bias_ref, pos_ref, w_ref, meta_ref, blke_ref, cnt_sc, base_sc, *, tt):
    ps = pl.program_id(0)
    i = pl.program_id(1)
    per_group = N_EXPERTS // N_EXPERT_GROUPS
    neg_inf = -jnp.inf

    @pl.when(jnp.logical_and(ps == 0, i == 0))
    def _():
        cnt_sc[...] = jnp.zeros_like(cnt_sc)

    scores = _sigmoid(lg_ref[...])
    biased = scores + bias_ref[...]
    b3 = biased.reshape(N_EXPERT_GROUPS, per_group, tt)
    mem = lax.broadcasted_iota(jnp.int32, b3.shape, 1)
    m1 = jnp.max(b3, axis=1, keepdims=True)
    first = jnp.min(jnp.where(b3 == m1, mem, per_group), axis=1, keepdims=True)
    m2 = jnp.max(jnp.where(mem == first, neg_inf, b3), axis=1, keepdims=True)
    gs = m1 + m2
    gidx = lax.broadcasted_iota(jnp.int32, gs.shape, 0)
    grank = jnp.zeros(gs.shape, jnp.int32)
    for g2 in range(N_EXPERT_GROUPS):
        r = gs[g2:g2 + 1]
        beats = jnp.logical_or(r > gs, jnp.logical_and(r == gs, g2 < gidx))
        grank = grank + jnp.where(beats, 1, 0)
    sel = jnp.where(grank < TOPK_GROUPS, b3, neg_inf).reshape(N_EXPERTS, tt)
    eidx = lax.broadcasted_iota(jnp.int32, sel.shape, 0)
    erank = jnp.zeros(sel.shape, jnp.int32)
    for e2 in range(N_EXPERTS):
        r = sel[e2:e2 + 1, :]
        beats = jnp.logical_or(r > sel, jnp.logical_and(r == sel, e2 < eidx))
        erank = erank + jnp.where(beats, 1, 0)
    esel = erank < TOP_K
    mask_f = jnp.where(esel, 1.0, 0.0)
    tile_cnt = jnp.sum(mask_f, axis=1, keepdims=True).astype(jnp.int32)

    @pl.when(ps == 0)
    def _():
        cnt_sc[...] = cnt_sc[...] + tile_cnt

    @pl.when(jnp.logical_and(ps == 1, i == 0))
    def _():
        cnt = cnt_sc[...]
        pc = ((cnt + (MOE_BLOCK - 1)) // MOE_BLOCK) * MOE_BLOCK
        pcb = jnp.broadcast_to(pc, (N_EXPERTS, LANES))
        eid = lax.broadcasted_iota(jnp.int32, (N_EXPERTS, LANES), 0)
        pends = jnp.zeros((N_EXPERTS, LANES), jnp.int32)
        for e2 in range(N_EXPERTS):
            pends = pends + jnp.where(eid >= e2, pcb[e2:e2 + 1, :], 0)
        pst = pends - pcb
        base_sc[...] = pst[:, 0:1]
        meta_ref[0] = jnp.broadcast_to(cnt, (N_EXPERTS, LANES))
        meta_ref[1] = pst
        meta_ref[2] = pends
        nbl = blke_ref.shape[1]
        blk_start = lax.broadcasted_iota(jnp.int32, (N_EXPERTS, nbl), 1) * MOE_BLOCK
        pend_b = jnp.broadcast_to(pends[:, 0:1], (N_EXPERTS, nbl))
        be = jnp.sum(jnp.where(pend_b <= blk_start, 1, 0), axis=0, keepdims=True)
        blke_ref[...] = jnp.broadcast_to(jnp.minimum(be, N_EXPERTS - 1), blke_ref.shape)

    @pl.when(ps == 1)
    def _():
        rr = lax.broadcasted_iota(jnp.int32, (tt, tt), 0)
        cc = lax.broadcasted_iota(jnp.int32, (tt, tt), 1)
        upper = jnp.where(rr < cc, 1.0, 0.0).astype(BF16)
        prefix = jnp.dot(mask_f.astype(BF16), upper, preferred_element_type=F32)
        posd = base_sc[...] + prefix.astype(jnp.int32)
        base_sc[...] = base_sc[...] + tile_cnt
        wsel = jnp.where(esel, scores, 0.0)
        denom = jnp.sum(wsel, axis=0, keepdims=True)
        wn = wsel / (denom + 1e-20) * ROUTED_SCALE
        prow, wrow = [], []
        for kk in range(TOP_K):
            hit = erank == kk
            prow.append(jnp.sum(jnp.where(hit, posd, 0), axis=0, keepdims=True))
            wrow.append(jnp.sum(jnp.where(hit, wn, 0.0), axis=0, keepdims=True))
        pad = pos_ref.shape[0] - TOP_K
        pos_ref[...] = jnp.concatenate(prow + [jnp.zeros((pad, tt), jnp.int32)], axis=0)
        w_ref[...] = jnp.concatenate(wrow + [jnp.zeros((pad, tt), F32)], axis=0)


def _route(logits_t, bias_col, nb, tt=256):
    n = logits_t.shape[1]
    nbl = -(-nb // LANES) * LANES
    return pl.pallas_call(
        functools.partial(_route_kernel, tt=tt),
        grid=(2, n // tt),
        in_specs=[pl.BlockSpec((N_EXPERTS, tt), lambda ps, i: (0, i)),
                  pl.BlockSpec((N_EXPERTS, 1), lambda ps, i: (0, 0))],
        out_specs=[pl.BlockSpec((8, tt), lambda ps, i: (0, ps * i)),
                   pl.BlockSpec((8, tt), lambda ps, i: (0, ps * i)),
                   pl.BlockSpec((3, N_EXPERTS, LANES), lambda ps, i: (0, 0, 0)),
                   pl.BlockSpec((8, nbl), lambda ps, i: (0, 0))],
        out_shape=[jax.ShapeDtypeStruct((8, n), jnp.int32),
                   jax.ShapeDtypeStruct((8, n), F32),
                   jax.ShapeDtypeStruct((3, N_EXPERTS, LANES), jnp.int32),
                   jax.ShapeDtypeStruct((8, nbl), jnp.int32)],
        scratch_shapes=[pltpu.VMEM((N_EXPERTS, 1), jnp.int32), pltpu.VMEM((N_EXPERTS, 1), jnp.int32)],
        compiler_params=_cparams("arbitrary", "arbitrary"),
        name="moe_route",
    )(logits_t, bias_col)


def _dispatch_kernel(pos_ref, cnt_ref, pst_ref, h_ref, xs_hbm, zrow, sem, *, tm, n):
    i = pl.program_id(0)

    def body(g, carry):
        for u in range(ROW_DMA_UNROLL):
            r = g * ROW_DMA_UNROLL + u
            for kk in range(TOP_K):
                p = pos_ref[kk * n + i * tm + r]
                pltpu.make_async_copy(h_ref.at[pl.ds(r, 1), :], xs_hbm.at[pl.ds(p, 1), :],
                                      sem.at[0]).start(priority=kk % 2)
        return carry
    lax.fori_loop(0, tm // ROW_DMA_UNROLL, body, 0)

    @pl.when(i == pl.num_programs(0) - 1)
    def _():
        zrow[...] = jnp.zeros_like(zrow)

        def per_expert(e, carry):
            cnt = cnt_ref[e]
            first = pst_ref[e] + cnt
            npad = ((cnt + (MOE_BLOCK - 1)) // MOE_BLOCK) * MOE_BLOCK - cnt

            def start(s, c2):
                pltpu.make_async_copy(zrow.at[pl.ds(0, 1), :], xs_hbm.at[pl.ds(first + s, 1), :], sem.at[1]).start()
                return c2

            def wait(s, c2):
                pltpu.make_async_copy(zrow.at[pl.ds(0, 1), :], xs_hbm.at[pl.ds(0, 1), :], sem.at[1]).wait()
                return c2
            lax.fori_loop(0, npad, start, 0)
            lax.fori_loop(0, npad, wait, 0)
            return carry
        lax.fori_loop(0, N_EXPERTS, per_expert, 0)

        last = N_EXPERTS - 1
        used = (pst_ref[last] + cnt_ref[last] + (MOE_BLOCK - 1)) // MOE_BLOCK

        def tail(b, carry):
            cp = pltpu.make_async_copy(zrow, xs_hbm.at[pl.ds(b * MOE_BLOCK, MOE_BLOCK), :], sem.at[1])
            cp.start()
            cp.wait()
            return carry
        lax.fori_loop(used, xs_hbm.shape[0] // MOE_BLOCK, tail, 0)

    for kk in range(TOP_K):
        pltpu.make_async_copy(h_ref, xs_hbm.at[pl.ds(0, tm), :], sem.at[0]).wait()


def _dispatch(pos_flat, cnt, pst, h2pk, nb, tm=256):
    n, w = h2pk.shape
    grid_spec = pltpu.PrefetchScalarGridSpec(
        num_scalar_prefetch=3,
        grid=(n // tm,),
        in_specs=[pl.BlockSpec((tm, w), lambda i, a, b, c: (i, 0))],
        out_specs=pl.BlockSpec(memory_space=pl.ANY),
        scratch_shapes=[pltpu.VMEM((MOE_BLOCK, w), jnp.uint32), pltpu.SemaphoreType.DMA((2,))],
    )
    return pl.pallas_call(
        functools.partial(_dispatch_kernel, tm=tm, n=n),
        grid_spec=grid_spec,
        out_shape=jax.ShapeDtypeStruct((nb * MOE_BLOCK, w), jnp.uint32),
        compiler_params=_cparams("arbitrary"),
        name="moe_dispatch",
    )(pos_flat, cnt, pst, h2pk)


def _gmm_kernel(nact_ref, blke_ref, xs_ref, wg_hbm, wu_hbm, wd_hbm, o_ref,
                wgf, wuf, wdf, wgb, wub, wdb, run_sc, sem):
    i = pl.program_id(0)
    nact = nact_ref[0]

    def fetch(e, slot):
        return (pltpu.make_async_copy(wg_hbm.at[e], wgf.at[slot], sem.at[slot, 0]),
                pltpu.make_async_copy(wu_hbm.at[e], wuf.at[slot], sem.at[slot, 1]),
                pltpu.make_async_copy(wd_hbm.at[e], wdf.at[slot], sem.at[slot, 2]))

    @pl.when(i == 0)
    def _():
        run_sc[0] = 0
        for cp in fetch(blke_ref[0], 0):
            cp.start()

    @pl.when(i < nact)
    def _():
        e = blke_ref[i]
        changed = jnp.logical_or(i == 0, e != blke_ref[jnp.maximum(i - 1, 0)])

        @pl.when(changed)
        def _():
            run = run_sc[0]
            slot = run % 2
            for cp in fetch(e, slot):
                cp.wait()
            wgb[...] = wgf[slot].astype(BF16)
            wub[...] = wuf[slot].astype(BF16)
            wdb[...] = wdf[slot].astype(BF16)
            nxt = lax.while_loop(lambda j: jnp.logical_and(j < nact, blke_ref[jnp.minimum(j, nact - 1)] == e),
                                 lambda j: j + 1, i + 1)

            @pl.when(nxt < nact)
            def _():
                for cp in fetch(blke_ref[jnp.minimum(nxt, nact - 1)], 1 - slot):
                    cp.start()
            run_sc[0] = run + 1

        lo, hi = _unpack_halves(xs_ref[...])
        lo, hi = lo.astype(BF16), hi.astype(BF16)
        half = lo.shape[1]
        g = (jnp.dot(lo, wgb[:half, :], preferred_element_type=F32)
             + jnp.dot(hi, wgb[half:, :], preferred_element_type=F32))
        u = (jnp.dot(lo, wub[:half, :], preferred_element_type=F32)
             + jnp.dot(hi, wub[half:, :], preferred_element_type=F32))
        a = (g * _sigmoid(g) * u).astype(BF16)
        o_ref[...] = _pack_halves(jnp.dot(a, wdb[...], preferred_element_type=F32))

    @pl.when(i >= nact_ref[0])
    def _():
        o_ref[...] = jnp.zeros_like(o_ref)


def _gmm(nact, blk_e, xs, w_gate, w_up, w_down, nb):
    d, f = w_gate.shape[1:]
    blk = lambda i, na: jnp.minimum(i, na[0] - 1)
    hbm = pl.BlockSpec(memory_space=pl.ANY)
    grid_spec = pltpu.PrefetchScalarGridSpec(
        num_scalar_prefetch=2,
        grid=(nb,),
        in_specs=[pl.BlockSpec((MOE_BLOCK, d // 2), lambda i, na, be: (blk(i, na), 0)), hbm, hbm, hbm],
        out_specs=pl.BlockSpec((MOE_BLOCK, d // 2), lambda i, na, be: (i, 0)),
        scratch_shapes=[pltpu.VMEM((2, d, f), F32), pltpu.VMEM((2, d, f), F32), pltpu.VMEM((2, f, d), F32),
                        pltpu.VMEM((d, f), BF16), pltpu.VMEM((d, f), BF16), pltpu.VMEM((f, d), BF16),
                        pltpu.SMEM((1,), jnp.int32), pltpu.SemaphoreType.DMA((2, 3))],
    )
    return pl.pallas_call(
        _gmm_kernel,
        grid_spec=grid_spec,
        out_shape=jax.ShapeDtypeStruct((nb * MOE_BLOCK, d // 2), jnp.uint32),
        compiler_params=_cparams("arbitrary"),
        name="moe_experts",
    )(nact, blk_e, xs, w_gate, w_up, w_down)


def _final_kernel(pos_ref, tw_ref, h2_ref, x1_ref, gate_ref, gpost_ref, wsg_ref, wsu_ref, wsd_ref, yb_hbm,
                  o_ref, rbuf, sem, *, tm, n):
    i = pl.program_id(0)

    def body(g, carry):
        for u in range(ROW_DMA_UNROLL):
            r = g * ROW_DMA_UNROLL + u
            for kk in range(TOP_K):
                p = pos_ref[kk * n + i * tm + r]
                pltpu.make_async_copy(yb_hbm.at[pl.ds(p, 1), :], rbuf.at[kk, pl.ds(r, 1), :],
                                      sem.at[0]).start(priority=kk % 2)
        return carry
    lax.fori_loop(0, tm // ROW_DMA_UNROLL, body, 0)

    hb = h2_ref[...]
    g = jnp.dot(hb, wsg_ref[...], preferred_element_type=F32)
    u = jnp.dot(hb, wsu_ref[...], preferred_element_type=F32)
    y = jnp.dot((g * _sigmoid(g) * u).astype(BF16), wsd_ref[...], preferred_element_type=F32)

    for kk in range(TOP_K):
        pltpu.make_async_copy(yb_hbm.at[pl.ds(0, tm), :], rbuf.at[kk], sem.at[0]).wait()
    tw = tw_ref[...]
    r_lo = jnp.zeros((tm, rbuf.shape[2]), F32)
    r_hi = jnp.zeros((tm, rbuf.shape[2]), F32)
    for kk in range(TOP_K):
        lo, hi = _unpack_halves(rbuf[kk])
        wk = tw[:, kk:kk + 1]
        r_lo = r_lo + wk * lo
        r_hi = r_hi + wk * hi
    y = y + jnp.concatenate([r_lo, r_hi], axis=1)
    o_ref[...] = x1_ref[...] + gate_ref[0] * _rms(y, gpost_ref[...])


def _final(pos_flat, top_w8, h2, x1, mod3, gpost, wsg, wsu, wsd, yb, seq, tm=256):
    n, d = x1.shape
    per_b = seq // tm
    const = lambda a: pl.BlockSpec(a.shape, lambda i, ps: (0,) * a.ndim)
    row = lambda w: pl.BlockSpec((tm, w), lambda i, ps: (i, 0))
    grid_spec = pltpu.PrefetchScalarGridSpec(
        num_scalar_prefetch=1,
        grid=(n // tm,),
        in_specs=[row(top_w8.shape[1]), row(d), row(d),
                  pl.BlockSpec((1, 1, d), lambda i, ps: ((i // per_b) * ADALN_CHUNKS + 5, 0, 0)),
                  const(gpost), const(wsg), const(wsu), const(wsd),
                  pl.BlockSpec(memory_space=pl.ANY)],
        out_specs=row(d),
        scratch_shapes=[pltpu.VMEM((TOP_K, tm, d // 2), jnp.uint32), pltpu.SemaphoreType.DMA((1,))],
    )
    return pl.pallas_call(
        functools.partial(_final_kernel, tm=tm, n=n),
        grid_spec=grid_spec,
        out_shape=jax.ShapeDtypeStruct((n, d), F32),
        compiler_params=_cparams("arbitrary"),
        name="moe_combine_final",
    )(pos_flat, top_w8, h2, x1, mod3, gpost, wsg, wsu, wsd, yb)


def _rope_tables(positions, dim):
    half = dim // 2
    inv_freq = 1.0 / (ROPE_THETA ** (jnp.arange(0, dim, 2, dtype=F32) / dim))
    ang = positions.astype(F32).reshape(-1, 1) * inv_freq
    cos, sin = jnp.cos(ang), jnp.sin(ang)
    n = cos.shape[0]
    zeros = jnp.zeros((n, LANES - dim), F32)
    zh = jnp.zeros((n, half), F32)
    return cos, sin, zeros, zh


def kernel(x, c, positions, w_ada, b_ada, attn_pre_g, w_in, q_a_norm_g, w_q_up, kv_a_norm_g, w_kv_up, w_mla_o, w_dil_o, w_out, attn_post_g, ffn_pre_g, w_router, router_bias, w_exp_gate, w_exp_up, w_exp_down, w_sh_gate, w_sh_up, w_sh_down, ffn_post_g):
    batch, seq, d = x.shape
    n = batch * seq
    depth = w_ada.shape[0]

    cos, sin, zeros, zh = _rope_tables(positions, QK_ROPE_DIM)
    m_c = jnp.concatenate([cos, cos, zeros], axis=1)
    m_sf = jnp.concatenate([zh, sin, zeros], axis=1)
    m_sb = jnp.concatenate([-sin, zh, zeros], axis=1)
    cos, sin, zeros, zh = _rope_tables(positions, DIL_ROT_DIM)
    d_c = jnp.concatenate([cos, cos, jnp.ones_like(zeros)], axis=1)
    d_sf = jnp.concatenate([zh, sin, zeros], axis=1)
    d_sb = jnp.concatenate([-sin, zh, zeros], axis=1)

    x2 = x.reshape(n, d)
    c8 = jnp.pad(c, ((0, 8 - batch), (0, 0)))
    for l in range(depth):
        mod = _ada(c8, w_ada[l], b_ada[l].reshape(1, -1))
        mod3 = mod[:batch].reshape(batch * ADALN_CHUNKS, 1, d)

        wi = w_in[l]
        o_q, o_kv, o_dil, o_ga = Q_LORA_RANK, Q_LORA_RANK + KV_LORA_RANK + QK_ROPE_DIM, 0, 0
        o_dil = o_kv
        o_ga = o_dil + 3 * DIL_HEADS * DIL_HEAD_DIM
        w_a = jnp.concatenate([wi[:, :o_kv], jnp.zeros((d, LANES - QK_ROPE_DIM), F32)], axis=1).astype(BF16)
        wd3 = wi[:, o_dil:o_ga].reshape(d, 3, DIL_GROUPS, DIL_GROUP_COLS)
        w_dil = [wd3[:, :, g, :].reshape(d, 3 * DIL_GROUP_COLS).astype(BF16) for g in range(DIL_GROUPS)]
        w_g = wi[:, o_ga:].astype(BF16)
        wq3 = w_q_up[l].reshape(Q_LORA_RANK, MLA_HEADS, MLA_QK_DIM)
        wq = jnp.concatenate([wq3, jnp.zeros((Q_LORA_RANK, MLA_HEADS, MLA_QK_PAD - MLA_QK_DIM), F32)],
                             axis=2).reshape(Q_LORA_RANK, MLA_HEADS * MLA_QK_PAD).astype(BF16)
        wkv3 = w_kv_up[l].reshape(KV_LORA_RANK, MLA_HEADS, QK_NOPE_DIM + V_HEAD_DIM)
        wkv = jnp.concatenate([wkv3[:, :, :QK_NOPE_DIM].reshape(KV_LORA_RANK, -1),
                               wkv3[:, :, QK_NOPE_DIM:].reshape(KV_LORA_RANK, -1)], axis=1).astype(BF16)

        h = _prenorm(x2, attn_pre_g[l].reshape(1, d), mod3, seq)
        a = _mm(h, w_a, tn=w_a.shape[1])
        gates = _mm(h, w_g, act="sigmoid")
        q, k, v = _mlaprep(a, q_a_norm_g[l].reshape(1, -1), kv_a_norm_g[l].reshape(1, -1), wq, wkv,
                           m_c, m_sf, m_sb, batch, seq)
        o_mla = _mla_attn(q, k, v).reshape(n, MLA_HEADS * V_HEAD_DIM)
        dil_o, dil_lse = [], []
        for g, (_, dilation) in enumerate(DIL_PATTERNS):
            qkv = _dilproj(h, w_dil[g], d_c, d_sf, d_sb)
            o_g, lse_g = _dil_attn(qkv, batch, seq, dilation)
            dil_o.append(o_g)
            dil_lse.append(lse_g)

        wr = jnp.pad(w_router[l].T, ((0, LANES - N_EXPERTS), (0, 0)))
        wr_hi = wr.astype(BF16)
        wr_lo = (wr - wr_hi.astype(F32)).astype(BF16)
        x1, h2, h2pk, logits_t = _merge(o_mla, dil_o, dil_lse, gates, x2, mod3,
                                        attn_post_g[l].reshape(1, d), ffn_pre_g[l].reshape(1, d),
                                        w_mla_o[l].astype(BF16), w_dil_o[l].astype(BF16), w_out[l].astype(BF16),
                                        wr_hi, wr_lo, seq)

        nb = -(-(n * TOP_K + N_EXPERTS * (MOE_BLOCK - 1)) // MOE_BLOCK)
        pos_t, w_t, meta, blk_e = _route(logits_t, router_bias[l].astype(F32).reshape(N_EXPERTS, 1), nb)
        pos_flat = pos_t.reshape(-1)
        nact = meta[2, N_EXPERTS - 1, :1] // MOE_BLOCK
        xs = _dispatch(pos_flat, meta[0, :, 0], meta[1, :, 0], h2pk, nb)
        yb = _gmm(nact, blk_e[0], xs, w_exp_gate[l], w_exp_up[l], w_exp_down[l], nb)
        x2 = _final(pos_flat, w_t.T, h2, x1, mod3, ffn_post_g[l].reshape(1, d),
                    w_sh_gate[l].astype(BF16), w_sh_up[l].astype(BF16), w_sh_down[l].astype(BF16), yb, seq)
    return x2.reshape(batch, seq, d)
```

```python
import functools

import jax
import jax.numpy as jnp
from jax import lax
from jax.experimental import pallas as pl
from jax.experimental.pallas import tpu as pltpu

F32 = jnp.float32
BF16 = jnp.bfloat16

D_MODEL = 2048
NORM_EPS = 1e-6
ROPE_THETA = 500000.0
ADALN_CHUNKS = 6

MLA_HEADS = 8
Q_LORA_RANK = 512
KV_LORA_RANK = 512
QK_NOPE_DIM = 128
QK_ROPE_DIM = 64
V_HEAD_DIM = 128
MLA_QK_DIM = QK_NOPE_DIM + QK_ROPE_DIM
MLA_QK_PAD = 256

DIL_PATTERNS = ((128, 1), (512, 4), (2048, 16))
DIL_GROUPS = len(DIL_PATTERNS)
DIL_HEADS_PER_GROUP = 4
DIL_HEADS = DIL_GROUPS * DIL_HEADS_PER_GROUP
DIL_HEAD_DIM = 128
DIL_ROT_DIM = DIL_HEAD_DIM // 4
DIL_SPAN = 128
DIL_GROUP_COLS = DIL_HEADS_PER_GROUP * DIL_HEAD_DIM

N_EXPERTS = 64
N_EXPERT_GROUPS = 8
TOPK_GROUPS = 4
TOP_K = 6
EXPERT_DIM = 512
SHARED_DIM = 512
ROUTED_SCALE = 2.5
MOE_BLOCK = 256

LANES = 128
NEG_BIG = -1e30
LOG2_E = 1.4426950408889634
ROW_DMA_UNROLL = 4
VMEM_LIMIT = 56 * 1024 * 1024


def _cparams(*sem):
    return pltpu.CompilerParams(dimension_semantics=sem, vmem_limit_bytes=VMEM_LIMIT)


def _sigmoid(v):
    return 1.0 / (1.0 + jnp.exp(-v))


def _rms(v, g):
    ms = jnp.mean(v * v, axis=-1, keepdims=True)
    return v * lax.rsqrt(ms + NORM_EPS) * g


def _ada_kernel(c_ref, w_ref, b_ref, o_ref):
    c = c_ref[...]
    a = (c * _sigmoid(c)).astype(BF16)
    o_ref[...] = jnp.dot(a, w_ref[...].astype(BF16), preferred_element_type=F32) + b_ref[...]


def _ada(c8, w_ada, b_ada, tn=1536):
    d, n = w_ada.shape
    return pl.pallas_call(
        _ada_kernel,
        grid=(n // tn,),
        in_specs=[pl.BlockSpec((8, d), lambda j: (0, 0)),
                  pl.BlockSpec((d, tn), lambda j: (0, j)),
                  pl.BlockSpec((1, tn), lambda j: (0, j))],
        out_specs=pl.BlockSpec((8, tn), lambda j: (0, j)),
        out_shape=jax.ShapeDtypeStruct((8, n), F32),
        compiler_params=_cparams("arbitrary"),
        name="ada_mod",
    )(c8, w_ada, b_ada)


def _prenorm_kernel(x_ref, g_ref, scale_ref, shift_ref, o_ref):
    xn = _rms(x_ref[...], g_ref[...])
    o_ref[...] = (xn * (1.0 + scale_ref[0]) + shift_ref[0]).astype(BF16)


def _prenorm(x2, g, mod3, seq, tm=512):
    n, d = x2.shape
    per_b = seq // tm
    return pl.pallas_call(
        _prenorm_kernel,
        grid=(n // tm,),
        in_specs=[pl.BlockSpec((tm, d), lambda i: (i, 0)),
                  pl.BlockSpec((1, d), lambda i: (0, 0)),
                  pl.BlockSpec((1, 1, d), lambda i: ((i // per_b) * ADALN_CHUNKS + 1, 0, 0)),
                  pl.BlockSpec((1, 1, d), lambda i: ((i // per_b) * ADALN_CHUNKS + 0, 0, 0))],
        out_specs=pl.BlockSpec((tm, d), lambda i: (i, 0)),
        out_shape=jax.ShapeDtypeStruct((n, d), BF16),
        compiler_params=_cparams("arbitrary"),
        name="prenorm_attn",
    )(x2, g, mod3, mod3)


def _mm_kernel(h_ref, w_ref, o_ref, *, act):
    y = jnp.dot(h_ref[...], w_ref[...], preferred_element_type=F32)
    if act == "sigmoid":
        y = _sigmoid(y)
    o_ref[...] = y.astype(o_ref.dtype)


def _mm(h, w, act=None, tm=1024, tn=1024):
    n, k = h.shape
    cols = w.shape[1]
    tn = min(tn, cols)
    return pl.pallas_call(
        functools.partial(_mm_kernel, act=act),
        grid=(cols // tn, n // tm),
        in_specs=[pl.BlockSpec((tm, k), lambda j, i: (i, 0)),
                  pl.BlockSpec((k, tn), lambda j, i: (0, j))],
        out_specs=pl.BlockSpec((tm, tn), lambda j, i: (i, j)),
        out_shape=jax.ShapeDtypeStruct((n, cols), BF16),
        compiler_params=_cparams("arbitrary", "arbitrary"),
        name="in_proj_" + (act or "plain"),
    )(h, w)


def _rope_lanes(t, c_tab, s_fwd, s_bwd, half):
    return t * c_tab + pltpu.roll(t, half, 1) * s_fwd + pltpu.roll(t, LANES - half, 1) * s_bwd


def _dilproj_kernel(h_ref, w_ref, c_ref, sf_ref, sb_ref, o_ref):
    y = jnp.dot(h_ref[...], w_ref[...], preferred_element_type=F32)
    c_tab, s_fwd, s_bwd = c_ref[...], sf_ref[...], sb_ref[...]
    n_rot = 2 * DIL_HEADS_PER_GROUP
    for hh in range(3 * DIL_HEADS_PER_GROUP):
        t = y[:, hh * LANES:(hh + 1) * LANES]
        if hh < n_rot:
            t = _rope_lanes(t, c_tab, s_fwd, s_bwd, DIL_ROT_DIM // 2)
        o_ref[:, hh * LANES:(hh + 1) * LANES] = t.astype(BF16)


def _dilproj(h, w, c_tab, s_fwd, s_bwd, tm=1024):
    n, k = h.shape
    cols = w.shape[1]
    tab = pl.BlockSpec((tm, LANES), lambda i: (i, 0))
    return pl.pallas_call(
        _dilproj_kernel,
        grid=(n // tm,),
        in_specs=[pl.BlockSpec((tm, k), lambda i: (i, 0)),
                  pl.BlockSpec((k, cols), lambda i: (0, 0)),
                  tab, tab, tab],
        out_specs=pl.BlockSpec((tm, cols), lambda i: (i, 0)),
        out_shape=jax.ShapeDtypeStruct((n, cols), BF16),
        compiler_params=_cparams("arbitrary"),
        name="dil_proj",
    )(h, w, c_tab, s_fwd, s_bwd)


def _mlaprep_kernel(a_ref, gq_ref, gkv_ref, wq_ref, wkv_ref, c_ref, sf_ref, sb_ref,
                    q_ref, k_ref, v_ref):
    a = a_ref[...].astype(F32)
    qa = a[:, :Q_LORA_RANK]
    ckv = a[:, Q_LORA_RANK:Q_LORA_RANK + KV_LORA_RANK]
    kr = a[:, Q_LORA_RANK + KV_LORA_RANK:]
    c_tab, s_fwd, s_bwd = c_ref[...], sf_ref[...], sb_ref[...]
    half = QK_ROPE_DIM // 2
    q = jnp.dot(_rms(qa, gq_ref[...]).astype(BF16), wq_ref[...], preferred_element_type=F32)
    q = q * (MLA_QK_DIM ** -0.5 * LOG2_E)
    kv = jnp.dot(_rms(ckv, gkv_ref[...]).astype(BF16), wkv_ref[...], preferred_element_type=F32)
    k_rot = _rope_lanes(kr, c_tab, s_fwd, s_bwd, half).astype(BF16)
    lane = lax.broadcasted_iota(jnp.int32, (a.shape[0], LANES), 1)
    ones_col = jnp.where(lane == 0, 1.0, 0.0).astype(BF16)
    for hh in range(MLA_HEADS):
        base = hh * MLA_QK_PAD
        q_ref[0, hh, :, :LANES] = q[:, base:base + LANES].astype(BF16)
        q_ref[0, hh, :, LANES:] = _rope_lanes(q[:, base + LANES:base + 2 * LANES],
                                              c_tab, s_fwd, s_bwd, half).astype(BF16)
        k_ref[0, hh, :, :LANES] = kv[:, hh * LANES:(hh + 1) * LANES].astype(BF16)
        k_ref[0, hh, :, LANES:] = k_rot
        v_off = MLA_HEADS * LANES + hh * LANES
        v_ref[0, hh, :, :LANES] = kv[:, v_off:v_off + LANES].astype(BF16)
        v_ref[0, hh, :, LANES:] = ones_col


def _mlaprep(a, gq, gkv, wq, wkv, c_tab, s_fwd, s_bwd, batch, seq, tm=512):
    n, cols = a.shape
    per_b = seq // tm
    tab = pl.BlockSpec((tm, LANES), lambda i: (i, 0))
    head_major = lambda w: pl.BlockSpec((1, MLA_HEADS, tm, w), lambda i: (i // per_b, 0, i % per_b, 0))
    return pl.pallas_call(
        _mlaprep_kernel,
        grid=(n // tm,),
        in_specs=[pl.BlockSpec((tm, cols), lambda i: (i, 0)),
                  pl.BlockSpec((1, Q_LORA_RANK), lambda i: (0, 0)),
                  pl.BlockSpec((1, KV_LORA_RANK), lambda i: (0, 0)),
                  pl.BlockSpec(wq.shape, lambda i: (0, 0)),
                  pl.BlockSpec(wkv.shape, lambda i: (0, 0)),
                  tab, tab, tab],
        out_specs=[head_major(MLA_QK_PAD), head_major(MLA_QK_PAD), head_major(2 * V_HEAD_DIM)],
        out_shape=[jax.ShapeDtypeStruct((batch, MLA_HEADS, seq, MLA_QK_PAD), BF16),
                   jax.ShapeDtypeStruct((batch, MLA_HEADS, seq, MLA_QK_PAD), BF16),
                   jax.ShapeDtypeStruct((batch, MLA_HEADS, seq, 2 * V_HEAD_DIM), BF16)],
        compiler_params=_cparams("arbitrary"),
        name="mla_prep",
    )(a, gq, gkv, wq, wkv, c_tab, s_fwd, s_bwd)


def _mla_attn_kernel(q_ref, k_ref, v_ref, o_ref, *, tq, nh):
    i = pl.program_id(2)
    qs = [q_ref[0, hh] for hh in range(nh)]

    def step(c, carry, masked):
        base = pl.multiple_of(c * tq, tq)
        ss = []
        for hh in range(nh):
            k = k_ref[0, hh, pl.ds(base, tq), :]
            s = lax.dot_general(qs[hh], k, (((1,), (1,)), ((), ())), preferred_element_type=F32)
            if masked:
                row = lax.broadcasted_iota(jnp.int32, (tq, tq), 0)
                col = lax.broadcasted_iota(jnp.int32, (tq, tq), 1)
                s = jnp.where(col <= row, s, NEG_BIG)
            ss.append(s)
        out = []
        for hh in range(nh):
            m, l, acc = carry[hh]
            v = v_ref[0, hh, pl.ds(base, tq), :]
            m_new = jnp.maximum(m, jnp.max(ss[hh], axis=-1, keepdims=True))
            alpha = jnp.exp2(m - m_new)
            pv = jnp.dot(jnp.exp2((ss[hh] - m_new).astype(BF16)), v, preferred_element_type=F32)
            out.append((m_new, alpha * l + pv[:, V_HEAD_DIM:V_HEAD_DIM + 1], alpha * acc + pv[:, :V_HEAD_DIM]))
        return tuple(out)

    init = tuple((jnp.full((tq, 1), NEG_BIG, F32), jnp.zeros((tq, 1), F32), jnp.zeros((tq, V_HEAD_DIM), F32))
                 for _ in range(nh))
    carry = lax.fori_loop(0, i, lambda c, cr: step(c, cr, False), init)
    carry = step(i, carry, True)
    for hh in range(nh):
        _, l, acc = carry[hh]
        o_ref[0, :, hh * V_HEAD_DIM:(hh + 1) * V_HEAD_DIM] = (acc / l).astype(BF16)


def _mla_attn(q, k, v, tq=512, nh=4):
    b, h, s, dk = q.shape
    dv = v.shape[-1]
    resident = pl.Buffered(1)
    return pl.pallas_call(
        functools.partial(_mla_attn_kernel, tq=tq, nh=nh),
        grid=(b, h // nh, s // tq),
        in_specs=[pl.BlockSpec((1, nh, tq, dk), lambda bi, hi, i: (bi, hi, i, 0)),
                  pl.BlockSpec((1, nh, s, dk), lambda bi, hi, i: (bi, hi, 0, 0), pipeline_mode=resident),
                  pl.BlockSpec((1, nh, s, dv), lambda bi, hi, i: (bi, hi, 0, 0), pipeline_mode=resident)],
        out_specs=pl.BlockSpec((1, tq, nh * V_HEAD_DIM), lambda bi, hi, i: (bi, i, hi)),
        out_shape=jax.ShapeDtypeStruct((b, s, h * V_HEAD_DIM), BF16),
        compiler_params=_cparams("arbitrary", "arbitrary", "arbitrary"),
        name="mla_attn",
    )(q, k, v)


def _dil_attn_kernel(q_ref, kc_ref, kp_ref, vc_ref, vp_ref, o_ref, lse_ref, *, tq):
    i = pl.program_id(2)
    sub = DIL_SPAN
    row = lax.broadcasted_iota(jnp.int32, (sub, 2 * sub), 0)
    col = lax.broadcasted_iota(jnp.int32, (sub, 2 * sub), 1)
    band = jnp.logical_and(col >= row, col <= row + sub)
    first = jnp.logical_and(band, col >= jnp.where(i > 0, 0, sub))
    lane = lax.broadcasted_iota(jnp.int32, (sub, LANES), 1)
    scale = DIL_HEAD_DIM ** -0.5
    dn = (((1,), (1,)), ((), ()))
    chains = [(j, hh) for j in range(tq // sub) for hh in range(DIL_HEADS_PER_GROUP)]

    def window(cur_ref, prev_ref, j, cs):
        if j == 0:
            return jnp.concatenate([prev_ref[0, :, cs], cur_ref[0, :sub, cs]], axis=0)
        return cur_ref[0, (j - 1) * sub:(j + 1) * sub, cs]

    scores = []
    for j, hh in chains:
        cs = slice(hh * LANES, (hh + 1) * LANES)
        s = lax.dot_general(q_ref[0, j * sub:(j + 1) * sub, cs], window(kc_ref, kp_ref, j, cs), dn,
                            preferred_element_type=F32) * scale
        scores.append(jnp.where(first if j == 0 else band, s, NEG_BIG))
    lse_blk = [jnp.zeros((sub, LANES), F32) for _ in range(tq // sub)]
    for (j, hh), s in zip(chains, scores):
        cs = slice(hh * LANES, (hh + 1) * LANES)
        m = jnp.max(s, axis=-1, keepdims=True)
        p = jnp.exp(s - m)
        l = jnp.sum(p, axis=-1, keepdims=True)
        acc = jnp.dot(p.astype(BF16), window(vc_ref, vp_ref, j, cs), preferred_element_type=F32)
        o_ref[0, j * sub:(j + 1) * sub, cs] = (acc * (1.0 / l)).astype(BF16)
        lse_blk[j] = jnp.where(lane == hh, m + jnp.log(l), lse_blk[j])
    for j in range(tq // sub):
        lse_ref[0, j * sub:(j + 1) * sub, :] = lse_blk[j]


def _dil_attn(qkv, batch, seq, dilation):
    ln = seq // dilation
    tq = min(ln, 4 * DIL_SPAN)
    gc = DIL_GROUP_COLS
    t = qkv.reshape(batch, ln, dilation * 3 * gc)
    ratio = tq // DIL_SPAN
    cur = lambda which: pl.BlockSpec((1, tq, gc), lambda b, r, i: (b, i, r * 3 + which))
    prev = lambda which: pl.BlockSpec(
        (1, DIL_SPAN, gc), lambda b, r, i: (b, jnp.maximum(i * ratio - 1, 0), r * 3 + which))
    o, lse = pl.pallas_call(
        functools.partial(_dil_attn_kernel, tq=tq),
        grid=(batch, dilation, ln // tq),
        in_specs=[cur(0), cur(1), prev(1), cur(2), prev(2)],
        out_specs=[pl.BlockSpec((1, tq, gc), lambda b, r, i: (b, i, r)),
                   pl.BlockSpec((1, tq, LANES), lambda b, r, i: (b, i, r))],
        out_shape=[jax.ShapeDtypeStruct((batch, ln, dilation * gc), BF16),
                   jax.ShapeDtypeStruct((batch, ln, dilation * LANES), F32)],
        compiler_params=_cparams("arbitrary", "arbitrary", "arbitrary"),
        name=f"dil_attn_d{dilation}",
    )(t, t, t, t, t)
    return o.reshape(batch * seq, gc), lse.reshape(batch * seq, LANES)


def _pack_halves(v):
    w = v.shape[1] // 2
    lo = lax.bitcast_convert_type(v[:, :w].astype(BF16).astype(F32), jnp.uint32)
    hi = lax.bitcast_convert_type(v[:, w:].astype(BF16).astype(F32), jnp.uint32)
    return (lo >> 16) | (hi & jnp.uint32(0xFFFF0000))


ROW_TILE = 8


def _row_tile(p):
    return (pl.ds(pl.multiple_of(p * ROW_TILE, ROW_TILE), ROW_TILE), slice(None))


def _store_row_tiles(ref, pk):
    rows = pk.shape[0]
    for c in range(ROW_TILE):
        ref[pl.ds(c, rows, stride=ROW_TILE), :] = pk[:, c * LANES:(c + 1) * LANES]


def _load_row_tiles(ref, rows):
    return jnp.concatenate([ref[pl.ds(c, rows, stride=ROW_TILE), :] for c in range(ROW_TILE)], axis=1)


def _unpack_halves(pk):
    lo = lax.bitcast_convert_type(pk << 16, F32)
    hi = lax.bitcast_convert_type(pk & jnp.uint32(0xFFFF0000), F32)
    return lo, hi


def _merge_kernel(oa_ref, o0_ref, o1_ref, o2_ref, l0_ref, l1_ref, l2_ref, ga_ref, gb_ref, x_ref,
                  gate_ref, shift_ref, scale_ref, gpost_ref, gpre_ref,
                  wa_ref, wb_ref, wo_ref, wrh_ref, wrl_ref,
                  x1_ref, h2_ref, h2pk_ref, logit_ref):
    l0, l1, l2 = l0_ref[...], l1_ref[...], l2_ref[...]
    m = jnp.maximum(jnp.maximum(l0, l1), l2)
    e0, e1, e2 = jnp.exp(l0 - m), jnp.exp(l1 - m), jnp.exp(l2 - m)
    inv = 1.0 / (e0 + e1 + e2)
    w0, w1, w2 = e0 * inv, e1 * inv, e2 * inv
    parts = []
    for hh in range(DIL_HEADS_PER_GROUP):
        cs = slice(hh * LANES, (hh + 1) * LANES)
        parts.append(w0[:, hh:hh + 1] * o0_ref[:, cs].astype(F32)
                     + w1[:, hh:hh + 1] * o1_ref[:, cs].astype(F32)
                     + w2[:, hh:hh + 1] * o2_ref[:, cs].astype(F32))
    o_dil = jnp.concatenate(parts, axis=1).astype(BF16)
    y_a = jnp.dot(oa_ref[...], wa_ref[...], preferred_element_type=F32)
    y_b = jnp.dot(o_dil, wb_ref[...], preferred_element_type=F32)
    merged = ga_ref[...].astype(F32) * y_a + gb_ref[...].astype(F32) * y_b
    y = jnp.dot(merged.astype(BF16), wo_ref[...], preferred_element_type=F32)
    x1 = x_ref[...] + gate_ref[0] * _rms(y, gpost_ref[...])
    x1_ref[...] = x1
    h2 = _rms(x1, gpre_ref[...]) * (1.0 + scale_ref[0]) + shift_ref[0]
    _store_row_tiles(h2pk_ref, _pack_halves(h2))
    h2_hi = h2.astype(BF16)
    h2_ref[...] = h2_hi
    h2_lo = (h2 - h2_hi.astype(F32)).astype(BF16)
    dn = (((1,), (1,)), ((), ()))
    logit_ref[...] = (lax.dot_general(wrh_ref[...], h2_hi, dn, preferred_element_type=F32)
                      + lax.dot_general(wrh_ref[...], h2_lo, dn, preferred_element_type=F32)
                      + lax.dot_general(wrl_ref[...], h2_hi, dn, preferred_element_type=F32))


def _merge(oa, dil_o, dil_lse, gates, x2, mod3, gpost, gpre, wa, wb, wo, wr_hi, wr_lo, seq, tm=256):
    n, d = x2.shape
    per_b = seq // tm
    row = lambda w: pl.BlockSpec((tm, w), lambda i: (i, 0))
    const = lambda a: pl.BlockSpec(a.shape, lambda i: (0,) * a.ndim, pipeline_mode=pl.Buffered(1))
    modspec =lambda ch: pl.BlockSpec((1, 1, d), lambda i: ((i // per_b) * ADALN_CHUNKS + ch, 0, 0))
    return pl.pallas_call(
        _merge_kernel,
        grid=(n // tm,),
        in_specs=[row(oa.shape[1]),
                  row(DIL_GROUP_COLS), row(DIL_GROUP_COLS), row(DIL_GROUP_COLS),
                  row(LANES), row(LANES), row(LANES),
                  pl.BlockSpec((tm, d), lambda i: (i, 0)), pl.BlockSpec((tm, d), lambda i: (i, 1)),
                  row(d),
                  modspec(2), modspec(3), modspec(4),
                  const(gpost), const(gpre),
                  const(wa), const(wb), const(wo), const(wr_hi), const(wr_lo)],
        out_specs=[row(d), row(d), pl.BlockSpec((tm * ROW_TILE, LANES), lambda i: (i, 0)),
                   pl.BlockSpec((LANES, tm), lambda i: (0, i))],
        out_shape=[jax.ShapeDtypeStruct((n, d), F32),
                   jax.ShapeDtypeStruct((n, d), BF16),
                   jax.ShapeDtypeStruct((n * ROW_TILE, LANES), jnp.uint32),
                   jax.ShapeDtypeStruct((LANES, n), F32)],
        compiler_params=_cparams("arbitrary"),
        name="merge_outproj",
    )(oa, *dil_o, *dil_lse, gates, gates, x2, mod3, mod3, mod3, gpost, gpre, wa, wb, wo, wr_hi, wr_lo)


def _route_kernel(lg_ref, bias_ref, pos_ref, w_ref, meta_ref, blke_ref, cnt_sc, base_sc, *, tt):
    ps = pl.program_id(0)
    i = pl.program_id(1)
    per_group = N_EXPERTS // N_EXPERT_GROUPS
    neg_inf = -jnp.inf

    @pl.when(jnp.logical_and(ps == 0, i == 0))
    def _():
        cnt_sc[...] = jnp.zeros_like(cnt_sc)

    scores = _sigmoid(lg_ref[...])
    biased = scores + bias_ref[...]
    b3 = biased.reshape(N_EXPERT_GROUPS, per_group, tt)
    mem = lax.broadcasted_iota(jnp.int32, b3.shape, 1)
    m1 = jnp.max(b3, axis=1, keepdims=True)
    first = jnp.min(jnp.where(b3 == m1, mem, per_group), axis=1, keepdims=True)
    m2 = jnp.max(jnp.where(mem == first, neg_inf, b3), axis=1, keepdims=True)
    gs = m1 + m2
    gidx = lax.broadcasted_iota(jnp.int32, gs.shape, 0)
    grank = jnp.zeros(gs.shape, jnp.int32)
    for g2 in range(N_EXPERT_GROUPS):
        r = gs[g2:g2 + 1]
        beats = jnp.logical_or(r > gs, jnp.logical_and(r == gs, g2 < gidx))
        grank = grank + jnp.where(beats, 1, 0)
    sel = jnp.where(grank < TOPK_GROUPS, b3, neg_inf).reshape(N_EXPERTS, tt)
    eidx = lax.broadcasted_iota(jnp.int32, sel.shape, 0)
    erank = jnp.zeros(sel.shape, jnp.int32)
    for e2 in range(N_EXPERTS):
        r = sel[e2:e2 + 1, :]
        beats = jnp.logical_or(r > sel, jnp.logical_and(r == sel, e2 < eidx))
        erank = erank + jnp.where(beats, 1, 0)
    esel = erank < TOP_K
    mask_f = jnp.where(esel, 1.0, 0.0)
    tile_cnt = jnp.sum(mask_f, axis=1, keepdims=True).astype(jnp.int32)

    @pl.when(ps == 0)
    def _():
        cnt_sc[...] = cnt_sc[...] + tile_cnt

    @pl.when(jnp.logical_and(ps == 1, i == 0))
    def _():
        cnt = cnt_sc[...]
        pc = ((cnt + (MOE_BLOCK - 1)) // MOE_BLOCK) * MOE_BLOCK
        pcb = jnp.broadcast_to(pc, (N_EXPERTS, LANES))
        eid = lax.broadcasted_iota(jnp.int32, (N_EXPERTS, LANES), 0)
        pends = jnp.zeros((N_EXPERTS, LANES), jnp.int32)
        for e2 in range(N_EXPERTS):
            pends = pends + jnp.where(eid >= e2, pcb[e2:e2 + 1, :], 0)
        pst = pends - pcb
        base_sc[...] = pst[:, 0:1]
        meta_ref[0] = jnp.broadcast_to(cnt, (N_EXPERTS, LANES))
        meta_ref[1] = pst
        meta_ref[2] = pends
        nbl = blke_ref.shape[1]
        blk_start = lax.broadcasted_iota(jnp.int32, (N_EXPERTS, nbl), 1) * MOE_BLOCK
        pend_b = jnp.broadcast_to(pends[:, 0:1], (N_EXPERTS, nbl))
        be = jnp.sum(jnp.where(pend_b <= blk_start, 1, 0), axis=0, keepdims=True)
        blke_ref[...] = jnp.broadcast_to(jnp.minimum(be, N_EXPERTS - 1), blke_ref.shape)

    @pl.when(ps == 1)
    def _():
        rr = lax.broadcasted_iota(jnp.int32, (tt, tt), 0)
        cc = lax.broadcasted_iota(jnp.int32, (tt, tt), 1)
        upper = jnp.where(rr < cc, 1.0, 0.0).astype(BF16)
        prefix = jnp.dot(mask_f.astype(BF16), upper, preferred_element_type=F32)
        posd = base_sc[...] + prefix.astype(jnp.int32)
        base_sc[...] = base_sc[...] + tile_cnt
        wsel = jnp.where(esel, scores, 0.0)
        denom = jnp.sum(wsel, axis=0, keepdims=True)
        wn = wsel / (denom + 1e-20) * ROUTED_SCALE
        prow, wrow = [], []
        for kk in range(TOP_K):
            hit = erank == kk
            prow.append(jnp.sum(jnp.where(hit, posd, 0), axis=0, keepdims=True))
            wrow.append(jnp.sum(jnp.where(hit, wn, 0.0), axis=0, keepdims=True))
        pad = pos_ref.shape[0] - TOP_K
        pos_ref[...] = jnp.concatenate(prow + [jnp.zeros((pad, tt), jnp.int32)], axis=0)
        w_ref[...] = jnp.concatenate(wrow + [jnp.zeros((pad, tt), F32)], axis=0)


def _route(logits_t, bias_col, nb, tt=256):
    n = logits_t.shape[1]
    nbl = -(-nb // LANES) * LANES
    return pl.pallas_call(
        functools.partial(_route_kernel, tt=tt),
        grid=(2, n // tt),
        in_specs=[pl.BlockSpec((N_EXPERTS, tt), lambda ps, i: (0, i)),
                  pl.BlockSpec((N_EXPERTS, 1), lambda ps, i: (0, 0))],
        out_specs=[pl.BlockSpec((8, tt), lambda ps, i: (0, ps * i)),
                   pl.BlockSpec((8, tt), lambda ps, i: (0, ps * i)),
                   pl.BlockSpec((3, N_EXPERTS, LANES), lambda ps, i: (0, 0, 0)),
                   pl.BlockSpec((8, nbl), lambda ps, i: (0, 0))],
        out_shape=[jax.ShapeDtypeStruct((8, n), jnp.int32),
                   jax.ShapeDtypeStruct((8, n), F32),
                   jax.ShapeDtypeStruct((3, N_EXPERTS, LANES), jnp.int32),
                   jax.ShapeDtypeStruct((8, nbl), jnp.int32)],
        scratch_shapes=[pltpu.VMEM((N_EXPERTS, 1), jnp.int32), pltpu.VMEM((N_EXPERTS, 1), jnp.int32)],
        compiler_params=_cparams("arbitrary", "arbitrary"),
        name="moe_route",
    )(logits_t, bias_col)


def _dispatch_kernel(pos_ref, cnt_ref, pst_ref, h_ref, xs_hbm, zrow, sem, *, tm, n):
    i = pl.program_id(0)

    def body(g, carry):
        for u in range(ROW_DMA_UNROLL):
            r = g * ROW_DMA_UNROLL + u
            for kk in range(TOP_K):
                p = pos_ref[kk * n + i * tm + r]
                pltpu.make_async_copy(h_ref.at[_row_tile(r)], xs_hbm.at[_row_tile(p)],
                                      sem.at[0]).start(priority=kk % 2)
        return carry
    lax.fori_loop(0, tm // ROW_DMA_UNROLL, body, 0)

    @pl.when(i == pl.num_programs(0) - 1)
    def _():
        zrow[...] = jnp.zeros_like(zrow)

        def per_expert(e, carry):
            cnt = cnt_ref[e]
            first = pst_ref[e] + cnt
            npad = ((cnt + (MOE_BLOCK - 1)) // MOE_BLOCK) * MOE_BLOCK - cnt

            def start(s, c2):
                pltpu.make_async_copy(zrow.at[_row_tile(0)], xs_hbm.at[_row_tile(first + s)], sem.at[1]).start()
                return c2

            def wait(s, c2):
                pltpu.make_async_copy(zrow.at[_row_tile(0)], xs_hbm.at[_row_tile(0)], sem.at[1]).wait()
                return c2
            lax.fori_loop(0, npad, start, 0)
            lax.fori_loop(0, npad, wait, 0)
            return carry
        lax.fori_loop(0, N_EXPERTS, per_expert, 0)

        last = N_EXPERTS - 1
        used = (pst_ref[last] + cnt_ref[last] + (MOE_BLOCK - 1)) // MOE_BLOCK
        blk_rows = MOE_BLOCK * ROW_TILE

        def tail(b, carry):
            cp = pltpu.make_async_copy(zrow, xs_hbm.at[pl.ds(pl.multiple_of(b * blk_rows, blk_rows), blk_rows), :],
                                       sem.at[1])
            cp.start()
            cp.wait()
            return carry
        lax.fori_loop(used, xs_hbm.shape[0] // blk_rows, tail, 0)

    for kk in range(TOP_K):
        pltpu.make_async_copy(h_ref, xs_hbm.at[pl.ds(0, tm * ROW_TILE), :], sem.at[0]).wait()


def _dispatch(pos_flat, cnt, pst, h2pk, nb, tm=256):
    n = h2pk.shape[0] // ROW_TILE
    grid_spec = pltpu.PrefetchScalarGridSpec(
        num_scalar_prefetch=3,
        grid=(n // tm,),
        in_specs=[pl.BlockSpec((tm * ROW_TILE, LANES), lambda i, a, b, c: (i, 0))],
        out_specs=pl.BlockSpec(memory_space=pl.ANY),
        scratch_shapes=[pltpu.VMEM((MOE_BLOCK * ROW_TILE, LANES), jnp.uint32), pltpu.SemaphoreType.DMA((2,))],
    )
    return pl.pallas_call(
        functools.partial(_dispatch_kernel, tm=tm, n=n),
        grid_spec=grid_spec,
        out_shape=jax.ShapeDtypeStruct((nb * MOE_BLOCK * ROW_TILE, LANES), jnp.uint32),
        compiler_params=_cparams("arbitrary"),
        name="moe_dispatch",
    )(pos_flat, cnt, pst, h2pk)


def _gmm_kernel(nact_ref, blke_ref, xs_ref, wg_hbm, wu_hbm, wd_hbm, o_ref,
                wgf, wuf, wdf, wgb, wub, wdb, run_sc, sem):
    i = pl.program_id(0)
    nact = nact_ref[0]

    def fetch(e, slot):
        return (pltpu.make_async_copy(wg_hbm.at[e], wgf.at[slot], sem.at[slot, 0]),
                pltpu.make_async_copy(wu_hbm.at[e], wuf.at[slot], sem.at[slot, 1]),
                pltpu.make_async_copy(wd_hbm.at[e], wdf.at[slot], sem.at[slot, 2]))

    @pl.when(i == 0)
    def _():
        run_sc[0] = 0
        for cp in fetch(blke_ref[0], 0):
            cp.start()

    @pl.when(i < nact)
    def _():
        e = blke_ref[i]
        changed = jnp.logical_or(i == 0, e != blke_ref[jnp.maximum(i - 1, 0)])

        @pl.when(changed)
        def _():
            run = run_sc[0]
            slot = run % 2
            for cp in fetch(e, slot):
                cp.wait()
            wgb[...] = wgf[slot].astype(BF16)
            wub[...] = wuf[slot].astype(BF16)
            wdb[...] = wdf[slot].astype(BF16)
            nxt = lax.while_loop(lambda j: jnp.logical_and(j < nact, blke_ref[jnp.minimum(j, nact - 1)] == e),
                                 lambda j: j + 1, i + 1)

            @pl.when(nxt < nact)
            def _():
                for cp in fetch(blke_ref[jnp.minimum(nxt, nact - 1)], 1 - slot):
                    cp.start()
            run_sc[0] = run + 1

        lo, hi = _unpack_halves(_load_row_tiles(xs_ref, MOE_BLOCK))
        lo, hi = lo.astype(BF16), hi.astype(BF16)
        half = lo.shape[1]
        g = (jnp.dot(lo, wgb[:half, :], preferred_element_type=F32)
             + jnp.dot(hi, wgb[half:, :], preferred_element_type=F32))
        u = (jnp.dot(lo, wub[:half, :], preferred_element_type=F32)
             + jnp.dot(hi, wub[half:, :], preferred_element_type=F32))
        a = (g * _sigmoid(g) * u).astype(BF16)
        _store_row_tiles(o_ref, _pack_halves(jnp.dot(a, wdb[...], preferred_element_type=F32)))

    @pl.when(i >= nact_ref[0])
    def _():
        o_ref[...] = jnp.zeros_like(o_ref)


def _gmm(nact, blk_e, xs, w_gate, w_up, w_down, nb):
    d, f = w_gate.shape[1:]
    blk = lambda i, na: jnp.minimum(i, na[0] - 1)
    hbm = pl.BlockSpec(memory_space=pl.ANY)
    grid_spec = pltpu.PrefetchScalarGridSpec(
        num_scalar_prefetch=2,
        grid=(nb,),
        in_specs=[pl.BlockSpec((MOE_BLOCK * ROW_TILE, LANES), lambda i, na, be: (blk(i, na), 0)), hbm, hbm, hbm],
        out_specs=pl.BlockSpec((MOE_BLOCK * ROW_TILE, LANES), lambda i, na, be: (i, 0)),
        scratch_shapes=[pltpu.VMEM((2, d, f), F32), pltpu.VMEM((2, d, f), F32), pltpu.VMEM((2, f, d), F32),
                        pltpu.VMEM((d, f), BF16), pltpu.VMEM((d, f), BF16), pltpu.VMEM((f, d), BF16),
                        pltpu.SMEM((1,), jnp.int32), pltpu.SemaphoreType.DMA((2, 3))],
    )
    return pl.pallas_call(
        _gmm_kernel,
        grid_spec=grid_spec,
        out_shape=jax.ShapeDtypeStruct((nb * MOE_BLOCK * ROW_TILE, LANES), jnp.uint32),
        compiler_params=_cparams("arbitrary"),
        name="moe_experts",
    )(nact, blk_e, xs, w_gate, w_up, w_down)


def _final_kernel(pos_ref, tw_ref, h2_ref, x1_ref, gate_ref, gpost_ref, wsg_ref, wsu_ref, wsd_ref, yb_hbm,
                  o_ref, rbuf, sem, *, tm, n):
    i = pl.program_id(0)
    slot = i % 2

    def gather(tile, sl):
        def body(g, carry):
            for u in range(ROW_DMA_UNROLL):
                r = g * ROW_DMA_UNROLL + u
                for kk in range(TOP_K):
                    p = pos_ref[kk * n + tile * tm + r]
                    pltpu.make_async_copy(yb_hbm.at[_row_tile(p)], rbuf.at[sl, kk].at[_row_tile(r)],
                                          sem.at[sl]).start(priority=kk % 2)
            return carry
        lax.fori_loop(0, tm // ROW_DMA_UNROLL, body, 0)

    @pl.when(i == 0)
    def _():
        gather(0, 0)

    @pl.when(i + 1 < pl.num_programs(0))
    def _():
        gather(i + 1, 1 - slot)

    hb = h2_ref[...]
    g = jnp.dot(hb, wsg_ref[...], preferred_element_type=F32)
    u = jnp.dot(hb, wsu_ref[...], preferred_element_type=F32)
    y = jnp.dot((g * _sigmoid(g) * u).astype(BF16), wsd_ref[...], preferred_element_type=F32)

    for kk in range(TOP_K):
        pltpu.make_async_copy(yb_hbm.at[pl.ds(0, tm * ROW_TILE), :], rbuf.at[slot, kk], sem.at[slot]).wait()
    tw = tw_ref[...]
    half = ROW_TILE * LANES
    r_lo = jnp.zeros((tm, half), F32)
    r_hi = jnp.zeros((tm, half), F32)
    for kk in range(TOP_K):
        lo, hi = _unpack_halves(_load_row_tiles(rbuf.at[slot, kk], tm))
        wk = tw[:, kk:kk + 1]
        r_lo = r_lo + wk * lo
        r_hi = r_hi + wk * hi
    y = y + jnp.concatenate([r_lo, r_hi], axis=1)
    o_ref[...] = x1_ref[...] + gate_ref[0] * _rms(y, gpost_ref[...])


def _final(pos_flat, top_w8, h2, x1, mod3, gpost, wsg, wsu, wsd, yb, seq, tm=256):
    n, d = x1.shape
    per_b = seq // tm
    const = lambda a: pl.BlockSpec(a.shape, lambda i, ps: (0,) * a.ndim)
    row = lambda w: pl.BlockSpec((tm, w), lambda i, ps: (i, 0))
    grid_spec = pltpu.PrefetchScalarGridSpec(
        num_scalar_prefetch=1,
        grid=(n // tm,),
        in_specs=[row(top_w8.shape[1]), row(d), row(d),
                  pl.BlockSpec((1, 1, d), lambda i, ps: ((i // per_b) * ADALN_CHUNKS + 5, 0, 0)),
                  const(gpost), const(wsg), const(wsu), const(wsd),
                  pl.BlockSpec(memory_space=pl.ANY)],
        out_specs=row(d),
        scratch_shapes=[pltpu.VMEM((2, TOP_K, tm * ROW_TILE, LANES), jnp.uint32), pltpu.SemaphoreType.DMA((2,))],
    )
    return pl.pallas_call(
        functools.partial(_final_kernel, tm=tm, n=n),
        grid_spec=grid_spec,
        out_shape=jax.ShapeDtypeStruct((n, d), F32),
        compiler_params=_cparams("arbitrary"),
        name="moe_combine_final",
    )(pos_flat, top_w8, h2, x1, mod3, gpost, wsg, wsu, wsd, yb)


def _rope_tables(positions, dim):
    half = dim // 2
    inv_freq = 1.0 / (ROPE_THETA ** (jnp.arange(0, dim, 2, dtype=F32) / dim))
    ang = positions.astype(F32).reshape(-1, 1) * inv_freq
    cos, sin = jnp.cos(ang), jnp.sin(ang)
    n = cos.shape[0]
    zeros = jnp.zeros((n, LANES - dim), F32)
    zh = jnp.zeros((n, half), F32)
    return cos, sin, zeros, zh


def kernel(x, c, positions, w_ada, b_ada, attn_pre_g, w_in, q_a_norm_g, w_q_up, kv_a_norm_g, w_kv_up, w_mla_o, w_dil_o, w_out, attn_post_g, ffn_pre_g, w_router, router_bias, w_exp_gate, w_exp_up, w_exp_down, w_sh_gate, w_sh_up, w_sh_down, ffn_post_g):
    batch, seq, d = x.shape
    n = batch * seq
    depth = w_ada.shape[0]

    cos, sin, zeros, zh = _rope_tables(positions, QK_ROPE_DIM)
    m_c = jnp.concatenate([cos, cos, zeros], axis=1)
    m_sf = jnp.concatenate([zh, sin, zeros], axis=1)
    m_sb = jnp.concatenate([-sin, zh, zeros], axis=1)
    cos, sin, zeros, zh = _rope_tables(positions, DIL_ROT_DIM)
    d_c = jnp.concatenate([cos, cos, jnp.ones_like(zeros)], axis=1)
    d_sf = jnp.concatenate([zh, sin, zeros], axis=1)
    d_sb = jnp.concatenate([-sin, zh, zeros], axis=1)

    x2 = x.reshape(n, d)
    c8 = jnp.pad(c, ((0, 8 - batch), (0, 0)))
    for l in range(depth):
        mod = _ada(c8, w_ada[l], b_ada[l].reshape(1, -1))
        mod3 = mod[:batch].reshape(batch * ADALN_CHUNKS, 1, d)

        wi = w_in[l]
        o_q, o_kv, o_dil, o_ga = Q_LORA_RANK, Q_LORA_RANK + KV_LORA_RANK + QK_ROPE_DIM, 0, 0
        o_dil = o_kv
        o_ga = o_dil + 3 * DIL_HEADS * DIL_HEAD_DIM
        w_a = jnp.concatenate([wi[:, :o_kv], jnp.zeros((d, LANES - QK_ROPE_DIM), F32)], axis=1).astype(BF16)
        wd3 = wi[:, o_dil:o_ga].reshape(d, 3, DIL_GROUPS, DIL_GROUP_COLS)
        w_dil = [wd3[:, :, g, :].reshape(d, 3 * DIL_GROUP_COLS).astype(BF16) for g in range(DIL_GROUPS)]
        w_g = wi[:, o_ga:].astype(BF16)
        wq3 = w_q_up[l].reshape(Q_LORA_RANK, MLA_HEADS, MLA_QK_DIM)
        wq = jnp.concatenate([wq3, jnp.zeros((Q_LORA_RANK, MLA_HEADS, MLA_QK_PAD - MLA_QK_DIM), F32)],
                             axis=2).reshape(Q_LORA_RANK, MLA_HEADS * MLA_QK_PAD).astype(BF16)
        wkv3 = w_kv_up[l].reshape(KV_LORA_RANK, MLA_HEADS, QK_NOPE_DIM + V_HEAD_DIM)
        wkv = jnp.concatenate([wkv3[:, :, :QK_NOPE_DIM].reshape(KV_LORA_RANK, -1),
                               wkv3[:, :, QK_NOPE_DIM:].reshape(KV_LORA_RANK, -1)], axis=1).astype(BF16)

        h = _prenorm(x2, attn_pre_g[l].reshape(1, d), mod3, seq)
        a = _mm(h, w_a, tn=w_a.shape[1])
        gates = _mm(h, w_g, act="sigmoid")
        q, k, v = _mlaprep(a, q_a_norm_g[l].reshape(1, -1), kv_a_norm_g[l].reshape(1, -1), wq, wkv,
                           m_c, m_sf, m_sb, batch, seq)
        o_mla = _mla_attn(q, k, v).reshape(n, MLA_HEADS * V_HEAD_DIM)
        dil_o, dil_lse = [], []
        for g, (_, dilation) in enumerate(DIL_PATTERNS):
            qkv = _dilproj(h, w_dil[g], d_c, d_sf, d_sb)
            o_g, lse_g = _dil_attn(qkv, batch, seq, dilation)
            dil_o.append(o_g)
            dil_lse.append(lse_g)

        wr = jnp.pad(w_router[l].T, ((0, LANES - N_EXPERTS), (0, 0)))
        wr_hi = wr.astype(BF16)
        wr_lo = (wr - wr_hi.astype(F32)).astype(BF16)
        x1, h2, h2pk, logits_t = _merge(o_mla, dil_o, dil_lse, gates, x2, mod3,
                                        attn_post_g[l].reshape(1, d), ffn_pre_g[l].reshape(1, d),
                                        w_mla_o[l].astype(BF16), w_dil_o[l].astype(BF16), w_out[l].astype(BF16),
                                        wr_hi, wr_lo, seq)

        nb = -(-(n * TOP_K + N_EXPERTS * (MOE_BLOCK - 1)) // MOE_BLOCK)
        pos_t, w_t, meta, blk_e = _route(logits_t, router_bias[l].astype(F32).reshape(N_EXPERTS, 1), nb)
        pos_flat = pos_t.reshape(-1)
        nact = meta[2, N_EXPERTS - 1, :1] // MOE_BLOCK
        xs = _dispatch(pos_flat, meta[0, :, 0], meta[1, :, 0], h2pk, nb)
        yb = _gmm(nact, blk_e[0], xs, w_exp_gate[l], w_exp_up[l], w_exp_down[l], nb)
        x2 = _final(pos_flat, w_t.T, h2, x1, mod3, ffn_post_g[l].reshape(1, d),
                    w_sh_gate[l].astype(BF16), w_sh_up[l].astype(BF16), w_sh_down[l].astype(BF16), yb, seq)
    return x2.reshape(batch, seq, d)
```

```python
import functools

import jax
import jax.numpy as jnp
from jax import lax
from jax.experimental import pallas as pl
from jax.experimental.pallas import tpu as pltpu

F32 = jnp.float32
BF16 = jnp.bfloat16

D_MODEL = 2048
NORM_EPS = 1e-6
ROPE_THETA = 500000.0
ADALN_CHUNKS = 6

MLA_HEADS = 8
Q_LORA_RANK = 512
KV_LORA_RANK = 512
QK_NOPE_DIM = 128
QK_ROPE_DIM = 64
V_HEAD_DIM = 128
MLA_QK_DIM = QK_NOPE_DIM + QK_ROPE_DIM
MLA_QK_PAD = 256

DIL_PATTERNS = ((128, 1), (512, 4), (2048, 16))
DIL_GROUPS = len(DIL_PATTERNS)
DIL_HEADS_PER_GROUP = 4
DIL_HEADS = DIL_GROUPS * DIL_HEADS_PER_GROUP
DIL_HEAD_DIM = 128
DIL_ROT_DIM = DIL_HEAD_DIM // 4
DIL_SPAN = 128
DIL_GROUP_COLS = DIL_HEADS_PER_GROUP * DIL_HEAD_DIM

N_EXPERTS = 64
N_EXPERT_GROUPS = 8
TOPK_GROUPS = 4
TOP_K = 6
EXPERT_DIM = 512
SHARED_DIM = 512
ROUTED_SCALE = 2.5
MOE_BLOCK = 256

LANES = 128
NEG_BIG = -1e30
LOG2_E = 1.4426950408889634
ROW_DMA_UNROLL = 4
VMEM_LIMIT = 56 * 1024 * 1024


def _cparams(*sem):
    return pltpu.CompilerParams(dimension_semantics=sem, vmem_limit_bytes=VMEM_LIMIT)


def _sigmoid(v):
    return 1.0 / (1.0 + jnp.exp(-v))


def _rms(v, g):
    ms = jnp.mean(v * v, axis=-1, keepdims=True)
    return v * lax.rsqrt(ms + NORM_EPS) * g


def _ada_kernel(c_ref, w_ref, b_ref, o_ref):
    c = c_ref[...]
    a = (c * _sigmoid(c)).astype(BF16)
    o_ref[...] = jnp.dot(a, w_ref[...].astype(BF16), preferred_element_type=F32) + b_ref[...]


def _ada(c8, w_ada, b_ada, tn=1536):
    d, n = w_ada.shape
    return pl.pallas_call(
        _ada_kernel,
        grid=(n // tn,),
        in_specs=[pl.BlockSpec((8, d), lambda j: (0, 0)),
                  pl.BlockSpec((d, tn), lambda j: (0, j)),
                  pl.BlockSpec((1, tn), lambda j: (0, j))],
        out_specs=pl.BlockSpec((8, tn), lambda j: (0, j)),
        out_shape=jax.ShapeDtypeStruct((8, n), F32),
        compiler_params=_cparams("arbitrary"),
        name="ada_mod",
    )(c8, w_ada, b_ada)


def _prenorm_kernel(x_ref, g_ref, scale_ref, shift_ref, o_ref):
    xn = _rms(x_ref[...], g_ref[...])
    o_ref[...] = (xn * (1.0 + scale_ref[0]) + shift_ref[0]).astype(BF16)


def _prenorm(x2, g, mod3, seq, tm=512):
    n, d = x2.shape
    per_b = seq // tm
    return pl.pallas_call(
        _prenorm_kernel,
        grid=(n // tm,),
        in_specs=[pl.BlockSpec((tm, d), lambda i: (i, 0)),
                  pl.BlockSpec((1, d), lambda i: (0, 0)),
                  pl.BlockSpec((1, 1, d), lambda i: ((i // per_b) * ADALN_CHUNKS + 1, 0, 0)),
                  pl.BlockSpec((1, 1, d), lambda i: ((i // per_b) * ADALN_CHUNKS + 0, 0, 0))],
        out_specs=pl.BlockSpec((tm, d), lambda i: (i, 0)),
        out_shape=jax.ShapeDtypeStruct((n, d), BF16),
        compiler_params=_cparams("arbitrary"),
        name="prenorm_attn",
    )(x2, g, mod3, mod3)


def _mm_kernel(h_ref, w_ref, o_ref, *, act):
    y = jnp.dot(h_ref[...], w_ref[...], preferred_element_type=F32)
    if act == "sigmoid":
        y = _sigmoid(y)
    o_ref[...] = y.astype(o_ref.dtype)


def _mm(h, w, act=None, tm=1024, tn=1024):
    n, k = h.shape
    cols = w.shape[1]
    tn = min(tn, cols)
    return pl.pallas_call(
        functools.partial(_mm_kernel, act=act),
        grid=(cols // tn, n // tm),
        in_specs=[pl.BlockSpec((tm, k), lambda j, i: (i, 0)),
                  pl.BlockSpec((k, tn), lambda j, i: (0, j))],
        out_specs=pl.BlockSpec((tm, tn), lambda j, i: (i, j)),
        out_shape=jax.ShapeDtypeStruct((n, cols), BF16),
        compiler_params=_cparams("arbitrary", "arbitrary"),
        name="in_proj_" + (act or "plain"),
    )(h, w)


def _rope_lanes(t, c_tab, s_fwd, s_bwd, half):
    return t * c_tab + pltpu.roll(t, half, 1) * s_fwd + pltpu.roll(t, LANES - half, 1) * s_bwd


def _dilproj_kernel(h_ref, w_ref, c_ref, sf_ref, sb_ref, o_ref, y_sc, *, dilation):
    y = jnp.dot(h_ref[...], w_ref[...], preferred_element_type=F32)
    c_tab, s_fwd, s_bwd = c_ref[...], sf_ref[...], sb_ref[...]
    cols = w_ref.shape[1]
    rows = h_ref.shape[0] // dilation
    n_rot = 2 * DIL_HEADS_PER_GROUP
    for hh in range(3 * DIL_HEADS_PER_GROUP):
        t = y[:, hh * LANES:(hh + 1) * LANES]
        if hh < n_rot:
            t = _rope_lanes(t, c_tab, s_fwd, s_bwd, DIL_ROT_DIM // 2)
        if dilation == 1:
            o_ref[0, :, hh * LANES:(hh + 1) * LANES] = t.astype(BF16)
        else:
            y_sc[hh] = t
    if dilation > 1:
        for r in range(dilation):
            for hh in range(3 * DIL_HEADS_PER_GROUP):
                c0 = r * cols + hh * LANES
                o_ref[0, :, c0:c0 + LANES] = y_sc.at[hh][pl.ds(r, rows, stride=dilation), :].astype(BF16)


def _dilproj(h, w, c_tab, s_fwd, s_bwd, batch, seq, dilation, tm=1024):
    n, k = h.shape
    cols = w.shape[1]
    per_b = seq // tm
    tab = pl.BlockSpec((tm, LANES), lambda i: (i, 0))
    return pl.pallas_call(
        functools.partial(_dilproj_kernel, dilation=dilation),
        grid=(n // tm,),
        in_specs=[pl.BlockSpec((tm, k), lambda i: (i, 0)),
                  pl.BlockSpec((k, cols), lambda i: (0, 0)),
                  tab, tab, tab],
        out_specs=pl.BlockSpec((1, tm // dilation, dilation * cols), lambda i: (i // per_b, i % per_b, 0)),
        out_shape=jax.ShapeDtypeStruct((batch, seq // dilation, dilation * cols), BF16),
        scratch_shapes=[pltpu.VMEM((cols // LANES, tm, LANES), F32)],
        compiler_params=_cparams("arbitrary"),
        name="dil_proj",
    )(h, w, c_tab, s_fwd, s_bwd)


def _mlaprep_kernel(a_ref, gq_ref, gkv_ref, wq_ref, wkv_ref, c_ref, sf_ref, sb_ref,
                    q_ref, k_ref, v_ref):
    a = a_ref[...].astype(F32)
    qa = a[:, :Q_LORA_RANK]
    ckv = a[:, Q_LORA_RANK:Q_LORA_RANK + KV_LORA_RANK]
    kr = a[:, Q_LORA_RANK + KV_LORA_RANK:]
    c_tab, s_fwd, s_bwd = c_ref[...], sf_ref[...], sb_ref[...]
    half = QK_ROPE_DIM // 2
    q = jnp.dot(_rms(qa, gq_ref[...]).astype(BF16), wq_ref[...], preferred_element_type=F32)
    q = q * (MLA_QK_DIM ** -0.5 * LOG2_E)
    kv = jnp.dot(_rms(ckv, gkv_ref[...]).astype(BF16), wkv_ref[...], preferred_element_type=F32)
    k_rot = _rope_lanes(kr, c_tab, s_fwd, s_bwd, half).astype(BF16)
    lane = lax.broadcasted_iota(jnp.int32, (a.shape[0], LANES), 1)
    ones_col = jnp.where(lane == 0, 1.0, 0.0).astype(BF16)
    for hh in range(MLA_HEADS):
        base = hh * MLA_QK_PAD
        q_ref[0, hh, :, :LANES] = q[:, base:base + LANES].astype(BF16)
        q_ref[0, hh, :, LANES:] = _rope_lanes(q[:, base + LANES:base + 2 * LANES],
                                              c_tab, s_fwd, s_bwd, half).astype(BF16)
        k_ref[0, hh, :, :LANES] = kv[:, hh * LANES:(hh + 1) * LANES].astype(BF16)
        k_ref[0, hh, :, LANES:] = k_rot
        v_off = MLA_HEADS * LANES + hh * LANES
        v_ref[0, hh, :, :LANES] = kv[:, v_off:v_off + LANES].astype(BF16)
        v_ref[0, hh, :, LANES:] = ones_col


def _mlaprep(a, gq, gkv, wq, wkv, c_tab, s_fwd, s_bwd, batch, seq, tm=512):
    n, cols = a.shape
    per_b = seq // tm
    tab = pl.BlockSpec((tm, LANES), lambda i: (i, 0))
    head_major = lambda w: pl.BlockSpec((1, MLA_HEADS, tm, w), lambda i: (i // per_b, 0, i % per_b, 0))
    return pl.pallas_call(
        _mlaprep_kernel,
        grid=(n // tm,),
        in_specs=[pl.BlockSpec((tm, cols), lambda i: (i, 0)),
                  pl.BlockSpec((1, Q_LORA_RANK), lambda i: (0, 0)),
                  pl.BlockSpec((1, KV_LORA_RANK), lambda i: (0, 0)),
                  pl.BlockSpec(wq.shape, lambda i: (0, 0)),
                  pl.BlockSpec(wkv.shape, lambda i: (0, 0)),
                  tab, tab, tab],
        out_specs=[head_major(MLA_QK_PAD), head_major(MLA_QK_PAD), head_major(2 * V_HEAD_DIM)],
        out_shape=[jax.ShapeDtypeStruct((batch, MLA_HEADS, seq, MLA_QK_PAD), BF16),
                   jax.ShapeDtypeStruct((batch, MLA_HEADS, seq, MLA_QK_PAD), BF16),
                   jax.ShapeDtypeStruct((batch, MLA_HEADS, seq, 2 * V_HEAD_DIM), BF16)],
        compiler_params=_cparams("arbitrary"),
        name="mla_prep",
    )(a, gq, gkv, wq, wkv, c_tab, s_fwd, s_bwd)


def _mla_attn_kernel(q_ref, k_ref, v_ref, o_ref, *, tq, nh):
    i = pl.program_id(2)
    qs = [q_ref[0, hh] for hh in range(nh)]

    def step(c, carry, masked):
        base = pl.multiple_of(c * tq, tq)
        ss = []
        for hh in range(nh):
            k = k_ref[0, hh, pl.ds(base, tq), :]
            s = lax.dot_general(qs[hh], k, (((1,), (1,)), ((), ())), preferred_element_type=F32)
            if masked:
                row = lax.broadcasted_iota(jnp.int32, (tq, tq), 0)
                col = lax.broadcasted_iota(jnp.int32, (tq, tq), 1)
                s = jnp.where(col <= row, s, NEG_BIG)
            ss.append(s)
        out = []
        for hh in range(nh):
            m, l, acc = carry[hh]
            v = v_ref[0, hh, pl.ds(base, tq), :]
            m_new = jnp.maximum(m, jnp.max(ss[hh], axis=-1, keepdims=True))
            alpha = jnp.exp2(m - m_new)
            pv = jnp.dot(jnp.exp2((ss[hh] - m_new).astype(BF16)), v, preferred_element_type=F32)
            out.append((m_new, alpha * l + pv[:, V_HEAD_DIM:V_HEAD_DIM + 1], alpha * acc + pv[:, :V_HEAD_DIM]))
        return tuple(out)

    init = tuple((jnp.full((tq, 1), NEG_BIG, F32), jnp.zeros((tq, 1), F32), jnp.zeros((tq, V_HEAD_DIM), F32))
                 for _ in range(nh))
    carry = lax.fori_loop(0, i, lambda c, cr: step(c, cr, False), init)
    carry = step(i, carry, True)
    for hh in range(nh):
        _, l, acc = carry[hh]
        o_ref[0, :, hh * V_HEAD_DIM:(hh + 1) * V_HEAD_DIM] = (acc / l).astype(BF16)


def _mla_attn(q, k, v, tq=512, nh=4):
    b, h, s, dk = q.shape
    dv = v.shape[-1]
    resident = pl.Buffered(1)
    return pl.pallas_call(
        functools.partial(_mla_attn_kernel, tq=tq, nh=nh),
        grid=(b, h // nh, s // tq),
        in_specs=[pl.BlockSpec((1, nh, tq, dk), lambda bi, hi, i: (bi, hi, i, 0)),
                  pl.BlockSpec((1, nh, s, dk), lambda bi, hi, i: (bi, hi, 0, 0), pipeline_mode=resident),
                  pl.BlockSpec((1, nh, s, dv), lambda bi, hi, i: (bi, hi, 0, 0), pipeline_mode=resident)],
        out_specs=pl.BlockSpec((1, tq, nh * V_HEAD_DIM), lambda bi, hi, i: (bi, i, hi)),
        out_shape=jax.ShapeDtypeStruct((b, s, h * V_HEAD_DIM), BF16),
        compiler_params=_cparams("arbitrary", "arbitrary", "arbitrary"),
        name="mla_attn",
    )(q, k, v)


def _dil_attn_kernel(q_ref, kc_ref, kp_ref, vc_ref, vp_ref, o_ref, lse_ref, *, tq):
    i = pl.program_id(2)
    sub = DIL_SPAN
    row = lax.broadcasted_iota(jnp.int32, (sub, 2 * sub), 0)
    col = lax.broadcasted_iota(jnp.int32, (sub, 2 * sub), 1)
    band = jnp.logical_and(col >= row, col <= row + sub)
    first = jnp.logical_and(band, col >= jnp.where(i > 0, 0, sub))
    lane = lax.broadcasted_iota(jnp.int32, (sub, LANES), 1)
    scale = DIL_HEAD_DIM ** -0.5
    dn = (((1,), (1,)), ((), ()))
    chains = [(j, hh) for j in range(tq // sub) for hh in range(DIL_HEADS_PER_GROUP)]

    def window(cur_ref, prev_ref, j, cs):
        if j == 0:
            return jnp.concatenate([prev_ref[0, :, cs], cur_ref[0, :sub, cs]], axis=0)
        return cur_ref[0, (j - 1) * sub:(j + 1) * sub, cs]

    scores = []
    for j, hh in chains:
        cs = slice(hh * LANES, (hh + 1) * LANES)
        s = lax.dot_general(q_ref[0, j * sub:(j + 1) * sub, cs], window(kc_ref, kp_ref, j, cs), dn,
                            preferred_element_type=F32) * scale
        scores.append(jnp.where(first if j == 0 else band, s, NEG_BIG))
    lse_blk = [jnp.zeros((sub, LANES), F32) for _ in range(tq // sub)]
    for (j, hh), s in zip(chains, scores):
        cs = slice(hh * LANES, (hh + 1) * LANES)
        m = jnp.max(s, axis=-1, keepdims=True)
        p = jnp.exp(s - m)
        l = jnp.sum(p, axis=-1, keepdims=True)
        acc = jnp.dot(p.astype(BF16), window(vc_ref, vp_ref, j, cs), preferred_element_type=F32)
        o_ref[0, j * sub:(j + 1) * sub, cs] = (acc * (1.0 / l)).astype(BF16)
        lse_blk[j] = jnp.where(lane == hh, m + jnp.log(l), lse_blk[j])
    for j in range(tq // sub):
        lse_ref[0, j * sub:(j + 1) * sub, :] = lse_blk[j]


def _dil_attn(t, batch, seq, dilation):
    ln = seq // dilation
    tq = min(ln, 4 * DIL_SPAN)
    gc = DIL_GROUP_COLS
    ratio = tq // DIL_SPAN
    cur = lambda which: pl.BlockSpec((1, tq, gc), lambda b, r, i: (b, i, r * 3 + which))
    prev = lambda which: pl.BlockSpec(
        (1, DIL_SPAN, gc), lambda b, r, i: (b, jnp.maximum(i * ratio - 1, 0), r * 3 + which))
    o, lse = pl.pallas_call(
        functools.partial(_dil_attn_kernel, tq=tq),
        grid=(batch, dilation, ln // tq),
        in_specs=[cur(0), cur(1), prev(1), cur(2), prev(2)],
        out_specs=[pl.BlockSpec((1, tq, gc), lambda b, r, i: (b, i, r)),
                   pl.BlockSpec((1, tq, LANES), lambda b, r, i: (b, i, r))],
        out_shape=[jax.ShapeDtypeStruct((batch, ln, dilation * gc), BF16),
                   jax.ShapeDtypeStruct((batch, ln, dilation * LANES), F32)],
        compiler_params=_cparams("arbitrary", "arbitrary", "arbitrary"),
        name=f"dil_attn_d{dilation}",
    )(t, t, t, t, t)
    return o, lse


def _pack_halves(v):
    w = v.shape[1] // 2
    lo = lax.bitcast_convert_type(v[:, :w].astype(BF16).astype(F32), jnp.uint32)
    hi = lax.bitcast_convert_type(v[:, w:].astype(BF16).astype(F32), jnp.uint32)
    return (lo >> 16) | (hi & jnp.uint32(0xFFFF0000))


ROW_TILE = 8


def _row_tile(p):
    return (pl.ds(pl.multiple_of(p * ROW_TILE, ROW_TILE), ROW_TILE), slice(None))


def _store_row_tiles(ref, pk):
    rows = pk.shape[0]
    for c in range(ROW_TILE):
        ref[pl.ds(c, rows, stride=ROW_TILE), :] = pk[:, c * LANES:(c + 1) * LANES]


def _load_row_tiles(ref, rows):
    return jnp.concatenate([ref[pl.ds(c, rows, stride=ROW_TILE), :] for c in range(ROW_TILE)], axis=1)


def _unpack_halves(pk):
    lo = lax.bitcast_convert_type(pk << 16, F32)
    hi = lax.bitcast_convert_type(pk & jnp.uint32(0xFFFF0000), F32)
    return lo, hi


def _merge_kernel(oa_ref, o0_ref, o1_ref, o2_ref, l0_ref, l1_ref, l2_ref, ga_ref, gb_ref, x_ref,
                  gate_ref, shift_ref, scale_ref, gpost_ref, gpre_ref,
                  wa_ref, wb_ref, wo_ref, wrh_ref, wrl_ref,
                  x1_ref, h2_ref, h2pk_ref, logit_ref, o_sc, l_sc):
    tm = x_ref.shape[0]

    def natural(ref, sc, gi, chunks):
        dil = DIL_PATTERNS[gi][1]
        if dil == 1:
            return [ref[0, :, c * LANES:(c + 1) * LANES].astype(F32) for c in range(chunks)]
        for r in range(dil):
            for c in range(chunks):
                c0 = (r * chunks + c) * LANES
                sc.at[gi, c][pl.ds(r, tm // dil, stride=dil), :] = ref[0, :, c0:c0 + LANES].astype(F32)
        return [sc[gi, c] for c in range(chunks)]

    (l0,), (l1,), (l2,) = [natural(ref, l_sc, gi, 1) for gi, ref in enumerate((l0_ref, l1_ref, l2_ref))]
    o0, o1, o2 = [natural(ref, o_sc, gi, DIL_HEADS_PER_GROUP) for gi, ref in enumerate((o0_ref, o1_ref, o2_ref))]
    m = jnp.maximum(jnp.maximum(l0, l1), l2)
    e0, e1, e2 = jnp.exp(l0 - m), jnp.exp(l1 - m), jnp.exp(l2 - m)
    inv = 1.0 / (e0 + e1 + e2)
    w0, w1, w2 = e0 * inv, e1 * inv, e2 * inv
    parts = []
    for hh in range(DIL_HEADS_PER_GROUP):
        parts.append(w0[:, hh:hh + 1] * o0[hh] + w1[:, hh:hh + 1] * o1[hh] + w2[:, hh:hh + 1] * o2[hh])
    o_dil = jnp.concatenate(parts, axis=1).astype(BF16)
    y_a = jnp.dot(oa_ref[...], wa_ref[...], preferred_element_type=F32)
    y_b = jnp.dot(o_dil, wb_ref[...], preferred_element_type=F32)
    merged = ga_ref[...].astype(F32) * y_a + gb_ref[...].astype(F32) * y_b
    y = jnp.dot(merged.astype(BF16), wo_ref[...], preferred_element_type=F32)
    x1 = x_ref[...] + gate_ref[0] * _rms(y, gpost_ref[...])
    x1_ref[...] = x1
    h2 = _rms(x1, gpre_ref[...]) * (1.0 + scale_ref[0]) + shift_ref[0]
    _store_row_tiles(h2pk_ref, _pack_halves(h2))
    h2_hi = h2.astype(BF16)
    h2_ref[...] = h2_hi
    h2_lo = (h2 - h2_hi.astype(F32)).astype(BF16)
    dn = (((1,), (1,)), ((), ()))
    logit_ref[...] = (lax.dot_general(wrh_ref[...], h2_hi, dn, preferred_element_type=F32)
                      + lax.dot_general(wrh_ref[...], h2_lo, dn, preferred_element_type=F32)
                      + lax.dot_general(wrl_ref[...], h2_hi, dn, preferred_element_type=F32))


def _merge(oa, dil_o, dil_lse, gates, x2, mod3, gpost, gpre, wa, wb, wo, wr_hi, wr_lo, seq, tm=256):
    n, d = x2.shape
    per_b = seq // tm
    row = lambda w: pl.BlockSpec((tm, w), lambda i: (i, 0))
    const = lambda a: pl.BlockSpec(a.shape, lambda i: (0,) * a.ndim, pipeline_mode=pl.Buffered(1))
    modspec = lambda ch: pl.BlockSpec((1, 1, d), lambda i: ((i // per_b) * ADALN_CHUNKS + ch, 0, 0))
    strided = lambda gi, w: pl.BlockSpec((1, tm // DIL_PATTERNS[gi][1], DIL_PATTERNS[gi][1] * w),
                                         lambda i: (i // per_b, i % per_b, 0))
    return pl.pallas_call(
        _merge_kernel,
        grid=(n // tm,),
        in_specs=[row(oa.shape[1]),
                  strided(0, DIL_GROUP_COLS), strided(1, DIL_GROUP_COLS), strided(2, DIL_GROUP_COLS),
                  strided(0, LANES), strided(1, LANES), strided(2, LANES),
                  pl.BlockSpec((tm, d), lambda i: (i, 0)), pl.BlockSpec((tm, d), lambda i: (i, 1)),
                  row(d),
                  modspec(2), modspec(3), modspec(4),
                  const(gpost), const(gpre),
                  const(wa), const(wb), const(wo), const(wr_hi), const(wr_lo)],
        out_specs=[row(d), row(d), pl.BlockSpec((tm * ROW_TILE, LANES), lambda i: (i, 0)),
                   pl.BlockSpec((LANES, tm), lambda i: (0, i))],
        out_shape=[jax.ShapeDtypeStruct((n, d), F32),
                   jax.ShapeDtypeStruct((n, d), BF16),
                   jax.ShapeDtypeStruct((n * ROW_TILE, LANES), jnp.uint32),
                   jax.ShapeDtypeStruct((LANES, n), F32)],
        scratch_shapes=[pltpu.VMEM((DIL_GROUPS, DIL_HEADS_PER_GROUP, tm, LANES), F32),
                        pltpu.VMEM((DIL_GROUPS, 1, tm, LANES), F32)],
        compiler_params=_cparams("arbitrary"),
        name="merge_outproj",
    )(oa, *dil_o, *dil_lse, gates, gates, x2, mod3, mod3, mod3, gpost, gpre, wa, wb, wo, wr_hi, wr_lo)


def _route_kernel(lg_ref, bias_ref, pos_ref, w_ref, meta_ref, blke_ref, cnt_sc, base_sc, *, tt):
    ps = pl.program_id(0)
    i = pl.program_id(1)
    per_group = N_EXPERTS // N_EXPERT_GROUPS
    neg_inf = -jnp.inf

    @pl.when(jnp.logical_and(ps == 0, i == 0))
    def _():
        cnt_sc[...] = jnp.zeros_like(cnt_sc)

    scores = _sigmoid(lg_ref[...])
    biased = scores + bias_ref[...]
    b3 = biased.reshape(N_EXPERT_GROUPS, per_group, tt)
    mem = lax.broadcasted_iota(jnp.int32, b3.shape, 1)
    m1 = jnp.max(b3, axis=1, keepdims=True)
    first = jnp.min(jnp.where(b3 == m1, mem, per_group), axis=1, keepdims=True)
    m2 = jnp.max(jnp.where(mem == first, neg_inf, b3), axis=1, keepdims=True)
    gs = m1 + m2
    gidx = lax.broadcasted_iota(jnp.int32, gs.shape, 0)
    grank = jnp.zeros(gs.shape, jnp.int32)
    for g2 in range(N_EXPERT_GROUPS):
        r = gs[g2:g2 + 1]
        beats = jnp.logical_or(r > gs, jnp.logical_and(r == gs, g2 < gidx))
        grank = grank + jnp.where(beats, 1, 0)
    sel = jnp.where(grank < TOPK_GROUPS, b3, neg_inf).reshape(N_EXPERTS, tt)
    eidx = lax.broadcasted_iota(jnp.int32, sel.shape, 0)
    erank = jnp.zeros(sel.shape, jnp.int32)
    for e2 in range(N_EXPERTS):
        r = sel[e2:e2 + 1, :]
        beats = jnp.logical_or(r > sel, jnp.logical_and(r == sel, e2 < eidx))
        erank = erank + jnp.where(beats, 1, 0)
    esel = erank < TOP_K
    mask_f = jnp.where(esel, 1.0, 0.0)
    tile_cnt = jnp.sum(mask_f, axis=1, keepdims=True).astype(jnp.int32)

    @pl.when(ps == 0)
    def _():
        cnt_sc[...] = cnt_sc[...] + tile_cnt

    @pl.when(jnp.logical_and(ps == 1, i == 0))
    def _():
        cnt = cnt_sc[...]
        pc = ((cnt + (MOE_BLOCK - 1)) // MOE_BLOCK) * MOE_BLOCK
        pcb = jnp.broadcast_to(pc, (N_EXPERTS, LANES))
        eid = lax.broadcasted_iota(jnp.int32, (N_EXPERTS, LANES), 0)
        pends = jnp.zeros((N_EXPERTS, LANES), jnp.int32)
        for e2 in range(N_EXPERTS):
            pends = pends + jnp.where(eid >= e2, pcb[e2:e2 + 1, :], 0)
        pst = pends - pcb
        base_sc[...] = pst[:, 0:1]
        meta_ref[0] = jnp.broadcast_to(cnt, (N_EXPERTS, LANES))
        meta_ref[1] = pst
        meta_ref[2] = pends
        nbl = blke_ref.shape[1]
        blk_start = lax.broadcasted_iota(jnp.int32, (N_EXPERTS, nbl), 1) * MOE_BLOCK
        pend_b = jnp.broadcast_to(pends[:, 0:1], (N_EXPERTS, nbl))
        be = jnp.sum(jnp.where(pend_b <= blk_start, 1, 0), axis=0, keepdims=True)
        blke_ref[...] = jnp.broadcast_to(jnp.minimum(be, N_EXPERTS - 1), blke_ref.shape)

    @pl.when(ps == 1)
    def _():
        rr = lax.broadcasted_iota(jnp.int32, (tt, tt), 0)
        cc = lax.broadcasted_iota(jnp.int32, (tt, tt), 1)
        upper = jnp.where(rr < cc, 1.0, 0.0).astype(BF16)
        prefix = jnp.dot(mask_f.astype(BF16), upper, preferred_element_type=F32)
        posd = base_sc[...] + prefix.astype(jnp.int32)
        base_sc[...] = base_sc[...] + tile_cnt
        wsel = jnp.where(esel, scores, 0.0)
        denom = jnp.sum(wsel, axis=0, keepdims=True)
        wn = wsel / (denom + 1e-20) * ROUTED_SCALE
        prow, wrow = [], []
        for kk in range(TOP_K):
            hit = erank == kk
            prow.append(jnp.sum(jnp.where(hit, posd, 0), axis=0, keepdims=True))
            wrow.append(jnp.sum(jnp.where(hit, wn, 0.0), axis=0, keepdims=True))
        pad = pos_ref.shape[0] - TOP_K
        pos_ref[...] = jnp.concatenate(prow + [jnp.zeros((pad, tt), jnp.int32)], axis=0)
        w_ref[...] = jnp.concatenate(wrow + [jnp.zeros((pad, tt), F32)], axis=0)


def _route(logits_t, bias_col, nb, tt=256):
    n = logits_t.shape[1]
    nbl = -(-nb // LANES) * LANES
    return pl.pallas_call(
        functools.partial(_route_kernel, tt=tt),
        grid=(2, n // tt),
        in_specs=[pl.BlockSpec((N_EXPERTS, tt), lambda ps, i: (0, i)),
                  pl.BlockSpec((N_EXPERTS, 1), lambda ps, i: (0, 0))],
        out_specs=[pl.BlockSpec((8, tt), lambda ps, i: (0, ps * i)),
                   pl.BlockSpec((8, tt), lambda ps, i: (0, ps * i)),
                   pl.BlockSpec((3, N_EXPERTS, LANES), lambda ps, i: (0, 0, 0)),
                   pl.BlockSpec((8, nbl), lambda ps, i: (0, 0))],
        out_shape=[jax.ShapeDtypeStruct((8, n), jnp.int32),
                   jax.ShapeDtypeStruct((8, n), F32),
                   jax.ShapeDtypeStruct((3, N_EXPERTS, LANES), jnp.int32),
                   jax.ShapeDtypeStruct((8, nbl), jnp.int32)],
        scratch_shapes=[pltpu.VMEM((N_EXPERTS, 1), jnp.int32), pltpu.VMEM((N_EXPERTS, 1), jnp.int32)],
        compiler_params=_cparams("arbitrary", "arbitrary"),
        name="moe_route",
    )(logits_t, bias_col)


def _dispatch_kernel(pos_ref, cnt_ref, pst_ref, h_ref, hb_ref, wsg_ref, wsu_ref, wsd_ref, xs_hbm, ysh_ref,
                     zrow, sem, *, tm, n):
    i = pl.program_id(0)

    def body(g, carry):
        for u in range(ROW_DMA_UNROLL):
            r = g * ROW_DMA_UNROLL + u
            for kk in range(TOP_K):
                p = pos_ref[kk * n + i * tm + r]
                pltpu.make_async_copy(h_ref.at[_row_tile(r)], xs_hbm.at[_row_tile(p)],
                                      sem.at[0]).start(priority=kk % 2)
        return carry
    lax.fori_loop(0, tm // ROW_DMA_UNROLL, body, 0)

    @pl.when(i == pl.num_programs(0) - 1)
    def _():
        zrow[...] = jnp.zeros_like(zrow)

        def per_expert(e, carry):
            cnt = cnt_ref[e]
            first = pst_ref[e] + cnt
            npad = ((cnt + (MOE_BLOCK - 1)) // MOE_BLOCK) * MOE_BLOCK - cnt

            def start(s, c2):
                pltpu.make_async_copy(zrow.at[_row_tile(0)], xs_hbm.at[_row_tile(first + s)], sem.at[1]).start()
                return c2

            def wait(s, c2):
                pltpu.make_async_copy(zrow.at[_row_tile(0)], xs_hbm.at[_row_tile(0)], sem.at[1]).wait()
                return c2
            lax.fori_loop(0, npad, start, 0)
            lax.fori_loop(0, npad, wait, 0)
            return carry
        lax.fori_loop(0, N_EXPERTS, per_expert, 0)

        last = N_EXPERTS - 1
        used = (pst_ref[last] + cnt_ref[last] + (MOE_BLOCK - 1)) // MOE_BLOCK
        blk_rows = MOE_BLOCK * ROW_TILE

        def tail(b, carry):
            cp = pltpu.make_async_copy(zrow, xs_hbm.at[pl.ds(pl.multiple_of(b * blk_rows, blk_rows), blk_rows), :],
                                       sem.at[1])
            cp.start()
            cp.wait()
            return carry
        lax.fori_loop(used, xs_hbm.shape[0] // blk_rows, tail, 0)

    hb = hb_ref[...]
    g = jnp.dot(hb, wsg_ref[...], preferred_element_type=F32)
    u = jnp.dot(hb, wsu_ref[...], preferred_element_type=F32)
    ysh_ref[...] = jnp.dot((g * _sigmoid(g) * u).astype(BF16), wsd_ref[...],
                           preferred_element_type=F32).astype(BF16)

    for kk in range(TOP_K):
        pltpu.make_async_copy(h_ref, xs_hbm.at[pl.ds(0, tm * ROW_TILE), :], sem.at[0]).wait()


def _dispatch(pos_flat, cnt, pst, h2pk, h2, wsg, wsu, wsd, nb, tm=256):
    n, d = h2.shape
    const = lambda a: pl.BlockSpec(a.shape, lambda i, *_: (0,) * a.ndim)
    grid_spec = pltpu.PrefetchScalarGridSpec(
        num_scalar_prefetch=3,
        grid=(n // tm,),
        in_specs=[pl.BlockSpec((tm * ROW_TILE, LANES), lambda i, *_: (i, 0)),
                  pl.BlockSpec((tm, d), lambda i, *_: (i, 0)),
                  const(wsg), const(wsu), const(wsd)],
        out_specs=[pl.BlockSpec(memory_space=pl.ANY), pl.BlockSpec((tm, d), lambda i, *_: (i, 0))],
        scratch_shapes=[pltpu.VMEM((MOE_BLOCK * ROW_TILE, LANES), jnp.uint32), pltpu.SemaphoreType.DMA((2,))],
    )
    return pl.pallas_call(
        functools.partial(_dispatch_kernel, tm=tm, n=n),
        grid_spec=grid_spec,
        out_shape=[jax.ShapeDtypeStruct((nb * MOE_BLOCK * ROW_TILE, LANES), jnp.uint32),
                   jax.ShapeDtypeStruct((n, d), BF16)],
        compiler_params=_cparams("arbitrary"),
        name="moe_dispatch",
    )(pos_flat, cnt, pst, h2pk, h2, wsg, wsu, wsd)


def _gmm_kernel(nact_ref, blke_ref, xs_ref, wg_hbm, wu_hbm, wd_hbm, o_ref,
                wgf, wuf, wdf, wgb, wub, wdb, run_sc, sem):
    i = pl.program_id(0)
    nact = nact_ref[0]

    def fetch(e, slot):
        return (pltpu.make_async_copy(wg_hbm.at[e], wgf.at[slot], sem.at[slot, 0]),
                pltpu.make_async_copy(wu_hbm.at[e], wuf.at[slot], sem.at[slot, 1]),
                pltpu.make_async_copy(wd_hbm.at[e], wdf.at[slot], sem.at[slot, 2]))

    @pl.when(i == 0)
    def _():
        run_sc[0] = 0
        for cp in fetch(blke_ref[0], 0):
            cp.start()

    @pl.when(i < nact)
    def _():
        e = blke_ref[i]
        changed = jnp.logical_or(i == 0, e != blke_ref[jnp.maximum(i - 1, 0)])

        @pl.when(changed)
        def _():
            run = run_sc[0]
            slot = run % 2
            for cp in fetch(e, slot):
                cp.wait()
            wgb[...] = wgf[slot].astype(BF16)
            wub[...] = wuf[slot].astype(BF16)
            wdb[...] = wdf[slot].astype(BF16)
            nxt = lax.while_loop(lambda j: jnp.logical_and(j < nact, blke_ref[jnp.minimum(j, nact - 1)] == e),
                                 lambda j: j + 1, i + 1)

            @pl.when(nxt < nact)
            def _():
                for cp in fetch(blke_ref[jnp.minimum(nxt, nact - 1)], 1 - slot):
                    cp.start(priority=1)
            run_sc[0] = run + 1

        lo, hi = _unpack_halves(_load_row_tiles(xs_ref, MOE_BLOCK))
        lo, hi = lo.astype(BF16), hi.astype(BF16)
        half = lo.shape[1]
        g = (jnp.dot(lo, wgb[:half, :], preferred_element_type=F32)
             + jnp.dot(hi, wgb[half:, :], preferred_element_type=F32))
        u = (jnp.dot(lo, wub[:half, :], preferred_element_type=F32)
             + jnp.dot(hi, wub[half:, :], preferred_element_type=F32))
        a = (g * _sigmoid(g) * u).astype(BF16)
        _store_row_tiles(o_ref, _pack_halves(jnp.dot(a, wdb[...], preferred_element_type=F32)))

    @pl.when(i >= nact_ref[0])
    def _():
        o_ref[...] = jnp.zeros_like(o_ref)


def _gmm(nact, blk_e, xs, w_gate, w_up, w_down, nb):
    d, f = w_gate.shape[1:]
    blk = lambda i, na: jnp.minimum(i, na[0] - 1)
    hbm = pl.BlockSpec(memory_space=pl.ANY)
    grid_spec = pltpu.PrefetchScalarGridSpec(
        num_scalar_prefetch=2,
        grid=(nb,),
        in_specs=[pl.BlockSpec((MOE_BLOCK * ROW_TILE, LANES), lambda i, na, be: (blk(i, na), 0)), hbm, hbm, hbm],
        out_specs=pl.BlockSpec((MOE_BLOCK * ROW_TILE, LANES), lambda i, na, be: (i, 0)),
        scratch_shapes=[pltpu.VMEM((2, d, f), F32), pltpu.VMEM((2, d, f), F32), pltpu.VMEM((2, f, d), F32),
                        pltpu.VMEM((d, f), BF16), pltpu.VMEM((d, f), BF16), pltpu.VMEM((f, d), BF16),
                        pltpu.SMEM((1,), jnp.int32), pltpu.SemaphoreType.DMA((2, 3))],
    )
    return pl.pallas_call(
        _gmm_kernel,
        grid_spec=grid_spec,
        out_shape=jax.ShapeDtypeStruct((nb * MOE_BLOCK * ROW_TILE, LANES), jnp.uint32),
        compiler_params=_cparams("arbitrary"),
        name="moe_experts",
    )(nact, blk_e, xs, w_gate, w_up, w_down)


def _final_kernel(pos_ref, tw_ref, ysh_ref, x1_ref, gate_ref, gpost_ref, yb_hbm, o_ref, rbuf, sem, *, tm, n):
    i = pl.program_id(0)
    slot = i % 2

    def gather(tile, sl):
        def body(g, carry):
            for u in range(ROW_DMA_UNROLL):
                r = g * ROW_DMA_UNROLL + u
                for kk in range(TOP_K):
                    p = pos_ref[kk * n + tile * tm + r]
                    pltpu.make_async_copy(yb_hbm.at[_row_tile(p)], rbuf.at[sl, kk].at[_row_tile(r)],
                                          sem.at[sl]).start(priority=kk % 2)
            return carry
        lax.fori_loop(0, tm // ROW_DMA_UNROLL, body, 0)

    @pl.when(i == 0)
    def _():
        gather(0, 0)

    @pl.when(i + 1 < pl.num_programs(0))
    def _():
        gather(i + 1, 1 - slot)

    for kk in range(TOP_K):
        pltpu.make_async_copy(yb_hbm.at[pl.ds(0, tm * ROW_TILE), :], rbuf.at[slot, kk], sem.at[slot]).wait()
    tw = tw_ref[...]
    half = ROW_TILE * LANES
    r_lo = jnp.zeros((tm, half), F32)
    r_hi = jnp.zeros((tm, half), F32)
    for kk in range(TOP_K):
        lo, hi = _unpack_halves(_load_row_tiles(rbuf.at[slot, kk], tm))
        wk = tw[:, kk:kk + 1]
        r_lo = r_lo + wk * lo
        r_hi = r_hi + wk * hi
    y = ysh_ref[...].astype(F32) + jnp.concatenate([r_lo, r_hi], axis=1)
    o_ref[...] = x1_ref[...] + gate_ref[0] * _rms(y, gpost_ref[...])


def _final(pos_flat, top_w8, ysh, x1, mod3, gpost, yb, seq, tm=256):
    n, d = x1.shape
    per_b = seq // tm
    const = lambda a: pl.BlockSpec(a.shape, lambda i, ps: (0,) * a.ndim)
    row = lambda w: pl.BlockSpec((tm, w), lambda i, ps: (i, 0))
    grid_spec = pltpu.PrefetchScalarGridSpec(
        num_scalar_prefetch=1,
        grid=(n // tm,),
        in_specs=[row(top_w8.shape[1]), row(d), row(d),
                  pl.BlockSpec((1, 1, d), lambda i, ps: ((i // per_b) * ADALN_CHUNKS + 5, 0, 0)),
                  const(gpost),
                  pl.BlockSpec(memory_space=pl.ANY)],
        out_specs=row(d),
        scratch_shapes=[pltpu.VMEM((2, TOP_K, tm * ROW_TILE, LANES), jnp.uint32), pltpu.SemaphoreType.DMA((2,))],
    )
    return pl.pallas_call(
        functools.partial(_final_kernel, tm=tm, n=n),
        grid_spec=grid_spec,
        out_shape=jax.ShapeDtypeStruct((n, d), F32),
        compiler_params=_cparams("arbitrary"),
        name="moe_combine_final",
    )(pos_flat, top_w8, ysh, x1, mod3, gpost, yb)


def _rope_tables(positions, dim):
    half = dim // 2
    inv_freq = 1.0 / (ROPE_THETA ** (jnp.arange(0, dim, 2, dtype=F32) / dim))
    ang = positions.astype(F32).reshape(-1, 1) * inv_freq
    cos, sin = jnp.cos(ang), jnp.sin(ang)
    n = cos.shape[0]
    zeros = jnp.zeros((n, LANES - dim), F32)
    zh = jnp.zeros((n, half), F32)
    return cos, sin, zeros, zh


def kernel(x, c, positions, w_ada, b_ada, attn_pre_g, w_in, q_a_norm_g, w_q_up, kv_a_norm_g, w_kv_up, w_mla_o, w_dil_o, w_out, attn_post_g, ffn_pre_g, w_router, router_bias, w_exp_gate, w_exp_up, w_exp_down, w_sh_gate, w_sh_up, w_sh_down, ffn_post_g):
    batch, seq, d = x.shape
    n = batch * seq
    depth = w_ada.shape[0]

    cos, sin, zeros, zh = _rope_tables(positions, QK_ROPE_DIM)
    m_c = jnp.concatenate([cos, cos, zeros], axis=1)
    m_sf = jnp.concatenate([zh, sin, zeros], axis=1)
    m_sb = jnp.concatenate([-sin, zh, zeros], axis=1)
    cos, sin, zeros, zh = _rope_tables(positions, DIL_ROT_DIM)
    d_c = jnp.concatenate([cos, cos, jnp.ones_like(zeros)], axis=1)
    d_sf = jnp.concatenate([zh, sin, zeros], axis=1)
    d_sb = jnp.concatenate([-sin, zh, zeros], axis=1)

    x2 = x.reshape(n, d)
    c8 = jnp.pad(c, ((0, 8 - batch), (0, 0)))
    for l in range(depth):
        mod = _ada(c8, w_ada[l], b_ada[l].reshape(1, -1))
        mod3 = mod[:batch].reshape(batch * ADALN_CHUNKS, 1, d)

        wi = w_in[l]
        o_q, o_kv, o_dil, o_ga = Q_LORA_RANK, Q_LORA_RANK + KV_LORA_RANK + QK_ROPE_DIM, 0, 0
        o_dil = o_kv
        o_ga = o_dil + 3 * DIL_HEADS * DIL_HEAD_DIM
        w_a = jnp.concatenate([wi[:, :o_kv], jnp.zeros((d, LANES - QK_ROPE_DIM), F32)], axis=1).astype(BF16)
        wd3 = wi[:, o_dil:o_ga].reshape(d, 3, DIL_GROUPS, DIL_GROUP_COLS)
        w_dil = [wd3[:, :, g, :].reshape(d, 3 * DIL_GROUP_COLS).astype(BF16) for g in range(DIL_GROUPS)]
        w_g = wi[:, o_ga:].astype(BF16)
        wq3 = w_q_up[l].reshape(Q_LORA_RANK, MLA_HEADS, MLA_QK_DIM)
        wq = jnp.concatenate([wq3, jnp.zeros((Q_LORA_RANK, MLA_HEADS, MLA_QK_PAD - MLA_QK_DIM), F32)],
                             axis=2).reshape(Q_LORA_RANK, MLA_HEADS * MLA_QK_PAD).astype(BF16)
        wkv3 = w_kv_up[l].reshape(KV_LORA_RANK, MLA_HEADS, QK_NOPE_DIM + V_HEAD_DIM)
        wkv = jnp.concatenate([wkv3[:, :, :QK_NOPE_DIM].reshape(KV_LORA_RANK, -1),
                               wkv3[:, :, QK_NOPE_DIM:].reshape(KV_LORA_RANK, -1)], axis=1).astype(BF16)

        h = _prenorm(x2, attn_pre_g[l].reshape(1, d), mod3, seq)
        a = _mm(h, w_a, tn=w_a.shape[1])
        gates = _mm(h, w_g, act="sigmoid")
        q, k, v = _mlaprep(a, q_a_norm_g[l].reshape(1, -1), kv_a_norm_g[l].reshape(1, -1), wq, wkv,
                           m_c, m_sf, m_sb, batch, seq)
        o_mla = _mla_attn(q, k, v).reshape(n, MLA_HEADS * V_HEAD_DIM)
        dil_o, dil_lse = [], []
        for g, (_, dilation) in enumerate(DIL_PATTERNS):
            qkv = _dilproj(h, w_dil[g], d_c, d_sf, d_sb, batch, seq, dilation)
            o_g, lse_g = _dil_attn(qkv, batch, seq, dilation)
            dil_o.append(o_g)
            dil_lse.append(lse_g)

        wr = jnp.pad(w_router[l].T, ((0, LANES - N_EXPERTS), (0, 0)))
        wr_hi = wr.astype(BF16)
        wr_lo = (wr - wr_hi.astype(F32)).astype(BF16)
        x1, h2, h2pk, logits_t = _merge(o_mla, dil_o, dil_lse, gates, x2, mod3,
                                        attn_post_g[l].reshape(1, d), ffn_pre_g[l].reshape(1, d),
                                        w_mla_o[l].astype(BF16), w_dil_o[l].astype(BF16), w_out[l].astype(BF16),
                                        wr_hi, wr_lo, seq)

        nb = -(-(n * TOP_K + N_EXPERTS * (MOE_BLOCK - 1)) // MOE_BLOCK)
        pos_t, w_t, meta, blk_e = _route(logits_t, router_bias[l].astype(F32).reshape(N_EXPERTS, 1), nb)
        pos_flat = pos_t.reshape(-1)
        nact = meta[2, N_EXPERTS - 1, :1] // MOE_BLOCK
        xs, ysh = _dispatch(pos_flat, meta[0, :, 0], meta[1, :, 0], h2pk, h2,
                            w_sh_gate[l].astype(BF16), w_sh_up[l].astype(BF16), w_sh_down[l].astype(BF16), nb)
        yb = _gmm(nact, blk_e[0], xs, w_exp_gate[l], w_exp_up[l], w_exp_down[l], nb)
        x2 = _final(pos_flat, w_t.T, ysh, x1, mod3, ffn_post_g[l].reshape(1, d), yb, seq)
    return x2.reshape(batch, seq, d)
```

```python
import functools

import jax
import jax.numpy as jnp
from jax import lax
from jax.experimental import pallas as pl
from jax.experimental.pallas import tpu as pltpu

F32 = jnp.float32
BF16 = jnp.bfloat16

D_MODEL = 2048
NORM_EPS = 1e-6
ROPE_THETA = 500000.0
ADALN_CHUNKS = 6

MLA_HEADS = 8
Q_LORA_RANK = 512
KV_LORA_RANK = 512
QK_NOPE_DIM = 128
QK_ROPE_DIM = 64
V_HEAD_DIM = 128
MLA_QK_DIM = QK_NOPE_DIM + QK_ROPE_DIM
MLA_QK_PAD = 256

DIL_PATTERNS = ((128, 1), (512, 4), (2048, 16))
DIL_GROUPS = len(DIL_PATTERNS)
DIL_HEADS_PER_GROUP = 4
DIL_HEADS = DIL_GROUPS * DIL_HEADS_PER_GROUP
DIL_HEAD_DIM = 128
DIL_ROT_DIM = DIL_HEAD_DIM // 4
DIL_SPAN = 128
DIL_GROUP_COLS = DIL_HEADS_PER_GROUP * DIL_HEAD_DIM

N_EXPERTS = 64
N_EXPERT_GROUPS = 8
TOPK_GROUPS = 4
TOP_K = 6
EXPERT_DIM = 512
SHARED_DIM = 512
ROUTED_SCALE = 2.5
MOE_BLOCK = 256

LANES = 128
NEG_BIG = -1e30
LOG2_E = 1.4426950408889634
ROW_DMA_UNROLL = 4
VMEM_LIMIT = 56 * 1024 * 1024


def _cparams(*sem):
    return pltpu.CompilerParams(dimension_semantics=sem, vmem_limit_bytes=VMEM_LIMIT)


def _sigmoid(v):
    return 1.0 / (1.0 + jnp.exp(-v))


def _rms(v, g):
    ms = jnp.mean(v * v, axis=-1, keepdims=True)
    return v * lax.rsqrt(ms + NORM_EPS) * g


def _ada_kernel(c_ref, w_ref, b_ref, o_ref):
    c = c_ref[...]
    a = (c * _sigmoid(c)).astype(BF16)
    o_ref[...] = jnp.dot(a, w_ref[...].astype(BF16), preferred_element_type=F32) + b_ref[...]


def _ada(c8, w_ada, b_ada, tn=1536):
    d, n = w_ada.shape
    return pl.pallas_call(
        _ada_kernel,
        grid=(n // tn,),
        in_specs=[pl.BlockSpec((8, d), lambda j: (0, 0)),
                  pl.BlockSpec((d, tn), lambda j: (0, j)),
                  pl.BlockSpec((1, tn), lambda j: (0, j))],
        out_specs=pl.BlockSpec((8, tn), lambda j: (0, j)),
        out_shape=jax.ShapeDtypeStruct((8, n), F32),
        compiler_params=_cparams("arbitrary"),
        name="ada_mod",
    )(c8, w_ada, b_ada)


def _prenorm_kernel(x_ref, g_ref, scale_ref, shift_ref, o_ref):
    xn = _rms(x_ref[...], g_ref[...])
    o_ref[...] = (xn * (1.0 + scale_ref[0]) + shift_ref[0]).astype(BF16)


def _prenorm(x2, g, mod3, seq, tm=512):
    n, d = x2.shape
    per_b = seq // tm
    return pl.pallas_call(
        _prenorm_kernel,
        grid=(n // tm,),
        in_specs=[pl.BlockSpec((tm, d), lambda i: (i, 0)),
                  pl.BlockSpec((1, d), lambda i: (0, 0)),
                  pl.BlockSpec((1, 1, d), lambda i: ((i // per_b) * ADALN_CHUNKS + 1, 0, 0)),
                  pl.BlockSpec((1, 1, d), lambda i: ((i // per_b) * ADALN_CHUNKS + 0, 0, 0))],
        out_specs=pl.BlockSpec((tm, d), lambda i: (i, 0)),
        out_shape=jax.ShapeDtypeStruct((n, d), BF16),
        compiler_params=_cparams("arbitrary"),
        name="prenorm_attn",
    )(x2, g, mod3, mod3)


def _mm_kernel(h_ref, w_ref, o_ref, *, act):
    y = jnp.dot(h_ref[...], w_ref[...], preferred_element_type=F32)
    if act == "sigmoid":
        y = _sigmoid(y)
    o_ref[...] = y.astype(o_ref.dtype)


def _mm(h, w, col0, cols, act=None, tm=1024, tn=1024):
    n, k = h.shape
    tn = min(tn, cols)
    j0 = col0 // tn
    return pl.pallas_call(
        functools.partial(_mm_kernel, act=act),
        grid=(cols // tn, n // tm),
        in_specs=[pl.BlockSpec((tm, k), lambda j, i: (i, 0)),
                  pl.BlockSpec((k, tn), lambda j, i: (0, j0 + j))],
        out_specs=pl.BlockSpec((tm, tn), lambda j, i: (i, j)),
        out_shape=jax.ShapeDtypeStruct((n, cols), BF16),
        compiler_params=_cparams("arbitrary", "arbitrary"),
        name="in_proj_" + (act or "plain"),
    )(h, w)


def _rope_lanes(t, c_tab, s_fwd, s_bwd, half):
    return t * c_tab + pltpu.roll(t, half, 1) * s_fwd + pltpu.roll(t, LANES - half, 1) * s_bwd


def _dilproj_kernel(h_ref, wq_ref, wk_ref, wv_ref, c_ref, sf_ref, sb_ref, o_ref, y_sc, *, dilation):
    hb = h_ref[...]
    y = jnp.concatenate([jnp.dot(hb, w_ref[...], preferred_element_type=F32) for w_ref in (wq_ref, wk_ref, wv_ref)],
                        axis=1)
    c_tab, s_fwd, s_bwd = c_ref[...], sf_ref[...], sb_ref[...]
    cols = 3 * DIL_GROUP_COLS
    rows = h_ref.shape[0] // dilation
    n_rot = 2 * DIL_HEADS_PER_GROUP
    for hh in range(3 * DIL_HEADS_PER_GROUP):
        t = y[:, hh * LANES:(hh + 1) * LANES]
        if hh < n_rot:
            t = _rope_lanes(t, c_tab, s_fwd, s_bwd, DIL_ROT_DIM // 2)
        if dilation == 1:
            o_ref[0, :, hh * LANES:(hh + 1) * LANES] = t.astype(BF16)
        else:
            y_sc[hh] = t
    if dilation > 1:
        for r in range(dilation):
            for hh in range(3 * DIL_HEADS_PER_GROUP):
                c0 = r * cols + hh * LANES
                o_ref[0, :, c0:c0 + LANES] = y_sc.at[hh][pl.ds(r, rows, stride=dilation), :].astype(BF16)


def _dilproj(h, w, col0, group, c_tab, s_fwd, s_bwd, batch, seq, dilation, tm=1024):
    n, k = h.shape
    gc = DIL_GROUP_COLS
    cols = 3 * gc
    per_b = seq // tm
    tab = pl.BlockSpec((tm, LANES), lambda i: (i, 0))
    part = lambda which: pl.BlockSpec((k, gc), lambda i: (0, col0 // gc + which * DIL_GROUPS + group))
    return pl.pallas_call(
        functools.partial(_dilproj_kernel, dilation=dilation),
        grid=(n // tm,),
        in_specs=[pl.BlockSpec((tm, k), lambda i: (i, 0)),
                  part(0), part(1), part(2),
                  tab, tab, tab],
        out_specs=pl.BlockSpec((1, tm // dilation, dilation * cols), lambda i: (i // per_b, i % per_b, 0)),
        out_shape=jax.ShapeDtypeStruct((batch, seq // dilation, dilation * cols), BF16),
        scratch_shapes=[pltpu.VMEM((cols // LANES, tm, LANES), F32)],
        compiler_params=_cparams("arbitrary"),
        name="dil_proj",
    )(h, w, w, w, c_tab, s_fwd, s_bwd)


def _mlaprep_kernel(a_ref, gq_ref, gkv_ref, wq_ref, wkv_ref, c_ref, sf_ref, sb_ref,
                    q_ref, k_ref, v_ref):
    a = a_ref[...].astype(F32)
    qa = a[:, :Q_LORA_RANK]
    ckv = a[:, Q_LORA_RANK:Q_LORA_RANK + KV_LORA_RANK]
    kr = a[:, Q_LORA_RANK + KV_LORA_RANK:]
    c_tab, s_fwd, s_bwd = c_ref[...], sf_ref[...], sb_ref[...]
    half = QK_ROPE_DIM // 2
    q = jnp.dot(_rms(qa, gq_ref[...]).astype(BF16), wq_ref[...], preferred_element_type=F32)
    q = q * (MLA_QK_DIM ** -0.5 * LOG2_E)
    kv = jnp.dot(_rms(ckv, gkv_ref[...]).astype(BF16), wkv_ref[...], preferred_element_type=F32)
    k_rot = _rope_lanes(kr, c_tab, s_fwd, s_bwd, half).astype(BF16)
    lane = lax.broadcasted_iota(jnp.int32, (a.shape[0], LANES), 1)
    ones_col = jnp.where(lane == 0, 1.0, 0.0).astype(BF16)
    for hh in range(MLA_HEADS):
        base = hh * MLA_QK_PAD
        q_ref[0, hh, :, :LANES] = q[:, base:base + LANES].astype(BF16)
        q_ref[0, hh, :, LANES:] = _rope_lanes(q[:, base + LANES:base + 2 * LANES],
                                              c_tab, s_fwd, s_bwd, half).astype(BF16)
        k_ref[0, hh, :, :LANES] = kv[:, hh * LANES:(hh + 1) * LANES].astype(BF16)
        k_ref[0, hh, :, LANES:] = k_rot
        v_off = MLA_HEADS * LANES + hh * LANES
        v_ref[0, hh, :, :LANES] = kv[:, v_off:v_off + LANES].astype(BF16)
        v_ref[0, hh, :, LANES:] = ones_col


def _mlaprep(a, gq, gkv, wq, wkv, c_tab, s_fwd, s_bwd, batch, seq, tm=512):
    n, cols = a.shape
    per_b = seq // tm
    tab = pl.BlockSpec((tm, LANES), lambda i: (i, 0))
    head_major = lambda w: pl.BlockSpec((1, MLA_HEADS, tm, w), lambda i: (i // per_b, 0, i % per_b, 0))
    return pl.pallas_call(
        _mlaprep_kernel,
        grid=(n // tm,),
        in_specs=[pl.BlockSpec((tm, cols), lambda i: (i, 0)),
                  pl.BlockSpec((1, Q_LORA_RANK), lambda i: (0, 0)),
                  pl.BlockSpec((1, KV_LORA_RANK), lambda i: (0, 0)),
                  pl.BlockSpec(wq.shape, lambda i: (0, 0)),
                  pl.BlockSpec(wkv.shape, lambda i: (0, 0)),
                  tab, tab, tab],
        out_specs=[head_major(MLA_QK_PAD), head_major(MLA_QK_PAD), head_major(2 * V_HEAD_DIM)],
        out_shape=[jax.ShapeDtypeStruct((batch, MLA_HEADS, seq, MLA_QK_PAD), BF16),
                   jax.ShapeDtypeStruct((batch, MLA_HEADS, seq, MLA_QK_PAD), BF16),
                   jax.ShapeDtypeStruct((batch, MLA_HEADS, seq, 2 * V_HEAD_DIM), BF16)],
        compiler_params=_cparams("arbitrary"),
        name="mla_prep",
    )(a, gq, gkv, wq, wkv, c_tab, s_fwd, s_bwd)


def _mla_attn_kernel(q_ref, k_ref, v_ref, o_ref, *, tq, nh):
    i = pl.program_id(2)
    qs = [q_ref[0, hh] for hh in range(nh)]

    def step(c, carry, masked):
        base = pl.multiple_of(c * tq, tq)
        ss = []
        for hh in range(nh):
            k = k_ref[0, hh, pl.ds(base, tq), :]
            s = lax.dot_general(qs[hh], k, (((1,), (1,)), ((), ())), preferred_element_type=F32)
            if masked:
                row = lax.broadcasted_iota(jnp.int32, (tq, tq), 0)
                col = lax.broadcasted_iota(jnp.int32, (tq, tq), 1)
                s = jnp.where(col <= row, s, NEG_BIG)
            ss.append(s)
        out = []
        for hh in range(nh):
            m, l, acc = carry[hh]
            v = v_ref[0, hh, pl.ds(base, tq), :]
            m_new = jnp.maximum(m, jnp.max(ss[hh], axis=-1, keepdims=True))
            alpha = jnp.exp2(m - m_new)
            pv = jnp.dot(jnp.exp2((ss[hh] - m_new).astype(BF16)), v, preferred_element_type=F32)
            out.append((m_new, alpha * l + pv[:, V_HEAD_DIM:V_HEAD_DIM + 1], alpha * acc + pv[:, :V_HEAD_DIM]))
        return tuple(out)

    init = tuple((jnp.full((tq, 1), NEG_BIG, F32), jnp.zeros((tq, 1), F32), jnp.zeros((tq, V_HEAD_DIM), F32))
                 for _ in range(nh))
    carry = lax.fori_loop(0, i, lambda c, cr: step(c, cr, False), init)
    carry = step(i, carry, True)
    for hh in range(nh):
        _, l, acc = carry[hh]
        o_ref[0, :, hh * V_HEAD_DIM:(hh + 1) * V_HEAD_DIM] = (acc / l).astype(BF16)


def _mla_attn(q, k, v, tq=512, nh=4):
    b, h, s, dk = q.shape
    dv = v.shape[-1]
    resident = pl.Buffered(1)
    return pl.pallas_call(
        functools.partial(_mla_attn_kernel, tq=tq, nh=nh),
        grid=(b, h // nh, s // tq),
        in_specs=[pl.BlockSpec((1, nh, tq, dk), lambda bi, hi, i: (bi, hi, i, 0)),
                  pl.BlockSpec((1, nh, s, dk), lambda bi, hi, i: (bi, hi, 0, 0), pipeline_mode=resident),
                  pl.BlockSpec((1, nh, s, dv), lambda bi, hi, i: (bi, hi, 0, 0), pipeline_mode=resident)],
        out_specs=pl.BlockSpec((1, tq, nh * V_HEAD_DIM), lambda bi, hi, i: (bi, i, hi)),
        out_shape=jax.ShapeDtypeStruct((b, s, h * V_HEAD_DIM), BF16),
        compiler_params=_cparams("arbitrary", "arbitrary", "arbitrary"),
        name="mla_attn",
    )(q, k, v)


def _dil_attn_kernel(q_ref, kc_ref, kp_ref, vc_ref, vp_ref, o_ref, lse_ref, *, tq):
    i = pl.program_id(2)
    sub = DIL_SPAN
    row = lax.broadcasted_iota(jnp.int32, (sub, 2 * sub), 0)
    col = lax.broadcasted_iota(jnp.int32, (sub, 2 * sub), 1)
    band = jnp.logical_and(col >= row, col <= row + sub)
    first = jnp.logical_and(band, col >= jnp.where(i > 0, 0, sub))
    lane = lax.broadcasted_iota(jnp.int32, (sub, LANES), 1)
    scale = DIL_HEAD_DIM ** -0.5
    dn = (((1,), (1,)), ((), ()))
    chains = [(j, hh) for j in range(tq // sub) for hh in range(DIL_HEADS_PER_GROUP)]

    def window(cur_ref, prev_ref, j, cs):
        if j == 0:
            return jnp.concatenate([prev_ref[0, :, cs], cur_ref[0, :sub, cs]], axis=0)
        return cur_ref[0, (j - 1) * sub:(j + 1) * sub, cs]

    scores = []
    for j, hh in chains:
        cs = slice(hh * LANES, (hh + 1) * LANES)
        s = lax.dot_general(q_ref[0, j * sub:(j + 1) * sub, cs], window(kc_ref, kp_ref, j, cs), dn,
                            preferred_element_type=F32) * scale
        scores.append(jnp.where(first if j == 0 else band, s, NEG_BIG))
    lse_blk = [jnp.zeros((sub, LANES), F32) for _ in range(tq // sub)]
    for (j, hh), s in zip(chains, scores):
        cs = slice(hh * LANES, (hh + 1) * LANES)
        m = jnp.max(s, axis=-1, keepdims=True)
        p = jnp.exp(s - m)
        l = jnp.sum(p, axis=-1, keepdims=True)
        acc = jnp.dot(p.astype(BF16), window(vc_ref, vp_ref, j, cs), preferred_element_type=F32)
        o_ref[0, j * sub:(j + 1) * sub, cs] = (acc * (1.0 / l)).astype(BF16)
        lse_blk[j] = jnp.where(lane == hh, m + jnp.log(l), lse_blk[j])
    for j in range(tq // sub):
        lse_ref[0, j * sub:(j + 1) * sub, :] = lse_blk[j]


def _dil_attn(t, batch, seq, dilation):
    ln = seq // dilation
    tq = min(ln, 4 * DIL_SPAN)
    gc = DIL_GROUP_COLS
    ratio = tq // DIL_SPAN
    cur = lambda which: pl.BlockSpec((1, tq, gc), lambda b, r, i: (b, i, r * 3 + which))
    prev = lambda which: pl.BlockSpec(
        (1, DIL_SPAN, gc), lambda b, r, i: (b, jnp.maximum(i * ratio - 1, 0), r * 3 + which))
    o, lse = pl.pallas_call(
        functools.partial(_dil_attn_kernel, tq=tq),
        grid=(batch, dilation, ln // tq),
        in_specs=[cur(0), cur(1), prev(1), cur(2), prev(2)],
        out_specs=[pl.BlockSpec((1, tq, gc), lambda b, r, i: (b, i, r)),
                   pl.BlockSpec((1, tq, LANES), lambda b, r, i: (b, i, r))],
        out_shape=[jax.ShapeDtypeStruct((batch, ln, dilation * gc), BF16),
                   jax.ShapeDtypeStruct((batch, ln, dilation * LANES), F32)],
        compiler_params=_cparams("arbitrary", "arbitrary", "arbitrary"),
        name=f"dil_attn_d{dilation}",
    )(t, t, t, t, t)
    return o, lse


def _pack_halves(v):
    w = v.shape[1] // 2
    lo = lax.bitcast_convert_type(v[:, :w].astype(BF16).astype(F32), jnp.uint32)
    hi = lax.bitcast_convert_type(v[:, w:].astype(BF16).astype(F32), jnp.uint32)
    return (lo >> 16) | (hi & jnp.uint32(0xFFFF0000))


ROW_TILE = 8


def _row_tile(p):
    return (pl.ds(pl.multiple_of(p * ROW_TILE, ROW_TILE), ROW_TILE), slice(None))


def _store_row_tiles(ref, pk):
    rows = pk.shape[0]
    for c in range(ROW_TILE):
        ref[pl.ds(c, rows, stride=ROW_TILE), :] = pk[:, c * LANES:(c + 1) * LANES]


def _load_row_tiles(ref, rows):
    return jnp.concatenate([ref[pl.ds(c, rows, stride=ROW_TILE), :] for c in range(ROW_TILE)], axis=1)


def _unpack_halves(pk):
    lo = lax.bitcast_convert_type(pk << 16, F32)
    hi = lax.bitcast_convert_type(pk & jnp.uint32(0xFFFF0000), F32)
    return lo, hi


def _merge_kernel(oa_ref, o0_ref, o1_ref, o2_ref, l0_ref, l1_ref, l2_ref, ga_ref, gb_ref, x_ref,
                  gate_ref, shift_ref, scale_ref, gpost_ref, gpre_ref,
                  wa_ref, wb_ref, wo_ref, wrh_ref, wrl_ref,
                  x1_ref, h2_ref, h2pk_ref, logit_ref, o_sc, l_sc):
    tm = x_ref.shape[0]

    def natural(ref, sc, gi, chunks):
        dil = DIL_PATTERNS[gi][1]
        if dil == 1:
            return [ref[0, :, c * LANES:(c + 1) * LANES].astype(F32) for c in range(chunks)]
        for r in range(dil):
            for c in range(chunks):
                c0 = (r * chunks + c) * LANES
                sc.at[gi, c][pl.ds(r, tm // dil, stride=dil), :] = ref[0, :, c0:c0 + LANES].astype(F32)
        return [sc[gi, c] for c in range(chunks)]

    (l0,), (l1,), (l2,) = [natural(ref, l_sc, gi, 1) for gi, ref in enumerate((l0_ref, l1_ref, l2_ref))]
    o0, o1, o2 = [natural(ref, o_sc, gi, DIL_HEADS_PER_GROUP) for gi, ref in enumerate((o0_ref, o1_ref, o2_ref))]
    m = jnp.maximum(jnp.maximum(l0, l1), l2)
    e0, e1, e2 = jnp.exp(l0 - m), jnp.exp(l1 - m), jnp.exp(l2 - m)
    inv = 1.0 / (e0 + e1 + e2)
    w0, w1, w2 = e0 * inv, e1 * inv, e2 * inv
    parts = []
    for hh in range(DIL_HEADS_PER_GROUP):
        parts.append(w0[:, hh:hh + 1] * o0[hh] + w1[:, hh:hh + 1] * o1[hh] + w2[:, hh:hh + 1] * o2[hh])
    o_dil = jnp.concatenate(parts, axis=1).astype(BF16)
    y_a = jnp.dot(oa_ref[...], wa_ref[...], preferred_element_type=F32)
    y_b = jnp.dot(o_dil, wb_ref[...], preferred_element_type=F32)
    merged = ga_ref[...].astype(F32) * y_a + gb_ref[...].astype(F32) * y_b
    y = jnp.dot(merged.astype(BF16), wo_ref[...], preferred_element_type=F32)
    x1 = x_ref[...] + gate_ref[0] * _rms(y, gpost_ref[...])
    x1_ref[...] = x1
    h2 = _rms(x1, gpre_ref[...]) * (1.0 + scale_ref[0]) + shift_ref[0]
    _store_row_tiles(h2pk_ref, _pack_halves(h2))
    h2_hi = h2.astype(BF16)
    h2_ref[...] = h2_hi
    h2_lo = (h2 - h2_hi.astype(F32)).astype(BF16)
    both = jnp.dot(h2_hi, wrh_ref[...], preferred_element_type=F32)
    logits = both[:, :LANES] + both[:, LANES:] + jnp.dot(h2_lo, wrl_ref[...], preferred_element_type=F32)
    logit_ref[...] = logits.T


def _merge(oa, dil_o, dil_lse, gates, x2, mod3, gpost, gpre, wa, wb, wo, wr_hi, wr_lo, seq, tm=256):
    n, d = x2.shape
    per_b = seq // tm
    row = lambda w: pl.BlockSpec((tm, w), lambda i: (i, 0))
    const = lambda a: pl.BlockSpec(a.shape, lambda i: (0,) * a.ndim, pipeline_mode=pl.Buffered(1))
    modspec = lambda ch: pl.BlockSpec((1, 1, d), lambda i: ((i // per_b) * ADALN_CHUNKS + ch, 0, 0))
    strided = lambda gi, w: pl.BlockSpec((1, tm // DIL_PATTERNS[gi][1], DIL_PATTERNS[gi][1] * w),
                                         lambda i: (i // per_b, i % per_b, 0))
    return pl.pallas_call(
        _merge_kernel,
        grid=(n // tm,),
        in_specs=[row(oa.shape[1]),
                  strided(0, DIL_GROUP_COLS), strided(1, DIL_GROUP_COLS), strided(2, DIL_GROUP_COLS),
                  strided(0, LANES), strided(1, LANES), strided(2, LANES),
                  pl.BlockSpec((tm, d), lambda i: (i, 0)), pl.BlockSpec((tm, d), lambda i: (i, 1)),
                  row(d),
                  modspec(2), modspec(3), modspec(4),
                  const(gpost), const(gpre),
                  const(wa), const(wb), const(wo), const(wr_hi), const(wr_lo)],
        out_specs=[row(d), row(d), pl.BlockSpec((tm * ROW_TILE, LANES), lambda i: (i, 0)),
                   pl.BlockSpec((LANES, tm), lambda i: (0, i))],
        out_shape=[jax.ShapeDtypeStruct((n, d), F32),
                   jax.ShapeDtypeStruct((n, d), BF16),
                   jax.ShapeDtypeStruct((n * ROW_TILE, LANES), jnp.uint32),
                   jax.ShapeDtypeStruct((LANES, n), F32)],
        scratch_shapes=[pltpu.VMEM((DIL_GROUPS, DIL_HEADS_PER_GROUP, tm, LANES), F32),
                        pltpu.VMEM((DIL_GROUPS, 1, tm, LANES), F32)],
        compiler_params=_cparams("arbitrary"),
        name="merge_outproj",
    )(oa, *dil_o, *dil_lse, gates, gates, x2, mod3, mod3, mod3, gpost, gpre, wa, wb, wo, wr_hi, wr_lo)


def _route_kernel(lg_ref, bias_ref, pos_ref, w_ref, meta_ref, blke_ref, cnt_sc, base_sc, rank_sc, score_sc, *, tt):
    ps = pl.program_id(0)
    i = pl.program_id(1)
    per_group = N_EXPERTS // N_EXPERT_GROUPS
    neg_inf = -jnp.inf
    tile = pl.ds(pl.multiple_of(i * tt, tt), tt)

    @pl.when(jnp.logical_and(ps == 0, i == 0))
    def _():
        cnt_sc[...] = jnp.zeros_like(cnt_sc)

    @pl.when(ps == 0)
    def _():
        scores = _sigmoid(lg_ref[...])
        biased = scores + bias_ref[...]
        b3 = biased.reshape(N_EXPERT_GROUPS, per_group, tt)
        mem = lax.broadcasted_iota(jnp.int32, b3.shape, 1)
        m1 = jnp.max(b3, axis=1, keepdims=True)
        first = jnp.min(jnp.where(b3 == m1, mem, per_group), axis=1, keepdims=True)
        m2 = jnp.max(jnp.where(mem == first, neg_inf, b3), axis=1, keepdims=True)
        gs = m1 + m2
        gidx = lax.broadcasted_iota(jnp.int32, gs.shape, 0)
        grank = jnp.zeros(gs.shape, jnp.int32)
        for g2 in range(N_EXPERT_GROUPS):
            r = gs[g2:g2 + 1]
            beats = jnp.logical_or(r > gs, jnp.logical_and(r == gs, g2 < gidx))
            grank = grank + jnp.where(beats, 1, 0)
        sel = jnp.where(grank < TOPK_GROUPS, b3, neg_inf).reshape(N_EXPERTS, tt)
        eidx = lax.broadcasted_iota(jnp.int32, sel.shape, 0)
        erank = jnp.zeros(sel.shape, jnp.int32)
        for e2 in range(N_EXPERTS):
            r = sel[e2:e2 + 1, :]
            beats = jnp.logical_or(r > sel, jnp.logical_and(r == sel, e2 < eidx))
            erank = erank + jnp.where(beats, 1, 0)
        rank_sc[:, tile] = erank
        score_sc[:, tile] = scores
        cnt_sc[...] = cnt_sc[...] + jnp.sum(jnp.where(erank < TOP_K, 1.0, 0.0), axis=1,
                                            keepdims=True).astype(jnp.int32)

    @pl.when(jnp.logical_and(ps == 1, i == 0))
    def _():
        cnt = cnt_sc[...]
        pc = ((cnt + (MOE_BLOCK - 1)) // MOE_BLOCK) * MOE_BLOCK
        pcb = jnp.broadcast_to(pc, (N_EXPERTS, LANES))
        eid = lax.broadcasted_iota(jnp.int32, (N_EXPERTS, LANES), 0)
        pends = jnp.zeros((N_EXPERTS, LANES), jnp.int32)
        for e2 in range(N_EXPERTS):
            pends = pends + jnp.where(eid >= e2, pcb[e2:e2 + 1, :], 0)
        pst = pends - pcb
        base_sc[...] = pst[:, 0:1]
        meta_ref[0] = jnp.broadcast_to(cnt, (N_EXPERTS, LANES))
        meta_ref[1] = pst
        meta_ref[2] = pends
        nbl = blke_ref.shape[1]
        blk_start = lax.broadcasted_iota(jnp.int32, (N_EXPERTS, nbl), 1) * MOE_BLOCK
        pend_b = jnp.broadcast_to(pends[:, 0:1], (N_EXPERTS, nbl))
        be = jnp.sum(jnp.where(pend_b <= blk_start, 1, 0), axis=0, keepdims=True)
        blke_ref[...] = jnp.broadcast_to(jnp.minimum(be, N_EXPERTS - 1), blke_ref.shape)

    @pl.when(ps == 1)
    def _():
        erank = rank_sc[:, tile]
        scores = score_sc[:, tile]
        esel = erank < TOP_K
        mask_f = jnp.where(esel, 1.0, 0.0)
        tile_cnt = jnp.sum(mask_f, axis=1, keepdims=True).astype(jnp.int32)
        rr = lax.broadcasted_iota(jnp.int32, (tt, tt), 0)
        cc = lax.broadcasted_iota(jnp.int32, (tt, tt), 1)
        upper = jnp.where(rr < cc, 1.0, 0.0).astype(BF16)
        prefix = jnp.dot(mask_f.astype(BF16), upper, preferred_element_type=F32)
        posd = base_sc[...] + prefix.astype(jnp.int32)
        base_sc[...] = base_sc[...] + tile_cnt
        wsel = jnp.where(esel, scores, 0.0)
        denom = jnp.sum(wsel, axis=0, keepdims=True)
        wn = wsel / (denom + 1e-20) * ROUTED_SCALE
        prow, wrow = [], []
        for kk in range(TOP_K):
            hit = erank == kk
            prow.append(jnp.sum(jnp.where(hit, posd, 0), axis=0, keepdims=True))
            wrow.append(jnp.sum(jnp.where(hit, wn, 0.0), axis=0, keepdims=True))
        pad = pos_ref.shape[0] - TOP_K
        pos_ref[...] = jnp.concatenate(prow + [jnp.zeros((pad, tt), jnp.int32)], axis=0)
        w_ref[...] = jnp.concatenate(wrow + [jnp.zeros((pad, tt), F32)], axis=0)


def _route(logits_t, bias_col, nb, tt=256):
    n = logits_t.shape[1]
    nbl = -(-nb // LANES) * LANES
    return pl.pallas_call(
        functools.partial(_route_kernel, tt=tt),
        grid=(2, n // tt),
        in_specs=[pl.BlockSpec((N_EXPERTS, tt), lambda ps, i: (0, i * (1 - ps))),
                  pl.BlockSpec((N_EXPERTS, 1), lambda ps, i: (0, 0))],
        out_specs=[pl.BlockSpec((8, tt), lambda ps, i: (0, ps * i)),
                   pl.BlockSpec((8, tt), lambda ps, i: (0, ps * i)),
                   pl.BlockSpec((3, N_EXPERTS, LANES), lambda ps, i: (0, 0, 0)),
                   pl.BlockSpec((8, nbl), lambda ps, i: (0, 0))],
        out_shape=[jax.ShapeDtypeStruct((8, n), jnp.int32),
                   jax.ShapeDtypeStruct((8, n), F32),
                   jax.ShapeDtypeStruct((3, N_EXPERTS, LANES), jnp.int32),
                   jax.ShapeDtypeStruct((8, nbl), jnp.int32)],
        scratch_shapes=[pltpu.VMEM((N_EXPERTS, 1), jnp.int32), pltpu.VMEM((N_EXPERTS, 1), jnp.int32),
                        pltpu.VMEM((N_EXPERTS, n), jnp.int32), pltpu.VMEM((N_EXPERTS, n), F32)],
        compiler_params=_cparams("arbitrary", "arbitrary"),
        name="moe_route",
    )(logits_t, bias_col)


def _dispatch_kernel(pos_ref, cnt_ref, pst_ref, h_ref, hb_ref, wsg_ref, wsu_ref, wsd_ref, xs_hbm, ysh_ref,
                     zrow, sem, *, tm, n):
    i = pl.program_id(0)

    def body(g, carry):
        for u in range(ROW_DMA_UNROLL):
            r = g * ROW_DMA_UNROLL + u
            for kk in range(TOP_K):
                p = pos_ref[kk * n + i * tm + r]
                pltpu.make_async_copy(h_ref.at[_row_tile(r)], xs_hbm.at[_row_tile(p)],
                                      sem.at[0]).start(priority=kk % 2)
        return carry
    lax.fori_loop(0, tm // ROW_DMA_UNROLL, body, 0)

    @pl.when(i == pl.num_programs(0) - 1)
    def _():
        zrow[...] = jnp.zeros_like(zrow)

        def per_expert(e, carry):
            cnt = cnt_ref[e]
            first = pst_ref[e] + cnt
            npad = ((cnt + (MOE_BLOCK - 1)) // MOE_BLOCK) * MOE_BLOCK - cnt

            def start(s, c2):
                pltpu.make_async_copy(zrow.at[_row_tile(0)], xs_hbm.at[_row_tile(first + s)], sem.at[1]).start()
                return c2

            def wait(s, c2):
                pltpu.make_async_copy(zrow.at[_row_tile(0)], xs_hbm.at[_row_tile(0)], sem.at[1]).wait()
                return c2
            lax.fori_loop(0, npad, start, 0)
            lax.fori_loop(0, npad, wait, 0)
            return carry
        lax.fori_loop(0, N_EXPERTS, per_expert, 0)

        last = N_EXPERTS - 1
        used = (pst_ref[last] + cnt_ref[last] + (MOE_BLOCK - 1)) // MOE_BLOCK
        blk_rows = MOE_BLOCK * ROW_TILE

        def tail(b, carry):
            cp = pltpu.make_async_copy(zrow, xs_hbm.at[pl.ds(pl.multiple_of(b * blk_rows, blk_rows), blk_rows), :],
                                       sem.at[1])
            cp.start()
            cp.wait()
            return carry
        lax.fori_loop(used, xs_hbm.shape[0] // blk_rows, tail, 0)

    hb = hb_ref[...]
    g = jnp.dot(hb, wsg_ref[...], preferred_element_type=F32)
    u = jnp.dot(hb, wsu_ref[...], preferred_element_type=F32)
    ysh_ref[...] = jnp.dot((g * _sigmoid(g) * u).astype(BF16), wsd_ref[...],
                           preferred_element_type=F32).astype(BF16)

    for kk in range(TOP_K):
        pltpu.make_async_copy(h_ref, xs_hbm.at[pl.ds(0, tm * ROW_TILE), :], sem.at[0]).wait()


def _dispatch(pos_flat, cnt, pst, h2pk, h2, wsg, wsu, wsd, nb, tm=256):
    n, d = h2.shape
    const = lambda a: pl.BlockSpec(a.shape, lambda i, *_: (0,) * a.ndim)
    grid_spec = pltpu.PrefetchScalarGridSpec(
        num_scalar_prefetch=3,
        grid=(n // tm,),
        in_specs=[pl.BlockSpec((tm * ROW_TILE, LANES), lambda i, *_: (i, 0)),
                  pl.BlockSpec((tm, d), lambda i, *_: (i, 0)),
                  const(wsg), const(wsu), const(wsd)],
        out_specs=[pl.BlockSpec(memory_space=pl.ANY), pl.BlockSpec((tm, d), lambda i, *_: (i, 0))],
        scratch_shapes=[pltpu.VMEM((MOE_BLOCK * ROW_TILE, LANES), jnp.uint32), pltpu.SemaphoreType.DMA((2,))],
    )
    return pl.pallas_call(
        functools.partial(_dispatch_kernel, tm=tm, n=n),
        grid_spec=grid_spec,
        out_shape=[jax.ShapeDtypeStruct((nb * MOE_BLOCK * ROW_TILE, LANES), jnp.uint32),
                   jax.ShapeDtypeStruct((n, d), BF16)],
        compiler_params=_cparams("arbitrary"),
        name="moe_dispatch",
    )(pos_flat, cnt, pst, h2pk, h2, wsg, wsu, wsd)


def _gmm_kernel(nact_ref, blke_ref, xs_ref, wg_hbm, wu_hbm, wd_hbm, o_ref,
                wgf, wuf, wdf, wgb, wub, wdb, run_sc, sem):
    i = pl.program_id(0)
    nact = nact_ref[0]

    def fetch(e, slot):
        return (pltpu.make_async_copy(wg_hbm.at[e], wgf.at[slot], sem.at[slot, 0]),
                pltpu.make_async_copy(wu_hbm.at[e], wuf.at[slot], sem.at[slot, 1]),
                pltpu.make_async_copy(wd_hbm.at[e], wdf.at[slot], sem.at[slot, 2]))

    @pl.when(i == 0)
    def _():
        run_sc[0] = 0
        for cp in fetch(blke_ref[0], 0):
            cp.start()

    @pl.when(i < nact)
    def _():
        e = blke_ref[i]
        changed = jnp.logical_or(i == 0, e != blke_ref[jnp.maximum(i - 1, 0)])

        @pl.when(changed)
        def _():
            run = run_sc[0]
            slot = run % 2
            for cp in fetch(e, slot):
                cp.wait()
            wgb[...] = wgf[slot].astype(BF16)
            wub[...] = wuf[slot].astype(BF16)
            wdb[...] = wdf[slot].astype(BF16)
            nxt = lax.while_loop(lambda j: jnp.logical_and(j < nact, blke_ref[jnp.minimum(j, nact - 1)] == e),
                                 lambda j: j + 1, i + 1)

            @pl.when(nxt < nact)
            def _():
                for cp in fetch(blke_ref[jnp.minimum(nxt, nact - 1)], 1 - slot):
                    cp.start(priority=1)
            run_sc[0] = run + 1

        lo, hi = _unpack_halves(_load_row_tiles(xs_ref, MOE_BLOCK))
        lo, hi = lo.astype(BF16), hi.astype(BF16)
        half = lo.shape[1]
        g = (jnp.dot(lo, wgb[:half, :], preferred_element_type=F32)
             + jnp.dot(hi, wgb[half:, :], preferred_element_type=F32))
        u = (jnp.dot(lo, wub[:half, :], preferred_element_type=F32)
             + jnp.dot(hi, wub[half:, :], preferred_element_type=F32))
        a = (g * _sigmoid(g) * u).astype(BF16)
        _store_row_tiles(o_ref, _pack_halves(jnp.dot(a, wdb[...], preferred_element_type=F32)))

    @pl.when(i >= nact_ref[0])
    def _():
        o_ref[...] = jnp.zeros_like(o_ref)


def _gmm(nact, blk_e, xs, w_gate, w_up, w_down, nb):
    d, f = w_gate.shape[1:]
    blk = lambda i, na: jnp.minimum(i, na[0] - 1)
    hbm = pl.BlockSpec(memory_space=pl.ANY)
    grid_spec = pltpu.PrefetchScalarGridSpec(
        num_scalar_prefetch=2,
        grid=(nb,),
        in_specs=[pl.BlockSpec((MOE_BLOCK * ROW_TILE, LANES), lambda i, na, be: (blk(i, na), 0)), hbm, hbm, hbm],
        out_specs=pl.BlockSpec((MOE_BLOCK * ROW_TILE, LANES), lambda i, na, be: (i, 0)),
        scratch_shapes=[pltpu.VMEM((2, d, f), F32), pltpu.VMEM((2, d, f), F32), pltpu.VMEM((2, f, d), F32),
                        pltpu.VMEM((d, f), BF16), pltpu.VMEM((d, f), BF16), pltpu.VMEM((f, d), BF16),
                        pltpu.SMEM((1,), jnp.int32), pltpu.SemaphoreType.DMA((2, 3))],
    )
    return pl.pallas_call(
        _gmm_kernel,
        grid_spec=grid_spec,
        out_shape=jax.ShapeDtypeStruct((nb * MOE_BLOCK * ROW_TILE, LANES), jnp.uint32),
        compiler_params=_cparams("arbitrary"),
        name="moe_experts",
    )(nact, blk_e, xs, w_gate, w_up, w_down)


def _final_kernel(pos_ref, tw_ref, ysh_ref, x1_ref, gate_ref, gpost_ref, yb_hbm, o_ref, rbuf, sem, *, tm, n):
    i = pl.program_id(0)
    slot = i % 2

    def gather(tile, sl):
        def body(g, carry):
            for u in range(ROW_DMA_UNROLL):
                r = g * ROW_DMA_UNROLL + u
                for kk in range(TOP_K):
                    p = pos_ref[kk * n + tile * tm + r]
                    pltpu.make_async_copy(yb_hbm.at[_row_tile(p)], rbuf.at[sl, kk].at[_row_tile(r)],
                                          sem.at[sl]).start(priority=kk % 2)
            return carry
        lax.fori_loop(0, tm // ROW_DMA_UNROLL, body, 0)

    @pl.when(i == 0)
    def _():
        gather(0, 0)

    @pl.when(i + 1 < pl.num_programs(0))
    def _():
        gather(i + 1, 1 - slot)

    for kk in range(TOP_K):
        pltpu.make_async_copy(yb_hbm.at[pl.ds(0, tm * ROW_TILE), :], rbuf.at[slot, kk], sem.at[slot]).wait()
    tw = tw_ref[...]
    half = ROW_TILE * LANES
    r_lo = jnp.zeros((tm, half), F32)
    r_hi = jnp.zeros((tm, half), F32)
    for kk in range(TOP_K):
        lo, hi = _unpack_halves(_load_row_tiles(rbuf.at[slot, kk], tm))
        wk = tw[:, kk:kk + 1]
        r_lo = r_lo + wk * lo
        r_hi = r_hi + wk * hi
    y = ysh_ref[...].astype(F32) + jnp.concatenate([r_lo, r_hi], axis=1)
    o_ref[...] = x1_ref[...] + gate_ref[0] * _rms(y, gpost_ref[...])


def _final(pos_flat, top_w8, ysh, x1, mod3, gpost, yb, seq, tm=256):
    n, d = x1.shape
    per_b = seq // tm
    const = lambda a: pl.BlockSpec(a.shape, lambda i, ps: (0,) * a.ndim)
    row = lambda w: pl.BlockSpec((tm, w), lambda i, ps: (i, 0))
    grid_spec = pltpu.PrefetchScalarGridSpec(
        num_scalar_prefetch=1,
        grid=(n // tm,),
        in_specs=[row(top_w8.shape[1]), row(d), row(d),
                  pl.BlockSpec((1, 1, d), lambda i, ps: ((i // per_b) * ADALN_CHUNKS + 5, 0, 0)),
                  const(gpost),
                  pl.BlockSpec(memory_space=pl.ANY)],
        out_specs=row(d),
        scratch_shapes=[pltpu.VMEM((2, TOP_K, tm * ROW_TILE, LANES), jnp.uint32), pltpu.SemaphoreType.DMA((2,))],
    )
    return pl.pallas_call(
        functools.partial(_final_kernel, tm=tm, n=n),
        grid_spec=grid_spec,
        out_shape=jax.ShapeDtypeStruct((n, d), F32),
        compiler_params=_cparams("arbitrary"),
        name="moe_combine_final",
    )(pos_flat, top_w8, ysh, x1, mod3, gpost, yb)


def _rope_tables(positions, dim, passthrough):
    half = dim // 2
    inv_freq = 1.0 / (ROPE_THETA ** (jnp.arange(0, dim, 2, dtype=F32) / dim))
    freq = jnp.concatenate([inv_freq, inv_freq, jnp.zeros((LANES - dim,), F32)])
    ang = positions.astype(F32).reshape(-1, 1) * freq[None, :]
    cos, sin = jnp.cos(ang), jnp.sin(ang)
    lane = jnp.arange(LANES)[None, :]
    c_tab = jnp.where(lane < dim, cos, 1.0 if passthrough else 0.0)
    s_fwd = jnp.where(jnp.logical_and(lane >= half, lane < dim), sin, 0.0)
    s_bwd = jnp.where(lane < half, -sin, 0.0)
    return c_tab, s_fwd, s_bwd


def kernel(x, c, positions, w_ada, b_ada, attn_pre_g, w_in, q_a_norm_g, w_q_up, kv_a_norm_g, w_kv_up, w_mla_o, w_dil_o, w_out, attn_post_g, ffn_pre_g, w_router, router_bias, w_exp_gate, w_exp_up, w_exp_down, w_sh_gate, w_sh_up, w_sh_down, ffn_post_g):
    batch, seq, d = x.shape
    n = batch * seq
    depth = w_ada.shape[0]

    m_c, m_sf, m_sb = _rope_tables(positions, QK_ROPE_DIM, passthrough=False)
    d_c, d_sf, d_sb = _rope_tables(positions, DIL_ROT_DIM, passthrough=True)

    x2 = x.reshape(n, d)
    c8 = jnp.pad(c, ((0, 8 - batch), (0, 0)))
    for l in range(depth):
        mod = _ada(c8, w_ada[l], b_ada[l].reshape(1, -1))
        mod3 = mod[:batch].reshape(batch * ADALN_CHUNKS, 1, d)

        wi = w_in[l]
        o_dil = Q_LORA_RANK + KV_LORA_RANK + QK_ROPE_DIM
        o_ga = o_dil + 3 * DIL_HEADS * DIL_HEAD_DIM
        n_gate, n_dil = wi.shape[1] - o_ga, o_ga - o_dil
        n_a = o_dil + LANES - QK_ROPE_DIM
        c_dil, c_a = n_gate, -(-(n_gate + n_dil) // n_a) * n_a
        w_full = jnp.concatenate([wi[:, o_ga:], wi[:, o_dil:o_ga], jnp.zeros((d, c_a - n_gate - n_dil), F32),
                                  wi[:, :o_dil], jnp.zeros((d, LANES - QK_ROPE_DIM), F32)], axis=1).astype(BF16)
        wq3 = w_q_up[l].reshape(Q_LORA_RANK, MLA_HEADS, MLA_QK_DIM)
        wq = jnp.concatenate([wq3, jnp.zeros((Q_LORA_RANK, MLA_HEADS, MLA_QK_PAD - MLA_QK_DIM), F32)],
                             axis=2).reshape(Q_LORA_RANK, MLA_HEADS * MLA_QK_PAD).astype(BF16)
        wkv3 = w_kv_up[l].reshape(KV_LORA_RANK, MLA_HEADS, QK_NOPE_DIM + V_HEAD_DIM)
        wkv = jnp.concatenate([wkv3[:, :, :QK_NOPE_DIM].reshape(KV_LORA_RANK, -1),
                               wkv3[:, :, QK_NOPE_DIM:].reshape(KV_LORA_RANK, -1)], axis=1).astype(BF16)

        h = _prenorm(x2, attn_pre_g[l].reshape(1, d), mod3, seq)
        a = _mm(h, w_full, c_a, n_a, tn=n_a)
        gates = _mm(h, w_full, 0, n_gate, act="sigmoid")
        q, k, v = _mlaprep(a, q_a_norm_g[l].reshape(1, -1), kv_a_norm_g[l].reshape(1, -1), wq, wkv,
                           m_c, m_sf, m_sb, batch, seq)
        o_mla = _mla_attn(q, k, v).reshape(n, MLA_HEADS * V_HEAD_DIM)
        dil_o, dil_lse = [], []
        for g, (_, dilation) in enumerate(DIL_PATTERNS):
            qkv = _dilproj(h, w_full, c_dil, g, d_c, d_sf, d_sb, batch, seq, dilation)
            o_g, lse_g = _dil_attn(qkv, batch, seq, dilation)
            dil_o.append(o_g)
            dil_lse.append(lse_g)

        wr = jnp.pad(w_router[l], ((0, 0), (0, LANES - N_EXPERTS)))
        wr_hi = wr.astype(BF16)
        wr_lo = (wr - wr_hi.astype(F32)).astype(BF16)
        x1, h2, h2pk, logits_t = _merge(o_mla, dil_o, dil_lse, gates, x2, mod3,
                                        attn_post_g[l].reshape(1, d), ffn_pre_g[l].reshape(1, d),
                                        w_mla_o[l].astype(BF16), w_dil_o[l].astype(BF16), w_out[l].astype(BF16),
                                        jnp.concatenate([wr_hi, wr_lo], axis=1), wr_hi, seq)

        nb = -(-(n * TOP_K + N_EXPERTS * (MOE_BLOCK - 1)) // MOE_BLOCK)
        pos_t, w_t, meta, blk_e = _route(logits_t, router_bias[l].astype(F32).reshape(N_EXPERTS, 1), nb)
        pos_flat = pos_t.reshape(-1)
        nact = meta[2, N_EXPERTS - 1, :1] // MOE_BLOCK
        xs, ysh = _dispatch(pos_flat, meta[0, :, 0], meta[1, :, 0], h2pk, h2,
                            w_sh_gate[l].astype(BF16), w_sh_up[l].astype(BF16), w_sh_down[l].astype(BF16), nb)
        yb = _gmm(nact, blk_e[0], xs, w_exp_gate[l], w_exp_up[l], w_exp_down[l], nb)
        x2 = _final(pos_flat, w_t.T, ysh, x1, mod3, ffn_post_g[l].reshape(1, d), yb, seq)
    return x2.reshape(batch, seq, d)
```

```python
import functools

import jax
import jax.numpy as jnp
from jax import lax
from jax.experimental import pallas as pl
from jax.experimental.pallas import tpu as pltpu

F32 = jnp.float32
BF16 = jnp.bfloat16

D_MODEL = 2048
NORM_EPS = 1e-6
ROPE_THETA = 500000.0
ADALN_CHUNKS = 6

MLA_HEADS = 8
Q_LORA_RANK = 512
KV_LORA_RANK = 512
QK_NOPE_DIM = 128
QK_ROPE_DIM = 64
V_HEAD_DIM = 128
MLA_QK_DIM = QK_NOPE_DIM + QK_ROPE_DIM
MLA_QK_PAD = 256

DIL_PATTERNS = ((128, 1), (512, 4), (2048, 16))
DIL_GROUPS = len(DIL_PATTERNS)
DIL_HEADS_PER_GROUP = 4
DIL_HEADS = DIL_GROUPS * DIL_HEADS_PER_GROUP
DIL_HEAD_DIM = 128
DIL_ROT_DIM = DIL_HEAD_DIM // 4
DIL_SPAN = 128
DIL_GROUP_COLS = DIL_HEADS_PER_GROUP * DIL_HEAD_DIM

N_EXPERTS = 64
N_EXPERT_GROUPS = 8
TOPK_GROUPS = 4
TOP_K = 6
EXPERT_DIM = 512
SHARED_DIM = 512
ROUTED_SCALE = 2.5
MOE_BLOCK = 256

LANES = 128
NEG_BIG = -1e30
LOG2_E = 1.4426950408889634
ROW_DMA_UNROLL = 4
GMM_SUB = 2
VMEM_LIMIT = 56 * 1024 * 1024


def _cparams(*sem):
    return pltpu.CompilerParams(dimension_semantics=sem, vmem_limit_bytes=VMEM_LIMIT)


def _sigmoid(v):
    return 1.0 / (1.0 + jnp.exp(-v))


def _rms(v, g):
    ms = jnp.mean(v * v, axis=-1, keepdims=True)
    return v * lax.rsqrt(ms + NORM_EPS) * g


def _ada_kernel(c_ref, w_ref, b_ref, o_ref):
    c = c_ref[...]
    a = (c * _sigmoid(c)).astype(BF16)
    o_ref[...] = jnp.dot(a, w_ref[...].astype(BF16), preferred_element_type=F32) + b_ref[...]


def _ada(c8, w_ada, b_ada, tn=1536):
    d, n = w_ada.shape
    return pl.pallas_call(
        _ada_kernel,
        grid=(n // tn,),
        in_specs=[pl.BlockSpec((8, d), lambda j: (0, 0)),
                  pl.BlockSpec((d, tn), lambda j: (0, j)),
                  pl.BlockSpec((1, tn), lambda j: (0, j))],
        out_specs=pl.BlockSpec((8, tn), lambda j: (0, j)),
        out_shape=jax.ShapeDtypeStruct((8, n), F32),
        compiler_params=_cparams("arbitrary"),
        name="ada_mod",
    )(c8, w_ada, b_ada)


def _wprep_kernel(w_ref, o_ref, *, segments):
    rows = w_ref.shape[0]
    half = LANES // 2
    filled = 0
    for dst, src, width in segments:
        assert dst >= filled and dst % LANES == 0 and src % half == 0 and width % half == 0
        if dst > filled:
            o_ref[:, filled:dst] = jnp.zeros((rows, dst - filled), BF16)
        for j in range(-(-width // LANES)):
            lo = src + j * LANES
            take = min(LANES, width - j * LANES)
            if lo % LANES == 0:
                piece = w_ref[:, lo:lo + take]
            else:
                first = w_ref[:, lo - half:lo + half][:, half:]
                piece = first if take == half else jnp.concatenate([first, w_ref[:, lo + half:lo + LANES][:, :half]],
                                                                   axis=1)
            if take < LANES:
                piece = jnp.concatenate([piece, jnp.zeros((rows, LANES - take), F32)], axis=1)
            o_ref[:, dst + j * LANES:dst + (j + 1) * LANES] = piece.astype(BF16)
        filled = dst + -(-width // LANES) * LANES
    if filled < o_ref.shape[1]:
        o_ref[:, filled:] = jnp.zeros((rows, o_ref.shape[1] - filled), BF16)


def _wprep(w, segments, cols, tk=256):
    k, src_cols = w.shape
    return pl.pallas_call(
        functools.partial(_wprep_kernel, segments=segments),
        grid=(k // tk,),
        in_specs=[pl.BlockSpec((tk, src_cols), lambda i: (i, 0))],
        out_specs=pl.BlockSpec((tk, cols), lambda i: (i, 0)),
        out_shape=jax.ShapeDtypeStruct((k, cols), BF16),
        compiler_params=_cparams("arbitrary"),
        name="w_in_prep",
    )(w)


def _prenorm_kernel(x_ref, g_ref, scale_ref, shift_ref, o_ref):
    xn = _rms(x_ref[...], g_ref[...])
    o_ref[...] = (xn * (1.0 + scale_ref[0]) + shift_ref[0]).astype(BF16)


def _prenorm(x2, g, mod3, seq, tm=512):
    n, d = x2.shape
    per_b = seq // tm
    return pl.pallas_call(
        _prenorm_kernel,
        grid=(n // tm,),
        in_specs=[pl.BlockSpec((tm, d), lambda i: (i, 0)),
                  pl.BlockSpec((1, d), lambda i: (0, 0)),
                  pl.BlockSpec((1, 1, d), lambda i: ((i // per_b) * ADALN_CHUNKS + 1, 0, 0)),
                  pl.BlockSpec((1, 1, d), lambda i: ((i // per_b) * ADALN_CHUNKS + 0, 0, 0))],
        out_specs=pl.BlockSpec((tm, d), lambda i: (i, 0)),
        out_shape=jax.ShapeDtypeStruct((n, d), BF16),
        compiler_params=_cparams("arbitrary"),
        name="prenorm_attn",
    )(x2, g, mod3, mod3)


def _mm_kernel(h_ref, w_ref, o_ref, *, act):
    y = jnp.dot(h_ref[...], w_ref[...], preferred_element_type=F32)
    if act == "sigmoid":
        y = _sigmoid(y)
    o_ref[...] = y.astype(o_ref.dtype)


def _mm(h, w, col0, cols, act=None, tm=1024, tn=1024):
    n, k = h.shape
    tn = min(tn, cols)
    j0 = col0 // tn
    return pl.pallas_call(
        functools.partial(_mm_kernel, act=act),
        grid=(cols // tn, n // tm),
        in_specs=[pl.BlockSpec((tm, k), lambda j, i: (i, 0)),
                  pl.BlockSpec((k, tn), lambda j, i: (0, j0 + j))],
        out_specs=pl.BlockSpec((tm, tn), lambda j, i: (i, j)),
        out_shape=jax.ShapeDtypeStruct((n, cols), BF16),
        compiler_params=_cparams("arbitrary", "arbitrary"),
        name="in_proj_" + (act or "plain"),
    )(h, w)


def _rope_lanes(t, c_tab, s_fwd, s_bwd, half):
    return t * c_tab + pltpu.roll(t, half, 1) * s_fwd + pltpu.roll(t, LANES - half, 1) * s_bwd


def _dilproj_kernel(h_ref, wq_ref, wk_ref, wv_ref, c_ref, sf_ref, sb_ref, o_ref, y_sc, *, dilation):
    hb = h_ref[...]
    y = jnp.concatenate([jnp.dot(hb, w_ref[...], preferred_element_type=F32) for w_ref in (wq_ref, wk_ref, wv_ref)],
                        axis=1)
    c_tab, s_fwd, s_bwd = c_ref[...], sf_ref[...], sb_ref[...]
    cols = 3 * DIL_GROUP_COLS
    rows = h_ref.shape[0] // dilation
    n_rot = 2 * DIL_HEADS_PER_GROUP
    for hh in range(3 * DIL_HEADS_PER_GROUP):
        t = y[:, hh * LANES:(hh + 1) * LANES]
        if hh < n_rot:
            t = _rope_lanes(t, c_tab, s_fwd, s_bwd, DIL_ROT_DIM // 2)
        if dilation == 1:
            o_ref[0, :, hh * LANES:(hh + 1) * LANES] = t.astype(BF16)
        else:
            y_sc[hh] = t
    if dilation > 1:
        for r in range(dilation):
            for hh in range(3 * DIL_HEADS_PER_GROUP):
                c0 = r * cols + hh * LANES
                o_ref[0, :, c0:c0 + LANES] = y_sc.at[hh][pl.ds(r, rows, stride=dilation), :].astype(BF16)


def _dilproj(h, w, col0, group, c_tab, s_fwd, s_bwd, batch, seq, dilation, tm=1024):
    n, k = h.shape
    gc = DIL_GROUP_COLS
    cols = 3 * gc
    per_b = seq // tm
    tab = pl.BlockSpec((tm, LANES), lambda i: (i, 0))
    part = lambda which: pl.BlockSpec((k, gc), lambda i: (0, col0 // gc + which * DIL_GROUPS + group))
    return pl.pallas_call(
        functools.partial(_dilproj_kernel, dilation=dilation),
        grid=(n // tm,),
        in_specs=[pl.BlockSpec((tm, k), lambda i: (i, 0)),
                  part(0), part(1), part(2),
                  tab, tab, tab],
        out_specs=pl.BlockSpec((1, tm // dilation, dilation * cols), lambda i: (i // per_b, i % per_b, 0)),
        out_shape=jax.ShapeDtypeStruct((batch, seq // dilation, dilation * cols), BF16),
        scratch_shapes=[pltpu.VMEM((cols // LANES, tm, LANES), F32)],
        compiler_params=_cparams("arbitrary"),
        name="dil_proj",
    )(h, w, w, w, c_tab, s_fwd, s_bwd)


def _mlaprep_kernel(a_ref, gq_ref, gkv_ref, wq_ref, wkv_ref, c_ref, sf_ref, sb_ref,
                    q_ref, k_ref, v_ref):
    a = a_ref[...].astype(F32)
    qa = a[:, :Q_LORA_RANK]
    ckv = a[:, Q_LORA_RANK:Q_LORA_RANK + KV_LORA_RANK]
    kr = a[:, Q_LORA_RANK + KV_LORA_RANK:]
    c_tab, s_fwd, s_bwd = c_ref[...], sf_ref[...], sb_ref[...]
    half = QK_ROPE_DIM // 2
    q = jnp.dot(_rms(qa, gq_ref[...]).astype(BF16), wq_ref[...], preferred_element_type=F32)
    q = q * (MLA_QK_DIM ** -0.5 * LOG2_E)
    kv = jnp.dot(_rms(ckv, gkv_ref[...]).astype(BF16), wkv_ref[...], preferred_element_type=F32)
    k_rot = _rope_lanes(kr, c_tab, s_fwd, s_bwd, half).astype(BF16)
    lane = lax.broadcasted_iota(jnp.int32, (a.shape[0], LANES), 1)
    ones_col = jnp.where(lane == 0, 1.0, 0.0).astype(BF16)
    for hh in range(MLA_HEADS):
        base = hh * MLA_QK_PAD
        q_ref[0, hh, :, :LANES] = q[:, base:base + LANES].astype(BF16)
        q_ref[0, hh, :, LANES:] = _rope_lanes(q[:, base + LANES:base + 2 * LANES],
                                              c_tab, s_fwd, s_bwd, half).astype(BF16)
        k_ref[0, hh, :, :LANES] = kv[:, hh * LANES:(hh + 1) * LANES].astype(BF16)
        k_ref[0, hh, :, LANES:] = k_rot
        v_off = MLA_HEADS * LANES + hh * LANES
        v_ref[0, hh, :, :LANES] = kv[:, v_off:v_off + LANES].astype(BF16)
        v_ref[0, hh, :, LANES:] = ones_col


def _mlaprep(a, gq, gkv, wq, wkv, c_tab, s_fwd, s_bwd, batch, seq, tm=512):
    n, cols = a.shape
    per_b = seq // tm
    tab = pl.BlockSpec((tm, LANES), lambda i: (i, 0))
    head_major = lambda w: pl.BlockSpec((1, MLA_HEADS, tm, w), lambda i: (i // per_b, 0, i % per_b, 0))
    return pl.pallas_call(
        _mlaprep_kernel,
        grid=(n // tm,),
        in_specs=[pl.BlockSpec((tm, cols), lambda i: (i, 0)),
                  pl.BlockSpec((1, Q_LORA_RANK), lambda i: (0, 0)),
                  pl.BlockSpec((1, KV_LORA_RANK), lambda i: (0, 0)),
                  pl.BlockSpec(wq.shape, lambda i: (0, 0)),
                  pl.BlockSpec(wkv.shape, lambda i: (0, 0)),
                  tab, tab, tab],
        out_specs=[head_major(MLA_QK_PAD), head_major(MLA_QK_PAD), head_major(2 * V_HEAD_DIM)],
        out_shape=[jax.ShapeDtypeStruct((batch, MLA_HEADS, seq, MLA_QK_PAD), BF16),
                   jax.ShapeDtypeStruct((batch, MLA_HEADS, seq, MLA_QK_PAD), BF16),
                   jax.ShapeDtypeStruct((batch, MLA_HEADS, seq, 2 * V_HEAD_DIM), BF16)],
        compiler_params=_cparams("arbitrary"),
        name="mla_prep",
    )(a, gq, gkv, wq, wkv, c_tab, s_fwd, s_bwd)


def _mla_attn_kernel(q_ref, k_ref, v_ref, o_ref, *, tq, nh):
    i = pl.program_id(2)
    qs = [q_ref[0, hh] for hh in range(nh)]

    def step(c, carry, masked):
        base = pl.multiple_of(c * tq, tq)
        ss = []
        for hh in range(nh):
            k = k_ref[0, hh, pl.ds(base, tq), :]
            s = lax.dot_general(qs[hh], k, (((1,), (1,)), ((), ())), preferred_element_type=F32)
            if masked:
                row = lax.broadcasted_iota(jnp.int32, (tq, tq), 0)
                col = lax.broadcasted_iota(jnp.int32, (tq, tq), 1)
                s = jnp.where(col <= row, s, NEG_BIG)
            ss.append(s)
        out = []
        for hh in range(nh):
            m, l, acc = carry[hh]
            v = v_ref[0, hh, pl.ds(base, tq), :]
            m_new = jnp.maximum(m, jnp.max(ss[hh], axis=-1, keepdims=True))
            alpha = jnp.exp2(m - m_new)
            pv = jnp.dot(jnp.exp2((ss[hh] - m_new).astype(BF16)), v, preferred_element_type=F32)
            out.append((m_new, alpha * l + pv[:, V_HEAD_DIM:V_HEAD_DIM + 1], alpha * acc + pv[:, :V_HEAD_DIM]))
        return tuple(out)

    init = tuple((jnp.full((tq, 1), NEG_BIG, F32), jnp.zeros((tq, 1), F32), jnp.zeros((tq, V_HEAD_DIM), F32))
                 for _ in range(nh))
    carry = lax.fori_loop(0, i, lambda c, cr: step(c, cr, False), init)
    carry = step(i, carry, True)
    for hh in range(nh):
        _, l, acc = carry[hh]
        o_ref[0, :, hh * V_HEAD_DIM:(hh + 1) * V_HEAD_DIM] = (acc / l).astype(BF16)


def _mla_attn(q, k, v, tq=512, nh=4):
    b, h, s, dk = q.shape
    dv = v.shape[-1]
    resident = pl.Buffered(1)
    return pl.pallas_call(
        functools.partial(_mla_attn_kernel, tq=tq, nh=nh),
        grid=(b, h // nh, s // tq),
        in_specs=[pl.BlockSpec((1, nh, tq, dk), lambda bi, hi, i: (bi, hi, i, 0)),
                  pl.BlockSpec((1, nh, s, dk), lambda bi, hi, i: (bi, hi, 0, 0), pipeline_mode=resident),
                  pl.BlockSpec((1, nh, s, dv), lambda bi, hi, i: (bi, hi, 0, 0), pipeline_mode=resident)],
        out_specs=pl.BlockSpec((1, tq, nh * V_HEAD_DIM), lambda bi, hi, i: (bi, i, hi)),
        out_shape=jax.ShapeDtypeStruct((b, s, h * V_HEAD_DIM), BF16),
        compiler_params=_cparams("arbitrary", "arbitrary", "arbitrary"),
        name="mla_attn",
    )(q, k, v)


def _dil_attn_kernel(q_ref, kc_ref, kp_ref, vc_ref, vp_ref, o_ref, lse_ref, *, tq):
    i = pl.program_id(2)
    sub = DIL_SPAN
    row = lax.broadcasted_iota(jnp.int32, (sub, 2 * sub), 0)
    col = lax.broadcasted_iota(jnp.int32, (sub, 2 * sub), 1)
    band = jnp.logical_and(col >= row, col <= row + sub)
    first = jnp.logical_and(band, col >= jnp.where(i > 0, 0, sub))
    lane = lax.broadcasted_iota(jnp.int32, (sub, LANES), 1)
    scale = DIL_HEAD_DIM ** -0.5
    dn = (((1,), (1,)), ((), ()))
    chains = [(j, hh) for j in range(tq // sub) for hh in range(DIL_HEADS_PER_GROUP)]

    def window(cur_ref, prev_ref, j, cs):
        if j == 0:
            return jnp.concatenate([prev_ref[0, :, cs], cur_ref[0, :sub, cs]], axis=0)
        return cur_ref[0, (j - 1) * sub:(j + 1) * sub, cs]

    scores = []
    for j, hh in chains:
        cs = slice(hh * LANES, (hh + 1) * LANES)
        s = lax.dot_general(q_ref[0, j * sub:(j + 1) * sub, cs], window(kc_ref, kp_ref, j, cs), dn,
                            preferred_element_type=F32) * scale
        scores.append(jnp.where(first if j == 0 else band, s, NEG_BIG))
    lse_blk = [jnp.zeros((sub, LANES), F32) for _ in range(tq // sub)]
    for (j, hh), s in zip(chains, scores):
        cs = slice(hh * LANES, (hh + 1) * LANES)
        m = jnp.max(s, axis=-1, keepdims=True)
        p = jnp.exp(s - m)
        l = jnp.sum(p, axis=-1, keepdims=True)
        acc = jnp.dot(p.astype(BF16), window(vc_ref, vp_ref, j, cs), preferred_element_type=F32)
        o_ref[0, j * sub:(j + 1) * sub, cs] = (acc * (1.0 / l)).astype(BF16)
        lse_blk[j] = jnp.where(lane == hh, m + jnp.log(l), lse_blk[j])
    for j in range(tq // sub):
        lse_ref[0, j * sub:(j + 1) * sub, :] = lse_blk[j]


def _dil_attn(t, batch, seq, dilation):
    ln = seq // dilation
    tq = min(ln, 4 * DIL_SPAN)
    gc = DIL_GROUP_COLS
    ratio = tq // DIL_SPAN
    cur = lambda which: pl.BlockSpec((1, tq, gc), lambda b, r, i: (b, i, r * 3 + which))
    prev = lambda which: pl.BlockSpec(
        (1, DIL_SPAN, gc), lambda b, r, i: (b, jnp.maximum(i * ratio - 1, 0), r * 3 + which))
    o, lse = pl.pallas_call(
        functools.partial(_dil_attn_kernel, tq=tq),
        grid=(batch, dilation, ln // tq),
        in_specs=[cur(0), cur(1), prev(1), cur(2), prev(2)],
        out_specs=[pl.BlockSpec((1, tq, gc), lambda b, r, i: (b, i, r)),
                   pl.BlockSpec((1, tq, LANES), lambda b, r, i: (b, i, r))],
        out_shape=[jax.ShapeDtypeStruct((batch, ln, dilation * gc), BF16),
                   jax.ShapeDtypeStruct((batch, ln, dilation * LANES), F32)],
        compiler_params=_cparams("arbitrary", "arbitrary", "arbitrary"),
        name=f"dil_attn_d{dilation}",
    )(t, t, t, t, t)
    return o, lse


def _pack_halves(v):
    w = v.shape[1] // 2
    lo = lax.bitcast_convert_type(v[:, :w].astype(BF16).astype(F32), jnp.uint32)
    hi = lax.bitcast_convert_type(v[:, w:].astype(BF16).astype(F32), jnp.uint32)
    return (lo >> 16) | (hi & jnp.uint32(0xFFFF0000))


ROW_TILE = 8


def _row_tile(p):
    return (pl.ds(pl.multiple_of(p * ROW_TILE, ROW_TILE), ROW_TILE), slice(None))


def _store_row_tiles(ref, pk, base=0):
    rows = pk.shape[0]
    for c in range(ROW_TILE):
        ref[pl.ds(base + c, rows, stride=ROW_TILE), :] = pk[:, c * LANES:(c + 1) * LANES]


def _load_row_tiles(ref, rows, base=0):
    return jnp.concatenate([ref[pl.ds(base + c, rows, stride=ROW_TILE), :] for c in range(ROW_TILE)], axis=1)


def _unpack_halves(pk):
    lo = lax.bitcast_convert_type(pk << 16, F32)
    hi = lax.bitcast_convert_type(pk & jnp.uint32(0xFFFF0000), F32)
    return lo, hi


def _merge_kernel(oa_ref, o0_ref, o1_ref, o2_ref, l0_ref, l1_ref, l2_ref, ga_ref, gb_ref, x_ref,
                  gate_ref, shift_ref, scale_ref, gpost_ref, gpre_ref,
                  wa_ref, wb_ref, wo_ref, wrh_ref, wrl_ref,
                  x1_ref, h2_ref, h2pk_ref, logit_ref, o_sc, l_sc):
    tm = x_ref.shape[0]

    def natural(ref, sc, gi, chunks):
        dil = DIL_PATTERNS[gi][1]
        if dil == 1:
            return [ref[0, :, c * LANES:(c + 1) * LANES].astype(F32) for c in range(chunks)]
        for r in range(dil):
            for c in range(chunks):
                c0 = (r * chunks + c) * LANES
                sc.at[gi, c][pl.ds(r, tm // dil, stride=dil), :] = ref[0, :, c0:c0 + LANES].astype(F32)
        return [sc[gi, c] for c in range(chunks)]

    (l0,), (l1,), (l2,) = [natural(ref, l_sc, gi, 1) for gi, ref in enumerate((l0_ref, l1_ref, l2_ref))]
    o0, o1, o2 = [natural(ref, o_sc, gi, DIL_HEADS_PER_GROUP) for gi, ref in enumerate((o0_ref, o1_ref, o2_ref))]
    m = jnp.maximum(jnp.maximum(l0, l1), l2)
    e0, e1, e2 = jnp.exp(l0 - m), jnp.exp(l1 - m), jnp.exp(l2 - m)
    inv = 1.0 / (e0 + e1 + e2)
    w0, w1, w2 = e0 * inv, e1 * inv, e2 * inv
    parts = []
    for hh in range(DIL_HEADS_PER_GROUP):
        parts.append(w0[:, hh:hh + 1] * o0[hh] + w1[:, hh:hh + 1] * o1[hh] + w2[:, hh:hh + 1] * o2[hh])
    o_dil = jnp.concatenate(parts, axis=1).astype(BF16)
    y_a = jnp.dot(oa_ref[...], wa_ref[...], preferred_element_type=F32)
    y_b = jnp.dot(o_dil, wb_ref[...], preferred_element_type=F32)
    merged = ga_ref[...].astype(F32) * y_a + gb_ref[...].astype(F32) * y_b
    y = jnp.dot(merged.astype(BF16), wo_ref[...], preferred_element_type=F32)
    x1 = x_ref[...] + gate_ref[0] * _rms(y, gpost_ref[...])
    x1_ref[...] = x1
    h2 = _rms(x1, gpre_ref[...]) * (1.0 + scale_ref[0]) + shift_ref[0]
    _store_row_tiles(h2pk_ref, _pack_halves(h2))
    h2_hi = h2.astype(BF16)
    h2_ref[...] = h2_hi
    h2_lo = (h2 - h2_hi.astype(F32)).astype(BF16)
    both = jnp.dot(h2_hi, wrh_ref[...], preferred_element_type=F32)
    logits = both[:, :LANES] + both[:, LANES:] + jnp.dot(h2_lo, wrl_ref[...], preferred_element_type=F32)
    logit_ref[...] = logits.T


def _merge(oa, dil_o, dil_lse, gates, x2, mod3, gpost, gpre, wa, wb, wo, wr_hi, wr_lo, seq, tm=256):
    n, d = x2.shape
    per_b = seq // tm
    row = lambda w: pl.BlockSpec((tm, w), lambda i: (i, 0))
    const = lambda a: pl.BlockSpec(a.shape, lambda i: (0,) * a.ndim, pipeline_mode=pl.Buffered(1))
    modspec = lambda ch: pl.BlockSpec((1, 1, d), lambda i: ((i // per_b) * ADALN_CHUNKS + ch, 0, 0))
    strided = lambda gi, w: pl.BlockSpec((1, tm // DIL_PATTERNS[gi][1], DIL_PATTERNS[gi][1] * w),
                                         lambda i: (i // per_b, i % per_b, 0))
    return pl.pallas_call(
        _merge_kernel,
        grid=(n // tm,),
        in_specs=[row(oa.shape[1]),
                  strided(0, DIL_GROUP_COLS), strided(1, DIL_GROUP_COLS), strided(2, DIL_GROUP_COLS),
                  strided(0, LANES), strided(1, LANES), strided(2, LANES),
                  pl.BlockSpec((tm, d), lambda i: (i, 0)), pl.BlockSpec((tm, d), lambda i: (i, 1)),
                  row(d),
                  modspec(2), modspec(3), modspec(4),
                  const(gpost), const(gpre),
                  const(wa), const(wb), const(wo), const(wr_hi), const(wr_lo)],
        out_specs=[row(d), row(d), pl.BlockSpec((tm * ROW_TILE, LANES), lambda i: (i, 0)),
                   pl.BlockSpec((LANES, tm), lambda i: (0, i))],
        out_shape=[jax.ShapeDtypeStruct((n, d), F32),
                   jax.ShapeDtypeStruct((n, d), BF16),
                   jax.ShapeDtypeStruct((n * ROW_TILE, LANES), jnp.uint32),
                   jax.ShapeDtypeStruct((LANES, n), F32)],
        scratch_shapes=[pltpu.VMEM((DIL_GROUPS, DIL_HEADS_PER_GROUP, tm, LANES), F32),
                        pltpu.VMEM((DIL_GROUPS, 1, tm, LANES), F32)],
        compiler_params=_cparams("arbitrary"),
        name="merge_outproj",
    )(oa, *dil_o, *dil_lse, gates, gates, x2, mod3, mod3, mod3, gpost, gpre, wa, wb, wo, wr_hi, wr_lo)


def _route_kernel(lg_ref, bias_ref, pos_ref, w_ref, meta_ref, blke_ref, cnt_sc, base_sc, rank_sc, score_sc, *, tt):
    ps = pl.program_id(0)
    i = pl.program_id(1)
    per_group = N_EXPERTS // N_EXPERT_GROUPS
    neg_inf = -jnp.inf
    tile = pl.ds(pl.multiple_of(i * tt, tt), tt)

    @pl.when(jnp.logical_and(ps == 0, i == 0))
    def _():
        cnt_sc[...] = jnp.zeros_like(cnt_sc)

    @pl.when(ps == 0)
    def _():
        scores = _sigmoid(lg_ref[...])
        biased = scores + bias_ref[...]
        b3 = biased.reshape(N_EXPERT_GROUPS, per_group, tt)
        mem = lax.broadcasted_iota(jnp.int32, b3.shape, 1)
        m1 = jnp.max(b3, axis=1, keepdims=True)
        first = jnp.min(jnp.where(b3 == m1, mem, per_group), axis=1, keepdims=True)
        m2 = jnp.max(jnp.where(mem == first, neg_inf, b3), axis=1, keepdims=True)
        gs = m1 + m2
        gidx = lax.broadcasted_iota(jnp.int32, gs.shape, 0)
        grank = jnp.zeros(gs.shape, jnp.int32)
        for g2 in range(N_EXPERT_GROUPS):
            r = gs[g2:g2 + 1]
            beats = jnp.logical_or(r > gs, jnp.logical_and(r == gs, g2 < gidx))
            grank = grank + jnp.where(beats, 1, 0)
        sel = jnp.where(grank < TOPK_GROUPS, b3, neg_inf).reshape(N_EXPERTS, tt)
        eidx = lax.broadcasted_iota(jnp.int32, sel.shape, 0)
        erank = jnp.zeros(sel.shape, jnp.int32)
        for e2 in range(N_EXPERTS):
            r = sel[e2:e2 + 1, :]
            beats = jnp.logical_or(r > sel, jnp.logical_and(r == sel, e2 < eidx))
            erank = erank + jnp.where(beats, 1, 0)
        rank_sc[:, tile] = erank
        score_sc[:, tile] = scores
        cnt_sc[...] = cnt_sc[...] + jnp.sum(jnp.where(erank < TOP_K, 1.0, 0.0), axis=1,
                                            keepdims=True).astype(jnp.int32)

    @pl.when(jnp.logical_and(ps == 1, i == 0))
    def _():
        cnt = cnt_sc[...]
        pc = ((cnt + (MOE_BLOCK - 1)) // MOE_BLOCK) * MOE_BLOCK
        pcb = jnp.broadcast_to(pc, (N_EXPERTS, LANES))
        eid = lax.broadcasted_iota(jnp.int32, (N_EXPERTS, LANES), 0)
        pends = jnp.zeros((N_EXPERTS, LANES), jnp.int32)
        for e2 in range(N_EXPERTS):
            pends = pends + jnp.where(eid >= e2, pcb[e2:e2 + 1, :], 0)
        pst = pends - pcb
        base_sc[...] = pst[:, 0:1]
        meta_ref[0] = jnp.broadcast_to(cnt, (N_EXPERTS, LANES))
        meta_ref[1] = pst
        meta_ref[2] = pends
        nbl = blke_ref.shape[1]
        blk_start = lax.broadcasted_iota(jnp.int32, (N_EXPERTS, nbl), 1) * MOE_BLOCK
        pend_b = jnp.broadcast_to(pends[:, 0:1], (N_EXPERTS, nbl))
        be = jnp.sum(jnp.where(pend_b <= blk_start, 1, 0), axis=0, keepdims=True)
        blke_ref[...] = jnp.broadcast_to(jnp.minimum(be, N_EXPERTS - 1), blke_ref.shape)

    @pl.when(ps == 1)
    def _():
        erank = rank_sc[:, tile]
        scores = score_sc[:, tile]
        esel = erank < TOP_K
        mask_f = jnp.where(esel, 1.0, 0.0)
        tile_cnt = jnp.sum(mask_f, axis=1, keepdims=True).astype(jnp.int32)
        rr = lax.broadcasted_iota(jnp.int32, (tt, tt), 0)
        cc = lax.broadcasted_iota(jnp.int32, (tt, tt), 1)
        upper = jnp.where(rr < cc, 1.0, 0.0).astype(BF16)
        prefix = jnp.dot(mask_f.astype(BF16), upper, preferred_element_type=F32)
        posd = base_sc[...] + prefix.astype(jnp.int32)
        base_sc[...] = base_sc[...] + tile_cnt
        wsel = jnp.where(esel, scores, 0.0)
        denom = jnp.sum(wsel, axis=0, keepdims=True)
        wn = wsel / (denom + 1e-20) * ROUTED_SCALE
        prow, wrow = [], []
        for kk in range(TOP_K):
            hit = erank == kk
            prow.append(jnp.sum(jnp.where(hit, posd, 0), axis=0, keepdims=True))
            wrow.append(jnp.sum(jnp.where(hit, wn, 0.0), axis=0, keepdims=True))
        pad = pos_ref.shape[0] - TOP_K
        pos_ref[...] = jnp.concatenate(prow + [jnp.zeros((pad, tt), jnp.int32)], axis=0)
        w_ref[...] = jnp.concatenate(wrow + [jnp.zeros((pad, tt), F32)], axis=0)


def _route(logits_t, bias_col, nb, tt=256):
    n = logits_t.shape[1]
    nbl = -(-nb // LANES) * LANES
    return pl.pallas_call(
        functools.partial(_route_kernel, tt=tt),
        grid=(2, n // tt),
        in_specs=[pl.BlockSpec((N_EXPERTS, tt), lambda ps, i: (0, i * (1 - ps))),
                  pl.BlockSpec((N_EXPERTS, 1), lambda ps, i: (0, 0))],
        out_specs=[pl.BlockSpec((8, tt), lambda ps, i: (0, ps * i)),
                   pl.BlockSpec((8, tt), lambda ps, i: (0, ps * i)),
                   pl.BlockSpec((3, N_EXPERTS, LANES), lambda ps, i: (0, 0, 0)),
                   pl.BlockSpec((8, nbl), lambda ps, i: (0, 0))],
        out_shape=[jax.ShapeDtypeStruct((8, n), jnp.int32),
                   jax.ShapeDtypeStruct((8, n), F32),
                   jax.ShapeDtypeStruct((3, N_EXPERTS, LANES), jnp.int32),
                   jax.ShapeDtypeStruct((8, nbl), jnp.int32)],
        scratch_shapes=[pltpu.VMEM((N_EXPERTS, 1), jnp.int32), pltpu.VMEM((N_EXPERTS, 1), jnp.int32),
                        pltpu.VMEM((N_EXPERTS, n), jnp.int32), pltpu.VMEM((N_EXPERTS, n), F32)],
        compiler_params=_cparams("arbitrary", "arbitrary"),
        name="moe_route",
    )(logits_t, bias_col)


def _dispatch_kernel(pos_ref, cnt_ref, pst_ref, h_hbm, hb_ref, wsg_ref, wsu_ref, wsd_ref, xs_hbm, ysh_ref,
                     hbuf, zrow, in_sem, sc_sem, sem, *, tm, n):
    i = pl.program_id(0)
    nt = pl.num_programs(0)
    tile_rows = tm * ROW_TILE

    def fetch(tile, slot):
        return pltpu.make_async_copy(h_hbm.at[pl.ds(pl.multiple_of(tile * tile_rows, tile_rows), tile_rows), :],
                                     hbuf.at[slot], in_sem.at[slot])

    def wait_rows(slot):
        for kk in range(TOP_K):
            pltpu.make_async_copy(hbuf.at[slot], xs_hbm.at[pl.ds(0, tile_rows), :], sc_sem.at[slot]).wait()

    @pl.when(i == 0)
    def _():
        fetch(0, 0).start()
        fetch(1, 1).start()

    slot = i % 3
    fetch(i, slot).wait()
    src = hbuf.at[slot]

    def body(g, carry):
        for u in range(ROW_DMA_UNROLL):
            r = g * ROW_DMA_UNROLL + u
            for kk in range(TOP_K):
                p = pos_ref[kk * n + i * tm + r]
                pltpu.make_async_copy(src.at[_row_tile(r)], xs_hbm.at[_row_tile(p)],
                                      sc_sem.at[slot]).start(priority=kk % 2)
        return carry
    lax.fori_loop(0, tm // ROW_DMA_UNROLL, body, 0)

    hb = hb_ref[...]
    g = jnp.dot(hb, wsg_ref[...], preferred_element_type=F32)
    u = jnp.dot(hb, wsu_ref[...], preferred_element_type=F32)
    ysh_ref[...] = jnp.dot((g * _sigmoid(g) * u).astype(BF16), wsd_ref[...],
                           preferred_element_type=F32).astype(BF16)

    @pl.when(i >= 1)
    def _():
        wait_rows((i + 2) % 3)

    @pl.when(i + 2 < nt)
    def _():
        fetch(i + 2, (i + 2) % 3).start()

    @pl.when(i == nt - 1)
    def _():
        wait_rows(slot)

    @pl.when(i == nt - 1)
    def _():
        zrow[...] = jnp.zeros_like(zrow)

        def per_expert(e, carry):
            cnt = cnt_ref[e]
            first = pst_ref[e] + cnt
            npad = ((cnt + (MOE_BLOCK - 1)) // MOE_BLOCK) * MOE_BLOCK - cnt

            def start(s, c2):
                pltpu.make_async_copy(zrow.at[_row_tile(0)], xs_hbm.at[_row_tile(first + s)], sem.at[1]).start()
                return c2

            def wait(s, c2):
                pltpu.make_async_copy(zrow.at[_row_tile(0)], xs_hbm.at[_row_tile(0)], sem.at[1]).wait()
                return c2
            lax.fori_loop(0, npad, start, 0)
            lax.fori_loop(0, npad, wait, 0)
            return carry
        lax.fori_loop(0, N_EXPERTS, per_expert, 0)

        last = N_EXPERTS - 1
        used = (pst_ref[last] + cnt_ref[last] + (MOE_BLOCK - 1)) // MOE_BLOCK
        blk_rows = MOE_BLOCK * ROW_TILE

        def tail(b, carry):
            cp = pltpu.make_async_copy(zrow, xs_hbm.at[pl.ds(pl.multiple_of(b * blk_rows, blk_rows), blk_rows), :],
                                       sem.at[1])
            cp.start()
            cp.wait()
            return carry
        lax.fori_loop(used, xs_hbm.shape[0] // blk_rows, tail, 0)


def _dispatch(pos_flat, cnt, pst, h2pk, h2, wsg, wsu, wsd, nb, tm=256):
    n, d = h2.shape
    assert n // tm >= 2
    const = lambda a: pl.BlockSpec(a.shape, lambda i, *_: (0,) * a.ndim)
    grid_spec = pltpu.PrefetchScalarGridSpec(
        num_scalar_prefetch=3,
        grid=(n // tm,),
        in_specs=[pl.BlockSpec(memory_space=pl.ANY),
                  pl.BlockSpec((tm, d), lambda i, *_: (i, 0)),
                  const(wsg), const(wsu), const(wsd)],
        out_specs=[pl.BlockSpec(memory_space=pl.ANY), pl.BlockSpec((tm, d), lambda i, *_: (i, 0))],
        scratch_shapes=[pltpu.VMEM((3, tm * ROW_TILE, LANES), jnp.uint32),
                        pltpu.VMEM((MOE_BLOCK * ROW_TILE, LANES), jnp.uint32),
                        pltpu.SemaphoreType.DMA((3,)), pltpu.SemaphoreType.DMA((3,)),
                        pltpu.SemaphoreType.DMA((2,))],
    )
    return pl.pallas_call(
        functools.partial(_dispatch_kernel, tm=tm, n=n),
        grid_spec=grid_spec,
        out_shape=[jax.ShapeDtypeStruct((nb * MOE_BLOCK * ROW_TILE, LANES), jnp.uint32),
                   jax.ShapeDtypeStruct((n, d), BF16)],
        compiler_params=_cparams("arbitrary"),
        name="moe_dispatch",
    )(pos_flat, cnt, pst, h2pk, h2, wsg, wsu, wsd)


def _gmm_kernel(nact_ref, blke_ref, xs_ref, wg_hbm, wu_hbm, wd_hbm, o_ref,
                wgf, wuf, wdf, wgb, wub, wdb, run_sc, sem):
    nact = nact_ref[0]

    def fetch(e, slot):
        return (pltpu.make_async_copy(wg_hbm.at[e], wgf.at[slot], sem.at[slot, 0]),
                pltpu.make_async_copy(wu_hbm.at[e], wuf.at[slot], sem.at[slot, 1]),
                pltpu.make_async_copy(wd_hbm.at[e], wdf.at[slot], sem.at[slot, 2]))

    @pl.when(pl.program_id(0) == 0)
    def _():
        run_sc[0] = 0
        for cp in fetch(blke_ref[0], 0):
            cp.start()

    for sb in range(GMM_SUB):
        i = pl.program_id(0) * GMM_SUB + sb
        base = sb * MOE_BLOCK * ROW_TILE

        @pl.when(i < nact)
        def _(i=i, base=base):
            e = blke_ref[i]
            changed = jnp.logical_or(i == 0, e != blke_ref[jnp.maximum(i - 1, 0)])

            @pl.when(changed)
            def _():
                run = run_sc[0]
                slot = run % 2
                for cp in fetch(e, slot):
                    cp.wait()
                wgb[...] = wgf[slot].astype(BF16)
                wub[...] = wuf[slot].astype(BF16)
                wdb[...] = wdf[slot].astype(BF16)
                nxt = lax.while_loop(lambda j: jnp.logical_and(j < nact, blke_ref[jnp.minimum(j, nact - 1)] == e),
                                     lambda j: j + 1, i + 1)

                @pl.when(nxt < nact)
                def _():
                    for cp in fetch(blke_ref[jnp.minimum(nxt, nact - 1)], 1 - slot):
                        cp.start(priority=1)
                run_sc[0] = run + 1

            lo, hi = _unpack_halves(_load_row_tiles(xs_ref, MOE_BLOCK, base))
            lo, hi = lo.astype(BF16), hi.astype(BF16)
            half = lo.shape[1]
            g = (jnp.dot(lo, wgb[:half, :], preferred_element_type=F32)
                 + jnp.dot(hi, wgb[half:, :], preferred_element_type=F32))
            u = (jnp.dot(lo, wub[:half, :], preferred_element_type=F32)
                 + jnp.dot(hi, wub[half:, :], preferred_element_type=F32))
            a = (g * _sigmoid(g) * u).astype(BF16)
            _store_row_tiles(o_ref, _pack_halves(jnp.dot(a, wdb[...], preferred_element_type=F32)), base)

        @pl.when(i >= nact)
        def _(base=base):
            o_ref[pl.ds(base, MOE_BLOCK * ROW_TILE), :] = jnp.zeros((MOE_BLOCK * ROW_TILE, LANES), o_ref.dtype)


def _gmm(nact, blk_e, xs, w_gate, w_up, w_down, nb):
    d, f = w_gate.shape[1:]
    assert nb % GMM_SUB == 0
    rows = GMM_SUB * MOE_BLOCK * ROW_TILE
    blk = lambda i, na: jnp.minimum(i, (na[0] - 1) // GMM_SUB)
    hbm = pl.BlockSpec(memory_space=pl.ANY)
    grid_spec = pltpu.PrefetchScalarGridSpec(
        num_scalar_prefetch=2,
        grid=(nb // GMM_SUB,),
        in_specs=[pl.BlockSpec((rows, LANES), lambda i, na, be: (blk(i, na), 0)), hbm, hbm, hbm],
        out_specs=pl.BlockSpec((rows, LANES), lambda i, na, be: (i, 0)),
        scratch_shapes=[pltpu.VMEM((2, d, f), F32), pltpu.VMEM((2, d, f), F32), pltpu.VMEM((2, f, d), F32),
                        pltpu.VMEM((d, f), BF16), pltpu.VMEM((d, f), BF16), pltpu.VMEM((f, d), BF16),
                        pltpu.SMEM((1,), jnp.int32), pltpu.SemaphoreType.DMA((2, 3))],
    )
    return pl.pallas_call(
        _gmm_kernel,
        grid_spec=grid_spec,
        out_shape=jax.ShapeDtypeStruct((nb * MOE_BLOCK * ROW_TILE, LANES), jnp.uint32),
        compiler_params=_cparams("arbitrary"),
        name="moe_experts",
    )(nact, blk_e, xs, w_gate, w_up, w_down)


def _final_kernel(pos_ref, tw_ref, ysh_ref, x1_ref, gate_ref, gpost_ref, yb_hbm, o_ref, rbuf, sem, *, tm, n):
    i = pl.program_id(0)
    slot = i % 2

    def gather(tile, sl):
        def body(g, carry):
            for u in range(ROW_DMA_UNROLL):
                r = g * ROW_DMA_UNROLL + u
                for kk in range(TOP_K):
                    p = pos_ref[kk * n + tile * tm + r]
                    pltpu.make_async_copy(yb_hbm.at[_row_tile(p)], rbuf.at[sl, kk].at[_row_tile(r)],
                                          sem.at[sl]).start(priority=kk % 2)
            return carry
        lax.fori_loop(0, tm // ROW_DMA_UNROLL, body, 0)

    @pl.when(i == 0)
    def _():
        gather(0, 0)

    @pl.when(i + 1 < pl.num_programs(0))
    def _():
        gather(i + 1, 1 - slot)

    for kk in range(TOP_K):
        pltpu.make_async_copy(yb_hbm.at[pl.ds(0, tm * ROW_TILE), :], rbuf.at[slot, kk], sem.at[slot]).wait()
    tw = tw_ref[...]
    half = ROW_TILE * LANES
    r_lo = jnp.zeros((tm, half), F32)
    r_hi = jnp.zeros((tm, half), F32)
    for kk in range(TOP_K):
        lo, hi = _unpack_halves(_load_row_tiles(rbuf.at[slot, kk], tm))
        wk = tw[:, kk:kk + 1]
        r_lo = r_lo + wk * lo
        r_hi = r_hi + wk * hi
    y = ysh_ref[...].astype(F32) + jnp.concatenate([r_lo, r_hi], axis=1)
    o_ref[...] = x1_ref[...] + gate_ref[0] * _rms(y, gpost_ref[...])


def _final(pos_flat, top_w8, ysh, x1, mod3, gpost, yb, seq, tm=256):
    n, d = x1.shape
    per_b = seq // tm
    const = lambda a: pl.BlockSpec(a.shape, lambda i, ps: (0,) * a.ndim)
    row = lambda w: pl.BlockSpec((tm, w), lambda i, ps: (i, 0))
    grid_spec = pltpu.PrefetchScalarGridSpec(
        num_scalar_prefetch=1,
        grid=(n // tm,),
        in_specs=[row(top_w8.shape[1]), row(d), row(d),
                  pl.BlockSpec((1, 1, d), lambda i, ps: ((i // per_b) * ADALN_CHUNKS + 5, 0, 0)),
                  const(gpost),
                  pl.BlockSpec(memory_space=pl.ANY)],
        out_specs=row(d),
        scratch_shapes=[pltpu.VMEM((2, TOP_K, tm * ROW_TILE, LANES), jnp.uint32), pltpu.SemaphoreType.DMA((2,))],
    )
    return pl.pallas_call(
        functools.partial(_final_kernel, tm=tm, n=n),
        grid_spec=grid_spec,
        out_shape=jax.ShapeDtypeStruct((n, d), F32),
        compiler_params=_cparams("arbitrary"),
        name="moe_combine_final",
    )(pos_flat, top_w8, ysh, x1, mod3, gpost, yb)


def _rope_tables(positions, dim, passthrough):
    half = dim // 2
    inv_freq = 1.0 / (ROPE_THETA ** (jnp.arange(0, dim, 2, dtype=F32) / dim))
    freq = jnp.concatenate([inv_freq, inv_freq, jnp.zeros((LANES - dim,), F32)])
    ang = positions.astype(F32).reshape(-1, 1) * freq[None, :]
    cos, sin = jnp.cos(ang), jnp.sin(ang)
    lane = jnp.arange(LANES)[None, :]
    c_tab = jnp.where(lane < dim, cos, 1.0 if passthrough else 0.0)
    s_fwd = jnp.where(jnp.logical_and(lane >= half, lane < dim), sin, 0.0)
    s_bwd = jnp.where(lane < half, -sin, 0.0)
    return c_tab, s_fwd, s_bwd


def kernel(x, c, positions, w_ada, b_ada, attn_pre_g, w_in, q_a_norm_g, w_q_up, kv_a_norm_g, w_kv_up, w_mla_o, w_dil_o, w_out, attn_post_g, ffn_pre_g, w_router, router_bias, w_exp_gate, w_exp_up, w_exp_down, w_sh_gate, w_sh_up, w_sh_down, ffn_post_g):
    batch, seq, d = x.shape
    n = batch * seq
    depth = w_ada.shape[0]

    m_c, m_sf, m_sb = _rope_tables(positions, QK_ROPE_DIM, passthrough=False)
    d_c, d_sf, d_sb = _rope_tables(positions, DIL_ROT_DIM, passthrough=True)

    x2 = x.reshape(n, d)
    c8 = jnp.pad(c, ((0, 8 - batch), (0, 0)))
    for l in range(depth):
        mod = _ada(c8, w_ada[l], b_ada[l].reshape(1, -1))
        mod3 = mod[:batch].reshape(batch * ADALN_CHUNKS, 1, d)

        wi = w_in[l]
        o_dil = Q_LORA_RANK + KV_LORA_RANK + QK_ROPE_DIM
        o_ga = o_dil + 3 * DIL_HEADS * DIL_HEAD_DIM
        n_gate, n_dil = wi.shape[1] - o_ga, o_ga - o_dil
        n_a = o_dil + LANES - QK_ROPE_DIM
        c_dil, c_a = n_gate, -(-(n_gate + n_dil) // n_a) * n_a
        w_full = _wprep(wi, ((0, o_ga, n_gate), (c_dil, o_dil, n_dil), (c_a, 0, o_dil)), c_a + n_a)
        wq3 = w_q_up[l].reshape(Q_LORA_RANK, MLA_HEADS, MLA_QK_DIM)
        wq = jnp.concatenate([wq3, jnp.zeros((Q_LORA_RANK, MLA_HEADS, MLA_QK_PAD - MLA_QK_DIM), F32)],
                             axis=2).reshape(Q_LORA_RANK, MLA_HEADS * MLA_QK_PAD).astype(BF16)
        wkv3 = w_kv_up[l].reshape(KV_LORA_RANK, MLA_HEADS, QK_NOPE_DIM + V_HEAD_DIM)
        wkv = jnp.concatenate([wkv3[:, :, :QK_NOPE_DIM].reshape(KV_LORA_RANK, -1),
                               wkv3[:, :, QK_NOPE_DIM:].reshape(KV_LORA_RANK, -1)], axis=1).astype(BF16)

        h = _prenorm(x2, attn_pre_g[l].reshape(1, d), mod3, seq)
        a = _mm(h, w_full, c_a, n_a, tn=n_a)
        gates = _mm(h, w_full, 0, n_gate, act="sigmoid")
        q, k, v = _mlaprep(a, q_a_norm_g[l].reshape(1, -1), kv_a_norm_g[l].reshape(1, -1), wq, wkv,
                           m_c, m_sf, m_sb, batch, seq)
        o_mla = _mla_attn(q, k, v).reshape(n, MLA_HEADS * V_HEAD_DIM)
        dil_o, dil_lse = [], []
        for g, (_, dilation) in enumerate(DIL_PATTERNS):
            qkv = _dilproj(h, w_full, c_dil, g, d_c, d_sf, d_sb, batch, seq, dilation)
            o_g, lse_g = _dil_attn(qkv, batch, seq, dilation)
            dil_o.append(o_g)
            dil_lse.append(lse_g)

        wr = jnp.pad(w_router[l], ((0, 0), (0, LANES - N_EXPERTS)))
        wr_hi = wr.astype(BF16)
        wr_lo = (wr - wr_hi.astype(F32)).astype(BF16)
        x1, h2, h2pk, logits_t = _merge(o_mla, dil_o, dil_lse, gates, x2, mod3,
                                        attn_post_g[l].reshape(1, d), ffn_pre_g[l].reshape(1, d),
                                        w_mla_o[l].astype(BF16), w_dil_o[l].astype(BF16), w_out[l].astype(BF16),
                                        jnp.concatenate([wr_hi, wr_lo], axis=1), wr_hi, seq)

        nb = -(-(n * TOP_K + N_EXPERTS * (MOE_BLOCK - 1)) // MOE_BLOCK)
        pos_t, w_t, meta, blk_e = _route(logits_t, router_bias[l].astype(F32).reshape(N_EXPERTS, 1), nb)
        pos_flat = pos_t.reshape(-1)
        nact = meta[2, N_EXPERTS - 1, :1] // MOE_BLOCK
        xs, ysh = _dispatch(pos_flat, meta[0, :, 0], meta[1, :, 0], h2pk, h2,
                            w_sh_gate[l].astype(BF16), w_sh_up[l].astype(BF16), w_sh_down[l].astype(BF16), nb)
        yb = _gmm(nact, blk_e[0], xs, w_exp_gate[l], w_exp_up[l], w_exp_down[l], nb)
        x2 = _final(pos_flat, w_t.T, ysh, x1, mod3, ffn_post_g[l].reshape(1, d), yb, seq)
    return x2.reshape(batch, seq, d)
```

```python
import functools

import jax
import jax.numpy as jnp
from jax import lax
from jax.experimental import pallas as pl
from jax.experimental.pallas import tpu as pltpu

F32 = jnp.float32
BF16 = jnp.bfloat16

D_MODEL = 2048
NORM_EPS = 1e-6
ROPE_THETA = 500000.0
ADALN_CHUNKS = 6

MLA_HEADS = 8
Q_LORA_RANK = 512
KV_LORA_RANK = 512
QK_NOPE_DIM = 128
QK_ROPE_DIM = 64
V_HEAD_DIM = 128
MLA_QK_DIM = QK_NOPE_DIM + QK_ROPE_DIM
MLA_QK_PAD = 256

DIL_PATTERNS = ((128, 1), (512, 4), (2048, 16))
DIL_GROUPS = len(DIL_PATTERNS)
DIL_HEADS_PER_GROUP = 4
DIL_HEADS = DIL_GROUPS * DIL_HEADS_PER_GROUP
DIL_HEAD_DIM = 128
DIL_ROT_DIM = DIL_HEAD_DIM // 4
DIL_SPAN = 128
DIL_GROUP_COLS = DIL_HEADS_PER_GROUP * DIL_HEAD_DIM

N_EXPERTS = 64
N_EXPERT_GROUPS = 8
TOPK_GROUPS = 4
TOP_K = 6
EXPERT_DIM = 512
SHARED_DIM = 512
ROUTED_SCALE = 2.5
MOE_BLOCK = 256

LANES = 128
NEG_BIG = -1e30
LOG2_E = 1.4426950408889634
ROW_DMA_UNROLL = 4
GMM_SUB = 2
VMEM_LIMIT = 56 * 1024 * 1024


def _cparams(*sem):
    return pltpu.CompilerParams(dimension_semantics=sem, vmem_limit_bytes=VMEM_LIMIT)


def _sigmoid(v):
    return 1.0 / (1.0 + jnp.exp(-v))


def _rms(v, g):
    ms = jnp.mean(v * v, axis=-1, keepdims=True)
    return v * lax.rsqrt(ms + NORM_EPS) * g


def _ada_kernel(c_ref, w_ref, b_ref, o_ref):
    c = c_ref[...]
    a = (c * _sigmoid(c)).astype(BF16)
    o_ref[...] = jnp.dot(a, w_ref[...].astype(BF16), preferred_element_type=F32) + b_ref[...]


def _ada(c8, w_ada, b_ada, tn=1536):
    d, n = w_ada.shape
    return pl.pallas_call(
        _ada_kernel,
        grid=(n // tn,),
        in_specs=[pl.BlockSpec((8, d), lambda j: (0, 0)),
                  pl.BlockSpec((d, tn), lambda j: (0, j)),
                  pl.BlockSpec((1, tn), lambda j: (0, j))],
        out_specs=pl.BlockSpec((8, tn), lambda j: (0, j)),
        out_shape=jax.ShapeDtypeStruct((8, n), F32),
        compiler_params=_cparams("arbitrary"),
        name="ada_mod",
    )(c8, w_ada, b_ada)


def _wprep_source(j, segments):
    src = j * 0
    valid = j * 0
    for dst, s0, width in segments:
        t0, t1 = dst // LANES, (dst + width + LANES - 1) // LANES
        inside = jnp.logical_and(j >= t0, j < t1)
        src = jnp.where(inside, s0 + (j - t0) * LANES, src)
        valid = jnp.where(inside, jnp.minimum(dst + width - j * LANES, LANES), valid)
    return src, valid


def _wprep_kernel(lo_ref, hi_ref, o_ref, *, segments):
    _, valid = _wprep_source(pl.program_id(0), segments)
    t = jnp.concatenate([lo_ref[...], hi_ref[...]], axis=0).T
    lane = lax.broadcasted_iota(jnp.int32, t.shape, 1)
    o_ref[...] = jnp.where(lane < valid, t, 0.0).astype(BF16)


def _wprep(w_t, segments, cols):
    _, k = w_t.shape
    half = LANES // 2
    assert all(dst % LANES == 0 and s0 % half == 0 and width % half == 0 for dst, s0, width in segments)
    part = lambda which: pl.BlockSpec((half, k), lambda j: (_wprep_source(j, segments)[0] // half + which, 0))
    return pl.pallas_call(
        functools.partial(_wprep_kernel, segments=segments),
        grid=(cols // LANES,),
        in_specs=[part(0), part(1)],
        out_specs=pl.BlockSpec((k, LANES), lambda j: (0, j)),
        out_shape=jax.ShapeDtypeStruct((k, cols), BF16),
        compiler_params=_cparams("arbitrary"),
        name="w_in_prep",
    )(w_t, w_t)


def _rope_lanes(t, c_tab, s_fwd, s_bwd, half):
    return t * c_tab + pltpu.roll(t, half, 1) * s_fwd + pltpu.roll(t, LANES - half, 1) * s_bwd


def _front_kernel(x_ref, g_ref, scale_ref, shift_ref, wa_ref, gq_ref, gkv_ref, wq_ref, wkv_ref,
                  c_ref, sf_ref, sb_ref, h_ref, q_ref, k_ref, v_ref):
    h = (_rms(x_ref[...], g_ref[...]) * (1.0 + scale_ref[0]) + shift_ref[0]).astype(BF16)
    h_ref[...] = h
    a = jnp.dot(h, wa_ref[...], preferred_element_type=F32)
    qa = a[:, :Q_LORA_RANK]
    ckv = a[:, Q_LORA_RANK:Q_LORA_RANK + KV_LORA_RANK]
    kr = a[:, Q_LORA_RANK + KV_LORA_RANK:]
    c_tab, s_fwd, s_bwd = c_ref[...], sf_ref[...], sb_ref[...]
    half = QK_ROPE_DIM // 2
    q = jnp.dot(_rms(qa, gq_ref[...]).astype(BF16), wq_ref[...], preferred_element_type=F32)
    q = q * (MLA_QK_DIM ** -0.5 * LOG2_E)
    kv = jnp.dot(_rms(ckv, gkv_ref[...]).astype(BF16), wkv_ref[...], preferred_element_type=F32)
    k_rot = _rope_lanes(kr, c_tab, s_fwd, s_bwd, half).astype(BF16)
    lane = lax.broadcasted_iota(jnp.int32, (a.shape[0], LANES), 1)
    ones_col = jnp.where(lane == 0, 1.0, 0.0).astype(BF16)
    for hh in range(MLA_HEADS):
        base = hh * MLA_QK_PAD
        q_ref[0, hh, :, :LANES] = q[:, base:base + LANES].astype(BF16)
        q_ref[0, hh, :, LANES:] = _rope_lanes(q[:, base + LANES:base + 2 * LANES],
                                              c_tab, s_fwd, s_bwd, half).astype(BF16)
        k_ref[0, hh, :, :LANES] = kv[:, hh * LANES:(hh + 1) * LANES].astype(BF16)
        k_ref[0, hh, :, LANES:] = k_rot
        v_off = MLA_HEADS * LANES + hh * LANES
        v_ref[0, hh, :, :LANES] = kv[:, v_off:v_off + LANES].astype(BF16)
        v_ref[0, hh, :, LANES:] = ones_col


def _front(x2, g, mod3, w_full, col_a, n_a, gq, gkv, wq, wkv, c_tab, s_fwd, s_bwd, batch, seq, tm=512):
    n, d = x2.shape
    per_b = seq // tm
    tab = pl.BlockSpec((tm, LANES), lambda i: (i, 0))
    const = lambda a: pl.BlockSpec(a.shape, lambda i: (0,) * a.ndim, pipeline_mode=pl.Buffered(1))
    head_major = lambda w: pl.BlockSpec((1, MLA_HEADS, tm, w), lambda i: (i // per_b, 0, i % per_b, 0))
    return pl.pallas_call(
        _front_kernel,
        grid=(n // tm,),
        in_specs=[pl.BlockSpec((tm, d), lambda i: (i, 0)),
                  pl.BlockSpec((1, d), lambda i: (0, 0)),
                  pl.BlockSpec((1, 1, d), lambda i: ((i // per_b) * ADALN_CHUNKS + 1, 0, 0)),
                  pl.BlockSpec((1, 1, d), lambda i: ((i // per_b) * ADALN_CHUNKS + 0, 0, 0)),
                  pl.BlockSpec((d, n_a), lambda i: (0, col_a // n_a), pipeline_mode=pl.Buffered(1)),
                  const(gq), const(gkv), const(wq), const(wkv),
                  tab, tab, tab],
        out_specs=[pl.BlockSpec((tm, d), lambda i: (i, 0)),
                   head_major(MLA_QK_PAD), head_major(MLA_QK_PAD), head_major(2 * V_HEAD_DIM)],
        out_shape=[jax.ShapeDtypeStruct((n, d), BF16),
                   jax.ShapeDtypeStruct((batch, MLA_HEADS, seq, MLA_QK_PAD), BF16),
                   jax.ShapeDtypeStruct((batch, MLA_HEADS, seq, MLA_QK_PAD), BF16),
                   jax.ShapeDtypeStruct((batch, MLA_HEADS, seq, 2 * V_HEAD_DIM), BF16)],
        compiler_params=_cparams("arbitrary"),
        name="front_mla_prep",
    )(x2, g, mod3, mod3, w_full, gq, gkv, wq, wkv, c_tab, s_fwd, s_bwd)


def _mm_kernel(h_ref, w_ref, o_ref, *, act):
    y = jnp.dot(h_ref[...], w_ref[...], preferred_element_type=F32)
    if act == "sigmoid":
        y = _sigmoid(y)
    o_ref[...] = y.astype(o_ref.dtype)


def _mm(h, w, col0, cols, act=None, tm=1024, tn=1024):
    n, k = h.shape
    tn = min(tn, cols)
    j0 = col0 // tn
    return pl.pallas_call(
        functools.partial(_mm_kernel, act=act),
        grid=(cols // tn, n // tm),
        in_specs=[pl.BlockSpec((tm, k), lambda j, i: (i, 0)),
                  pl.BlockSpec((k, tn), lambda j, i: (0, j0 + j))],
        out_specs=pl.BlockSpec((tm, tn), lambda j, i: (i, j)),
        out_shape=jax.ShapeDtypeStruct((n, cols), BF16),
        compiler_params=_cparams("arbitrary", "arbitrary"),
        name="in_proj_" + (act or "plain"),
    )(h, w)


def _dilproj_kernel(h_ref, wq_ref, wk_ref, wv_ref, c_ref, sf_ref, sb_ref, o_ref, y_sc, *, dilation):
    hb = h_ref[...]
    y = jnp.concatenate([jnp.dot(hb, w_ref[...], preferred_element_type=F32) for w_ref in (wq_ref, wk_ref, wv_ref)],
                        axis=1)
    c_tab, s_fwd, s_bwd = c_ref[...], sf_ref[...], sb_ref[...]
    cols = 3 * DIL_GROUP_COLS
    rows = h_ref.shape[0] // dilation
    n_rot = 2 * DIL_HEADS_PER_GROUP
    for hh in range(3 * DIL_HEADS_PER_GROUP):
        t = y[:, hh * LANES:(hh + 1) * LANES]
        if hh < n_rot:
            t = _rope_lanes(t, c_tab, s_fwd, s_bwd, DIL_ROT_DIM // 2)
        if dilation == 1:
            o_ref[0, :, hh * LANES:(hh + 1) * LANES] = t.astype(BF16)
        else:
            y_sc[hh] = t
    if dilation > 1:
        for r in range(dilation):
            for hh in range(3 * DIL_HEADS_PER_GROUP):
                c0 = r * cols + hh * LANES
                o_ref[0, :, c0:c0 + LANES] = y_sc.at[hh][pl.ds(r, rows, stride=dilation), :].astype(BF16)


def _dilproj(h, w, col0, group, c_tab, s_fwd, s_bwd, batch, seq, dilation, tm=1024):
    n, k = h.shape
    gc = DIL_GROUP_COLS
    cols = 3 * gc
    per_b = seq // tm
    tab = pl.BlockSpec((tm, LANES), lambda i: (i, 0))
    part = lambda which: pl.BlockSpec((k, gc), lambda i: (0, col0 // gc + which * DIL_GROUPS + group))
    return pl.pallas_call(
        functools.partial(_dilproj_kernel, dilation=dilation),
        grid=(n // tm,),
        in_specs=[pl.BlockSpec((tm, k), lambda i: (i, 0)),
                  part(0), part(1), part(2),
                  tab, tab, tab],
        out_specs=pl.BlockSpec((1, tm // dilation, dilation * cols), lambda i: (i // per_b, i % per_b, 0)),
        out_shape=jax.ShapeDtypeStruct((batch, seq // dilation, dilation * cols), BF16),
        scratch_shapes=[pltpu.VMEM((cols // LANES, tm, LANES), F32)],
        compiler_params=_cparams("arbitrary"),
        name="dil_proj",
    )(h, w, w, w, c_tab, s_fwd, s_bwd)


def _mla_attn_kernel(q_ref, k_ref, v_ref, o_ref, *, tq, nh):
    i = pl.program_id(2)
    qs = [q_ref[0, hh] for hh in range(nh)]

    def step(c, carry, masked):
        base = pl.multiple_of(c * tq, tq)
        ss = []
        for hh in range(nh):
            k = k_ref[0, hh, pl.ds(base, tq), :]
            s = lax.dot_general(qs[hh], k, (((1,), (1,)), ((), ())), preferred_element_type=F32)
            if masked:
                row = lax.broadcasted_iota(jnp.int32, (tq, tq), 0)
                col = lax.broadcasted_iota(jnp.int32, (tq, tq), 1)
                s = jnp.where(col <= row, s, NEG_BIG)
            ss.append(s)
        out = []
        for hh in range(nh):
            m, l, acc = carry[hh]
            v = v_ref[0, hh, pl.ds(base, tq), :]
            m_new = jnp.maximum(m, jnp.max(ss[hh], axis=-1, keepdims=True))
            alpha = jnp.exp2(m - m_new)
            pv = jnp.dot(jnp.exp2((ss[hh] - m_new).astype(BF16)), v, preferred_element_type=F32)
            out.append((m_new, alpha * l + pv[:, V_HEAD_DIM:V_HEAD_DIM + 1], alpha * acc + pv[:, :V_HEAD_DIM]))
        return tuple(out)

    init = tuple((jnp.full((tq, 1), NEG_BIG, F32), jnp.zeros((tq, 1), F32), jnp.zeros((tq, V_HEAD_DIM), F32))
                 for _ in range(nh))
    carry = lax.fori_loop(0, i, lambda c, cr: step(c, cr, False), init)
    carry = step(i, carry, True)
    for hh in range(nh):
        _, l, acc = carry[hh]
        o_ref[0, :, hh * V_HEAD_DIM:(hh + 1) * V_HEAD_DIM] = (acc / l).astype(BF16)


def _mla_attn(q, k, v, tq=512, nh=4):
    b, h, s, dk = q.shape
    dv = v.shape[-1]
    resident = pl.Buffered(1)
    return pl.pallas_call(
        functools.partial(_mla_attn_kernel, tq=tq, nh=nh),
        grid=(b, h // nh, s // tq),
        in_specs=[pl.BlockSpec((1, nh, tq, dk), lambda bi, hi, i: (bi, hi, i, 0)),
                  pl.BlockSpec((1, nh, s, dk), lambda bi, hi, i: (bi, hi, 0, 0), pipeline_mode=resident),
                  pl.BlockSpec((1, nh, s, dv), lambda bi, hi, i: (bi, hi, 0, 0), pipeline_mode=resident)],
        out_specs=pl.BlockSpec((1, tq, nh * V_HEAD_DIM), lambda bi, hi, i: (bi, i, hi)),
        out_shape=jax.ShapeDtypeStruct((b, s, h * V_HEAD_DIM), BF16),
        compiler_params=_cparams("arbitrary", "arbitrary", "arbitrary"),
        name="mla_attn",
    )(q, k, v)


def _dil_attn_kernel(q_ref, kc_ref, kp_ref, vc_ref, vp_ref, o_ref, lse_ref, *, tq):
    i = pl.program_id(2)
    sub = DIL_SPAN
    row = lax.broadcasted_iota(jnp.int32, (sub, 2 * sub), 0)
    col = lax.broadcasted_iota(jnp.int32, (sub, 2 * sub), 1)
    band = jnp.logical_and(col >= row, col <= row + sub)
    first = jnp.logical_and(band, col >= jnp.where(i > 0, 0, sub))
    lane = lax.broadcasted_iota(jnp.int32, (sub, LANES), 1)
    scale = DIL_HEAD_DIM ** -0.5
    dn = (((1,), (1,)), ((), ()))
    chains = [(j, hh) for j in range(tq // sub) for hh in range(DIL_HEADS_PER_GROUP)]

    def window(cur_ref, prev_ref, j, cs):
        if j == 0:
            return jnp.concatenate([prev_ref[0, :, cs], cur_ref[0, :sub, cs]], axis=0)
        return cur_ref[0, (j - 1) * sub:(j + 1) * sub, cs]

    scores = []
    for j, hh in chains:
        cs = slice(hh * LANES, (hh + 1) * LANES)
        s = lax.dot_general(q_ref[0, j * sub:(j + 1) * sub, cs], window(kc_ref, kp_ref, j, cs), dn,
                            preferred_element_type=F32) * scale
        scores.append(jnp.where(first if j == 0 else band, s, NEG_BIG))
    lse_blk = [jnp.zeros((sub, LANES), F32) for _ in range(tq // sub)]
    for (j, hh), s in zip(chains, scores):
        cs = slice(hh * LANES, (hh + 1) * LANES)
        m = jnp.max(s, axis=-1, keepdims=True)
        p = jnp.exp(s - m)
        l = jnp.sum(p, axis=-1, keepdims=True)
        acc = jnp.dot(p.astype(BF16), window(vc_ref, vp_ref, j, cs), preferred_element_type=F32)
        o_ref[0, j * sub:(j + 1) * sub, cs] = (acc * (1.0 / l)).astype(BF16)
        lse_blk[j] = jnp.where(lane == hh, m + jnp.log(l), lse_blk[j])
    for j in range(tq // sub):
        lse_ref[0, j * sub:(j + 1) * sub, :] = lse_blk[j]


def _dil_attn(t, batch, seq, dilation):
    ln = seq // dilation
    tq = min(ln, 4 * DIL_SPAN)
    gc = DIL_GROUP_COLS
    ratio = tq // DIL_SPAN
    cur = lambda which: pl.BlockSpec((1, tq, gc), lambda b, r, i: (b, i, r * 3 + which))
    prev = lambda which: pl.BlockSpec(
        (1, DIL_SPAN, gc), lambda b, r, i: (b, jnp.maximum(i * ratio - 1, 0), r * 3 + which))
    o, lse = pl.pallas_call(
        functools.partial(_dil_attn_kernel, tq=tq),
        grid=(batch, dilation, ln // tq),
        in_specs=[cur(0), cur(1), prev(1), cur(2), prev(2)],
        out_specs=[pl.BlockSpec((1, tq, gc), lambda b, r, i: (b, i, r)),
                   pl.BlockSpec((1, tq, LANES), lambda b, r, i: (b, i, r))],
        out_shape=[jax.ShapeDtypeStruct((batch, ln, dilation * gc), BF16),
                   jax.ShapeDtypeStruct((batch, ln, dilation * LANES), F32)],
        compiler_params=_cparams("arbitrary", "arbitrary", "arbitrary"),
        name=f"dil_attn_d{dilation}",
    )(t, t, t, t, t)
    return o, lse


def _pack_halves(v):
    w = v.shape[1] // 2
    lo = lax.bitcast_convert_type(v[:, :w].astype(BF16).astype(F32), jnp.uint32)
    hi = lax.bitcast_convert_type(v[:, w:].astype(BF16).astype(F32), jnp.uint32)
    return (lo >> 16) | (hi & jnp.uint32(0xFFFF0000))


ROW_TILE = 8


def _row_tile(p):
    return (pl.ds(pl.multiple_of(p * ROW_TILE, ROW_TILE), ROW_TILE), slice(None))


def _store_row_tiles(ref, pk, base=0):
    rows = pk.shape[0]
    for c in range(ROW_TILE):
        ref[pl.ds(base + c, rows, stride=ROW_TILE), :] = pk[:, c * LANES:(c + 1) * LANES]


def _load_row_tiles(ref, rows, base=0):
    return jnp.concatenate([ref[pl.ds(base + c, rows, stride=ROW_TILE), :] for c in range(ROW_TILE)], axis=1)


def _unpack_halves(pk):
    lo = lax.bitcast_convert_type(pk << 16, F32)
    hi = lax.bitcast_convert_type(pk & jnp.uint32(0xFFFF0000), F32)
    return lo, hi


def _merge_kernel(oa_ref, o0_ref, o1_ref, o2_ref, l0_ref, l1_ref, l2_ref, ga_ref, gb_ref, x_ref,
                  gate_ref, shift_ref, scale_ref, gpost_ref, gpre_ref,
                  wa_ref, wb_ref, wo_ref, wrh_ref, wrl_ref,
                  x1_ref, h2_ref, h2pk_ref, logit_ref, o_sc, l_sc):
    tm = x_ref.shape[0]

    def natural(ref, sc, gi, chunks):
        dil = DIL_PATTERNS[gi][1]
        if dil == 1:
            return [ref[0, :, c * LANES:(c + 1) * LANES].astype(F32) for c in range(chunks)]
        for r in range(dil):
            for c in range(chunks):
                c0 = (r * chunks + c) * LANES
                sc.at[gi, c][pl.ds(r, tm // dil, stride=dil), :] = ref[0, :, c0:c0 + LANES].astype(F32)
        return [sc[gi, c] for c in range(chunks)]

    (l0,), (l1,), (l2,) = [natural(ref, l_sc, gi, 1) for gi, ref in enumerate((l0_ref, l1_ref, l2_ref))]
    o0, o1, o2 = [natural(ref, o_sc, gi, DIL_HEADS_PER_GROUP) for gi, ref in enumerate((o0_ref, o1_ref, o2_ref))]
    m = jnp.maximum(jnp.maximum(l0, l1), l2)
    e0, e1, e2 = jnp.exp(l0 - m), jnp.exp(l1 - m), jnp.exp(l2 - m)
    inv = 1.0 / (e0 + e1 + e2)
    w0, w1, w2 = e0 * inv, e1 * inv, e2 * inv
    parts = []
    for hh in range(DIL_HEADS_PER_GROUP):
        parts.append(w0[:, hh:hh + 1] * o0[hh] + w1[:, hh:hh + 1] * o1[hh] + w2[:, hh:hh + 1] * o2[hh])
    o_dil = jnp.concatenate(parts, axis=1).astype(BF16)
    y_a = jnp.dot(oa_ref[...], wa_ref[...], preferred_element_type=F32)
    y_b = jnp.dot(o_dil, wb_ref[...], preferred_element_type=F32)
    merged = ga_ref[...].astype(F32) * y_a + gb_ref[...].astype(F32) * y_b
    y = jnp.dot(merged.astype(BF16), wo_ref[...], preferred_element_type=F32)
    x1 = x_ref[...] + gate_ref[0] * _rms(y, gpost_ref[...])
    x1_ref[...] = x1
    h2 = _rms(x1, gpre_ref[...]) * (1.0 + scale_ref[0]) + shift_ref[0]
    _store_row_tiles(h2pk_ref, _pack_halves(h2))
    h2_hi = h2.astype(BF16)
    h2_ref[...] = h2_hi
    h2_lo = (h2 - h2_hi.astype(F32)).astype(BF16)
    both = jnp.dot(h2_hi, wrh_ref[...], preferred_element_type=F32)
    logits = both[:, :LANES] + both[:, LANES:] + jnp.dot(h2_lo, wrl_ref[...], preferred_element_type=F32)
    logit_ref[...] = logits.T


def _merge(oa, dil_o, dil_lse, gates, x2, mod3, gpost, gpre, wa, wb, wo, wr_hi, wr_lo, seq, tm=256):
    n, d = x2.shape
    per_b = seq // tm
    row = lambda w: pl.BlockSpec((tm, w), lambda i: (i, 0))
    const = lambda a: pl.BlockSpec(a.shape, lambda i: (0,) * a.ndim, pipeline_mode=pl.Buffered(1))
    modspec = lambda ch: pl.BlockSpec((1, 1, d), lambda i: ((i // per_b) * ADALN_CHUNKS + ch, 0, 0))
    strided = lambda gi, w: pl.BlockSpec((1, tm // DIL_PATTERNS[gi][1], DIL_PATTERNS[gi][1] * w),
                                         lambda i: (i // per_b, i % per_b, 0))
    return pl.pallas_call(
        _merge_kernel,
        grid=(n // tm,),
        in_specs=[row(oa.shape[1]),
                  strided(0, DIL_GROUP_COLS), strided(1, DIL_GROUP_COLS), strided(2, DIL_GROUP_COLS),
                  strided(0, LANES), strided(1, LANES), strided(2, LANES),
                  pl.BlockSpec((tm, d), lambda i: (i, 0)), pl.BlockSpec((tm, d), lambda i: (i, 1)),
                  row(d),
                  modspec(2), modspec(3), modspec(4),
                  const(gpost), const(gpre),
                  const(wa), const(wb), const(wo), const(wr_hi), const(wr_lo)],
        out_specs=[row(d), row(d), pl.BlockSpec((tm * ROW_TILE, LANES), lambda i: (i, 0)),
                   pl.BlockSpec((LANES, tm), lambda i: (0, i))],
        out_shape=[jax.ShapeDtypeStruct((n, d), F32),
                   jax.ShapeDtypeStruct((n, d), BF16),
                   jax.ShapeDtypeStruct((n * ROW_TILE, LANES), jnp.uint32),
                   jax.ShapeDtypeStruct((LANES, n), F32)],
        scratch_shapes=[pltpu.VMEM((DIL_GROUPS, DIL_HEADS_PER_GROUP, tm, LANES), F32),
                        pltpu.VMEM((DIL_GROUPS, 1, tm, LANES), F32)],
        compiler_params=_cparams("arbitrary"),
        name="merge_outproj",
    )(oa, *dil_o, *dil_lse, gates, gates, x2, mod3, mod3, mod3, gpost, gpre, wa, wb, wo, wr_hi, wr_lo)


def _route_kernel(lg_ref, bias_ref, pos_ref, w_ref, meta_ref, blke_ref, cnt_sc, base_sc, rank_sc, score_sc, *, tt):
    ps = pl.program_id(0)
    i = pl.program_id(1)
    per_group = N_EXPERTS // N_EXPERT_GROUPS
    neg_inf = -jnp.inf
    tile = pl.ds(pl.multiple_of(i * tt, tt), tt)

    @pl.when(jnp.logical_and(ps == 0, i == 0))
    def _():
        cnt_sc[...] = jnp.zeros_like(cnt_sc)

    @pl.when(ps == 0)
    def _():
        scores = _sigmoid(lg_ref[...])
        biased = scores + bias_ref[...]
        b3 = biased.reshape(N_EXPERT_GROUPS, per_group, tt)
        mem = lax.broadcasted_iota(jnp.int32, b3.shape, 1)
        m1 = jnp.max(b3, axis=1, keepdims=True)
        first = jnp.min(jnp.where(b3 == m1, mem, per_group), axis=1, keepdims=True)
        m2 = jnp.max(jnp.where(mem == first, neg_inf, b3), axis=1, keepdims=True)
        gs = m1 + m2
        gidx = lax.broadcasted_iota(jnp.int32, gs.shape, 0)
        grank = jnp.zeros(gs.shape, jnp.int32)
        for g2 in range(N_EXPERT_GROUPS):
            r = gs[g2:g2 + 1]
            beats = jnp.logical_or(r > gs, jnp.logical_and(r == gs, g2 < gidx))
            grank = grank + jnp.where(beats, 1, 0)
        sel = jnp.where(grank < TOPK_GROUPS, b3, neg_inf).reshape(N_EXPERTS, tt)
        eidx = lax.broadcasted_iota(jnp.int32, sel.shape, 0)
        erank = jnp.zeros(sel.shape, jnp.int32)
        for e2 in range(N_EXPERTS):
            r = sel[e2:e2 + 1, :]
            beats = jnp.logical_or(r > sel, jnp.logical_and(r == sel, e2 < eidx))
            erank = erank + jnp.where(beats, 1, 0)
        rank_sc[:, tile] = erank
        score_sc[:, tile] = scores
        cnt_sc[...] = cnt_sc[...] + jnp.sum(jnp.where(erank < TOP_K, 1.0, 0.0), axis=1,
                                            keepdims=True).astype(jnp.int32)

    @pl.when(jnp.logical_and(ps == 1, i == 0))
    def _():
        cnt = cnt_sc[...]
        pc = ((cnt + (MOE_BLOCK - 1)) // MOE_BLOCK) * MOE_BLOCK
        pcb = jnp.broadcast_to(pc, (N_EXPERTS, LANES))
        eid = lax.broadcasted_iota(jnp.int32, (N_EXPERTS, LANES), 0)
        pends = jnp.zeros((N_EXPERTS, LANES), jnp.int32)
        for e2 in range(N_EXPERTS):
            pends = pends + jnp.where(eid >= e2, pcb[e2:e2 + 1, :], 0)
        pst = pends - pcb
        base_sc[...] = pst[:, 0:1]
        meta_ref[0] = jnp.broadcast_to(cnt, (N_EXPERTS, LANES))
        meta_ref[1] = pst
        meta_ref[2] = pends
        nbl = blke_ref.shape[1]
        blk_start = lax.broadcasted_iota(jnp.int32, (N_EXPERTS, nbl), 1) * MOE_BLOCK
        pend_b = jnp.broadcast_to(pends[:, 0:1], (N_EXPERTS, nbl))
        be = jnp.sum(jnp.where(pend_b <= blk_start, 1, 0), axis=0, keepdims=True)
        blke_ref[...] = jnp.broadcast_to(jnp.minimum(be, N_EXPERTS - 1), blke_ref.shape)

    @pl.when(ps == 1)
    def _():
        erank = rank_sc[:, tile]
        scores = score_sc[:, tile]
        esel = erank < TOP_K
        mask_f = jnp.where(esel, 1.0, 0.0)
        tile_cnt = jnp.sum(mask_f, axis=1, keepdims=True).astype(jnp.int32)
        rr = lax.broadcasted_iota(jnp.int32, (tt, tt), 0)
        cc = lax.broadcasted_iota(jnp.int32, (tt, tt), 1)
        upper = jnp.where(rr < cc, 1.0, 0.0).astype(BF16)
        prefix = jnp.dot(mask_f.astype(BF16), upper, preferred_element_type=F32)
        posd = base_sc[...] + prefix.astype(jnp.int32)
        base_sc[...] = base_sc[...] + tile_cnt
        wsel = jnp.where(esel, scores, 0.0)
        denom = jnp.sum(wsel, axis=0, keepdims=True)
        wn = wsel / (denom + 1e-20) * ROUTED_SCALE
        prow, wrow = [], []
        for kk in range(TOP_K):
            hit = erank == kk
            prow.append(jnp.sum(jnp.where(hit, posd, 0), axis=0, keepdims=True))
            wrow.append(jnp.sum(jnp.where(hit, wn, 0.0), axis=0, keepdims=True))
        pad = pos_ref.shape[0] - TOP_K
        pos_ref[...] = jnp.concatenate(prow + [jnp.zeros((pad, tt), jnp.int32)], axis=0)
        w_ref[...] = jnp.concatenate(wrow + [jnp.zeros((pad, tt), F32)], axis=0)


def _route(logits_t, bias_col, nb, tt=256):
    n = logits_t.shape[1]
    nbl = -(-nb // LANES) * LANES
    return pl.pallas_call(
        functools.partial(_route_kernel, tt=tt),
        grid=(2, n // tt),
        in_specs=[pl.BlockSpec((N_EXPERTS, tt), lambda ps, i: (0, i * (1 - ps))),
                  pl.BlockSpec((N_EXPERTS, 1), lambda ps, i: (0, 0))],
        out_specs=[pl.BlockSpec((8, tt), lambda ps, i: (0, ps * i)),
                   pl.BlockSpec((8, tt), lambda ps, i: (0, ps * i)),
                   pl.BlockSpec((3, N_EXPERTS, LANES), lambda ps, i: (0, 0, 0)),
                   pl.BlockSpec((8, nbl), lambda ps, i: (0, 0))],
        out_shape=[jax.ShapeDtypeStruct((8, n), jnp.int32),
                   jax.ShapeDtypeStruct((8, n), F32),
                   jax.ShapeDtypeStruct((3, N_EXPERTS, LANES), jnp.int32),
                   jax.ShapeDtypeStruct((8, nbl), jnp.int32)],
        scratch_shapes=[pltpu.VMEM((N_EXPERTS, 1), jnp.int32), pltpu.VMEM((N_EXPERTS, 1), jnp.int32),
                        pltpu.VMEM((N_EXPERTS, n), jnp.int32), pltpu.VMEM((N_EXPERTS, n), F32)],
        compiler_params=_cparams("arbitrary", "arbitrary"),
        name="moe_route",
    )(logits_t, bias_col)


def _dispatch_kernel(pos_ref, cnt_ref, pst_ref, h_hbm, hb_ref, wsg_ref, wsu_ref, wsd_ref, xs_hbm, ysh_ref,
                     hbuf, zrow, in_sem, sc_sem, sem, *, tm, n):
    i = pl.program_id(0)
    nt = pl.num_programs(0)
    tile_rows = tm * ROW_TILE

    def fetch(tile, slot):
        return pltpu.make_async_copy(h_hbm.at[pl.ds(pl.multiple_of(tile * tile_rows, tile_rows), tile_rows), :],
                                     hbuf.at[slot], in_sem.at[slot])

    def wait_rows(slot):
        for kk in range(TOP_K):
            pltpu.make_async_copy(hbuf.at[slot], xs_hbm.at[pl.ds(0, tile_rows), :], sc_sem.at[slot]).wait()

    @pl.when(i == 0)
    def _():
        fetch(0, 0).start()
        fetch(1, 1).start()

    slot = i % 3
    fetch(i, slot).wait()
    src = hbuf.at[slot]

    def body(g, carry):
        for u in range(ROW_DMA_UNROLL):
            r = g * ROW_DMA_UNROLL + u
            for kk in range(TOP_K):
                p = pos_ref[kk * n + i * tm + r]
                pltpu.make_async_copy(src.at[_row_tile(r)], xs_hbm.at[_row_tile(p)],
                                      sc_sem.at[slot]).start(priority=kk % 2)
        return carry
    lax.fori_loop(0, tm // ROW_DMA_UNROLL, body, 0)

    hb = hb_ref[...]
    g = jnp.dot(hb, wsg_ref[...], preferred_element_type=F32)
    u = jnp.dot(hb, wsu_ref[...], preferred_element_type=F32)
    ysh_ref[...] = jnp.dot((g * _sigmoid(g) * u).astype(BF16), wsd_ref[...],
                           preferred_element_type=F32).astype(BF16)

    @pl.when(i >= 1)
    def _():
        wait_rows((i + 2) % 3)

    @pl.when(i + 2 < nt)
    def _():
        fetch(i + 2, (i + 2) % 3).start()

    @pl.when(i == nt - 1)
    def _():
        wait_rows(slot)

    @pl.when(i == nt - 1)
    def _():
        zrow[...] = jnp.zeros_like(zrow)

        def per_expert(e, carry):
            cnt = cnt_ref[e]
            first = pst_ref[e] + cnt
            npad = ((cnt + (MOE_BLOCK - 1)) // MOE_BLOCK) * MOE_BLOCK - cnt

            def start(s, c2):
                pltpu.make_async_copy(zrow.at[_row_tile(0)], xs_hbm.at[_row_tile(first + s)], sem.at[1]).start()
                return c2

            def wait(s, c2):
                pltpu.make_async_copy(zrow.at[_row_tile(0)], xs_hbm.at[_row_tile(0)], sem.at[1]).wait()
                return c2
            lax.fori_loop(0, npad, start, 0)
            lax.fori_loop(0, npad, wait, 0)
            return carry
        lax.fori_loop(0, N_EXPERTS, per_expert, 0)

        last = N_EXPERTS - 1
        used = (pst_ref[last] + cnt_ref[last] + (MOE_BLOCK - 1)) // MOE_BLOCK
        blk_rows = MOE_BLOCK * ROW_TILE

        def tail(b, carry):
            cp = pltpu.make_async_copy(zrow, xs_hbm.at[pl.ds(pl.multiple_of(b * blk_rows, blk_rows), blk_rows), :],
                                       sem.at[1])
            cp.start()
            cp.wait()
            return carry
        lax.fori_loop(used, xs_hbm.shape[0] // blk_rows, tail, 0)


def _dispatch(pos_flat, cnt, pst, h2pk, h2, wsg, wsu, wsd, nb, tm=256):
    n, d = h2.shape
    assert n // tm >= 2
    const = lambda a: pl.BlockSpec(a.shape, lambda i, *_: (0,) * a.ndim)
    grid_spec = pltpu.PrefetchScalarGridSpec(
        num_scalar_prefetch=3,
        grid=(n // tm,),
        in_specs=[pl.BlockSpec(memory_space=pl.ANY),
                  pl.BlockSpec((tm, d), lambda i, *_: (i, 0)),
                  const(wsg), const(wsu), const(wsd)],
        out_specs=[pl.BlockSpec(memory_space=pl.ANY), pl.BlockSpec((tm, d), lambda i, *_: (i, 0))],
        scratch_shapes=[pltpu.VMEM((3, tm * ROW_TILE, LANES), jnp.uint32),
                        pltpu.VMEM((MOE_BLOCK * ROW_TILE, LANES), jnp.uint32),
                        pltpu.SemaphoreType.DMA((3,)), pltpu.SemaphoreType.DMA((3,)),
                        pltpu.SemaphoreType.DMA((2,))],
    )
    return pl.pallas_call(
        functools.partial(_dispatch_kernel, tm=tm, n=n),
        grid_spec=grid_spec,
        out_shape=[jax.ShapeDtypeStruct((nb * MOE_BLOCK * ROW_TILE, LANES), jnp.uint32),
                   jax.ShapeDtypeStruct((n, d), BF16)],
        compiler_params=_cparams("arbitrary"),
        name="moe_dispatch",
    )(pos_flat, cnt, pst, h2pk, h2, wsg, wsu, wsd)


def _gmm_kernel(nact_ref, blke_ref, xs_ref, wg_hbm, wu_hbm, wd_hbm, o_ref,
                wgf, wuf, wdf, wgb, wub, wdb, run_sc, sem):
    nact = nact_ref[0]

    def fetch(e, slot):
        return (pltpu.make_async_copy(wg_hbm.at[e], wgf.at[slot], sem.at[slot, 0]),
                pltpu.make_async_copy(wu_hbm.at[e], wuf.at[slot], sem.at[slot, 1]),
                pltpu.make_async_copy(wd_hbm.at[e], wdf.at[slot], sem.at[slot, 2]))

    @pl.when(pl.program_id(0) == 0)
    def _():
        run_sc[0] = 0
        for cp in fetch(blke_ref[0], 0):
            cp.start()

    for sb in range(GMM_SUB):
        i = pl.program_id(0) * GMM_SUB + sb
        base = sb * MOE_BLOCK * ROW_TILE

        @pl.when(i < nact)
        def _(i=i, base=base):
            e = blke_ref[i]
            changed = jnp.logical_or(i == 0, e != blke_ref[jnp.maximum(i - 1, 0)])

            @pl.when(changed)
            def _():
                run = run_sc[0]
                slot = run % 2
                for cp in fetch(e, slot):
                    cp.wait()
                wgb[...] = wgf[slot].astype(BF16)
                wub[...] = wuf[slot].astype(BF16)
                wdb[...] = wdf[slot].astype(BF16)
                nxt = lax.while_loop(lambda j: jnp.logical_and(j < nact, blke_ref[jnp.minimum(j, nact - 1)] == e),
                                     lambda j: j + 1, i + 1)

                @pl.when(nxt < nact)
                def _():
                    for cp in fetch(blke_ref[jnp.minimum(nxt, nact - 1)], 1 - slot):
                        cp.start(priority=1)
                run_sc[0] = run + 1

            lo, hi = _unpack_halves(_load_row_tiles(xs_ref, MOE_BLOCK, base))
            lo, hi = lo.astype(BF16), hi.astype(BF16)
            half = lo.shape[1]
            g = (jnp.dot(lo, wgb[:half, :], preferred_element_type=F32)
                 + jnp.dot(hi, wgb[half:, :], preferred_element_type=F32))
            u = (jnp.dot(lo, wub[:half, :], preferred_element_type=F32)
                 + jnp.dot(hi, wub[half:, :], preferred_element_type=F32))
            a = (g * _sigmoid(g) * u).astype(BF16)
            _store_row_tiles(o_ref, _pack_halves(jnp.dot(a, wdb[...], preferred_element_type=F32)), base)

        @pl.when(i >= nact)
        def _(base=base):
            o_ref[pl.ds(base, MOE_BLOCK * ROW_TILE), :] = jnp.zeros((MOE_BLOCK * ROW_TILE, LANES), o_ref.dtype)


def _gmm(nact, blk_e, xs, w_gate, w_up, w_down, nb):
    d, f = w_gate.shape[1:]
    assert nb % GMM_SUB == 0
    rows = GMM_SUB * MOE_BLOCK * ROW_TILE
    blk = lambda i, na: jnp.minimum(i, (na[0] - 1) // GMM_SUB)
    hbm = pl.BlockSpec(memory_space=pl.ANY)
    grid_spec = pltpu.PrefetchScalarGridSpec(
        num_scalar_prefetch=2,
        grid=(nb // GMM_SUB,),
        in_specs=[pl.BlockSpec((rows, LANES), lambda i, na, be: (blk(i, na), 0)), hbm, hbm, hbm],
        out_specs=pl.BlockSpec((rows, LANES), lambda i, na, be: (i, 0)),
        scratch_shapes=[pltpu.VMEM((2, d, f), F32), pltpu.VMEM((2, d, f), F32), pltpu.VMEM((2, f, d), F32),
                        pltpu.VMEM((d, f), BF16), pltpu.VMEM((d, f), BF16), pltpu.VMEM((f, d), BF16),
                        pltpu.SMEM((1,), jnp.int32), pltpu.SemaphoreType.DMA((2, 3))],
    )
    return pl.pallas_call(
        _gmm_kernel,
        grid_spec=grid_spec,
        out_shape=jax.ShapeDtypeStruct((nb * MOE_BLOCK * ROW_TILE, LANES), jnp.uint32),
        compiler_params=_cparams("arbitrary"),
        name="moe_experts",
    )(nact, blk_e, xs, w_gate, w_up, w_down)


def _final_kernel(pos_ref, tw_ref, ysh_ref, x1_ref, gate_ref, gpost_ref, yb_hbm, o_ref, rbuf, sem, *, tm, n):
    i = pl.program_id(0)
    slot = i % 2

    def gather(tile, sl):
        def body(g, carry):
            for u in range(ROW_DMA_UNROLL):
                r = g * ROW_DMA_UNROLL + u
                for kk in range(TOP_K):
                    p = pos_ref[kk * n + tile * tm + r]
                    pltpu.make_async_copy(yb_hbm.at[_row_tile(p)], rbuf.at[sl, kk].at[_row_tile(r)],
                                          sem.at[sl]).start(priority=kk % 2)
            return carry
        lax.fori_loop(0, tm // ROW_DMA_UNROLL, body, 0)

    @pl.when(i == 0)
    def _():
        gather(0, 0)

    @pl.when(i + 1 < pl.num_programs(0))
    def _():
        gather(i + 1, 1 - slot)

    for kk in range(TOP_K):
        pltpu.make_async_copy(yb_hbm.at[pl.ds(0, tm * ROW_TILE), :], rbuf.at[slot, kk], sem.at[slot]).wait()
    tw = tw_ref[...]
    half = ROW_TILE * LANES
    r_lo = jnp.zeros((tm, half), F32)
    r_hi = jnp.zeros((tm, half), F32)
    for kk in range(TOP_K):
        lo, hi = _unpack_halves(_load_row_tiles(rbuf.at[slot, kk], tm))
        wk = tw[:, kk:kk + 1]
        r_lo = r_lo + wk * lo
        r_hi = r_hi + wk * hi
    y = ysh_ref[...].astype(F32) + jnp.concatenate([r_lo, r_hi], axis=1)
    o_ref[...] = x1_ref[...] + gate_ref[0] * _rms(y, gpost_ref[...])


def _final(pos_flat, top_w8, ysh, x1, mod3, gpost, yb, seq, tm=256):
    n, d = x1.shape
    per_b = seq // tm
    const = lambda a: pl.BlockSpec(a.shape, lambda i, ps: (0,) * a.ndim)
    row = lambda w: pl.BlockSpec((tm, w), lambda i, ps: (i, 0))
    grid_spec = pltpu.PrefetchScalarGridSpec(
        num_scalar_prefetch=1,
        grid=(n // tm,),
        in_specs=[row(top_w8.shape[1]), row(d), row(d),
                  pl.BlockSpec((1, 1, d), lambda i, ps: ((i // per_b) * ADALN_CHUNKS + 5, 0, 0)),
                  const(gpost),
                  pl.BlockSpec(memory_space=pl.ANY)],
        out_specs=row(d),
        scratch_shapes=[pltpu.VMEM((2, TOP_K, tm * ROW_TILE, LANES), jnp.uint32), pltpu.SemaphoreType.DMA((2,))],
    )
    return pl.pallas_call(
        functools.partial(_final_kernel, tm=tm, n=n),
        grid_spec=grid_spec,
        out_shape=jax.ShapeDtypeStruct((n, d), F32),
        compiler_params=_cparams("arbitrary"),
        name="moe_combine_final",
    )(pos_flat, top_w8, ysh, x1, mod3, gpost, yb)


def _rope_tables(positions, dim, passthrough):
    half = dim // 2
    inv_freq = 1.0 / (ROPE_THETA ** (jnp.arange(0, dim, 2, dtype=F32) / dim))
    freq = jnp.concatenate([inv_freq, inv_freq, jnp.zeros((LANES - dim,), F32)])
    ang = positions.astype(F32).reshape(-1, 1) * freq[None, :]
    cos, sin = jnp.cos(ang), jnp.sin(ang)
    lane = jnp.arange(LANES)[None, :]
    c_tab = jnp.where(lane < dim, cos, 1.0 if passthrough else 0.0)
    s_fwd = jnp.where(jnp.logical_and(lane >= half, lane < dim), sin, 0.0)
    s_bwd = jnp.where(lane < half, -sin, 0.0)
    return c_tab, s_fwd, s_bwd


def kernel(x, c, positions, w_ada, b_ada, attn_pre_g, w_in, q_a_norm_g, w_q_up, kv_a_norm_g, w_kv_up, w_mla_o, w_dil_o, w_out, attn_post_g, ffn_pre_g, w_router, router_bias, w_exp_gate, w_exp_up, w_exp_down, w_sh_gate, w_sh_up, w_sh_down, ffn_post_g):
    batch, seq, d = x.shape
    n = batch * seq
    depth = w_ada.shape[0]

    m_c, m_sf, m_sb = _rope_tables(positions, QK_ROPE_DIM, passthrough=False)
    d_c, d_sf, d_sb = _rope_tables(positions, DIL_ROT_DIM, passthrough=True)

    x2 = x.reshape(n, d)
    c8 = jnp.pad(c, ((0, 8 - batch), (0, 0)))
    for l in range(depth):
        mod = _ada(c8, w_ada[l], b_ada[l].reshape(1, -1))
        mod3 = mod[:batch].reshape(batch * ADALN_CHUNKS, 1, d)

        wi = w_in[l]
        o_dil = Q_LORA_RANK + KV_LORA_RANK + QK_ROPE_DIM
        o_ga = o_dil + 3 * DIL_HEADS * DIL_HEAD_DIM
        n_gate, n_dil = wi.shape[1] - o_ga, o_ga - o_dil
        n_a = o_dil + LANES - QK_ROPE_DIM
        c_dil, c_a = n_gate, -(-(n_gate + n_dil) // n_a) * n_a
        w_full = _wprep(wi.T, ((0, o_ga, n_gate), (c_dil, o_dil, n_dil), (c_a, 0, o_dil)), c_a + n_a)
        wq3 = w_q_up[l].reshape(Q_LORA_RANK, MLA_HEADS, MLA_QK_DIM)
        wq = jnp.concatenate([wq3, jnp.zeros((Q_LORA_RANK, MLA_HEADS, MLA_QK_PAD - MLA_QK_DIM), F32)],
                             axis=2).reshape(Q_LORA_RANK, MLA_HEADS * MLA_QK_PAD).astype(BF16)
        wkv3 = w_kv_up[l].reshape(KV_LORA_RANK, MLA_HEADS, QK_NOPE_DIM + V_HEAD_DIM)
        wkv = jnp.concatenate([wkv3[:, :, :QK_NOPE_DIM].reshape(KV_LORA_RANK, -1),
                               wkv3[:, :, QK_NOPE_DIM:].reshape(KV_LORA_RANK, -1)], axis=1).astype(BF16)

        h, q, k, v = _front(x2, attn_pre_g[l].reshape(1, d), mod3, w_full, c_a, n_a,
                            q_a_norm_g[l].reshape(1, -1), kv_a_norm_g[l].reshape(1, -1), wq, wkv,
                            m_c, m_sf, m_sb, batch, seq)
        gates = _mm(h, w_full, 0, n_gate, act="sigmoid")
        o_mla = _mla_attn(q, k, v).reshape(n, MLA_HEADS * V_HEAD_DIM)
        dil_o, dil_lse = [], []
        for g, (_, dilation) in enumerate(DIL_PATTERNS):
            qkv = _dilproj(h, w_full, c_dil, g, d_c, d_sf, d_sb, batch, seq, dilation)
            o_g, lse_g = _dil_attn(qkv, batch, seq, dilation)
            dil_o.append(o_g)
            dil_lse.append(lse_g)

        wr = jnp.pad(w_router[l], ((0, 0), (0, LANES - N_EXPERTS)))
        wr_hi = wr.astype(BF16)
        wr_lo = (wr - wr_hi.astype(F32)).astype(BF16)
        x1, h2, h2pk, logits_t = _merge(o_mla, dil_o, dil_lse, gates, x2, mod3,
                                        attn_post_g[l].reshape(1, d), ffn_pre_g[l].reshape(1, d),
                                        w_mla_o[l].astype(BF16), w_dil_o[l].astype(BF16), w_out[l].astype(BF16),
                                        jnp.concatenate([wr_hi, wr_lo], axis=1), wr_hi, seq)

        nb = -(-(n * TOP_K + N_EXPERTS * (MOE_BLOCK - 1)) // MOE_BLOCK)
        pos_t, w_t, meta, blk_e = _route(logits_t, router_bias[l].astype(F32).reshape(N_EXPERTS, 1), nb)
        pos_flat = pos_t.reshape(-1)
        nact = meta[2, N_EXPERTS - 1, :1] // MOE_BLOCK
        xs, ysh = _dispatch(pos_flat, meta[0, :, 0], meta[1, :, 0], h2pk, h2,
                            w_sh_gate[l].astype(BF16), w_sh_up[l].astype(BF16), w_sh_down[l].astype(BF16), nb)
        yb = _gmm(nact, blk_e[0], xs, w_exp_gate[l], w_exp_up[l], w_exp_down[l], nb)
        x2 = _final(pos_flat, w_t.T, ysh, x1, mod3, ffn_post_g[l].reshape(1, d), yb, seq)
    return x2.reshape(batch, seq, d)
```

```python
import functools

import jax
import jax.numpy as jnp
from jax import lax
from jax.experimental import pallas as pl
from jax.experimental.pallas import tpu as pltpu

F32 = jnp.float32
BF16 = jnp.bfloat16

D_MODEL = 2048
NORM_EPS = 1e-6
ROPE_THETA = 500000.0
ADALN_CHUNKS = 6

MLA_HEADS = 8
Q_LORA_RANK = 512
KV_LORA_RANK = 512
QK_NOPE_DIM = 128
QK_ROPE_DIM = 64
V_HEAD_DIM = 128
MLA_QK_DIM = QK_NOPE_DIM + QK_ROPE_DIM
MLA_QK_PAD = 256

DIL_PATTERNS = ((128, 1), (512, 4), (2048, 16))
DIL_GROUPS = len(DIL_PATTERNS)
DIL_HEADS_PER_GROUP = 4
DIL_HEADS = DIL_GROUPS * DIL_HEADS_PER_GROUP
DIL_HEAD_DIM = 128
DIL_ROT_DIM = DIL_HEAD_DIM // 4
DIL_SPAN = 128
DIL_GROUP_COLS = DIL_HEADS_PER_GROUP * DIL_HEAD_DIM

N_EXPERTS = 64
N_EXPERT_GROUPS = 8
TOPK_GROUPS = 4
TOP_K = 6
EXPERT_DIM = 512
SHARED_DIM = 512
ROUTED_SCALE = 2.5
MOE_BLOCK = 256

LANES = 128
NEG_BIG = -1e30
LOG2_E = 1.4426950408889634
ROW_DMA_UNROLL = 4
GMM_SUB = 2
VMEM_LIMIT = 56 * 1024 * 1024


def _cparams(*sem):
    return pltpu.CompilerParams(dimension_semantics=sem, vmem_limit_bytes=VMEM_LIMIT)


def _sigmoid(v):
    return 1.0 / (1.0 + jnp.exp(-v))


def _rms(v, g):
    ms = jnp.mean(v * v, axis=-1, keepdims=True)
    return v * lax.rsqrt(ms + NORM_EPS) * g


def _ada_kernel(c_ref, w_ref, b_ref, o_ref):
    c = c_ref[...]
    a = (c * _sigmoid(c)).astype(BF16)
    o_ref[...] = jnp.dot(a, w_ref[...].astype(BF16), preferred_element_type=F32) + b_ref[...]


def _ada(c8, w_ada, b_ada, tn=1536):
    d, n = w_ada.shape
    return pl.pallas_call(
        _ada_kernel,
        grid=(n // tn,),
        in_specs=[pl.BlockSpec((8, d), lambda j: (0, 0)),
                  pl.BlockSpec((d, tn), lambda j: (0, j)),
                  pl.BlockSpec((1, tn), lambda j: (0, j))],
        out_specs=pl.BlockSpec((8, tn), lambda j: (0, j)),
        out_shape=jax.ShapeDtypeStruct((8, n), F32),
        compiler_params=_cparams("arbitrary"),
        name="ada_mod",
    )(c8, w_ada, b_ada)


def _wprep_source(j, segments):
    src = j * 0
    valid = j * 0
    for dst, s0, width in segments:
        t0, t1 = dst // LANES, (dst + width + LANES - 1) // LANES
        inside = jnp.logical_and(j >= t0, j < t1)
        src = jnp.where(inside, s0 + (j - t0) * LANES, src)
        valid = jnp.where(inside, jnp.minimum(dst + width - j * LANES, LANES), valid)
    return src, valid


def _wprep_kernel(*refs, segments, tiles):
    o_ref = refs[-1]
    for t in range(tiles):
        _, valid = _wprep_source(pl.program_id(0) * tiles + t, segments)
        blk = jnp.concatenate([refs[2 * t][...], refs[2 * t + 1][...]], axis=0).T
        lane = lax.broadcasted_iota(jnp.int32, blk.shape, 1)
        o_ref[:, t * LANES:(t + 1) * LANES] = jnp.where(lane < valid, blk, 0.0).astype(BF16)


def _wprep(w_t, segments, cols, tiles=3):
    _, k = w_t.shape
    half = LANES // 2
    assert all(dst % LANES == 0 and s0 % half == 0 and width % half == 0 for dst, s0, width in segments)
    assert cols % (tiles * LANES) == 0
    part = lambda t, which: pl.BlockSpec(
        (half, k), lambda j: (_wprep_source(j * tiles + t, segments)[0] // half + which, 0))
    return pl.pallas_call(
        functools.partial(_wprep_kernel, segments=segments, tiles=tiles),
        grid=(cols // (tiles * LANES),),
        in_specs=[part(t, which) for t in range(tiles) for which in range(2)],
        out_specs=pl.BlockSpec((k, tiles * LANES), lambda j: (0, j)),
        out_shape=jax.ShapeDtypeStruct((k, cols), BF16),
        compiler_params=_cparams("arbitrary"),
        name="w_in_prep",
    )(*([w_t] * (2 * tiles)))


def _rope_lanes(t, c_tab, s_fwd, s_bwd, half):
    return t * c_tab + pltpu.roll(t, half, 1) * s_fwd + pltpu.roll(t, LANES - half, 1) * s_bwd


def _rope_tables(pos, freq, dim, passthrough):
    ang = pos * freq
    cos, sin = jnp.cos(ang), jnp.sin(ang)
    lane = lax.broadcasted_iota(jnp.int32, ang.shape, 1)
    half = dim // 2
    c_tab = jnp.where(lane < dim, cos, 1.0 if passthrough else 0.0)
    s_fwd = jnp.where(jnp.logical_and(lane >= half, lane < dim), sin, 0.0)
    s_bwd = jnp.where(lane < half, -sin, 0.0)
    return c_tab, s_fwd, s_bwd


def _front_kernel(x_ref, g_ref, scale_ref, shift_ref, wa_ref, gq_ref, gkv_ref, wq_ref, wkv_ref,
                  pos_ref, freq_ref, h_ref, q_ref, k_ref, v_ref):
    h = (_rms(x_ref[...], g_ref[...]) * (1.0 + scale_ref[0]) + shift_ref[0]).astype(BF16)
    h_ref[...] = h
    a = jnp.dot(h, wa_ref[...], preferred_element_type=F32)
    qa = a[:, :Q_LORA_RANK]
    ckv = a[:, Q_LORA_RANK:Q_LORA_RANK + KV_LORA_RANK]
    kr = a[:, Q_LORA_RANK + KV_LORA_RANK:]
    c_tab, s_fwd, s_bwd = _rope_tables(pos_ref[...], freq_ref[...], QK_ROPE_DIM, passthrough=False)
    half = QK_ROPE_DIM // 2
    q = jnp.dot(_rms(qa, gq_ref[...]).astype(BF16), wq_ref[...], preferred_element_type=F32)
    q = q * (MLA_QK_DIM ** -0.5 * LOG2_E)
    kv = jnp.dot(_rms(ckv, gkv_ref[...]).astype(BF16), wkv_ref[...], preferred_element_type=F32)
    k_rot = _rope_lanes(kr, c_tab, s_fwd, s_bwd, half).astype(BF16)
    lane = lax.broadcasted_iota(jnp.int32, (a.shape[0], LANES), 1)
    ones_col = jnp.where(lane == 0, 1.0, 0.0).astype(BF16)
    for hh in range(MLA_HEADS):
        base = hh * MLA_QK_PAD
        q_ref[0, hh, :, :LANES] = q[:, base:base + LANES].astype(BF16)
        q_ref[0, hh, :, LANES:] = _rope_lanes(q[:, base + LANES:base + 2 * LANES],
                                              c_tab, s_fwd, s_bwd, half).astype(BF16)
        k_ref[0, hh, :, :LANES] = kv[:, hh * LANES:(hh + 1) * LANES].astype(BF16)
        k_ref[0, hh, :, LANES:] = k_rot
        v_off = MLA_HEADS * LANES + hh * LANES
        v_ref[0, hh, :, :LANES] = kv[:, v_off:v_off + LANES].astype(BF16)
        v_ref[0, hh, :, LANES:] = ones_col


def _front(x2, g, mod3, w_full, col_a, n_a, gq, gkv, wq, wkv, pos_col, freq, batch, seq, tm=512):
    n, d = x2.shape
    per_b = seq // tm
    const = lambda a: pl.BlockSpec(a.shape, lambda i: (0,) * a.ndim, pipeline_mode=pl.Buffered(1))
    head_major = lambda w: pl.BlockSpec((1, MLA_HEADS, tm, w), lambda i: (i // per_b, 0, i % per_b, 0))
    return pl.pallas_call(
        _front_kernel,
        grid=(n // tm,),
        in_specs=[pl.BlockSpec((tm, d), lambda i: (i, 0)),
                  pl.BlockSpec((1, d), lambda i: (0, 0)),
                  pl.BlockSpec((1, 1, d), lambda i: ((i // per_b) * ADALN_CHUNKS + 1, 0, 0)),
                  pl.BlockSpec((1, 1, d), lambda i: ((i // per_b) * ADALN_CHUNKS + 0, 0, 0)),
                  pl.BlockSpec((d, n_a), lambda i: (0, col_a // n_a), pipeline_mode=pl.Buffered(1)),
                  const(gq), const(gkv), const(wq), const(wkv),
                  pl.BlockSpec((tm, 1), lambda i: (i, 0)), pl.BlockSpec((1, LANES), lambda i: (0, 0))],
        out_specs=[pl.BlockSpec((tm, d), lambda i: (i, 0)),
                   head_major(MLA_QK_PAD), head_major(MLA_QK_PAD), head_major(2 * V_HEAD_DIM)],
        out_shape=[jax.ShapeDtypeStruct((n, d), BF16),
                   jax.ShapeDtypeStruct((batch, MLA_HEADS, seq, MLA_QK_PAD), BF16),
                   jax.ShapeDtypeStruct((batch, MLA_HEADS, seq, MLA_QK_PAD), BF16),
                   jax.ShapeDtypeStruct((batch, MLA_HEADS, seq, 2 * V_HEAD_DIM), BF16)],
        compiler_params=_cparams("arbitrary"),
        name="front_mla_prep",
    )(x2, g, mod3, mod3, w_full, gq, gkv, wq, wkv, pos_col, freq)


def _mm_kernel(h_ref, w_ref, o_ref, *, act):
    y = jnp.dot(h_ref[...], w_ref[...], preferred_element_type=F32)
    if act == "sigmoid":
        y = _sigmoid(y)
    o_ref[...] = y.astype(o_ref.dtype)


def _mm(h, w, col0, cols, act=None, tm=1024, tn=1024):
    n, k = h.shape
    tn = min(tn, cols)
    j0 = col0 // tn
    return pl.pallas_call(
        functools.partial(_mm_kernel, act=act),
        grid=(cols // tn, n // tm),
        in_specs=[pl.BlockSpec((tm, k), lambda j, i: (i, 0)),
                  pl.BlockSpec((k, tn), lambda j, i: (0, j0 + j))],
        out_specs=pl.BlockSpec((tm, tn), lambda j, i: (i, j)),
        out_shape=jax.ShapeDtypeStruct((n, cols), BF16),
        compiler_params=_cparams("arbitrary", "arbitrary"),
        name="in_proj_" + (act or "plain"),
    )(h, w)


def _dilproj_kernel(h_ref, wq_ref, wk_ref, wv_ref, *rest, dilation, make_tables):
    hb = h_ref[...]
    y = jnp.concatenate([jnp.dot(hb, w_ref[...], preferred_element_type=F32) for w_ref in (wq_ref, wk_ref, wv_ref)],
                        axis=1)
    if make_tables:
        pos_ref, freq_ref, o_ref, c_out, sf_out, sb_out, y_sc = rest
        c_tab, s_fwd, s_bwd = _rope_tables(pos_ref[...], freq_ref[...], DIL_ROT_DIM, passthrough=True)
        c_out[...], sf_out[...], sb_out[...] = c_tab, s_fwd, s_bwd
    else:
        c_ref, sf_ref, sb_ref, o_ref, y_sc = rest
        c_tab, s_fwd, s_bwd = c_ref[...], sf_ref[...], sb_ref[...]
    cols = 3 * DIL_GROUP_COLS
    rows = h_ref.shape[0] // dilation
    n_rot = 2 * DIL_HEADS_PER_GROUP
    for hh in range(3 * DIL_HEADS_PER_GROUP):
        t = y[:, hh * LANES:(hh + 1) * LANES]
        if hh < n_rot:
            t = _rope_lanes(t, c_tab, s_fwd, s_bwd, DIL_ROT_DIM // 2)
        if dilation == 1:
            o_ref[0, :, hh * LANES:(hh + 1) * LANES] = t.astype(BF16)
        else:
            y_sc[hh] = t
    if dilation > 1:
        for r in range(dilation):
            for hh in range(3 * DIL_HEADS_PER_GROUP):
                c0 = r * cols + hh * LANES
                o_ref[0, :, c0:c0 + LANES] = y_sc.at[hh][pl.ds(r, rows, stride=dilation), :].astype(BF16)


def _dilproj(h, w, col0, group, rope_in, batch, seq, dilation, tm=1024):
    n, k = h.shape
    gc = DIL_GROUP_COLS
    cols = 3 * gc
    per_b = seq // tm
    make_tables = len(rope_in) == 2
    tab = pl.BlockSpec((tm, LANES), lambda i: (i, 0))
    part = lambda which: pl.BlockSpec((k, gc), lambda i: (0, col0 // gc + which * DIL_GROUPS + group))
    rope_specs = ([pl.BlockSpec((tm, 1), lambda i: (i, 0)), pl.BlockSpec((1, LANES), lambda i: (0, 0))]
                  if make_tables else [tab, tab, tab])
    out_specs = [pl.BlockSpec((1, tm // dilation, dilation * cols), lambda i: (i // per_b, i % per_b, 0))]
    out_shape = [jax.ShapeDtypeStruct((batch, seq // dilation, dilation * cols), BF16)]
    if make_tables:
        out_specs += [tab, tab, tab]
        out_shape += [jax.ShapeDtypeStruct((n, LANES), F32)] * 3
    out = pl.pallas_call(
        functools.partial(_dilproj_kernel, dilation=dilation, make_tables=make_tables),
        grid=(n // tm,),
        in_specs=[pl.BlockSpec((tm, k), lambda i: (i, 0)), part(0), part(1), part(2)] + rope_specs,
        out_specs=out_specs,
        out_shape=out_shape,
        scratch_shapes=[pltpu.VMEM((cols // LANES, tm, LANES), F32)],
        compiler_params=_cparams("arbitrary"),
        name="dil_proj",
    )(h, w, w, w, *rope_in)
    return out[0], tuple(out[1:]) if make_tables else rope_in


def _mla_attn_kernel(q_ref, k_ref, v_ref, o_ref, *, tq, nh):
    i = pl.program_id(2)
    qs = [q_ref[0, hh] for hh in range(nh)]

    def step(c, carry, masked):
        base = pl.multiple_of(c * tq, tq)
        ss = []
        for hh in range(nh):
            k = k_ref[0, hh, pl.ds(base, tq), :]
            s = lax.dot_general(qs[hh], k, (((1,), (1,)), ((), ())), preferred_element_type=F32)
            if masked:
                row = lax.broadcasted_iota(jnp.int32, (tq, tq), 0)
                col = lax.broadcasted_iota(jnp.int32, (tq, tq), 1)
                s = jnp.where(col <= row, s, NEG_BIG)
            ss.append(s)
        out = []
        for hh in range(nh):
            m, l, acc = carry[hh]
            v = v_ref[0, hh, pl.ds(base, tq), :]
            m_new = jnp.maximum(m, jnp.max(ss[hh], axis=-1, keepdims=True))
            alpha = jnp.exp2(m - m_new)
            pv = jnp.dot(jnp.exp2((ss[hh] - m_new).astype(BF16)), v, preferred_element_type=F32)
            out.append((m_new, alpha * l + pv[:, V_HEAD_DIM:V_HEAD_DIM + 1], alpha * acc + pv[:, :V_HEAD_DIM]))
        return tuple(out)

    init = tuple((jnp.full((tq, 1), NEG_BIG, F32), jnp.zeros((tq, 1), F32), jnp.zeros((tq, V_HEAD_DIM), F32))
                 for _ in range(nh))
    carry = lax.fori_loop(0, i, lambda c, cr: step(c, cr, False), init)
    carry = step(i, carry, True)
    for hh in range(nh):
        _, l, acc = carry[hh]
        o_ref[0, :, hh * V_HEAD_DIM:(hh + 1) * V_HEAD_DIM] = (acc / l).astype(BF16)


def _mla_attn(q, k, v, tq=512, nh=4):
    b, h, s, dk = q.shape
    dv = v.shape[-1]
    resident = pl.Buffered(1)
    return pl.pallas_call(
        functools.partial(_mla_attn_kernel, tq=tq, nh=nh),
        grid=(b, h // nh, s // tq),
        in_specs=[pl.BlockSpec((1, nh, tq, dk), lambda bi, hi, i: (bi, hi, i, 0)),
                  pl.BlockSpec((1, nh, s, dk), lambda bi, hi, i: (bi, hi, 0, 0), pipeline_mode=resident),
                  pl.BlockSpec((1, nh, s, dv), lambda bi, hi, i: (bi, hi, 0, 0), pipeline_mode=resident)],
        out_specs=pl.BlockSpec((1, tq, nh * V_HEAD_DIM), lambda bi, hi, i: (bi, i, hi)),
        out_shape=jax.ShapeDtypeStruct((b, s, h * V_HEAD_DIM), BF16),
        compiler_params=_cparams("arbitrary", "arbitrary", "arbitrary"),
        name="mla_attn",
    )(q, k, v)


def _dil_attn_kernel(q_ref, kc_ref, kp_ref, vc_ref, vp_ref, o_ref, lse_ref, *, tq):
    i = pl.program_id(2)
    sub = DIL_SPAN
    row = lax.broadcasted_iota(jnp.int32, (sub, 2 * sub), 0)
    col = lax.broadcasted_iota(jnp.int32, (sub, 2 * sub), 1)
    band = jnp.logical_and(col >= row, col <= row + sub)
    first = jnp.logical_and(band, col >= jnp.where(i > 0, 0, sub))
    lane = lax.broadcasted_iota(jnp.int32, (sub, LANES), 1)
    scale = DIL_HEAD_DIM ** -0.5
    dn = (((1,), (1,)), ((), ()))
    chains = [(j, hh) for j in range(tq // sub) for hh in range(DIL_HEADS_PER_GROUP)]

    def window(cur_ref, prev_ref, j, cs):
        if j == 0:
            return jnp.concatenate([prev_ref[0, :, cs], cur_ref[0, :sub, cs]], axis=0)
        return cur_ref[0, (j - 1) * sub:(j + 1) * sub, cs]

    scores = []
    for j, hh in chains:
        cs = slice(hh * LANES, (hh + 1) * LANES)
        s = lax.dot_general(q_ref[0, j * sub:(j + 1) * sub, cs], window(kc_ref, kp_ref, j, cs), dn,
                            preferred_element_type=F32) * scale
        scores.append(jnp.where(first if j == 0 else band, s, NEG_BIG))
    lse_blk = [jnp.zeros((sub, LANES), F32) for _ in range(tq // sub)]
    for (j, hh), s in zip(chains, scores):
        cs = slice(hh * LANES, (hh + 1) * LANES)
        m = jnp.max(s, axis=-1, keepdims=True)
        p = jnp.exp(s - m)
        l = jnp.sum(p, axis=-1, keepdims=True)
        acc = jnp.dot(p.astype(BF16), window(vc_ref, vp_ref, j, cs), preferred_element_type=F32)
        o_ref[0, j * sub:(j + 1) * sub, cs] = (acc * (1.0 / l)).astype(BF16)
        lse_blk[j] = jnp.where(lane == hh, m + jnp.log(l), lse_blk[j])
    for j in range(tq // sub):
        lse_ref[0, j * sub:(j + 1) * sub, :] = lse_blk[j]


def _dil_attn(t, batch, seq, dilation):
    ln = seq // dilation
    tq = min(ln, 4 * DIL_SPAN)
    gc = DIL_GROUP_COLS
    ratio = tq // DIL_SPAN
    cur = lambda which: pl.BlockSpec((1, tq, gc), lambda b, r, i: (b, i, r * 3 + which))
    prev = lambda which: pl.BlockSpec(
        (1, DIL_SPAN, gc), lambda b, r, i: (b, jnp.maximum(i * ratio - 1, 0), r * 3 + which))
    o, lse = pl.pallas_call(
        functools.partial(_dil_attn_kernel, tq=tq),
        grid=(batch, dilation, ln // tq),
        in_specs=[cur(0), cur(1), prev(1), cur(2), prev(2)],
        out_specs=[pl.BlockSpec((1, tq, gc), lambda b, r, i: (b, i, r)),
                   pl.BlockSpec((1, tq, LANES), lambda b, r, i: (b, i, r))],
        out_shape=[jax.ShapeDtypeStruct((batch, ln, dilation * gc), BF16),
                   jax.ShapeDtypeStruct((batch, ln, dilation * LANES), F32)],
        compiler_params=_cparams("arbitrary", "arbitrary", "arbitrary"),
        name=f"dil_attn_d{dilation}",
    )(t, t, t, t, t)
    return o, lse


def _pack_halves(v):
    w = v.shape[1] // 2
    lo = lax.bitcast_convert_type(v[:, :w].astype(BF16).astype(F32), jnp.uint32)
    hi = lax.bitcast_convert_type(v[:, w:].astype(BF16).astype(F32), jnp.uint32)
    return (lo >> 16) | (hi & jnp.uint32(0xFFFF0000))


ROW_TILE = 8


def _row_tile(p):
    return (pl.ds(pl.multiple_of(p * ROW_TILE, ROW_TILE), ROW_TILE), slice(None))


def _store_row_tiles(ref, pk, base=0):
    rows = pk.shape[0]
    for c in range(ROW_TILE):
        ref[pl.ds(base + c, rows, stride=ROW_TILE), :] = pk[:, c * LANES:(c + 1) * LANES]


def _load_row_tiles(ref, rows, base=0):
    return jnp.concatenate([ref[pl.ds(base + c, rows, stride=ROW_TILE), :] for c in range(ROW_TILE)], axis=1)


def _unpack_halves(pk):
    lo = lax.bitcast_convert_type(pk << 16, F32)
    hi = lax.bitcast_convert_type(pk & jnp.uint32(0xFFFF0000), F32)
    return lo, hi


def _merge_kernel(oa_ref, o0_ref, o1_ref, o2_ref, l0_ref, l1_ref, l2_ref, ga_ref, gb_ref, x_ref,
                  gate_ref, shift_ref, scale_ref, gpost_ref, gpre_ref,
                  wa_ref, wb_ref, wo_ref, wrh_ref, wrl_ref,
                  x1_ref, h2_ref, h2pk_ref, logit_ref, o_sc, l_sc):
    tm = x_ref.shape[0]

    def natural(ref, sc, gi, chunks):
        dil = DIL_PATTERNS[gi][1]
        if dil == 1:
            return [ref[0, :, c * LANES:(c + 1) * LANES].astype(F32) for c in range(chunks)]
        for r in range(dil):
            for c in range(chunks):
                c0 = (r * chunks + c) * LANES
                sc.at[gi, c][pl.ds(r, tm // dil, stride=dil), :] = ref[0, :, c0:c0 + LANES].astype(F32)
        return [sc[gi, c] for c in range(chunks)]

    (l0,), (l1,), (l2,) = [natural(ref, l_sc, gi, 1) for gi, ref in enumerate((l0_ref, l1_ref, l2_ref))]
    o0, o1, o2 = [natural(ref, o_sc, gi, DIL_HEADS_PER_GROUP) for gi, ref in enumerate((o0_ref, o1_ref, o2_ref))]
    m = jnp.maximum(jnp.maximum(l0, l1), l2)
    e0, e1, e2 = jnp.exp(l0 - m), jnp.exp(l1 - m), jnp.exp(l2 - m)
    inv = 1.0 / (e0 + e1 + e2)
    w0, w1, w2 = e0 * inv, e1 * inv, e2 * inv
    parts = []
    for hh in range(DIL_HEADS_PER_GROUP):
        parts.append(w0[:, hh:hh + 1] * o0[hh] + w1[:, hh:hh + 1] * o1[hh] + w2[:, hh:hh + 1] * o2[hh])
    o_dil = jnp.concatenate(parts, axis=1).astype(BF16)
    y_a = jnp.dot(oa_ref[...], wa_ref[...], preferred_element_type=F32)
    y_b = jnp.dot(o_dil, wb_ref[...], preferred_element_type=F32)
    merged = ga_ref[...].astype(F32) * y_a + gb_ref[...].astype(F32) * y_b
    y = jnp.dot(merged.astype(BF16), wo_ref[...], preferred_element_type=F32)
    x1 = x_ref[...] + gate_ref[0] * _rms(y, gpost_ref[...])
    x1_ref[...] = x1
    h2 = _rms(x1, gpre_ref[...]) * (1.0 + scale_ref[0]) + shift_ref[0]
    _store_row_tiles(h2pk_ref, _pack_halves(h2))
    h2_hi = h2.astype(BF16)
    h2_ref[...] = h2_hi
    h2_lo = (h2 - h2_hi.astype(F32)).astype(BF16)
    both = jnp.dot(h2_hi, wrh_ref[...], preferred_element_type=F32)
    logits = both[:, :LANES] + both[:, LANES:] + jnp.dot(h2_lo, wrl_ref[...], preferred_element_type=F32)
    logit_ref[...] = logits.T


def _merge(oa, dil_o, dil_lse, gates, x2, mod3, gpost, gpre, wa, wb, wo, wr_hi, wr_lo, seq, tm=256):
    n, d = x2.shape
    per_b = seq // tm
    row = lambda w: pl.BlockSpec((tm, w), lambda i: (i, 0))
    const = lambda a: pl.BlockSpec(a.shape, lambda i: (0,) * a.ndim, pipeline_mode=pl.Buffered(1))
    modspec = lambda ch: pl.BlockSpec((1, 1, d), lambda i: ((i // per_b) * ADALN_CHUNKS + ch, 0, 0))
    strided = lambda gi, w: pl.BlockSpec((1, tm // DIL_PATTERNS[gi][1], DIL_PATTERNS[gi][1] * w),
                                         lambda i: (i // per_b, i % per_b, 0))
    return pl.pallas_call(
        _merge_kernel,
        grid=(n // tm,),
        in_specs=[row(oa.shape[1]),
                  strided(0, DIL_GROUP_COLS), strided(1, DIL_GROUP_COLS), strided(2, DIL_GROUP_COLS),
                  strided(0, LANES), strided(1, LANES), strided(2, LANES),
                  pl.BlockSpec((tm, d), lambda i: (i, 0)), pl.BlockSpec((tm, d), lambda i: (i, 1)),
                  row(d),
                  modspec(2), modspec(3), modspec(4),
                  const(gpost), const(gpre),
                  const(wa), const(wb), const(wo), const(wr_hi), const(wr_lo)],
        out_specs=[row(d), row(d), pl.BlockSpec((tm * ROW_TILE, LANES), lambda i: (i, 0)),
                   pl.BlockSpec((LANES, tm), lambda i: (0, i))],
        out_shape=[jax.ShapeDtypeStruct((n, d), F32),
                   jax.ShapeDtypeStruct((n, d), BF16),
                   jax.ShapeDtypeStruct((n * ROW_TILE, LANES), jnp.uint32),
                   jax.ShapeDtypeStruct((LANES, n), F32)],
        scratch_shapes=[pltpu.VMEM((DIL_GROUPS, DIL_HEADS_PER_GROUP, tm, LANES), F32),
                        pltpu.VMEM((DIL_GROUPS, 1, tm, LANES), F32)],
        compiler_params=_cparams("arbitrary"),
        name="merge_outproj",
    )(oa, *dil_o, *dil_lse, gates, gates, x2, mod3, mod3, mod3, gpost, gpre, wa, wb, wo, wr_hi, wr_lo)


def _route_kernel(lg_ref, bias_ref, pos_ref, w_ref, meta_ref, blke_ref, cnt_sc, base_sc, rank_sc, score_sc, *, tt):
    ps = pl.program_id(0)
    i = pl.program_id(1)
    per_group = N_EXPERTS // N_EXPERT_GROUPS
    neg_inf = -jnp.inf
    tile = pl.ds(pl.multiple_of(i * tt, tt), tt)

    @pl.when(jnp.logical_and(ps == 0, i == 0))
    def _():
        cnt_sc[...] = jnp.zeros_like(cnt_sc)

    @pl.when(ps == 0)
    def _():
        scores = _sigmoid(lg_ref[...])
        biased = scores + bias_ref[...]
        b3 = biased.reshape(N_EXPERT_GROUPS, per_group, tt)
        mem = lax.broadcasted_iota(jnp.int32, b3.shape, 1)
        m1 = jnp.max(b3, axis=1, keepdims=True)
        first = jnp.min(jnp.where(b3 == m1, mem, per_group), axis=1, keepdims=True)
        m2 = jnp.max(jnp.where(mem == first, neg_inf, b3), axis=1, keepdims=True)
        gs = m1 + m2
        gidx = lax.broadcasted_iota(jnp.int32, gs.shape, 0)
        grank = jnp.zeros(gs.shape, jnp.int32)
        for g2 in range(N_EXPERT_GROUPS):
            r = gs[g2:g2 + 1]
            beats = jnp.logical_or(r > gs, jnp.logical_and(r == gs, g2 < gidx))
            grank = grank + jnp.where(beats, 1, 0)
        sel = jnp.where(grank < TOPK_GROUPS, b3, neg_inf).reshape(N_EXPERTS, tt)
        eidx = lax.broadcasted_iota(jnp.int32, sel.shape, 0)
        erank = jnp.zeros(sel.shape, jnp.int32)
        for e2 in range(N_EXPERTS):
            r = sel[e2:e2 + 1, :]
            beats = jnp.logical_or(r > sel, jnp.logical_and(r == sel, e2 < eidx))
            erank = erank + jnp.where(beats, 1, 0)
        rank_sc[:, tile] = erank
        score_sc[:, tile] = scores
        cnt_sc[...] = cnt_sc[...] + jnp.sum(jnp.where(erank < TOP_K, 1.0, 0.0), axis=1,
                                            keepdims=True).astype(jnp.int32)

    @pl.when(jnp.logical_and(ps == 1, i == 0))
    def _():
        cnt = cnt_sc[...]
        pc = ((cnt + (MOE_BLOCK - 1)) // MOE_BLOCK) * MOE_BLOCK
        pcb = jnp.broadcast_to(pc, (N_EXPERTS, LANES))
        eid = lax.broadcasted_iota(jnp.int32, (N_EXPERTS, LANES), 0)
        pends = jnp.zeros((N_EXPERTS, LANES), jnp.int32)
        for e2 in range(N_EXPERTS):
            pends = pends + jnp.where(eid >= e2, pcb[e2:e2 + 1, :], 0)
        pst = pends - pcb
        base_sc[...] = pst[:, 0:1]
        meta_ref[0] = jnp.broadcast_to(cnt, (N_EXPERTS, LANES))
        meta_ref[1] = pst
        meta_ref[2] = pends
        nbl = blke_ref.shape[1]
        blk_start = lax.broadcasted_iota(jnp.int32, (N_EXPERTS, nbl), 1) * MOE_BLOCK
        pend_b = jnp.broadcast_to(pends[:, 0:1], (N_EXPERTS, nbl))
        be = jnp.sum(jnp.where(pend_b <= blk_start, 1, 0), axis=0, keepdims=True)
        blke_ref[...] = jnp.broadcast_to(jnp.minimum(be, N_EXPERTS - 1), blke_ref.shape)

    @pl.when(ps == 1)
    def _():
        erank = rank_sc[:, tile]
        scores = score_sc[:, tile]
        esel = erank < TOP_K
        mask_f = jnp.where(esel, 1.0, 0.0)
        tile_cnt = jnp.sum(mask_f, axis=1, keepdims=True).astype(jnp.int32)
        rr = lax.broadcasted_iota(jnp.int32, (tt, tt), 0)
        cc = lax.broadcasted_iota(jnp.int32, (tt, tt), 1)
        upper = jnp.where(rr < cc, 1.0, 0.0).astype(BF16)
        prefix = jnp.dot(mask_f.astype(BF16), upper, preferred_element_type=F32)
        posd = base_sc[...] + prefix.astype(jnp.int32)
        base_sc[...] = base_sc[...] + tile_cnt
        wsel = jnp.where(esel, scores, 0.0)
        denom = jnp.sum(wsel, axis=0, keepdims=True)
        wn = wsel / (denom + 1e-20) * ROUTED_SCALE
        prow, wrow = [], []
        for kk in range(TOP_K):
            hit = erank == kk
            prow.append(jnp.sum(jnp.where(hit, posd, 0), axis=0, keepdims=True))
            wrow.append(jnp.sum(jnp.where(hit, wn, 0.0), axis=0, keepdims=True))
        pad = pos_ref.shape[0] - TOP_K
        pos_ref[...] = jnp.concatenate(prow + [jnp.zeros((pad, tt), jnp.int32)], axis=0)
        w_ref[...] = jnp.concatenate(wrow + [jnp.zeros((pad, tt), F32)], axis=0)


def _route(logits_t, bias_col, nb, tt=256):
    n = logits_t.shape[1]
    nbl = -(-nb // LANES) * LANES
    return pl.pallas_call(
        functools.partial(_route_kernel, tt=tt),
        grid=(2, n // tt),
        in_specs=[pl.BlockSpec((N_EXPERTS, tt), lambda ps, i: (0, i * (1 - ps))),
                  pl.BlockSpec((N_EXPERTS, 1), lambda ps, i: (0, 0))],
        out_specs=[pl.BlockSpec((8, tt), lambda ps, i: (0, ps * i)),
                   pl.BlockSpec((8, tt), lambda ps, i: (0, ps * i)),
                   pl.BlockSpec((3, N_EXPERTS, LANES), lambda ps, i: (0, 0, 0)),
                   pl.BlockSpec((8, nbl), lambda ps, i: (0, 0))],
        out_shape=[jax.ShapeDtypeStruct((8, n), jnp.int32),
                   jax.ShapeDtypeStruct((8, n), F32),
                   jax.ShapeDtypeStruct((3, N_EXPERTS, LANES), jnp.int32),
                   jax.ShapeDtypeStruct((8, nbl), jnp.int32)],
        scratch_shapes=[pltpu.VMEM((N_EXPERTS, 1), jnp.int32), pltpu.VMEM((N_EXPERTS, 1), jnp.int32),
                        pltpu.VMEM((N_EXPERTS, n), jnp.int32), pltpu.VMEM((N_EXPERTS, n), F32)],
        compiler_params=_cparams("arbitrary", "arbitrary"),
        name="moe_route",
    )(logits_t, bias_col)


def _dispatch_kernel(pos_ref, cnt_ref, pst_ref, h_hbm, hb_ref, wsg_ref, wsu_ref, wsd_ref, xs_hbm, ysh_ref,
                     hbuf, zrow, in_sem, sc_sem, sem, *, tm, n):
    i = pl.program_id(0)
    nt = pl.num_programs(0)
    tile_rows = tm * ROW_TILE

    def fetch(tile, slot):
        return pltpu.make_async_copy(h_hbm.at[pl.ds(pl.multiple_of(tile * tile_rows, tile_rows), tile_rows), :],
                                     hbuf.at[slot], in_sem.at[slot])

    def wait_rows(slot):
        for kk in range(TOP_K):
            pltpu.make_async_copy(hbuf.at[slot], xs_hbm.at[pl.ds(0, tile_rows), :], sc_sem.at[slot]).wait()

    @pl.when(i == 0)
    def _():
        fetch(0, 0).start()
        fetch(1, 1).start()

    slot = i % 3
    fetch(i, slot).wait()
    src = hbuf.at[slot]

    def body(g, carry):
        for u in range(ROW_DMA_UNROLL):
            r = g * ROW_DMA_UNROLL + u
            for kk in range(TOP_K):
                p = pos_ref[kk * n + i * tm + r]
                pltpu.make_async_copy(src.at[_row_tile(r)], xs_hbm.at[_row_tile(p)],
                                      sc_sem.at[slot]).start(priority=kk % 2)
        return carry
    lax.fori_loop(0, tm // ROW_DMA_UNROLL, body, 0)

    hb = hb_ref[...]
    g = jnp.dot(hb, wsg_ref[...], preferred_element_type=F32)
    u = jnp.dot(hb, wsu_ref[...], preferred_element_type=F32)
    ysh_ref[...] = jnp.dot((g * _sigmoid(g) * u).astype(BF16), wsd_ref[...],
                           preferred_element_type=F32).astype(BF16)

    @pl.when(i >= 1)
    def _():
        wait_rows((i + 2) % 3)

    @pl.when(i + 2 < nt)
    def _():
        fetch(i + 2, (i + 2) % 3).start()

    @pl.when(i == nt - 1)
    def _():
        wait_rows(slot)

    @pl.when(i == nt - 1)
    def _():
        zrow[...] = jnp.zeros_like(zrow)

        def per_expert(e, carry):
            cnt = cnt_ref[e]
            first = pst_ref[e] + cnt
            npad = ((cnt + (MOE_BLOCK - 1)) // MOE_BLOCK) * MOE_BLOCK - cnt

            def start(s, c2):
                pltpu.make_async_copy(zrow.at[_row_tile(0)], xs_hbm.at[_row_tile(first + s)], sem.at[1]).start()
                return c2

            def wait(s, c2):
                pltpu.make_async_copy(zrow.at[_row_tile(0)], xs_hbm.at[_row_tile(0)], sem.at[1]).wait()
                return c2
            lax.fori_loop(0, npad, start, 0)
            lax.fori_loop(0, npad, wait, 0)
            return carry
        lax.fori_loop(0, N_EXPERTS, per_expert, 0)

        last = N_EXPERTS - 1
        used = (pst_ref[last] + cnt_ref[last] + (MOE_BLOCK - 1)) // MOE_BLOCK
        blk_rows = MOE_BLOCK * ROW_TILE

        def tail(b, carry):
            cp = pltpu.make_async_copy(zrow, xs_hbm.at[pl.ds(pl.multiple_of(b * blk_rows, blk_rows), blk_rows), :],
                                       sem.at[1])
            cp.start()
            cp.wait()
            return carry
        lax.fori_loop(used, xs_hbm.shape[0] // blk_rows, tail, 0)


def _dispatch(pos_flat, cnt, pst, h2pk, h2, wsg, wsu, wsd, nb, tm=256):
    n, d = h2.shape
    assert n // tm >= 2
    const = lambda a: pl.BlockSpec(a.shape, lambda i, *_: (0,) * a.ndim)
    grid_spec = pltpu.PrefetchScalarGridSpec(
        num_scalar_prefetch=3,
        grid=(n // tm,),
        in_specs=[pl.BlockSpec(memory_space=pl.ANY),
                  pl.BlockSpec((tm, d), lambda i, *_: (i, 0)),
                  const(wsg), const(wsu), const(wsd)],
        out_specs=[pl.BlockSpec(memory_space=pl.ANY), pl.BlockSpec((tm, d), lambda i, *_: (i, 0))],
        scratch_shapes=[pltpu.VMEM((3, tm * ROW_TILE, LANES), jnp.uint32),
                        pltpu.VMEM((MOE_BLOCK * ROW_TILE, LANES), jnp.uint32),
                        pltpu.SemaphoreType.DMA((3,)), pltpu.SemaphoreType.DMA((3,)),
                        pltpu.SemaphoreType.DMA((2,))],
    )
    return pl.pallas_call(
        functools.partial(_dispatch_kernel, tm=tm, n=n),
        grid_spec=grid_spec,
        out_shape=[jax.ShapeDtypeStruct((nb * MOE_BLOCK * ROW_TILE, LANES), jnp.uint32),
                   jax.ShapeDtypeStruct((n, d), BF16)],
        compiler_params=_cparams("arbitrary"),
        name="moe_dispatch",
    )(pos_flat, cnt, pst, h2pk, h2, wsg, wsu, wsd)


def _gmm_kernel(nact_ref, blke_ref, xs_ref, wg_hbm, wu_hbm, wd_hbm, o_ref,
                wgf, wuf, wdf, wgb, wub, wdb, run_sc, sem):
    nact = nact_ref[0]

    def fetch(e, slot):
        return (pltpu.make_async_copy(wg_hbm.at[e], wgf.at[slot], sem.at[slot, 0]),
                pltpu.make_async_copy(wu_hbm.at[e], wuf.at[slot], sem.at[slot, 1]),
                pltpu.make_async_copy(wd_hbm.at[e], wdf.at[slot], sem.at[slot, 2]))

    @pl.when(pl.program_id(0) == 0)
    def _():
        run_sc[0] = 0
        for cp in fetch(blke_ref[0], 0):
            cp.start()

    def switch_weights(i):
        e = blke_ref[i]
        changed = jnp.logical_or(i == 0, e != blke_ref[jnp.maximum(i - 1, 0)])

        @pl.when(changed)
        def _():
            run = run_sc[0]
            slot = run % 2
            for cp in fetch(e, slot):
                cp.wait()
            wgb[...] = wgf[slot].astype(BF16)
            wub[...] = wuf[slot].astype(BF16)
            wdb[...] = wdf[slot].astype(BF16)
            nxt = lax.while_loop(lambda j: jnp.logical_and(j < nact, blke_ref[jnp.minimum(j, nact - 1)] == e),
                                 lambda j: j + 1, i + 1)

            @pl.when(nxt < nact)
            def _():
                for cp in fetch(blke_ref[jnp.minimum(nxt, nact - 1)], 1 - slot):
                    cp.start(priority=1)
            run_sc[0] = run + 1

    def swiglu(block, nblocks):
        base = block * MOE_BLOCK * ROW_TILE
        lo, hi = _unpack_halves(_load_row_tiles(xs_ref, nblocks * MOE_BLOCK, base))
        lo, hi = lo.astype(BF16), hi.astype(BF16)
        half = lo.shape[1]
        g = (jnp.dot(lo, wgb[:half, :], preferred_element_type=F32)
             + jnp.dot(hi, wgb[half:, :], preferred_element_type=F32))
        u = (jnp.dot(lo, wub[:half, :], preferred_element_type=F32)
             + jnp.dot(hi, wub[half:, :], preferred_element_type=F32))
        a = (g * _sigmoid(g) * u).astype(BF16)
        _store_row_tiles(o_ref, _pack_halves(jnp.dot(a, wdb[...], preferred_element_type=F32)), base)

    i0 = pl.program_id(0) * GMM_SUB
    i1 = i0 + 1
    paired = jnp.logical_and(i1 < nact, blke_ref[i0] == blke_ref[jnp.minimum(i1, nact - 1)])

    @pl.when(i0 < nact)
    def _():
        switch_weights(i0)

    @pl.when(paired)
    def _():
        swiglu(0, 2)

    @pl.when(jnp.logical_and(jnp.logical_not(paired), i0 < nact))
    def _():
        swiglu(0, 1)

    @pl.when(jnp.logical_and(jnp.logical_not(paired), i1 < nact))
    def _():
        switch_weights(i1)
        swiglu(1, 1)

    for sb, i in enumerate((i0, i1)):
        @pl.when(i >= nact)
        def _(sb=sb):
            o_ref[pl.ds(sb * MOE_BLOCK * ROW_TILE, MOE_BLOCK * ROW_TILE), :] = jnp.zeros(
                (MOE_BLOCK * ROW_TILE, LANES), o_ref.dtype)


def _gmm(nact, blk_e, xs, w_gate, w_up, w_down, nb):
    d, f = w_gate.shape[1:]
    assert nb % GMM_SUB == 0
    rows = GMM_SUB * MOE_BLOCK * ROW_TILE
    blk = lambda i, na: jnp.minimum(i, (na[0] - 1) // GMM_SUB)
    hbm = pl.BlockSpec(memory_space=pl.ANY)
    grid_spec = pltpu.PrefetchScalarGridSpec(
        num_scalar_prefetch=2,
        grid=(nb // GMM_SUB,),
        in_specs=[pl.BlockSpec((rows, LANES), lambda i, na, be: (blk(i, na), 0)), hbm, hbm, hbm],
        out_specs=pl.BlockSpec((rows, LANES), lambda i, na, be: (i, 0)),
        scratch_shapes=[pltpu.VMEM((2, d, f), F32), pltpu.VMEM((2, d, f), F32), pltpu.VMEM((2, f, d), F32),
                        pltpu.VMEM((d, f), BF16), pltpu.VMEM((d, f), BF16), pltpu.VMEM((f, d), BF16),
                        pltpu.SMEM((1,), jnp.int32), pltpu.SemaphoreType.DMA((2, 3))],
    )
    return pl.pallas_call(
        _gmm_kernel,
        grid_spec=grid_spec,
        out_shape=jax.ShapeDtypeStruct((nb * MOE_BLOCK * ROW_TILE, LANES), jnp.uint32),
        compiler_params=_cparams("arbitrary"),
        name="moe_experts",
    )(nact, blk_e, xs, w_gate, w_up, w_down)


def _final_kernel(pos_ref, tw_ref, ysh_ref, x1_ref, gate_ref, gpost_ref, yb_hbm, o_ref, rbuf, sem, *, tm, n):
    i = pl.program_id(0)
    slot = i % 2

    def gather(tile, sl):
        def body(g, carry):
            for u in range(ROW_DMA_UNROLL):
                r = g * ROW_DMA_UNROLL + u
                for kk in range(TOP_K):
                    p = pos_ref[kk * n + tile * tm + r]
                    pltpu.make_async_copy(yb_hbm.at[_row_tile(p)], rbuf.at[sl, kk].at[_row_tile(r)],
                                          sem.at[sl]).start(priority=kk % 2)
            return carry
        lax.fori_loop(0, tm // ROW_DMA_UNROLL, body, 0)

    @pl.when(i == 0)
    def _():
        gather(0, 0)

    @pl.when(i + 1 < pl.num_programs(0))
    def _():
        gather(i + 1, 1 - slot)

    for kk in range(TOP_K):
        pltpu.make_async_copy(yb_hbm.at[pl.ds(0, tm * ROW_TILE), :], rbuf.at[slot, kk], sem.at[slot]).wait()
    tw = tw_ref[...]
    half = ROW_TILE * LANES
    r_lo = jnp.zeros((tm, half), F32)
    r_hi = jnp.zeros((tm, half), F32)
    for kk in range(TOP_K):
        lo, hi = _unpack_halves(_load_row_tiles(rbuf.at[slot, kk], tm))
        wk = tw[:, kk:kk + 1]
        r_lo = r_lo + wk * lo
        r_hi = r_hi + wk * hi
    y = ysh_ref[...].astype(F32) + jnp.concatenate([r_lo, r_hi], axis=1)
    o_ref[...] = x1_ref[...] + gate_ref[0] * _rms(y, gpost_ref[...])


def _final(pos_flat, top_w8, ysh, x1, mod3, gpost, yb, seq, tm=256):
    n, d = x1.shape
    per_b = seq // tm
    const = lambda a: pl.BlockSpec(a.shape, lambda i, ps: (0,) * a.ndim)
    row = lambda w: pl.BlockSpec((tm, w), lambda i, ps: (i, 0))
    grid_spec = pltpu.PrefetchScalarGridSpec(
        num_scalar_prefetch=1,
        grid=(n // tm,),
        in_specs=[row(top_w8.shape[1]), row(d), row(d),
                  pl.BlockSpec((1, 1, d), lambda i, ps: ((i // per_b) * ADALN_CHUNKS + 5, 0, 0)),
                  const(gpost),
                  pl.BlockSpec(memory_space=pl.ANY)],
        out_specs=row(d),
        scratch_shapes=[pltpu.VMEM((2, TOP_K, tm * ROW_TILE, LANES), jnp.uint32), pltpu.SemaphoreType.DMA((2,))],
    )
    return pl.pallas_call(
        functools.partial(_final_kernel, tm=tm, n=n),
        grid_spec=grid_spec,
        out_shape=jax.ShapeDtypeStruct((n, d), F32),
        compiler_params=_cparams("arbitrary"),
        name="moe_combine_final",
    )(pos_flat, top_w8, ysh, x1, mod3, gpost, yb)


def _rope_freq_row(dim):
    inv_freq = 1.0 / (ROPE_THETA ** (jnp.arange(0, dim, 2, dtype=F32) / dim))
    return jnp.concatenate([inv_freq, inv_freq, jnp.zeros((LANES - dim,), F32)]).reshape(1, LANES)


def kernel(x, c, positions, w_ada, b_ada, attn_pre_g, w_in, q_a_norm_g, w_q_up, kv_a_norm_g, w_kv_up, w_mla_o, w_dil_o, w_out, attn_post_g, ffn_pre_g, w_router, router_bias, w_exp_gate, w_exp_up, w_exp_down, w_sh_gate, w_sh_up, w_sh_down, ffn_post_g):
    batch, seq, d = x.shape
    n = batch * seq
    depth = w_ada.shape[0]

    pos_col = positions.astype(F32).reshape(n, 1)
    freq_mla, freq_dil = _rope_freq_row(QK_ROPE_DIM), _rope_freq_row(DIL_ROT_DIM)

    x2 = x.reshape(n, d)
    c8 = jnp.pad(c, ((0, 8 - batch), (0, 0)))
    for l in range(depth):
        mod = _ada(c8, w_ada[l], b_ada[l].reshape(1, -1))
        mod3 = mod[:batch].reshape(batch * ADALN_CHUNKS, 1, d)

        wi = w_in[l]
        o_dil = Q_LORA_RANK + KV_LORA_RANK + QK_ROPE_DIM
        o_ga = o_dil + 3 * DIL_HEADS * DIL_HEAD_DIM
        n_gate, n_dil = wi.shape[1] - o_ga, o_ga - o_dil
        n_a = o_dil + LANES - QK_ROPE_DIM
        c_dil, c_a = n_gate, -(-(n_gate + n_dil) // n_a) * n_a
        w_full = _wprep(wi.T, ((0, o_ga, n_gate), (c_dil, o_dil, n_dil), (c_a, 0, o_dil)), c_a + n_a)
        wq3 = w_q_up[l].reshape(Q_LORA_RANK, MLA_HEADS, MLA_QK_DIM)
        wq = jnp.concatenate([wq3, jnp.zeros((Q_LORA_RANK, MLA_HEADS, MLA_QK_PAD - MLA_QK_DIM), F32)],
                             axis=2).reshape(Q_LORA_RANK, MLA_HEADS * MLA_QK_PAD).astype(BF16)
        wkv3 = w_kv_up[l].reshape(KV_LORA_RANK, MLA_HEADS, QK_NOPE_DIM + V_HEAD_DIM)
        wkv = jnp.concatenate([wkv3[:, :, :QK_NOPE_DIM].reshape(KV_LORA_RANK, -1),
                               wkv3[:, :, QK_NOPE_DIM:].reshape(KV_LORA_RANK, -1)], axis=1).astype(BF16)

        h, q, k, v = _front(x2, attn_pre_g[l].reshape(1, d), mod3, w_full, c_a, n_a,
                            q_a_norm_g[l].reshape(1, -1), kv_a_norm_g[l].reshape(1, -1), wq, wkv,
                            pos_col, freq_mla, batch, seq)
        gates = _mm(h, w_full, 0, n_gate, act="sigmoid")
        o_mla = _mla_attn(q, k, v).reshape(n, MLA_HEADS * V_HEAD_DIM)
        dil_o, dil_lse = [], []
        rope_dil = (pos_col, freq_dil)
        for g, (_, dilation) in enumerate(DIL_PATTERNS):
            qkv, rope_dil = _dilproj(h, w_full, c_dil, g, rope_dil, batch, seq, dilation)
            o_g, lse_g = _dil_attn(qkv, batch, seq, dilation)
            dil_o.append(o_g)
            dil_lse.append(lse_g)

        wr = jnp.pad(w_router[l], ((0, 0), (0, LANES - N_EXPERTS)))
        wr_hi = wr.astype(BF16)
        wr_lo = (wr - wr_hi.astype(F32)).astype(BF16)
        x1, h2, h2pk, logits_t = _merge(o_mla, dil_o, dil_lse, gates, x2, mod3,
                                        attn_post_g[l].reshape(1, d), ffn_pre_g[l].reshape(1, d),
                                        w_mla_o[l].astype(BF16), w_dil_o[l].astype(BF16), w_out[l].astype(BF16),
                                        jnp.concatenate([wr_hi, wr_lo], axis=1), wr_hi, seq)

        nb = -(-(n * TOP_K + N_EXPERTS * (MOE_BLOCK - 1)) // MOE_BLOCK)
        pos_t, w_t, meta, blk_e = _route(logits_t, router_bias[l].astype(F32).reshape(N_EXPERTS, 1), nb)
        pos_flat = pos_t.reshape(-1)
        nact = meta[2, N_EXPERTS - 1, :1] // MOE_BLOCK
        xs, ysh = _dispatch(pos_flat, meta[0, :, 0], meta[1, :, 0], h2pk, h2,
                            w_sh_gate[l].astype(BF16), w_sh_up[l].astype(BF16), w_sh_down[l].astype(BF16), nb)
        yb = _gmm(nact, blk_e[0], xs, w_exp_gate[l], w_exp_up[l], w_exp_down[l], nb)
        x2 = _final(pos_flat, w_t.T, ysh, x1, mod3, ffn_post_g[l].reshape(1, d), yb, seq)
    return x2.reshape(batch, seq, d)
```

```python
import functools

import jax
import jax.numpy as jnp
from jax import lax
from jax.experimental import pallas as pl
from jax.experimental.pallas import tpu as pltpu

F32 = jnp.float32
BF16 = jnp.bfloat16

D_MODEL = 2048
NORM_EPS = 1e-6
ROPE_THETA = 500000.0
ADALN_CHUNKS = 6

MLA_HEADS = 8
Q_LORA_RANK = 512
KV_LORA_RANK = 512
QK_NOPE_DIM = 128
QK_ROPE_DIM = 64
V_HEAD_DIM = 128
MLA_QK_DIM = QK_NOPE_DIM + QK_ROPE_DIM
MLA_QK_PAD = 256

DIL_PATTERNS = ((128, 1), (512, 4), (2048, 16))
DIL_GROUPS = len(DIL_PATTERNS)
DIL_HEADS_PER_GROUP = 4
DIL_HEADS = DIL_GROUPS * DIL_HEADS_PER_GROUP
DIL_HEAD_DIM = 128
DIL_ROT_DIM = DIL_HEAD_DIM // 4
DIL_SPAN = 128
DIL_GROUP_COLS = DIL_HEADS_PER_GROUP * DIL_HEAD_DIM

N_EXPERTS = 64
N_EXPERT_GROUPS = 8
TOPK_GROUPS = 4
TOP_K = 6
EXPERT_DIM = 512
SHARED_DIM = 512
ROUTED_SCALE = 2.5
MOE_BLOCK = 256

LANES = 128
NEG_BIG = -1e30
LOG2_E = 1.4426950408889634
ROW_DMA_UNROLL = 4
GMM_SUB = 4
VMEM_LIMIT = 56 * 1024 * 1024


def _cparams(*sem):
    return pltpu.CompilerParams(dimension_semantics=sem, vmem_limit_bytes=VMEM_LIMIT)


def _sigmoid(v):
    return 1.0 / (1.0 + jnp.exp(-v))


def _rms(v, g):
    ms = jnp.mean(v * v, axis=-1, keepdims=True)
    return v * lax.rsqrt(ms + NORM_EPS) * g


def _ada_kernel(c_ref, w_ref, b_ref, o_ref):
    c = c_ref[...]
    a = (c * _sigmoid(c)).astype(BF16)
    o_ref[...] = jnp.dot(a, w_ref[...].astype(BF16), preferred_element_type=F32) + b_ref[...]


def _ada(c8, w_ada, b_ada, tn=1536):
    d, n = w_ada.shape
    return pl.pallas_call(
        _ada_kernel,
        grid=(n // tn,),
        in_specs=[pl.BlockSpec((8, d), lambda j: (0, 0)),
                  pl.BlockSpec((d, tn), lambda j: (0, j)),
                  pl.BlockSpec((1, tn), lambda j: (0, j))],
        out_specs=pl.BlockSpec((8, tn), lambda j: (0, j)),
        out_shape=jax.ShapeDtypeStruct((8, n), F32),
        compiler_params=_cparams("arbitrary"),
        name="ada_mod",
    )(c8, w_ada, b_ada)


def _wprep_source(j, segments):
    src = j * 0
    valid = j * 0
    for dst, s0, width in segments:
        t0, t1 = dst // LANES, (dst + width + LANES - 1) // LANES
        inside = jnp.logical_and(j >= t0, j < t1)
        src = jnp.where(inside, s0 + (j - t0) * LANES, src)
        valid = jnp.where(inside, jnp.minimum(dst + width - j * LANES, LANES), valid)
    return src, valid


def _wprep_kernel(*refs, segments, tiles):
    o_ref = refs[-1]
    for t in range(tiles):
        _, valid = _wprep_source(pl.program_id(0) * tiles + t, segments)
        blk = jnp.concatenate([refs[2 * t][...], refs[2 * t + 1][...]], axis=0).T
        lane = lax.broadcasted_iota(jnp.int32, blk.shape, 1)
        o_ref[:, t * LANES:(t + 1) * LANES] = jnp.where(lane < valid, blk, 0.0).astype(BF16)


def _wprep(w_t, segments, cols, tiles=3):
    _, k = w_t.shape
    half = LANES // 2
    assert all(dst % LANES == 0 and s0 % half == 0 and width % half == 0 for dst, s0, width in segments)
    assert cols % (tiles * LANES) == 0
    part = lambda t, which: pl.BlockSpec(
        (half, k), lambda j: (_wprep_source(j * tiles + t, segments)[0] // half + which, 0))
    return pl.pallas_call(
        functools.partial(_wprep_kernel, segments=segments, tiles=tiles),
        grid=(cols // (tiles * LANES),),
        in_specs=[part(t, which) for t in range(tiles) for which in range(2)],
        out_specs=pl.BlockSpec((k, tiles * LANES), lambda j: (0, j)),
        out_shape=jax.ShapeDtypeStruct((k, cols), BF16),
        compiler_params=_cparams("arbitrary"),
        name="w_in_prep",
    )(*([w_t] * (2 * tiles)))


def _rope_lanes(t, c_tab, s_fwd, s_bwd, half):
    return t * c_tab + pltpu.roll(t, half, 1) * s_fwd + pltpu.roll(t, LANES - half, 1) * s_bwd


def _rope_tables(pos, freq, dim, passthrough):
    ang = pos * freq
    cos, sin = jnp.cos(ang), jnp.sin(ang)
    lane = lax.broadcasted_iota(jnp.int32, ang.shape, 1)
    half = dim // 2
    c_tab = jnp.where(lane < dim, cos, 1.0 if passthrough else 0.0)
    s_fwd = jnp.where(jnp.logical_and(lane >= half, lane < dim), sin, 0.0)
    s_bwd = jnp.where(lane < half, -sin, 0.0)
    return c_tab, s_fwd, s_bwd


def _front_kernel(x_ref, g_ref, scale_ref, shift_ref, wa_ref, gq_ref, gkv_ref, wq_ref, wkv_ref,
                  pos_ref, freq_ref, h_ref, q_ref, k_ref, v_ref):
    h = (_rms(x_ref[...], g_ref[...]) * (1.0 + scale_ref[0]) + shift_ref[0]).astype(BF16)
    h_ref[...] = h
    a = jnp.dot(h, wa_ref[...], preferred_element_type=F32)
    qa = a[:, :Q_LORA_RANK]
    ckv = a[:, Q_LORA_RANK:Q_LORA_RANK + KV_LORA_RANK]
    kr = a[:, Q_LORA_RANK + KV_LORA_RANK:]
    c_tab, s_fwd, s_bwd = _rope_tables(pos_ref[...], freq_ref[...], QK_ROPE_DIM, passthrough=False)
    half = QK_ROPE_DIM // 2
    q = jnp.dot(_rms(qa, gq_ref[...]).astype(BF16), wq_ref[...], preferred_element_type=F32)
    q = q * (MLA_QK_DIM ** -0.5 * LOG2_E)
    kv = jnp.dot(_rms(ckv, gkv_ref[...]).astype(BF16), wkv_ref[...], preferred_element_type=F32)
    k_rot = _rope_lanes(kr, c_tab, s_fwd, s_bwd, half).astype(BF16)
    lane = lax.broadcasted_iota(jnp.int32, (a.shape[0], LANES), 1)
    ones_col = jnp.where(lane == 0, 1.0, 0.0).astype(BF16)
    for hh in range(MLA_HEADS):
        base = hh * MLA_QK_PAD
        q_ref[0, hh, :, :LANES] = q[:, base:base + LANES].astype(BF16)
        q_ref[0, hh, :, LANES:] = _rope_lanes(q[:, base + LANES:base + 2 * LANES],
                                              c_tab, s_fwd, s_bwd, half).astype(BF16)
        k_ref[0, hh, :, :LANES] = kv[:, hh * LANES:(hh + 1) * LANES].astype(BF16)
        k_ref[0, hh, :, LANES:] = k_rot
        v_off = MLA_HEADS * LANES + hh * LANES
        v_ref[0, hh, :, :LANES] = kv[:, v_off:v_off + LANES].astype(BF16)
        v_ref[0, hh, :, LANES:] = ones_col


def _front(x2, g, mod3, w_full, col_a, n_a, gq, gkv, wq, wkv, pos_col, freq, batch, seq, tm=512):
    n, d = x2.shape
    per_b = seq // tm
    const = lambda a: pl.BlockSpec(a.shape, lambda i: (0,) * a.ndim, pipeline_mode=pl.Buffered(1))
    head_major = lambda w: pl.BlockSpec((1, MLA_HEADS, tm, w), lambda i: (i // per_b, 0, i % per_b, 0))
    return pl.pallas_call(
        _front_kernel,
        grid=(n // tm,),
        in_specs=[pl.BlockSpec((tm, d), lambda i: (i, 0)),
                  pl.BlockSpec((1, d), lambda i: (0, 0)),
                  pl.BlockSpec((1, 1, d), lambda i: ((i // per_b) * ADALN_CHUNKS + 1, 0, 0)),
                  pl.BlockSpec((1, 1, d), lambda i: ((i // per_b) * ADALN_CHUNKS + 0, 0, 0)),
                  pl.BlockSpec((d, n_a), lambda i: (0, col_a // n_a), pipeline_mode=pl.Buffered(1)),
                  const(gq), const(gkv), const(wq), const(wkv),
                  pl.BlockSpec((tm, 1), lambda i: (i, 0)), pl.BlockSpec((1, LANES), lambda i: (0, 0))],
        out_specs=[pl.BlockSpec((tm, d), lambda i: (i, 0)),
                   head_major(MLA_QK_PAD), head_major(MLA_QK_PAD), head_major(2 * V_HEAD_DIM)],
        out_shape=[jax.ShapeDtypeStruct((n, d), BF16),
                   jax.ShapeDtypeStruct((batch, MLA_HEADS, seq, MLA_QK_PAD), BF16),
                   jax.ShapeDtypeStruct((batch, MLA_HEADS, seq, MLA_QK_PAD), BF16),
                   jax.ShapeDtypeStruct((batch, MLA_HEADS, seq, 2 * V_HEAD_DIM), BF16)],
        compiler_params=_cparams("arbitrary"),
        name="front_mla_prep",
    )(x2, g, mod3, mod3, w_full, gq, gkv, wq, wkv, pos_col, freq)


def _mm_kernel(h_ref, w_ref, o_ref, *, act):
    y = jnp.dot(h_ref[...], w_ref[...], preferred_element_type=F32)
    if act == "sigmoid":
        y = _sigmoid(y)
    o_ref[...] = y.astype(o_ref.dtype)


def _mm(h, w, col0, cols, act=None, tm=1024, tn=1024):
    n, k = h.shape
    tn = min(tn, cols)
    j0 = col0 // tn
    return pl.pallas_call(
        functools.partial(_mm_kernel, act=act),
        grid=(cols // tn, n // tm),
        in_specs=[pl.BlockSpec((tm, k), lambda j, i: (i, 0)),
                  pl.BlockSpec((k, tn), lambda j, i: (0, j0 + j))],
        out_specs=pl.BlockSpec((tm, tn), lambda j, i: (i, j)),
        out_shape=jax.ShapeDtypeStruct((n, cols), BF16),
        compiler_params=_cparams("arbitrary", "arbitrary"),
        name="in_proj_" + (act or "plain"),
    )(h, w)


def _dilproj_kernel(h_ref, wq_ref, wk_ref, wv_ref, *rest, dilation, make_tables):
    hb = h_ref[...]
    y = jnp.concatenate([jnp.dot(hb, w_ref[...], preferred_element_type=F32) for w_ref in (wq_ref, wk_ref, wv_ref)],
                        axis=1)
    if make_tables:
        pos_ref, freq_ref, o_ref, c_out, sf_out, sb_out, y_sc = rest
        c_tab, s_fwd, s_bwd = _rope_tables(pos_ref[...], freq_ref[...], DIL_ROT_DIM, passthrough=True)
        c_out[...], sf_out[...], sb_out[...] = c_tab, s_fwd, s_bwd
    else:
        c_ref, sf_ref, sb_ref, o_ref, y_sc = rest
        c_tab, s_fwd, s_bwd = c_ref[...], sf_ref[...], sb_ref[...]
    cols = 3 * DIL_GROUP_COLS
    rows = h_ref.shape[0] // dilation
    n_rot = 2 * DIL_HEADS_PER_GROUP
    for hh in range(3 * DIL_HEADS_PER_GROUP):
        t = y[:, hh * LANES:(hh + 1) * LANES]
        if hh < n_rot:
            t = _rope_lanes(t, c_tab, s_fwd, s_bwd, DIL_ROT_DIM // 2)
        if dilation == 1:
            o_ref[0, :, hh * LANES:(hh + 1) * LANES] = t.astype(BF16)
        else:
            y_sc[hh] = t
    if dilation > 1:
        for r in range(dilation):
            for hh in range(3 * DIL_HEADS_PER_GROUP):
                c0 = r * cols + hh * LANES
                o_ref[0, :, c0:c0 + LANES] = y_sc.at[hh][pl.ds(r, rows, stride=dilation), :].astype(BF16)


def _dilproj(h, w, col0, group, rope_in, batch, seq, dilation, tm=1024):
    n, k = h.shape
    gc = DIL_GROUP_COLS
    cols = 3 * gc
    per_b = seq // tm
    make_tables = len(rope_in) == 2
    tab = pl.BlockSpec((tm, LANES), lambda i: (i, 0))
    part = lambda which: pl.BlockSpec((k, gc), lambda i: (0, col0 // gc + which * DIL_GROUPS + group))
    rope_specs = ([pl.BlockSpec((tm, 1), lambda i: (i, 0)), pl.BlockSpec((1, LANES), lambda i: (0, 0))]
                  if make_tables else [tab, tab, tab])
    out_specs = [pl.BlockSpec((1, tm // dilation, dilation * cols), lambda i: (i // per_b, i % per_b, 0))]
    out_shape = [jax.ShapeDtypeStruct((batch, seq // dilation, dilation * cols), BF16)]
    if make_tables:
        out_specs += [tab, tab, tab]
        out_shape += [jax.ShapeDtypeStruct((n, LANES), F32)] * 3
    out = pl.pallas_call(
        functools.partial(_dilproj_kernel, dilation=dilation, make_tables=make_tables),
        grid=(n // tm,),
        in_specs=[pl.BlockSpec((tm, k), lambda i: (i, 0)), part(0), part(1), part(2)] + rope_specs,
        out_specs=out_specs,
        out_shape=out_shape,
        scratch_shapes=[pltpu.VMEM((cols // LANES, tm, LANES), F32)],
        compiler_params=_cparams("arbitrary"),
        name="dil_proj",
    )(h, w, w, w, *rope_in)
    return out[0], tuple(out[1:]) if make_tables else rope_in


def _mla_attn_kernel(q_ref, k_ref, v_ref, o_ref, *, tq, nh):
    i = pl.program_id(2)
    qs = [q_ref[0, hh] for hh in range(nh)]

    def step(c, carry, masked):
        base = pl.multiple_of(c * tq, tq)
        ss = []
        for hh in range(nh):
            k = k_ref[0, hh, pl.ds(base, tq), :]
            s = lax.dot_general(qs[hh], k, (((1,), (1,)), ((), ())), preferred_element_type=F32)
            if masked:
                row = lax.broadcasted_iota(jnp.int32, (tq, tq), 0)
                col = lax.broadcasted_iota(jnp.int32, (tq, tq), 1)
                s = jnp.where(col <= row, s, NEG_BIG)
            ss.append(s)
        out = []
        for hh in range(nh):
            m, l, acc = carry[hh]
            v = v_ref[0, hh, pl.ds(base, tq), :]
            m_new = jnp.maximum(m, jnp.max(ss[hh], axis=-1, keepdims=True))
            alpha = jnp.exp2(m - m_new)
            pv = jnp.dot(jnp.exp2((ss[hh] - m_new).astype(BF16)), v, preferred_element_type=F32)
            out.append((m_new, alpha * l + pv[:, V_HEAD_DIM:V_HEAD_DIM + 1], alpha * acc + pv[:, :V_HEAD_DIM]))
        return tuple(out)

    init = tuple((jnp.full((tq, 1), NEG_BIG, F32), jnp.zeros((tq, 1), F32), jnp.zeros((tq, V_HEAD_DIM), F32))
                 for _ in range(nh))
    carry = lax.fori_loop(0, i, lambda c, cr: step(c, cr, False), init)
    carry = step(i, carry, True)
    for hh in range(nh):
        _, l, acc = carry[hh]
        o_ref[0, :, hh * V_HEAD_DIM:(hh + 1) * V_HEAD_DIM] = (acc / l).astype(BF16)


def _mla_attn(q, k, v, tq=512, nh=4):
    b, h, s, dk = q.shape
    dv = v.shape[-1]
    return pl.pallas_call(
        functools.partial(_mla_attn_kernel, tq=tq, nh=nh),
        grid=(b, h // nh, s // tq),
        in_specs=[pl.BlockSpec((1, nh, tq, dk), lambda bi, hi, i: (bi, hi, i, 0)),
                  pl.BlockSpec((1, nh, s, dk), lambda bi, hi, i: (bi, hi, 0, 0)),
                  pl.BlockSpec((1, nh, s, dv), lambda bi, hi, i: (bi, hi, 0, 0))],
        out_specs=pl.BlockSpec((1, tq, nh * V_HEAD_DIM), lambda bi, hi, i: (bi, i, hi)),
        out_shape=jax.ShapeDtypeStruct((b, s, h * V_HEAD_DIM), BF16),
        compiler_params=_cparams("arbitrary", "arbitrary", "arbitrary"),
        name="mla_attn",
    )(q, k, v)


def _dil_attn_kernel(q_ref, kc_ref, kp_ref, vc_ref, vp_ref, o_ref, lse_ref, *, tq):
    i = pl.program_id(2)
    sub = DIL_SPAN
    row = lax.broadcasted_iota(jnp.int32, (sub, 2 * sub), 0)
    col = lax.broadcasted_iota(jnp.int32, (sub, 2 * sub), 1)
    band = jnp.logical_and(col >= row, col <= row + sub)
    first = jnp.logical_and(band, col >= jnp.where(i > 0, 0, sub))
    lane = lax.broadcasted_iota(jnp.int32, (sub, LANES), 1)
    scale = DIL_HEAD_DIM ** -0.5
    dn = (((1,), (1,)), ((), ()))
    chains = [(j, hh) for j in range(tq // sub) for hh in range(DIL_HEADS_PER_GROUP)]

    def window(cur_ref, prev_ref, j, cs):
        if j == 0:
            return jnp.concatenate([prev_ref[0, :, cs], cur_ref[0, :sub, cs]], axis=0)
        return cur_ref[0, (j - 1) * sub:(j + 1) * sub, cs]

    scores = []
    for j, hh in chains:
        cs = slice(hh * LANES, (hh + 1) * LANES)
        s = lax.dot_general(q_ref[0, j * sub:(j + 1) * sub, cs], window(kc_ref, kp_ref, j, cs), dn,
                            preferred_element_type=F32) * scale
        scores.append(jnp.where(first if j == 0 else band, s, NEG_BIG))
    lse_blk = [jnp.zeros((sub, LANES), F32) for _ in range(tq // sub)]
    for (j, hh), s in zip(chains, scores):
        cs = slice(hh * LANES, (hh + 1) * LANES)
        m = jnp.max(s, axis=-1, keepdims=True)
        p = jnp.exp(s - m)
        l = jnp.sum(p, axis=-1, keepdims=True)
        acc = jnp.dot(p.astype(BF16), window(vc_ref, vp_ref, j, cs), preferred_element_type=F32)
        o_ref[0, j * sub:(j + 1) * sub, cs] = (acc * (1.0 / l)).astype(BF16)
        lse_blk[j] = jnp.where(lane == hh, m + jnp.log(l), lse_blk[j])
    for j in range(tq // sub):
        lse_ref[0, j * sub:(j + 1) * sub, :] = lse_blk[j]


def _dil_attn(t, batch, seq, dilation):
    ln = seq // dilation
    tq = min(ln, 4 * DIL_SPAN)
    gc = DIL_GROUP_COLS
    ratio = tq // DIL_SPAN
    cur = lambda which: pl.BlockSpec((1, tq, gc), lambda b, r, i: (b, i, r * 3 + which))
    prev = lambda which: pl.BlockSpec(
        (1, DIL_SPAN, gc), lambda b, r, i: (b, jnp.maximum(i * ratio - 1, 0), r * 3 + which))
    o, lse = pl.pallas_call(
        functools.partial(_dil_attn_kernel, tq=tq),
        grid=(batch, dilation, ln // tq),
        in_specs=[cur(0), cur(1), prev(1), cur(2), prev(2)],
        out_specs=[pl.BlockSpec((1, tq, gc), lambda b, r, i: (b, i, r)),
                   pl.BlockSpec((1, tq, LANES), lambda b, r, i: (b, i, r))],
        out_shape=[jax.ShapeDtypeStruct((batch, ln, dilation * gc), BF16),
                   jax.ShapeDtypeStruct((batch, ln, dilation * LANES), F32)],
        compiler_params=_cparams("arbitrary", "arbitrary", "arbitrary"),
        name=f"dil_attn_d{dilation}",
    )(t, t, t, t, t)
    return o, lse


def _pack_halves(v):
    w = v.shape[1] // 2
    lo = lax.bitcast_convert_type(v[:, :w].astype(BF16).astype(F32), jnp.uint32)
    hi = lax.bitcast_convert_type(v[:, w:].astype(BF16).astype(F32), jnp.uint32)
    return (lo >> 16) | (hi & jnp.uint32(0xFFFF0000))


ROW_TILE = 8


def _row_tile(p):
    return (pl.ds(pl.multiple_of(p * ROW_TILE, ROW_TILE), ROW_TILE), slice(None))


def _store_row_tiles(ref, pk, base=0):
    rows = pk.shape[0]
    for c in range(ROW_TILE):
        ref[pl.ds(base + c, rows, stride=ROW_TILE), :] = pk[:, c * LANES:(c + 1) * LANES]


def _load_row_tiles(ref, rows, base=0):
    return jnp.concatenate([ref[pl.ds(base + c, rows, stride=ROW_TILE), :] for c in range(ROW_TILE)], axis=1)


def _unpack_halves(pk):
    lo = lax.bitcast_convert_type(pk << 16, F32)
    hi = lax.bitcast_convert_type(pk & jnp.uint32(0xFFFF0000), F32)
    return lo, hi


def _merge_kernel(oa_ref, o0_ref, o1_ref, o2_ref, l0_ref, l1_ref, l2_ref, ga_ref, gb_ref, x_ref,
                  gate_ref, shift_ref, scale_ref, gpost_ref, gpre_ref,
                  wa_ref, wb_ref, wo_ref, wrh_ref, wrl_ref,
                  x1_ref, h2_ref, h2pk_ref, logit_ref, o_sc, l_sc):
    tm = x_ref.shape[0]

    def natural(ref, sc, gi, chunks):
        dil = DIL_PATTERNS[gi][1]
        if dil == 1:
            return [ref[0, :, c * LANES:(c + 1) * LANES].astype(F32) for c in range(chunks)]
        for r in range(dil):
            for c in range(chunks):
                c0 = (r * chunks + c) * LANES
                sc.at[gi, c][pl.ds(r, tm // dil, stride=dil), :] = ref[0, :, c0:c0 + LANES].astype(F32)
        return [sc[gi, c] for c in range(chunks)]

    (l0,), (l1,), (l2,) = [natural(ref, l_sc, gi, 1) for gi, ref in enumerate((l0_ref, l1_ref, l2_ref))]
    o0, o1, o2 = [natural(ref, o_sc, gi, DIL_HEADS_PER_GROUP) for gi, ref in enumerate((o0_ref, o1_ref, o2_ref))]
    m = jnp.maximum(jnp.maximum(l0, l1), l2)
    e0, e1, e2 = jnp.exp(l0 - m), jnp.exp(l1 - m), jnp.exp(l2 - m)
    inv = 1.0 / (e0 + e1 + e2)
    w0, w1, w2 = e0 * inv, e1 * inv, e2 * inv
    parts = []
    for hh in range(DIL_HEADS_PER_GROUP):
        parts.append(w0[:, hh:hh + 1] * o0[hh] + w1[:, hh:hh + 1] * o1[hh] + w2[:, hh:hh + 1] * o2[hh])
    o_dil = jnp.concatenate(parts, axis=1).astype(BF16)
    y_a = jnp.dot(oa_ref[...], wa_ref[...], preferred_element_type=F32)
    y_b = jnp.dot(o_dil, wb_ref[...], preferred_element_type=F32)
    merged = ga_ref[...].astype(F32) * y_a + gb_ref[...].astype(F32) * y_b
    y = jnp.dot(merged.astype(BF16), wo_ref[...], preferred_element_type=F32)
    x1 = x_ref[...] + gate_ref[0] * _rms(y, gpost_ref[...])
    x1_ref[...] = x1
    h2 = _rms(x1, gpre_ref[...]) * (1.0 + scale_ref[0]) + shift_ref[0]
    _store_row_tiles(h2pk_ref, _pack_halves(h2))
    h2_hi = h2.astype(BF16)
    h2_ref[...] = h2_hi
    h2_lo = (h2 - h2_hi.astype(F32)).astype(BF16)
    both = jnp.dot(h2_hi, wrh_ref[...], preferred_element_type=F32)
    logits = both[:, :LANES] + both[:, LANES:] + jnp.dot(h2_lo, wrl_ref[...], preferred_element_type=F32)
    logit_ref[...] = logits.T


def _merge(oa, dil_o, dil_lse, gates, x2, mod3, gpost, gpre, wa, wb, wo, wr_hi, wr_lo, seq, tm=256):
    n, d = x2.shape
    per_b = seq // tm
    row = lambda w: pl.BlockSpec((tm, w), lambda i: (i, 0))
    const = lambda a: pl.BlockSpec(a.shape, lambda i: (0,) * a.ndim, pipeline_mode=pl.Buffered(1))
    modspec = lambda ch: pl.BlockSpec((1, 1, d), lambda i: ((i // per_b) * ADALN_CHUNKS + ch, 0, 0))
    strided = lambda gi, w: pl.BlockSpec((1, tm // DIL_PATTERNS[gi][1], DIL_PATTERNS[gi][1] * w),
                                         lambda i: (i // per_b, i % per_b, 0))
    return pl.pallas_call(
        _merge_kernel,
        grid=(n // tm,),
        in_specs=[row(oa.shape[1]),
                  strided(0, DIL_GROUP_COLS), strided(1, DIL_GROUP_COLS), strided(2, DIL_GROUP_COLS),
                  strided(0, LANES), strided(1, LANES), strided(2, LANES),
                  pl.BlockSpec((tm, d), lambda i: (i, 0)), pl.BlockSpec((tm, d), lambda i: (i, 1)),
                  row(d),
                  modspec(2), modspec(3), modspec(4),
                  const(gpost), const(gpre),
                  const(wa), const(wb), const(wo), const(wr_hi), const(wr_lo)],
        out_specs=[row(d), row(d), pl.BlockSpec((tm * ROW_TILE, LANES), lambda i: (i, 0)),
                   pl.BlockSpec((LANES, tm), lambda i: (0, i))],
        out_shape=[jax.ShapeDtypeStruct((n, d), F32),
                   jax.ShapeDtypeStruct((n, d), BF16),
                   jax.ShapeDtypeStruct((n * ROW_TILE, LANES), jnp.uint32),
                   jax.ShapeDtypeStruct((LANES, n), F32)],
        scratch_shapes=[pltpu.VMEM((DIL_GROUPS, DIL_HEADS_PER_GROUP, tm, LANES), F32),
                        pltpu.VMEM((DIL_GROUPS, 1, tm, LANES), F32)],
        compiler_params=_cparams("arbitrary"),
        name="merge_outproj",
    )(oa, *dil_o, *dil_lse, gates, gates, x2, mod3, mod3, mod3, gpost, gpre, wa, wb, wo, wr_hi, wr_lo)


def _route_kernel(lg_ref, bias_ref, pos_ref, w_ref, meta_ref, blke_ref, cnt_sc, base_sc, rank_sc, score_sc, *, tt):
    ps = pl.program_id(0)
    i = pl.program_id(1)
    per_group = N_EXPERTS // N_EXPERT_GROUPS
    neg_inf = -jnp.inf
    tile = pl.ds(pl.multiple_of(i * tt, tt), tt)

    @pl.when(jnp.logical_and(ps == 0, i == 0))
    def _():
        cnt_sc[...] = jnp.zeros_like(cnt_sc)

    @pl.when(ps == 0)
    def _():
        scores = _sigmoid(lg_ref[...])
        biased = scores + bias_ref[...]
        b3 = biased.reshape(N_EXPERT_GROUPS, per_group, tt)
        mem = lax.broadcasted_iota(jnp.int32, b3.shape, 1)
        m1 = jnp.max(b3, axis=1, keepdims=True)
        first = jnp.min(jnp.where(b3 == m1, mem, per_group), axis=1, keepdims=True)
        m2 = jnp.max(jnp.where(mem == first, neg_inf, b3), axis=1, keepdims=True)
        gs = m1 + m2
        gidx = lax.broadcasted_iota(jnp.int32, gs.shape, 0)
        grank = jnp.zeros(gs.shape, jnp.int32)
        for g2 in range(N_EXPERT_GROUPS):
            r = gs[g2:g2 + 1]
            beats = jnp.logical_or(r > gs, jnp.logical_and(r == gs, g2 < gidx))
            grank = grank + jnp.where(beats, 1, 0)
        sel = jnp.where(grank < TOPK_GROUPS, b3, neg_inf).reshape(N_EXPERTS, tt)
        eidx = lax.broadcasted_iota(jnp.int32, sel.shape, 0)
        erank = jnp.zeros(sel.shape, jnp.int32)
        for e2 in range(N_EXPERTS):
            r = sel[e2:e2 + 1, :]
            beats = jnp.logical_or(r > sel, jnp.logical_and(r == sel, e2 < eidx))
            erank = erank + jnp.where(beats, 1, 0)
        rank_sc[:, tile] = erank
        score_sc[:, tile] = scores
        cnt_sc[...] = cnt_sc[...] + jnp.sum(jnp.where(erank < TOP_K, 1.0, 0.0), axis=1,
                                            keepdims=True).astype(jnp.int32)

    @pl.when(jnp.logical_and(ps == 1, i == 0))
    def _():
        cnt = cnt_sc[...]
        pc = ((cnt + (MOE_BLOCK - 1)) // MOE_BLOCK) * MOE_BLOCK
        pcb = jnp.broadcast_to(pc, (N_EXPERTS, LANES))
        eid = lax.broadcasted_iota(jnp.int32, (N_EXPERTS, LANES), 0)
        pends = jnp.zeros((N_EXPERTS, LANES), jnp.int32)
        for e2 in range(N_EXPERTS):
            pends = pends + jnp.where(eid >= e2, pcb[e2:e2 + 1, :], 0)
        pst = pends - pcb
        base_sc[...] = pst[:, 0:1]
        meta_ref[0] = jnp.broadcast_to(cnt, (N_EXPERTS, LANES))
        meta_ref[1] = pst
        meta_ref[2] = pends
        nbl = blke_ref.shape[1]
        blk_start = lax.broadcasted_iota(jnp.int32, (N_EXPERTS, nbl), 1) * MOE_BLOCK
        pend_b = jnp.broadcast_to(pends[:, 0:1], (N_EXPERTS, nbl))
        be = jnp.sum(jnp.where(pend_b <= blk_start, 1, 0), axis=0, keepdims=True)
        blke_ref[...] = jnp.broadcast_to(jnp.minimum(be, N_EXPERTS - 1), blke_ref.shape)

    @pl.when(ps == 1)
    def _():
        erank = rank_sc[:, tile]
        scores = score_sc[:, tile]
        esel = erank < TOP_K
        mask_f = jnp.where(esel, 1.0, 0.0)
        tile_cnt = jnp.sum(mask_f, axis=1, keepdims=True).astype(jnp.int32)
        rr = lax.broadcasted_iota(jnp.int32, (tt, tt), 0)
        cc = lax.broadcasted_iota(jnp.int32, (tt, tt), 1)
        upper = jnp.where(rr < cc, 1.0, 0.0).astype(BF16)
        prefix = jnp.dot(mask_f.astype(BF16), upper, preferred_element_type=F32)
        posd = base_sc[...] + prefix.astype(jnp.int32)
        base_sc[...] = base_sc[...] + tile_cnt
        wsel = jnp.where(esel, scores, 0.0)
        denom = jnp.sum(wsel, axis=0, keepdims=True)
        wn = wsel / (denom + 1e-20) * ROUTED_SCALE
        prow, wrow = [], []
        for kk in range(TOP_K):
            hit = erank == kk
            prow.append(jnp.sum(jnp.where(hit, posd, 0), axis=0, keepdims=True))
            wrow.append(jnp.sum(jnp.where(hit, wn, 0.0), axis=0, keepdims=True))
        pad = pos_ref.shape[0] - TOP_K
        pos_ref[...] = jnp.concatenate(prow + [jnp.zeros((pad, tt), jnp.int32)], axis=0)
        w_ref[...] = jnp.concatenate(wrow + [jnp.zeros((pad, tt), F32)], axis=0)


def _route(logits_t, bias_col, nb, tt=256):
    n = logits_t.shape[1]
    nbl = -(-nb // LANES) * LANES
    return pl.pallas_call(
        functools.partial(_route_kernel, tt=tt),
        grid=(2, n // tt),
        in_specs=[pl.BlockSpec((N_EXPERTS, tt), lambda ps, i: (0, i * (1 - ps))),
                  pl.BlockSpec((N_EXPERTS, 1), lambda ps, i: (0, 0))],
        out_specs=[pl.BlockSpec((8, tt), lambda ps, i: (0, ps * i)),
                   pl.BlockSpec((8, tt), lambda ps, i: (0, ps * i)),
                   pl.BlockSpec((3, N_EXPERTS, LANES), lambda ps, i: (0, 0, 0)),
                   pl.BlockSpec((8, nbl), lambda ps, i: (0, 0))],
        out_shape=[jax.ShapeDtypeStruct((8, n), jnp.int32),
                   jax.ShapeDtypeStruct((8, n), F32),
                   jax.ShapeDtypeStruct((3, N_EXPERTS, LANES), jnp.int32),
                   jax.ShapeDtypeStruct((8, nbl), jnp.int32)],
        scratch_shapes=[pltpu.VMEM((N_EXPERTS, 1), jnp.int32), pltpu.VMEM((N_EXPERTS, 1), jnp.int32),
                        pltpu.VMEM((N_EXPERTS, n), jnp.int32), pltpu.VMEM((N_EXPERTS, n), F32)],
        compiler_params=_cparams("arbitrary", "arbitrary"),
        name="moe_route",
    )(logits_t, bias_col)


def _dispatch_kernel(pos_ref, cnt_ref, pst_ref, h_hbm, hb_ref, wsg_ref, wsu_ref, wsd_ref, xs_hbm, ysh_ref,
                     hbuf, zrow, in_sem, sc_sem, sem, *, tm, n):
    i = pl.program_id(0)
    nt = pl.num_programs(0)
    tile_rows = tm * ROW_TILE

    def fetch(tile, slot):
        return pltpu.make_async_copy(h_hbm.at[pl.ds(pl.multiple_of(tile * tile_rows, tile_rows), tile_rows), :],
                                     hbuf.at[slot], in_sem.at[slot])

    def wait_rows(slot):
        for kk in range(TOP_K):
            pltpu.make_async_copy(hbuf.at[slot], xs_hbm.at[pl.ds(0, tile_rows), :], sc_sem.at[slot]).wait()

    @pl.when(i == 0)
    def _():
        fetch(0, 0).start()
        fetch(1, 1).start()

    slot = i % 3
    fetch(i, slot).wait()
    src = hbuf.at[slot]

    def body(g, carry):
        for u in range(ROW_DMA_UNROLL):
            r = g * ROW_DMA_UNROLL + u
            for kk in range(TOP_K):
                p = pos_ref[kk * n + i * tm + r]
                pltpu.make_async_copy(src.at[_row_tile(r)], xs_hbm.at[_row_tile(p)],
                                      sc_sem.at[slot]).start(priority=kk % 2)
        return carry
    lax.fori_loop(0, tm // ROW_DMA_UNROLL, body, 0)

    hb = hb_ref[...]
    g = jnp.dot(hb, wsg_ref[...], preferred_element_type=F32)
    u = jnp.dot(hb, wsu_ref[...], preferred_element_type=F32)
    ysh_ref[...] = jnp.dot((g * _sigmoid(g) * u).astype(BF16), wsd_ref[...],
                           preferred_element_type=F32).astype(BF16)

    @pl.when(i >= 1)
    def _():
        wait_rows((i + 2) % 3)

    @pl.when(i + 2 < nt)
    def _():
        fetch(i + 2, (i + 2) % 3).start()

    @pl.when(i == nt - 1)
    def _():
        wait_rows(slot)

    @pl.when(i == nt - 1)
    def _():
        zrow[...] = jnp.zeros_like(zrow)

        def per_expert(e, carry):
            cnt = cnt_ref[e]
            first = pst_ref[e] + cnt
            npad = ((cnt + (MOE_BLOCK - 1)) // MOE_BLOCK) * MOE_BLOCK - cnt

            def start(s, c2):
                pltpu.make_async_copy(zrow.at[_row_tile(0)], xs_hbm.at[_row_tile(first + s)], sem.at[1]).start()
                return c2

            def wait(s, c2):
                pltpu.make_async_copy(zrow.at[_row_tile(0)], xs_hbm.at[_row_tile(0)], sem.at[1]).wait()
                return c2
            lax.fori_loop(0, npad, start, 0)
            lax.fori_loop(0, npad, wait, 0)
            return carry
        lax.fori_loop(0, N_EXPERTS, per_expert, 0)

        last = N_EXPERTS - 1
        used = (pst_ref[last] + cnt_ref[last] + (MOE_BLOCK - 1)) // MOE_BLOCK
        blk_rows = MOE_BLOCK * ROW_TILE

        def tail(b, carry):
            cp = pltpu.make_async_copy(zrow, xs_hbm.at[pl.ds(pl.multiple_of(b * blk_rows, blk_rows), blk_rows), :],
                                       sem.at[1])
            cp.start()
            cp.wait()
            return carry
        lax.fori_loop(used, xs_hbm.shape[0] // blk_rows, tail, 0)


def _dispatch(pos_flat, cnt, pst, h2pk, h2, wsg, wsu, wsd, nb, tm=256):
    n, d = h2.shape
    assert n // tm >= 2
    const = lambda a: pl.BlockSpec(a.shape, lambda i, *_: (0,) * a.ndim)
    grid_spec = pltpu.PrefetchScalarGridSpec(
        num_scalar_prefetch=3,
        grid=(n // tm,),
        in_specs=[pl.BlockSpec(memory_space=pl.ANY),
                  pl.BlockSpec((tm, d), lambda i, *_: (i, 0)),
                  const(wsg), const(wsu), const(wsd)],
        out_specs=[pl.BlockSpec(memory_space=pl.ANY), pl.BlockSpec((tm, d), lambda i, *_: (i, 0))],
        scratch_shapes=[pltpu.VMEM((3, tm * ROW_TILE, LANES), jnp.uint32),
                        pltpu.VMEM((MOE_BLOCK * ROW_TILE, LANES), jnp.uint32),
                        pltpu.SemaphoreType.DMA((3,)), pltpu.SemaphoreType.DMA((3,)),
                        pltpu.SemaphoreType.DMA((2,))],
    )
    return pl.pallas_call(
        functools.partial(_dispatch_kernel, tm=tm, n=n),
        grid_spec=grid_spec,
        out_shape=[jax.ShapeDtypeStruct((nb * MOE_BLOCK * ROW_TILE, LANES), jnp.uint32),
                   jax.ShapeDtypeStruct((n, d), BF16)],
        compiler_params=_cparams("arbitrary"),
        name="moe_dispatch",
    )(pos_flat, cnt, pst, h2pk, h2, wsg, wsu, wsd)


def _gmm_kernel(nact_ref, blke_ref, xs_ref, wg_hbm, wu_hbm, wd_hbm, o_ref,
                wgf, wuf, wdf, wgb, wub, wdb, sem):
    nact = nact_ref[0]

    def fetch(e):
        return (pltpu.make_async_copy(wg_hbm.at[e], wgf, sem.at[0]),
                pltpu.make_async_copy(wu_hbm.at[e], wuf, sem.at[1]),
                pltpu.make_async_copy(wd_hbm.at[e], wdf, sem.at[2]))

    @pl.when(pl.program_id(0) == 0)
    def _():
        for cp in fetch(blke_ref[0]):
            cp.start()

    def switch_weights(i):
        e = blke_ref[i]
        changed = jnp.logical_or(i == 0, e != blke_ref[jnp.maximum(i - 1, 0)])

        @pl.when(changed)
        def _():
            for cp in fetch(e):
                cp.wait()
            wgb[...] = wgf[...].astype(BF16)
            wub[...] = wuf[...].astype(BF16)
            wdb[...] = wdf[...].astype(BF16)
            nxt = lax.while_loop(lambda j: jnp.logical_and(j < nact, blke_ref[jnp.minimum(j, nact - 1)] == e),
                                 lambda j: j + 1, i + 1)

            @pl.when(nxt < nact)
            def _():
                for cp in fetch(blke_ref[jnp.minimum(nxt, nact - 1)]):
                    cp.start(priority=1)

    def swiglu(block, nblocks):
        base = block * MOE_BLOCK * ROW_TILE
        lo, hi = _unpack_halves(_load_row_tiles(xs_ref, nblocks * MOE_BLOCK, base))
        lo, hi = lo.astype(BF16), hi.astype(BF16)
        half = lo.shape[1]
        g = (jnp.dot(lo, wgb[:half, :], preferred_element_type=F32)
             + jnp.dot(hi, wgb[half:, :], preferred_element_type=F32))
        u = (jnp.dot(lo, wub[:half, :], preferred_element_type=F32)
             + jnp.dot(hi, wub[half:, :], preferred_element_type=F32))
        a = (g * _sigmoid(g) * u).astype(BF16)
        _store_row_tiles(o_ref, _pack_halves(jnp.dot(a, wdb[...], preferred_element_type=F32)), base)

    for pair in range(GMM_SUB // 2):
        sb0 = 2 * pair
        i0 = pl.program_id(0) * GMM_SUB + sb0
        i1 = i0 + 1
        paired = jnp.logical_and(i1 < nact, blke_ref[jnp.minimum(i0, nact - 1)] == blke_ref[jnp.minimum(i1, nact - 1)])

        @pl.when(i0 < nact)
        def _(i0=i0):
            switch_weights(i0)

        @pl.when(paired)
        def _(sb0=sb0):
            swiglu(sb0, 2)

        @pl.when(jnp.logical_and(jnp.logical_not(paired), i0 < nact))
        def _(sb0=sb0):
            swiglu(sb0, 1)

        @pl.when(jnp.logical_and(jnp.logical_not(paired), i1 < nact))
        def _(i1=i1, sb0=sb0):
            switch_weights(i1)
            swiglu(sb0 + 1, 1)

        for sb, i in ((sb0, i0), (sb0 + 1, i1)):
            @pl.when(i >= nact)
            def _(sb=sb):
                o_ref[pl.ds(sb * MOE_BLOCK * ROW_TILE, MOE_BLOCK * ROW_TILE), :] = jnp.zeros(
                    (MOE_BLOCK * ROW_TILE, LANES), o_ref.dtype)


def _gmm(nact, blk_e, xs, w_gate, w_up, w_down, nb):
    d, f = w_gate.shape[1:]
    assert GMM_SUB % 2 == 0 and nb % GMM_SUB == 0
    rows = GMM_SUB * MOE_BLOCK * ROW_TILE
    blk = lambda i, na: jnp.minimum(i, (na[0] - 1) // GMM_SUB)
    hbm = pl.BlockSpec(memory_space=pl.ANY)
    grid_spec = pltpu.PrefetchScalarGridSpec(
        num_scalar_prefetch=2,
        grid=(nb // GMM_SUB,),
        in_specs=[pl.BlockSpec((rows, LANES), lambda i, na, be: (blk(i, na), 0)), hbm, hbm, hbm],
        out_specs=pl.BlockSpec((rows, LANES), lambda i, na, be: (i, 0)),
        scratch_shapes=[pltpu.VMEM((d, f), F32), pltpu.VMEM((d, f), F32), pltpu.VMEM((f, d), F32),
                        pltpu.VMEM((d, f), BF16), pltpu.VMEM((d, f), BF16), pltpu.VMEM((f, d), BF16),
                        pltpu.SemaphoreType.DMA((3,))],
    )
    return pl.pallas_call(
        _gmm_kernel,
        grid_spec=grid_spec,
        out_shape=jax.ShapeDtypeStruct((nb * MOE_BLOCK * ROW_TILE, LANES), jnp.uint32),
        compiler_params=_cparams("arbitrary"),
        name="moe_experts",
    )(nact, blk_e, xs, w_gate, w_up, w_down)


def _final_kernel(pos_ref, tw_ref, ysh_ref, x1_ref, gate_ref, gpost_ref, yb_hbm, o_ref, rbuf, sem, *, tm, n):
    i = pl.program_id(0)
    slot = i % 2

    def gather(tile, sl):
        def body(g, carry):
            for u in range(ROW_DMA_UNROLL):
                r = g * ROW_DMA_UNROLL + u
                for kk in range(TOP_K):
                    p = pos_ref[kk * n + tile * tm + r]
                    pltpu.make_async_copy(yb_hbm.at[_row_tile(p)], rbuf.at[sl, kk].at[_row_tile(r)],
                                          sem.at[sl]).start(priority=kk % 2)
            return carry
        lax.fori_loop(0, tm // ROW_DMA_UNROLL, body, 0)

    @pl.when(i == 0)
    def _():
        gather(0, 0)

    @pl.when(i + 1 < pl.num_programs(0))
    def _():
        gather(i + 1, 1 - slot)

    for kk in range(TOP_K):
        pltpu.make_async_copy(yb_hbm.at[pl.ds(0, tm * ROW_TILE), :], rbuf.at[slot, kk], sem.at[slot]).wait()
    tw = tw_ref[...]
    half = ROW_TILE * LANES
    r_lo = jnp.zeros((tm, half), F32)
    r_hi = jnp.zeros((tm, half), F32)
    for kk in range(TOP_K):
        lo, hi = _unpack_halves(_load_row_tiles(rbuf.at[slot, kk], tm))
        wk = tw[:, kk:kk + 1]
        r_lo = r_lo + wk * lo
        r_hi = r_hi + wk * hi
    y = ysh_ref[...].astype(F32) + jnp.concatenate([r_lo, r_hi], axis=1)
    o_ref[...] = x1_ref[...] + gate_ref[0] * _rms(y, gpost_ref[...])


def _final(pos_flat, top_w8, ysh, x1, mod3, gpost, yb, seq, tm=256):
    n, d = x1.shape
    per_b = seq // tm
    const = lambda a: pl.BlockSpec(a.shape, lambda i, ps: (0,) * a.ndim)
    row = lambda w: pl.BlockSpec((tm, w), lambda i, ps: (i, 0))
    grid_spec = pltpu.PrefetchScalarGridSpec(
        num_scalar_prefetch=1,
        grid=(n // tm,),
        in_specs=[row(top_w8.shape[1]), row(d), row(d),
                  pl.BlockSpec((1, 1, d), lambda i, ps: ((i // per_b) * ADALN_CHUNKS + 5, 0, 0)),
                  const(gpost),
                  pl.BlockSpec(memory_space=pl.ANY)],
        out_specs=row(d),
        scratch_shapes=[pltpu.VMEM((2, TOP_K, tm * ROW_TILE, LANES), jnp.uint32), pltpu.SemaphoreType.DMA((2,))],
    )
    return pl.pallas_call(
        functools.partial(_final_kernel, tm=tm, n=n),
        grid_spec=grid_spec,
        out_shape=jax.ShapeDtypeStruct((n, d), F32),
        compiler_params=_cparams("arbitrary"),
        name="moe_combine_final",
    )(pos_flat, top_w8, ysh, x1, mod3, gpost, yb)


def _rope_freq_row(dim):
    inv_freq = 1.0 / (ROPE_THETA ** (jnp.arange(0, dim, 2, dtype=F32) / dim))
    return jnp.concatenate([inv_freq, inv_freq, jnp.zeros((LANES - dim,), F32)]).reshape(1, LANES)


def kernel(x, c, positions, w_ada, b_ada, attn_pre_g, w_in, q_a_norm_g, w_q_up, kv_a_norm_g, w_kv_up, w_mla_o, w_dil_o, w_out, attn_post_g, ffn_pre_g, w_router, router_bias, w_exp_gate, w_exp_up, w_exp_down, w_sh_gate, w_sh_up, w_sh_down, ffn_post_g):
    batch, seq, d = x.shape
    n = batch * seq
    depth = w_ada.shape[0]

    pos_col = positions.astype(F32).reshape(n, 1)
    freq_mla, freq_dil = _rope_freq_row(QK_ROPE_DIM), _rope_freq_row(DIL_ROT_DIM)

    x2 = x.reshape(n, d)
    c8 = jnp.pad(c, ((0, 8 - batch), (0, 0)))
    for l in range(depth):
        mod = _ada(c8, w_ada[l], b_ada[l].reshape(1, -1))
        mod3 = mod[:batch].reshape(batch * ADALN_CHUNKS, 1, d)

        wi = w_in[l]
        o_dil = Q_LORA_RANK + KV_LORA_RANK + QK_ROPE_DIM
        o_ga = o_dil + 3 * DIL_HEADS * DIL_HEAD_DIM
        n_gate, n_dil = wi.shape[1] - o_ga, o_ga - o_dil
        n_a = o_dil + LANES - QK_ROPE_DIM
        c_dil, c_a = n_gate, -(-(n_gate + n_dil) // n_a) * n_a
        w_full = _wprep(wi.T, ((0, o_ga, n_gate), (c_dil, o_dil, n_dil), (c_a, 0, o_dil)), c_a + n_a)
        wq3 = w_q_up[l].reshape(Q_LORA_RANK, MLA_HEADS, MLA_QK_DIM)
        wq = jnp.concatenate([wq3, jnp.zeros((Q_LORA_RANK, MLA_HEADS, MLA_QK_PAD - MLA_QK_DIM), F32)],
                             axis=2).reshape(Q_LORA_RANK, MLA_HEADS * MLA_QK_PAD).astype(BF16)
        wkv3 = w_kv_up[l].reshape(KV_LORA_RANK, MLA_HEADS, QK_NOPE_DIM + V_HEAD_DIM)
        wkv = jnp.concatenate([wkv3[:, :, :QK_NOPE_DIM].reshape(KV_LORA_RANK, -1),
                               wkv3[:, :, QK_NOPE_DIM:].reshape(KV_LORA_RANK, -1)], axis=1).astype(BF16)

        h, q, k, v = _front(x2, attn_pre_g[l].reshape(1, d), mod3, w_full, c_a, n_a,
                            q_a_norm_g[l].reshape(1, -1), kv_a_norm_g[l].reshape(1, -1), wq, wkv,
                            pos_col, freq_mla, batch, seq)
        gates = _mm(h, w_full, 0, n_gate, act="sigmoid")
        o_mla = _mla_attn(q, k, v).reshape(n, MLA_HEADS * V_HEAD_DIM)
        dil_o, dil_lse = [], []
        rope_dil = (pos_col, freq_dil)
        for g, (_, dilation) in enumerate(DIL_PATTERNS):
            qkv, rope_dil = _dilproj(h, w_full, c_dil, g, rope_dil, batch, seq, dilation)
            o_g, lse_g = _dil_attn(qkv, batch, seq, dilation)
            dil_o.append(o_g)
            dil_lse.append(lse_g)

        wr = jnp.pad(w_router[l], ((0, 0), (0, LANES - N_EXPERTS)))
        wr_hi = wr.astype(BF16)
        wr_lo = (wr - wr_hi.astype(F32)).astype(BF16)
        x1, h2, h2pk, logits_t = _merge(o_mla, dil_o, dil_lse, gates, x2, mod3,
                                        attn_post_g[l].reshape(1, d), ffn_pre_g[l].reshape(1, d),
                                        w_mla_o[l].astype(BF16), w_dil_o[l].astype(BF16), w_out[l].astype(BF16),
                                        jnp.concatenate([wr_hi, wr_lo], axis=1), wr_hi, seq)

        nb = -(-(n * TOP_K + N_EXPERTS * (MOE_BLOCK - 1)) // MOE_BLOCK)
        pos_t, w_t, meta, blk_e = _route(logits_t, router_bias[l].astype(F32).reshape(N_EXPERTS, 1), nb)
        pos_flat = pos_t.reshape(-1)
        nact = meta[2, N_EXPERTS - 1, :1] // MOE_BLOCK
        xs, ysh = _dispatch(pos_flat, meta[0, :, 0], meta[1, :, 0], h2pk, h2,
                            w_sh_gate[l].astype(BF16), w_sh_up[l].astype(BF16), w_sh_down[l].astype(BF16), nb)
        yb = _gmm(nact, blk_e[0], xs, w_exp_gate[l], w_exp_up[l], w_exp_down[l], nb)
        x2 = _final(pos_flat, w_t.T, ysh, x1, mod3, ffn_post_g[l].reshape(1, d), yb, seq)
    return x2.reshape(batch, seq, d)
```

```python
import functools

import jax
import jax.numpy as jnp
from jax import lax
from jax.experimental import pallas as pl
from jax.experimental.pallas import tpu as pltpu

F32 = jnp.float32
BF16 = jnp.bfloat16

D_MODEL = 2048
NORM_EPS = 1e-6
ROPE_THETA = 500000.0
ADALN_CHUNKS = 6

MLA_HEADS = 8
Q_LORA_RANK = 512
KV_LORA_RANK = 512
QK_NOPE_DIM = 128
QK_ROPE_DIM = 64
V_HEAD_DIM = 128
MLA_QK_DIM = QK_NOPE_DIM + QK_ROPE_DIM
MLA_QK_PAD = 256

DIL_PATTERNS = ((128, 1), (512, 4), (2048, 16))
DIL_GROUPS = len(DIL_PATTERNS)
DIL_HEADS_PER_GROUP = 4
DIL_HEADS = DIL_GROUPS * DIL_HEADS_PER_GROUP
DIL_HEAD_DIM = 128
DIL_ROT_DIM = DIL_HEAD_DIM // 4
DIL_SPAN = 128
DIL_GROUP_COLS = DIL_HEADS_PER_GROUP * DIL_HEAD_DIM

N_EXPERTS = 64
N_EXPERT_GROUPS = 8
TOPK_GROUPS = 4
TOP_K = 6
EXPERT_DIM = 512
SHARED_DIM = 512
ROUTED_SCALE = 2.5
MOE_BLOCK = 256

LANES = 128
NEG_BIG = -1e30
LOG2_E = 1.4426950408889634
ROW_DMA_UNROLL = 4
GMM_SUB = 4
VMEM_LIMIT = 56 * 1024 * 1024


def _cparams(*sem):
    return pltpu.CompilerParams(dimension_semantics=sem, vmem_limit_bytes=VMEM_LIMIT)


def _sigmoid(v):
    return 1.0 / (1.0 + jnp.exp(-v))


def _rms(v, g):
    ms = jnp.mean(v * v, axis=-1, keepdims=True)
    return v * lax.rsqrt(ms + NORM_EPS) * g


def _ada_kernel(c_ref, w_ref, b_ref, o_ref):
    c = c_ref[...]
    a = (c * _sigmoid(c)).astype(BF16)
    o_ref[...] = jnp.dot(a, w_ref[...].astype(BF16), preferred_element_type=F32) + b_ref[...]


def _ada(c8, w_ada, b_ada, tn=1536):
    d, n = w_ada.shape
    return pl.pallas_call(
        _ada_kernel,
        grid=(n // tn,),
        in_specs=[pl.BlockSpec((8, d), lambda j: (0, 0)),
                  pl.BlockSpec((d, tn), lambda j: (0, j)),
                  pl.BlockSpec((1, tn), lambda j: (0, j))],
        out_specs=pl.BlockSpec((8, tn), lambda j: (0, j)),
        out_shape=jax.ShapeDtypeStruct((8, n), F32),
        compiler_params=_cparams("arbitrary"),
        name="ada_mod",
    )(c8, w_ada, b_ada)


def _wprep_source(j, segments):
    src = j * 0
    valid = j * 0
    for dst, s0, width in segments:
        t0, t1 = dst // LANES, (dst + width + LANES - 1) // LANES
        inside = jnp.logical_and(j >= t0, j < t1)
        src = jnp.where(inside, s0 + (j - t0) * LANES, src)
        valid = jnp.where(inside, jnp.minimum(dst + width - j * LANES, LANES), valid)
    return src, valid


def _wprep_kernel(*refs, segments, tiles):
    o_ref = refs[-1]
    for t in range(tiles):
        _, valid = _wprep_source(pl.program_id(0) * tiles + t, segments)
        blk = jnp.concatenate([refs[2 * t][...], refs[2 * t + 1][...]], axis=0).T
        lane = lax.broadcasted_iota(jnp.int32, blk.shape, 1)
        o_ref[:, t * LANES:(t + 1) * LANES] = jnp.where(lane < valid, blk, 0.0).astype(BF16)


def _wprep(w_t, segments, cols, tiles=3):
    _, k = w_t.shape
    half = LANES // 2
    assert all(dst % LANES == 0 and s0 % half == 0 and width % half == 0 for dst, s0, width in segments)
    assert cols % (tiles * LANES) == 0
    part = lambda t, which: pl.BlockSpec(
        (half, k), lambda j: (_wprep_source(j * tiles + t, segments)[0] // half + which, 0))
    return pl.pallas_call(
        functools.partial(_wprep_kernel, segments=segments, tiles=tiles),
        grid=(cols // (tiles * LANES),),
        in_specs=[part(t, which) for t in range(tiles) for which in range(2)],
        out_specs=pl.BlockSpec((k, tiles * LANES), lambda j: (0, j)),
        out_shape=jax.ShapeDtypeStruct((k, cols), BF16),
        compiler_params=_cparams("arbitrary"),
        name="w_in_prep",
    )(*([w_t] * (2 * tiles)))


def _rope_lanes(t, c_tab, s_fwd, s_bwd, half):
    return t * c_tab + pltpu.roll(t, half, 1) * s_fwd + pltpu.roll(t, LANES - half, 1) * s_bwd


def _rope_tables(pos, freq, dim, passthrough):
    ang = pos * freq
    cos, sin = jnp.cos(ang), jnp.sin(ang)
    lane = lax.broadcasted_iota(jnp.int32, ang.shape, 1)
    half = dim // 2
    c_tab = jnp.where(lane < dim, cos, 1.0 if passthrough else 0.0)
    s_fwd = jnp.where(jnp.logical_and(lane >= half, lane < dim), sin, 0.0)
    s_bwd = jnp.where(lane < half, -sin, 0.0)
    return c_tab, s_fwd, s_bwd


def _front_kernel(x_ref, g_ref, scale_ref, shift_ref, wa_ref, gq_ref, gkv_ref, wq_ref, wkv_ref,
                  pos_ref, freq_ref, h_ref, q_ref, k_ref, v_ref):
    h = (_rms(x_ref[...], g_ref[...]) * (1.0 + scale_ref[0]) + shift_ref[0]).astype(BF16)
    h_ref[...] = h
    a = jnp.dot(h, wa_ref[...], preferred_element_type=F32)
    qa = a[:, :Q_LORA_RANK]
    ckv = a[:, Q_LORA_RANK:Q_LORA_RANK + KV_LORA_RANK]
    kr = a[:, Q_LORA_RANK + KV_LORA_RANK:]
    c_tab, s_fwd, s_bwd = _rope_tables(pos_ref[...], freq_ref[...], QK_ROPE_DIM, passthrough=False)
    half = QK_ROPE_DIM // 2
    q = jnp.dot(_rms(qa, gq_ref[...]).astype(BF16), wq_ref[...], preferred_element_type=F32)
    q = q * (MLA_QK_DIM ** -0.5 * LOG2_E)
    kv = jnp.dot(_rms(ckv, gkv_ref[...]).astype(BF16), wkv_ref[...], preferred_element_type=F32)
    k_rot = _rope_lanes(kr, c_tab, s_fwd, s_bwd, half).astype(BF16)
    lane = lax.broadcasted_iota(jnp.int32, (a.shape[0], LANES), 1)
    ones_col = jnp.where(lane == 0, 1.0, 0.0).astype(BF16)
    for hh in range(MLA_HEADS):
        base = hh * MLA_QK_PAD
        q_ref[0, hh, :, :LANES] = q[:, base:base + LANES].astype(BF16)
        q_ref[0, hh, :, LANES:] = _rope_lanes(q[:, base + LANES:base + 2 * LANES],
                                              c_tab, s_fwd, s_bwd, half).astype(BF16)
        k_ref[0, hh, :, :LANES] = kv[:, hh * LANES:(hh + 1) * LANES].astype(BF16)
        k_ref[0, hh, :, LANES:] = k_rot
        v_off = MLA_HEADS * LANES + hh * LANES
        v_ref[0, hh, :, :LANES] = kv[:, v_off:v_off + LANES].astype(BF16)
        v_ref[0, hh, :, LANES:] = ones_col


def _front(x2, g, mod3, w_full, col_a, n_a, gq, gkv, wq, wkv, pos_col, freq, batch, seq, tm=512):
    n, d = x2.shape
    per_b = seq // tm
    const = lambda a: pl.BlockSpec(a.shape, lambda i: (0,) * a.ndim, pipeline_mode=pl.Buffered(1))
    head_major = lambda w: pl.BlockSpec((1, MLA_HEADS, tm, w), lambda i: (i // per_b, 0, i % per_b, 0))
    return pl.pallas_call(
        _front_kernel,
        grid=(n // tm,),
        in_specs=[pl.BlockSpec((tm, d), lambda i: (i, 0)),
                  pl.BlockSpec((1, d), lambda i: (0, 0)),
                  pl.BlockSpec((1, 1, d), lambda i: ((i // per_b) * ADALN_CHUNKS + 1, 0, 0)),
                  pl.BlockSpec((1, 1, d), lambda i: ((i // per_b) * ADALN_CHUNKS + 0, 0, 0)),
                  pl.BlockSpec((d, n_a), lambda i: (0, col_a // n_a), pipeline_mode=pl.Buffered(1)),
                  const(gq), const(gkv), const(wq), const(wkv),
                  pl.BlockSpec((tm, 1), lambda i: (i, 0)), pl.BlockSpec((1, LANES), lambda i: (0, 0))],
        out_specs=[pl.BlockSpec((tm, d), lambda i: (i, 0)),
                   head_major(MLA_QK_PAD), head_major(MLA_QK_PAD), head_major(2 * V_HEAD_DIM)],
        out_shape=[jax.ShapeDtypeStruct((n, d), BF16),
                   jax.ShapeDtypeStruct((batch, MLA_HEADS, seq, MLA_QK_PAD), BF16),
                   jax.ShapeDtypeStruct((batch, MLA_HEADS, seq, MLA_QK_PAD), BF16),
                   jax.ShapeDtypeStruct((batch, MLA_HEADS, seq, 2 * V_HEAD_DIM), BF16)],
        compiler_params=_cparams("arbitrary"),
        name="front_mla_prep",
    )(x2, g, mod3, mod3, w_full, gq, gkv, wq, wkv, pos_col, freq)


def _mm_kernel(h_ref, w_ref, o_ref, *, act):
    y = jnp.dot(h_ref[...], w_ref[...], preferred_element_type=F32)
    if act == "sigmoid":
        y = _sigmoid(y)
    o_ref[...] = y.astype(o_ref.dtype)


def _mm(h, w, col0, cols, act=None, tm=1024, tn=1024):
    n, k = h.shape
    tn = min(tn, cols)
    j0 = col0 // tn
    return pl.pallas_call(
        functools.partial(_mm_kernel, act=act),
        grid=(cols // tn, n // tm),
        in_specs=[pl.BlockSpec((tm, k), lambda j, i: (i, 0)),
                  pl.BlockSpec((k, tn), lambda j, i: (0, j0 + j))],
        out_specs=pl.BlockSpec((tm, tn), lambda j, i: (i, j)),
        out_shape=jax.ShapeDtypeStruct((n, cols), BF16),
        compiler_params=_cparams("arbitrary", "arbitrary"),
        name="in_proj_" + (act or "plain"),
    )(h, w)


def _dilproj_kernel(h_ref, wq_ref, wk_ref, wv_ref, *rest, dilation, make_tables):
    hb = h_ref[...]
    y = jnp.concatenate([jnp.dot(hb, w_ref[...], preferred_element_type=F32) for w_ref in (wq_ref, wk_ref, wv_ref)],
                        axis=1)
    if make_tables:
        pos_ref, freq_ref, o_ref, c_out, sf_out, sb_out, y_sc = rest
        c_tab, s_fwd, s_bwd = _rope_tables(pos_ref[...], freq_ref[...], DIL_ROT_DIM, passthrough=True)
        c_out[...], sf_out[...], sb_out[...] = c_tab, s_fwd, s_bwd
    else:
        c_ref, sf_ref, sb_ref, o_ref, y_sc = rest
        c_tab, s_fwd, s_bwd = c_ref[...], sf_ref[...], sb_ref[...]
    cols = 3 * DIL_GROUP_COLS
    rows = h_ref.shape[0] // dilation
    n_rot = 2 * DIL_HEADS_PER_GROUP
    for hh in range(3 * DIL_HEADS_PER_GROUP):
        t = y[:, hh * LANES:(hh + 1) * LANES]
        if hh < n_rot:
            t = _rope_lanes(t, c_tab, s_fwd, s_bwd, DIL_ROT_DIM // 2)
        if dilation == 1:
            o_ref[0, :, hh * LANES:(hh + 1) * LANES] = t.astype(BF16)
        else:
            y_sc[hh] = t
    if dilation > 1:
        for r in range(dilation):
            for hh in range(3 * DIL_HEADS_PER_GROUP):
                c0 = r * cols + hh * LANES
                o_ref[0, :, c0:c0 + LANES] = y_sc.at[hh][pl.ds(r, rows, stride=dilation), :].astype(BF16)


def _dilproj(h, w, col0, group, rope_in, batch, seq, dilation, tm=1024):
    n, k = h.shape
    gc = DIL_GROUP_COLS
    cols = 3 * gc
    per_b = seq // tm
    make_tables = len(rope_in) == 2
    tab = pl.BlockSpec((tm, LANES), lambda i: (i, 0))
    part = lambda which: pl.BlockSpec((k, gc), lambda i: (0, col0 // gc + which * DIL_GROUPS + group))
    rope_specs = ([pl.BlockSpec((tm, 1), lambda i: (i, 0)), pl.BlockSpec((1, LANES), lambda i: (0, 0))]
                  if make_tables else [tab, tab, tab])
    out_specs = [pl.BlockSpec((1, tm // dilation, dilation * cols), lambda i: (i // per_b, i % per_b, 0))]
    out_shape = [jax.ShapeDtypeStruct((batch, seq // dilation, dilation * cols), BF16)]
    if make_tables:
        out_specs += [tab, tab, tab]
        out_shape += [jax.ShapeDtypeStruct((n, LANES), F32)] * 3
    out = pl.pallas_call(
        functools.partial(_dilproj_kernel, dilation=dilation, make_tables=make_tables),
        grid=(n // tm,),
        in_specs=[pl.BlockSpec((tm, k), lambda i: (i, 0)), part(0), part(1), part(2)] + rope_specs,
        out_specs=out_specs,
        out_shape=out_shape,
        scratch_shapes=[pltpu.VMEM((cols // LANES, tm, LANES), F32)],
        compiler_params=_cparams("arbitrary"),
        name="dil_proj",
    )(h, w, w, w, *rope_in)
    return out[0], tuple(out[1:]) if make_tables else rope_in


def _mla_attn_kernel(q_ref, k_ref, v_ref, o_ref, *, tq, nh):
    i = pl.program_id(2)
    qs = [q_ref[0, hh] for hh in range(nh)]

    def step(c, carry, masked):
        base = pl.multiple_of(c * tq, tq)
        ss = []
        for hh in range(nh):
            k = k_ref[0, hh, pl.ds(base, tq), :]
            s = lax.dot_general(qs[hh], k, (((1,), (1,)), ((), ())), preferred_element_type=F32)
            if masked:
                row = lax.broadcasted_iota(jnp.int32, (tq, tq), 0)
                col = lax.broadcasted_iota(jnp.int32, (tq, tq), 1)
                s = jnp.where(col <= row, s, NEG_BIG)
            ss.append(s)
        out = []
        for hh in range(nh):
            m, l, acc = carry[hh]
            v = v_ref[0, hh, pl.ds(base, tq), :]
            m_new = jnp.maximum(m, jnp.max(ss[hh], axis=-1, keepdims=True))
            alpha = jnp.exp2(m - m_new)
            pv = jnp.dot(jnp.exp2((ss[hh] - m_new).astype(BF16)), v, preferred_element_type=F32)
            out.append((m_new, alpha * l + pv[:, V_HEAD_DIM:V_HEAD_DIM + 1], alpha * acc + pv[:, :V_HEAD_DIM]))
        return tuple(out)

    init = tuple((jnp.full((tq, 1), NEG_BIG, F32), jnp.zeros((tq, 1), F32), jnp.zeros((tq, V_HEAD_DIM), F32))
                 for _ in range(nh))
    carry = lax.fori_loop(0, i, lambda c, cr: step(c, cr, False), init)
    carry = step(i, carry, True)
    for hh in range(nh):
        _, l, acc = carry[hh]
        o_ref[0, :, hh * V_HEAD_DIM:(hh + 1) * V_HEAD_DIM] = (acc / l).astype(BF16)


def _mla_attn(q, k, v, tq=512, nh=4):
    b, h, s, dk = q.shape
    dv = v.shape[-1]
    return pl.pallas_call(
        functools.partial(_mla_attn_kernel, tq=tq, nh=nh),
        grid=(b, h // nh, s // tq),
        in_specs=[pl.BlockSpec((1, nh, tq, dk), lambda bi, hi, i: (bi, hi, i, 0)),
                  pl.BlockSpec((1, nh, s, dk), lambda bi, hi, i: (bi, hi, 0, 0)),
                  pl.BlockSpec((1, nh, s, dv), lambda bi, hi, i: (bi, hi, 0, 0))],
        out_specs=pl.BlockSpec((1, tq, nh * V_HEAD_DIM), lambda bi, hi, i: (bi, i, hi)),
        out_shape=jax.ShapeDtypeStruct((b, s, h * V_HEAD_DIM), BF16),
        compiler_params=_cparams("arbitrary", "arbitrary", "arbitrary"),
        name="mla_attn",
    )(q, k, v)


def _dil_attn_kernel(q_ref, kc_ref, kp_ref, vc_ref, vp_ref, o_ref, lse_ref, *, tq):
    i = pl.program_id(2)
    sub = DIL_SPAN
    row = lax.broadcasted_iota(jnp.int32, (sub, 2 * sub), 0)
    col = lax.broadcasted_iota(jnp.int32, (sub, 2 * sub), 1)
    band = jnp.logical_and(col >= row, col <= row + sub)
    first = jnp.logical_and(band, col >= jnp.where(i > 0, 0, sub))
    lane = lax.broadcasted_iota(jnp.int32, (sub, LANES), 1)
    scale = DIL_HEAD_DIM ** -0.5
    dn = (((1,), (1,)), ((), ()))
    chains = [(j, hh) for j in range(tq // sub) for hh in range(DIL_HEADS_PER_GROUP)]

    def window(cur_ref, prev_ref, j, cs):
        if j == 0:
            return jnp.concatenate([prev_ref[0, :, cs], cur_ref[0, :sub, cs]], axis=0)
        return cur_ref[0, (j - 1) * sub:(j + 1) * sub, cs]

    scores = []
    for j, hh in chains:
        cs = slice(hh * LANES, (hh + 1) * LANES)
        s = lax.dot_general(q_ref[0, j * sub:(j + 1) * sub, cs], window(kc_ref, kp_ref, j, cs), dn,
                            preferred_element_type=F32) * scale
        scores.append(jnp.where(first if j == 0 else band, s, NEG_BIG))
    lse_blk = [jnp.zeros((sub, LANES), F32) for _ in range(tq // sub)]
    for (j, hh), s in zip(chains, scores):
        cs = slice(hh * LANES, (hh + 1) * LANES)
        m = jnp.max(s, axis=-1, keepdims=True)
        p = jnp.exp(s - m)
        l = jnp.sum(p, axis=-1, keepdims=True)
        acc = jnp.dot(p.astype(BF16), window(vc_ref, vp_ref, j, cs), preferred_element_type=F32)
        o_ref[0, j * sub:(j + 1) * sub, cs] = (acc * (1.0 / l)).astype(BF16)
        lse_blk[j] = jnp.where(lane == hh, m + jnp.log(l), lse_blk[j])
    for j in range(tq // sub):
        lse_ref[0, j * sub:(j + 1) * sub, :] = lse_blk[j]


def _dil_attn(t, batch, seq, dilation):
    ln = seq // dilation
    tq = min(ln, 4 * DIL_SPAN)
    gc = DIL_GROUP_COLS
    ratio = tq // DIL_SPAN
    cur = lambda which: pl.BlockSpec((1, tq, gc), lambda b, r, i: (b, i, r * 3 + which))
    prev = lambda which: pl.BlockSpec(
        (1, DIL_SPAN, gc), lambda b, r, i: (b, jnp.maximum(i * ratio - 1, 0), r * 3 + which))
    o, lse = pl.pallas_call(
        functools.partial(_dil_attn_kernel, tq=tq),
        grid=(batch, dilation, ln // tq),
        in_specs=[cur(0), cur(1), prev(1), cur(2), prev(2)],
        out_specs=[pl.BlockSpec((1, tq, gc), lambda b, r, i: (b, i, r)),
                   pl.BlockSpec((1, tq, LANES), lambda b, r, i: (b, i, r))],
        out_shape=[jax.ShapeDtypeStruct((batch, ln, dilation * gc), BF16),
                   jax.ShapeDtypeStruct((batch, ln, dilation * LANES), F32)],
        compiler_params=_cparams("arbitrary", "arbitrary", "arbitrary"),
        name=f"dil_attn_d{dilation}",
    )(t, t, t, t, t)
    return o, lse


def _pack_halves(v):
    w = v.shape[1] // 2
    lo = lax.bitcast_convert_type(v[:, :w].astype(BF16).astype(F32), jnp.uint32)
    hi = lax.bitcast_convert_type(v[:, w:].astype(BF16).astype(F32), jnp.uint32)
    return (lo >> 16) | (hi & jnp.uint32(0xFFFF0000))


ROW_TILE = 8


def _row_tile(p):
    return (pl.ds(pl.multiple_of(p * ROW_TILE, ROW_TILE), ROW_TILE), slice(None))


def _store_row_tiles(ref, pk, base=0):
    rows = pk.shape[0]
    for c in range(ROW_TILE):
        ref[pl.ds(base + c, rows, stride=ROW_TILE), :] = pk[:, c * LANES:(c + 1) * LANES]


def _load_row_tiles(ref, rows, base=0):
    return jnp.concatenate([ref[pl.ds(base + c, rows, stride=ROW_TILE), :] for c in range(ROW_TILE)], axis=1)


def _unpack_halves(pk):
    lo = lax.bitcast_convert_type(pk << 16, F32)
    hi = lax.bitcast_convert_type(pk & jnp.uint32(0xFFFF0000), F32)
    return lo, hi


def _merge_kernel(oa_ref, o0_ref, o1_ref, o2_ref, l0_ref, l1_ref, l2_ref, ga_ref, gb_ref, x_ref,
                  gate_ref, shift_ref, scale_ref, gpost_ref, gpre_ref,
                  wa_ref, wb_ref, wo_ref, wrh_ref, wrl_ref,
                  x1_ref, h2_ref, h2pk_ref, logit_ref, o_sc, l_sc):
    tm = x_ref.shape[0]

    def natural(ref, sc, gi, chunks):
        dil = DIL_PATTERNS[gi][1]
        if dil == 1:
            return [ref[0, :, c * LANES:(c + 1) * LANES].astype(F32) for c in range(chunks)]
        for r in range(dil):
            for c in range(chunks):
                c0 = (r * chunks + c) * LANES
                sc.at[gi, c][pl.ds(r, tm // dil, stride=dil), :] = ref[0, :, c0:c0 + LANES].astype(F32)
        return [sc[gi, c] for c in range(chunks)]

    (l0,), (l1,), (l2,) = [natural(ref, l_sc, gi, 1) for gi, ref in enumerate((l0_ref, l1_ref, l2_ref))]
    o0, o1, o2 = [natural(ref, o_sc, gi, DIL_HEADS_PER_GROUP) for gi, ref in enumerate((o0_ref, o1_ref, o2_ref))]
    m = jnp.maximum(jnp.maximum(l0, l1), l2)
    e0, e1, e2 = jnp.exp(l0 - m), jnp.exp(l1 - m), jnp.exp(l2 - m)
    inv = 1.0 / (e0 + e1 + e2)
    w0, w1, w2 = e0 * inv, e1 * inv, e2 * inv
    parts = []
    for hh in range(DIL_HEADS_PER_GROUP):
        parts.append(w0[:, hh:hh + 1] * o0[hh] + w1[:, hh:hh + 1] * o1[hh] + w2[:, hh:hh + 1] * o2[hh])
    o_dil = jnp.concatenate(parts, axis=1).astype(BF16)
    y_a = jnp.dot(oa_ref[...], wa_ref[...], preferred_element_type=F32)
    y_b = jnp.dot(o_dil, wb_ref[...], preferred_element_type=F32)
    merged = ga_ref[...].astype(F32) * y_a + gb_ref[...].astype(F32) * y_b
    y = jnp.dot(merged.astype(BF16), wo_ref[...], preferred_element_type=F32)
    x1 = x_ref[...] + gate_ref[0] * _rms(y, gpost_ref[...])
    x1_ref[...] = x1
    h2 = _rms(x1, gpre_ref[...]) * (1.0 + scale_ref[0]) + shift_ref[0]
    _store_row_tiles(h2pk_ref, _pack_halves(h2))
    h2_hi = h2.astype(BF16)
    h2_ref[...] = h2_hi
    h2_lo = (h2 - h2_hi.astype(F32)).astype(BF16)
    both = jnp.dot(h2_hi, wrh_ref[...], preferred_element_type=F32)
    logits = both[:, :LANES] + both[:, LANES:] + jnp.dot(h2_lo, wrl_ref[...], preferred_element_type=F32)
    logit_ref[...] = logits.T


def _merge(oa, dil_o, dil_lse, gates, x2, mod3, gpost, gpre, wa, wb, wo, wr_hi, wr_lo, seq, tm=256):
    n, d = x2.shape
    per_b = seq // tm
    row = lambda w: pl.BlockSpec((tm, w), lambda i: (i, 0))
    const = lambda a: pl.BlockSpec(a.shape, lambda i: (0,) * a.ndim, pipeline_mode=pl.Buffered(1))
    modspec = lambda ch: pl.BlockSpec((1, 1, d), lambda i: ((i // per_b) * ADALN_CHUNKS + ch, 0, 0))
    strided = lambda gi, w: pl.BlockSpec((1, tm // DIL_PATTERNS[gi][1], DIL_PATTERNS[gi][1] * w),
                                         lambda i: (i // per_b, i % per_b, 0))
    return pl.pallas_call(
        _merge_kernel,
        grid=(n // tm,),
        in_specs=[row(oa.shape[1]),
                  strided(0, DIL_GROUP_COLS), strided(1, DIL_GROUP_COLS), strided(2, DIL_GROUP_COLS),
                  strided(0, LANES), strided(1, LANES), strided(2, LANES),
                  pl.BlockSpec((tm, d), lambda i: (i, 0)), pl.BlockSpec((tm, d), lambda i: (i, 1)),
                  row(d),
                  modspec(2), modspec(3), modspec(4),
                  const(gpost), const(gpre),
                  const(wa), const(wb), const(wo), const(wr_hi), const(wr_lo)],
        out_specs=[row(d), row(d), pl.BlockSpec((tm * ROW_TILE, LANES), lambda i: (i, 0)),
                   pl.BlockSpec((LANES, tm), lambda i: (0, i))],
        out_shape=[jax.ShapeDtypeStruct((n, d), F32),
                   jax.ShapeDtypeStruct((n, d), BF16),
                   jax.ShapeDtypeStruct((n * ROW_TILE, LANES), jnp.uint32),
                   jax.ShapeDtypeStruct((LANES, n), F32)],
        scratch_shapes=[pltpu.VMEM((DIL_GROUPS, DIL_HEADS_PER_GROUP, tm, LANES), F32),
                        pltpu.VMEM((DIL_GROUPS, 1, tm, LANES), F32)],
        compiler_params=_cparams("arbitrary"),
        name="merge_outproj",
    )(oa, *dil_o, *dil_lse, gates, gates, x2, mod3, mod3, mod3, gpost, gpre, wa, wb, wo, wr_hi, wr_lo)


def _route_kernel(lg_ref, bias_ref, pos_ref, w_ref, meta_ref, blke_ref, cnt_sc, base_sc, rank_sc, score_sc, *, tt):
    ps = pl.program_id(0)
    i = pl.program_id(1)
    per_group = N_EXPERTS // N_EXPERT_GROUPS
    neg_inf = -jnp.inf
    tile = pl.ds(pl.multiple_of(i * tt, tt), tt)

    @pl.when(jnp.logical_and(ps == 0, i == 0))
    def _():
        cnt_sc[...] = jnp.zeros_like(cnt_sc)

    @pl.when(ps == 0)
    def _():
        scores = _sigmoid(lg_ref[...])
        biased = scores + bias_ref[...]
        b3 = biased.reshape(N_EXPERT_GROUPS, per_group, tt)
        mem = lax.broadcasted_iota(jnp.int32, b3.shape, 1)
        m1 = jnp.max(b3, axis=1, keepdims=True)
        first = jnp.min(jnp.where(b3 == m1, mem, per_group), axis=1, keepdims=True)
        m2 = jnp.max(jnp.where(mem == first, neg_inf, b3), axis=1, keepdims=True)
        gs = m1 + m2
        gidx = lax.broadcasted_iota(jnp.int32, gs.shape, 0)
        grank = jnp.zeros(gs.shape, jnp.int32)
        for g2 in range(N_EXPERT_GROUPS):
            r = gs[g2:g2 + 1]
            beats = jnp.logical_or(r > gs, jnp.logical_and(r == gs, g2 < gidx))
            grank = grank + jnp.where(beats, 1, 0)
        sel = jnp.where(grank < TOPK_GROUPS, b3, neg_inf).reshape(N_EXPERTS, tt)
        eidx = lax.broadcasted_iota(jnp.int32, sel.shape, 0)
        erank = jnp.full(sel.shape, TOP_K, jnp.int32)
        remaining = sel
        for kk in range(TOP_K):
            top = jnp.max(remaining, axis=0, keepdims=True)
            first = jnp.min(jnp.where(remaining == top, eidx, N_EXPERTS), axis=0, keepdims=True)
            hit = eidx == first
            erank = jnp.where(hit, kk, erank)
            remaining = jnp.where(hit, neg_inf, remaining)
        rank_sc[:, tile] = erank
        score_sc[:, tile] = scores
        cnt_sc[...] = cnt_sc[...] + jnp.sum(jnp.where(erank < TOP_K, 1.0, 0.0), axis=1,
                                            keepdims=True).astype(jnp.int32)

    @pl.when(jnp.logical_and(ps == 1, i == 0))
    def _():
        cnt = cnt_sc[...]
        pc = ((cnt + (MOE_BLOCK - 1)) // MOE_BLOCK) * MOE_BLOCK
        pcb = jnp.broadcast_to(pc, (N_EXPERTS, LANES))
        eid = lax.broadcasted_iota(jnp.int32, (N_EXPERTS, LANES), 0)
        pends = jnp.zeros((N_EXPERTS, LANES), jnp.int32)
        for e2 in range(N_EXPERTS):
            pends = pends + jnp.where(eid >= e2, pcb[e2:e2 + 1, :], 0)
        pst = pends - pcb
        base_sc[...] = pst[:, 0:1]
        meta_ref[0] = jnp.broadcast_to(cnt, (N_EXPERTS, LANES))
        meta_ref[1] = pst
        meta_ref[2] = pends
        nbl = blke_ref.shape[1]
        blk_start = lax.broadcasted_iota(jnp.int32, (N_EXPERTS, nbl), 1) * MOE_BLOCK
        pend_b = jnp.broadcast_to(pends[:, 0:1], (N_EXPERTS, nbl))
        be = jnp.sum(jnp.where(pend_b <= blk_start, 1, 0), axis=0, keepdims=True)
        blke_ref[...] = jnp.broadcast_to(jnp.minimum(be, N_EXPERTS - 1), blke_ref.shape)

    @pl.when(ps == 1)
    def _():
        erank = rank_sc[:, tile]
        scores = score_sc[:, tile]
        esel = erank < TOP_K
        mask_f = jnp.where(esel, 1.0, 0.0)
        tile_cnt = jnp.sum(mask_f, axis=1, keepdims=True).astype(jnp.int32)
        rr = lax.broadcasted_iota(jnp.int32, (tt, tt), 0)
        cc = lax.broadcasted_iota(jnp.int32, (tt, tt), 1)
        upper = jnp.where(rr < cc, 1.0, 0.0).astype(BF16)
        prefix = jnp.dot(mask_f.astype(BF16), upper, preferred_element_type=F32)
        posd = base_sc[...] + prefix.astype(jnp.int32)
        base_sc[...] = base_sc[...] + tile_cnt
        wsel = jnp.where(esel, scores, 0.0)
        denom = jnp.sum(wsel, axis=0, keepdims=True)
        wn = wsel / (denom + 1e-20) * ROUTED_SCALE
        prow, wrow = [], []
        for kk in range(TOP_K):
            hit = erank == kk
            prow.append(jnp.sum(jnp.where(hit, posd, 0), axis=0, keepdims=True))
            wrow.append(jnp.sum(jnp.where(hit, wn, 0.0), axis=0, keepdims=True))
        pad = pos_ref.shape[0] - TOP_K
        pos_ref[...] = jnp.concatenate(prow + [jnp.zeros((pad, tt), jnp.int32)], axis=0)
        w_ref[...] = jnp.concatenate(wrow + [jnp.zeros((pad, tt), F32)], axis=0)


def _route(logits_t, bias_col, nb, tt=256):
    n = logits_t.shape[1]
    nbl = -(-nb // LANES) * LANES
    return pl.pallas_call(
        functools.partial(_route_kernel, tt=tt),
        grid=(2, n // tt),
        in_specs=[pl.BlockSpec((N_EXPERTS, tt), lambda ps, i: (0, i * (1 - ps))),
                  pl.BlockSpec((N_EXPERTS, 1), lambda ps, i: (0, 0))],
        out_specs=[pl.BlockSpec((8, tt), lambda ps, i: (0, ps * i)),
                   pl.BlockSpec((8, tt), lambda ps, i: (0, ps * i)),
                   pl.BlockSpec((3, N_EXPERTS, LANES), lambda ps, i: (0, 0, 0)),
                   pl.BlockSpec((8, nbl), lambda ps, i: (0, 0))],
        out_shape=[jax.ShapeDtypeStruct((8, n), jnp.int32),
                   jax.ShapeDtypeStruct((8, n), F32),
                   jax.ShapeDtypeStruct((3, N_EXPERTS, LANES), jnp.int32),
                   jax.ShapeDtypeStruct((8, nbl), jnp.int32)],
        scratch_shapes=[pltpu.VMEM((N_EXPERTS, 1), jnp.int32), pltpu.VMEM((N_EXPERTS, 1), jnp.int32),
                        pltpu.VMEM((N_EXPERTS, n), jnp.int32), pltpu.VMEM((N_EXPERTS, n), F32)],
        compiler_params=_cparams("arbitrary", "arbitrary"),
        name="moe_route",
    )(logits_t, bias_col)


def _dispatch_kernel(pos_ref, cnt_ref, pst_ref, h_hbm, hb_ref, wsg_ref, wsu_ref, wsd_ref, xs_hbm, ysh_ref,
                     hbuf, zrow, in_sem, sc_sem, sem, *, tm, n):
    i = pl.program_id(0)
    nt = pl.num_programs(0)
    tile_rows = tm * ROW_TILE

    def fetch(tile, slot):
        return pltpu.make_async_copy(h_hbm.at[pl.ds(pl.multiple_of(tile * tile_rows, tile_rows), tile_rows), :],
                                     hbuf.at[slot], in_sem.at[slot])

    def wait_rows(slot):
        for kk in range(TOP_K):
            pltpu.make_async_copy(hbuf.at[slot], xs_hbm.at[pl.ds(0, tile_rows), :], sc_sem.at[slot]).wait()

    @pl.when(i == 0)
    def _():
        fetch(0, 0).start()
        fetch(1, 1).start()

    slot = i % 3
    fetch(i, slot).wait()
    src = hbuf.at[slot]

    def body(g, carry):
        for u in range(ROW_DMA_UNROLL):
            r = g * ROW_DMA_UNROLL + u
            for kk in range(TOP_K):
                p = pos_ref[kk * n + i * tm + r]
                pltpu.make_async_copy(src.at[_row_tile(r)], xs_hbm.at[_row_tile(p)],
                                      sc_sem.at[slot]).start(priority=kk % 2)
        return carry
    lax.fori_loop(0, tm // ROW_DMA_UNROLL, body, 0)

    hb = hb_ref[...]
    g = jnp.dot(hb, wsg_ref[...], preferred_element_type=F32)
    u = jnp.dot(hb, wsu_ref[...], preferred_element_type=F32)
    ysh_ref[...] = jnp.dot((g * _sigmoid(g) * u).astype(BF16), wsd_ref[...],
                           preferred_element_type=F32).astype(BF16)

    @pl.when(i >= 1)
    def _():
        wait_rows((i + 2) % 3)

    @pl.when(i + 2 < nt)
    def _():
        fetch(i + 2, (i + 2) % 3).start()

    @pl.when(i == nt - 1)
    def _():
        wait_rows(slot)

    @pl.when(i == nt - 1)
    def _():
        zrow[...] = jnp.zeros_like(zrow)

        def per_expert(e, carry):
            cnt = cnt_ref[e]
            first = pst_ref[e] + cnt
            npad = ((cnt + (MOE_BLOCK - 1)) // MOE_BLOCK) * MOE_BLOCK - cnt

            def start(s, c2):
                pltpu.make_async_copy(zrow.at[_row_tile(0)], xs_hbm.at[_row_tile(first + s)], sem.at[1]).start()
                return c2

            def wait(s, c2):
                pltpu.make_async_copy(zrow.at[_row_tile(0)], xs_hbm.at[_row_tile(0)], sem.at[1]).wait()
                return c2
            lax.fori_loop(0, npad, start, 0)
            lax.fori_loop(0, npad, wait, 0)
            return carry
        lax.fori_loop(0, N_EXPERTS, per_expert, 0)

        last = N_EXPERTS - 1
        used = (pst_ref[last] + cnt_ref[last] + (MOE_BLOCK - 1)) // MOE_BLOCK
        blk_rows = MOE_BLOCK * ROW_TILE

        def tail(b, carry):
            cp = pltpu.make_async_copy(zrow, xs_hbm.at[pl.ds(pl.multiple_of(b * blk_rows, blk_rows), blk_rows), :],
                                       sem.at[1])
            cp.start()
            cp.wait()
            return carry
        lax.fori_loop(used, xs_hbm.shape[0] // blk_rows, tail, 0)


def _dispatch(pos_flat, cnt, pst, h2pk, h2, wsg, wsu, wsd, nb, tm=256):
    n, d = h2.shape
    assert n // tm >= 2
    const = lambda a: pl.BlockSpec(a.shape, lambda i, *_: (0,) * a.ndim)
    grid_spec = pltpu.PrefetchScalarGridSpec(
        num_scalar_prefetch=3,
        grid=(n // tm,),
        in_specs=[pl.BlockSpec(memory_space=pl.ANY),
                  pl.BlockSpec((tm, d), lambda i, *_: (i, 0)),
                  const(wsg), const(wsu), const(wsd)],
        out_specs=[pl.BlockSpec(memory_space=pl.ANY), pl.BlockSpec((tm, d), lambda i, *_: (i, 0))],
        scratch_shapes=[pltpu.VMEM((3, tm * ROW_TILE, LANES), jnp.uint32),
                        pltpu.VMEM((MOE_BLOCK * ROW_TILE, LANES), jnp.uint32),
                        pltpu.SemaphoreType.DMA((3,)), pltpu.SemaphoreType.DMA((3,)),
                        pltpu.SemaphoreType.DMA((2,))],
    )
    return pl.pallas_call(
        functools.partial(_dispatch_kernel, tm=tm, n=n),
        grid_spec=grid_spec,
        out_shape=[jax.ShapeDtypeStruct((nb * MOE_BLOCK * ROW_TILE, LANES), jnp.uint32),
                   jax.ShapeDtypeStruct((n, d), BF16)],
        compiler_params=_cparams("arbitrary"),
        name="moe_dispatch",
    )(pos_flat, cnt, pst, h2pk, h2, wsg, wsu, wsd)


def _gmm_kernel(nact_ref, blke_ref, xs_ref, wg_hbm, wu_hbm, wd_hbm, o_ref,
                wgf, wuf, wdf, wgb, wub, wdb, sem):
    nact = nact_ref[0]

    def fetch(e):
        return (pltpu.make_async_copy(wg_hbm.at[e], wgf, sem.at[0]),
                pltpu.make_async_copy(wu_hbm.at[e], wuf, sem.at[1]),
                pltpu.make_async_copy(wd_hbm.at[e], wdf, sem.at[2]))

    @pl.when(pl.program_id(0) == 0)
    def _():
        for cp in fetch(blke_ref[0]):
            cp.start()

    def switch_weights(i):
        e = blke_ref[i]
        changed = jnp.logical_or(i == 0, e != blke_ref[jnp.maximum(i - 1, 0)])

        @pl.when(changed)
        def _():
            for cp in fetch(e):
                cp.wait()
            wgb[...] = wgf[...].astype(BF16)
            wub[...] = wuf[...].astype(BF16)
            wdb[...] = wdf[...].astype(BF16)
            nxt = lax.while_loop(lambda j: jnp.logical_and(j < nact, blke_ref[jnp.minimum(j, nact - 1)] == e),
                                 lambda j: j + 1, i + 1)

            @pl.when(nxt < nact)
            def _():
                for cp in fetch(blke_ref[jnp.minimum(nxt, nact - 1)]):
                    cp.start(priority=1)

    def swiglu(block, nblocks):
        base = block * MOE_BLOCK * ROW_TILE
        lo, hi = _unpack_halves(_load_row_tiles(xs_ref, nblocks * MOE_BLOCK, base))
        lo, hi = lo.astype(BF16), hi.astype(BF16)
        half = lo.shape[1]
        g = (jnp.dot(lo, wgb[:half, :], preferred_element_type=F32)
             + jnp.dot(hi, wgb[half:, :], preferred_element_type=F32))
        u = (jnp.dot(lo, wub[:half, :], preferred_element_type=F32)
             + jnp.dot(hi, wub[half:, :], preferred_element_type=F32))
        a = (g * _sigmoid(g) * u).astype(BF16)
        _store_row_tiles(o_ref, _pack_halves(jnp.dot(a, wdb[...], preferred_element_type=F32)), base)

    for pair in range(GMM_SUB // 2):
        sb0 = 2 * pair
        i0 = pl.program_id(0) * GMM_SUB + sb0
        i1 = i0 + 1
        paired = jnp.logical_and(i1 < nact, blke_ref[jnp.minimum(i0, nact - 1)] == blke_ref[jnp.minimum(i1, nact - 1)])

        @pl.when(i0 < nact)
        def _(i0=i0):
            switch_weights(i0)

        @pl.when(paired)
        def _(sb0=sb0):
            swiglu(sb0, 2)

        @pl.when(jnp.logical_and(jnp.logical_not(paired), i0 < nact))
        def _(sb0=sb0):
            swiglu(sb0, 1)

        @pl.when(jnp.logical_and(jnp.logical_not(paired), i1 < nact))
        def _(i1=i1, sb0=sb0):
            switch_weights(i1)
            swiglu(sb0 + 1, 1)

        for sb, i in ((sb0, i0), (sb0 + 1, i1)):
            @pl.when(i >= nact)
            def _(sb=sb):
                o_ref[pl.ds(sb * MOE_BLOCK * ROW_TILE, MOE_BLOCK * ROW_TILE), :] = jnp.zeros(
                    (MOE_BLOCK * ROW_TILE, LANES), o_ref.dtype)


def _gmm(nact, blk_e, xs, w_gate, w_up, w_down, nb):
    d, f = w_gate.shape[1:]
    assert GMM_SUB % 2 == 0 and nb % GMM_SUB == 0
    rows = GMM_SUB * MOE_BLOCK * ROW_TILE
    blk = lambda i, na: jnp.minimum(i, (na[0] - 1) // GMM_SUB)
    hbm = pl.BlockSpec(memory_space=pl.ANY)
    grid_spec = pltpu.PrefetchScalarGridSpec(
        num_scalar_prefetch=2,
        grid=(nb // GMM_SUB,),
        in_specs=[pl.BlockSpec((rows, LANES), lambda i, na, be: (blk(i, na), 0)), hbm, hbm, hbm],
        out_specs=pl.BlockSpec((rows, LANES), lambda i, na, be: (i, 0)),
        scratch_shapes=[pltpu.VMEM((d, f), F32), pltpu.VMEM((d, f), F32), pltpu.VMEM((f, d), F32),
                        pltpu.VMEM((d, f), BF16), pltpu.VMEM((d, f), BF16), pltpu.VMEM((f, d), BF16),
                        pltpu.SemaphoreType.DMA((3,))],
    )
    return pl.pallas_call(
        _gmm_kernel,
        grid_spec=grid_spec,
        out_shape=jax.ShapeDtypeStruct((nb * MOE_BLOCK * ROW_TILE, LANES), jnp.uint32),
        compiler_params=_cparams("arbitrary"),
        name="moe_experts",
    )(nact, blk_e, xs, w_gate, w_up, w_down)


def _final_kernel(pos_ref, tw_ref, ysh_ref, x1_ref, gate_ref, gpost_ref, yb_hbm, o_ref, rbuf, sem, *, tm, n):
    i = pl.program_id(0)
    slot = i % 2

    def gather(tile, sl):
        def body(g, carry):
            for u in range(ROW_DMA_UNROLL):
                r = g * ROW_DMA_UNROLL + u
                for kk in range(TOP_K):
                    p = pos_ref[kk * n + tile * tm + r]
                    pltpu.make_async_copy(yb_hbm.at[_row_tile(p)], rbuf.at[sl, kk].at[_row_tile(r)],
                                          sem.at[sl]).start(priority=kk % 2)
            return carry
        lax.fori_loop(0, tm // ROW_DMA_UNROLL, body, 0)

    @pl.when(i == 0)
    def _():
        gather(0, 0)

    @pl.when(i + 1 < pl.num_programs(0))
    def _():
        gather(i + 1, 1 - slot)

    for kk in range(TOP_K):
        pltpu.make_async_copy(yb_hbm.at[pl.ds(0, tm * ROW_TILE), :], rbuf.at[slot, kk], sem.at[slot]).wait()
    tw = tw_ref[...]
    half = ROW_TILE * LANES
    r_lo = jnp.zeros((tm, half), F32)
    r_hi = jnp.zeros((tm, half), F32)
    for kk in range(TOP_K):
        lo, hi = _unpack_halves(_load_row_tiles(rbuf.at[slot, kk], tm))
        wk = tw[:, kk:kk + 1]
        r_lo = r_lo + wk * lo
        r_hi = r_hi + wk * hi
    y = ysh_ref[...].astype(F32) + jnp.concatenate([r_lo, r_hi], axis=1)
    o_ref[...] = x1_ref[...] + gate_ref[0] * _rms(y, gpost_ref[...])


def _final(pos_flat, top_w8, ysh, x1, mod3, gpost, yb, seq, tm=256):
    n, d = x1.shape
    per_b = seq // tm
    const = lambda a: pl.BlockSpec(a.shape, lambda i, ps: (0,) * a.ndim)
    row = lambda w: pl.BlockSpec((tm, w), lambda i, ps: (i, 0))
    grid_spec = pltpu.PrefetchScalarGridSpec(
        num_scalar_prefetch=1,
        grid=(n // tm,),
        in_specs=[row(top_w8.shape[1]), row(d), row(d),
                  pl.BlockSpec((1, 1, d), lambda i, ps: ((i // per_b) * ADALN_CHUNKS + 5, 0, 0)),
                  const(gpost),
                  pl.BlockSpec(memory_space=pl.ANY)],
        out_specs=row(d),
        scratch_shapes=[pltpu.VMEM((2, TOP_K, tm * ROW_TILE, LANES), jnp.uint32), pltpu.SemaphoreType.DMA((2,))],
    )
    return pl.pallas_call(
        functools.partial(_final_kernel, tm=tm, n=n),
        grid_spec=grid_spec,
        out_shape=jax.ShapeDtypeStruct((n, d), F32),
        compiler_params=_cparams("arbitrary"),
        name="moe_combine_final",
    )(pos_flat, top_w8, ysh, x1, mod3, gpost, yb)


def _rope_freq_row(dim):
    inv_freq = 1.0 / (ROPE_THETA ** (jnp.arange(0, dim, 2, dtype=F32) / dim))
    return jnp.concatenate([inv_freq, inv_freq, jnp.zeros((LANES - dim,), F32)]).reshape(1, LANES)


def kernel(x, c, positions, w_ada, b_ada, attn_pre_g, w_in, q_a_norm_g, w_q_up, kv_a_norm_g, w_kv_up, w_mla_o, w_dil_o, w_out, attn_post_g, ffn_pre_g, w_router, router_bias, w_exp_gate, w_exp_up, w_exp_down, w_sh_gate, w_sh_up, w_sh_down, ffn_post_g):
    batch, seq, d = x.shape
    n = batch * seq
    depth = w_ada.shape[0]

    pos_col = positions.astype(F32).reshape(n, 1)
    freq_mla, freq_dil = _rope_freq_row(QK_ROPE_DIM), _rope_freq_row(DIL_ROT_DIM)

    x2 = x.reshape(n, d)
    c8 = jnp.pad(c, ((0, 8 - batch), (0, 0)))
    for l in range(depth):
        mod = _ada(c8, w_ada[l], b_ada[l].reshape(1, -1))
        mod3 = mod[:batch].reshape(batch * ADALN_CHUNKS, 1, d)

        wi = w_in[l]
        o_dil = Q_LORA_RANK + KV_LORA_RANK + QK_ROPE_DIM
        o_ga = o_dil + 3 * DIL_HEADS * DIL_HEAD_DIM
        n_gate, n_dil = wi.shape[1] - o_ga, o_ga - o_dil
        n_a = o_dil + LANES - QK_ROPE_DIM
        c_dil, c_a = n_gate, -(-(n_gate + n_dil) // n_a) * n_a
        w_full = _wprep(wi.T, ((0, o_ga, n_gate), (c_dil, o_dil, n_dil), (c_a, 0, o_dil)), c_a + n_a)
        wq3 = w_q_up[l].reshape(Q_LORA_RANK, MLA_HEADS, MLA_QK_DIM)
        wq = jnp.concatenate([wq3, jnp.zeros((Q_LORA_RANK, MLA_HEADS, MLA_QK_PAD - MLA_QK_DIM), F32)],
                             axis=2).reshape(Q_LORA_RANK, MLA_HEADS * MLA_QK_PAD).astype(BF16)
        wkv3 = w_kv_up[l].reshape(KV_LORA_RANK, MLA_HEADS, QK_NOPE_DIM + V_HEAD_DIM)
        wkv = jnp.concatenate([wkv3[:, :, :QK_NOPE_DIM].reshape(KV_LORA_RANK, -1),
                               wkv3[:, :, QK_NOPE_DIM:].reshape(KV_LORA_RANK, -1)], axis=1).astype(BF16)

        h, q, k, v = _front(x2, attn_pre_g[l].reshape(1, d), mod3, w_full, c_a, n_a,
                            q_a_norm_g[l].reshape(1, -1), kv_a_norm_g[l].reshape(1, -1), wq, wkv,
                            pos_col, freq_mla, batch, seq)
        gates = _mm(h, w_full, 0, n_gate, act="sigmoid")
        o_mla = _mla_attn(q, k, v).reshape(n, MLA_HEADS * V_HEAD_DIM)
        dil_o, dil_lse = [], []
        rope_dil = (pos_col, freq_dil)
        for g, (_, dilation) in enumerate(DIL_PATTERNS):
            qkv, rope_dil = _dilproj(h, w_full, c_dil, g, rope_dil, batch, seq, dilation)
            o_g, lse_g = _dil_attn(qkv, batch, seq, dilation)
            dil_o.append(o_g)
            dil_lse.append(lse_g)

        wr = jnp.pad(w_router[l], ((0, 0), (0, LANES - N_EXPERTS)))
        wr_hi = wr.astype(BF16)
        wr_lo = (wr - wr_hi.astype(F32)).astype(BF16)
        x1, h2, h2pk, logits_t = _merge(o_mla, dil_o, dil_lse, gates, x2, mod3,
                                        attn_post_g[l].reshape(1, d), ffn_pre_g[l].reshape(1, d),
                                        w_mla_o[l].astype(BF16), w_dil_o[l].astype(BF16), w_out[l].astype(BF16),
                                        jnp.concatenate([wr_hi, wr_lo], axis=1), wr_hi, seq)

        nb = -(-(n * TOP_K + N_EXPERTS * (MOE_BLOCK - 1)) // MOE_BLOCK)
        pos_t, w_t, meta, blk_e = _route(logits_t, router_bias[l].astype(F32).reshape(N_EXPERTS, 1), nb)
        pos_flat = pos_t.reshape(-1)
        nact = meta[2, N_EXPERTS - 1, :1] // MOE_BLOCK
        xs, ysh = _dispatch(pos_flat, meta[0, :, 0], meta[1, :, 0], h2pk, h2,
                            w_sh_gate[l].astype(BF16), w_sh_up[l].astype(BF16), w_sh_down[l].astype(BF16), nb)
        yb = _gmm(nact, blk_e[0], xs, w_exp_gate[l], w_exp_up[l], w_exp_down[l], nb)
        x2 = _final(pos_flat, w_t.T, ysh, x1, mod3, ffn_post_g[l].reshape(1, d), yb, seq)
    return x2.reshape(batch, seq, d)
```

```python
import functools

import jax
import jax.numpy as jnp
from jax import lax
from jax.experimental import pallas as pl
from jax.experimental.pallas import tpu as pltpu

F32 = jnp.float32
BF16 = jnp.bfloat16

D_MODEL = 2048
NORM_EPS = 1e-6
ROPE_THETA = 500000.0
ADALN_CHUNKS = 6

MLA_HEADS = 8
Q_LORA_RANK = 512
KV_LORA_RANK = 512
QK_NOPE_DIM = 128
QK_ROPE_DIM = 64
V_HEAD_DIM = 128
MLA_QK_DIM = QK_NOPE_DIM + QK_ROPE_DIM
MLA_QK_PAD = 256

DIL_PATTERNS = ((128, 1), (512, 4), (2048, 16))
DIL_GROUPS = len(DIL_PATTERNS)
DIL_HEADS_PER_GROUP = 4
DIL_HEADS = DIL_GROUPS * DIL_HEADS_PER_GROUP
DIL_HEAD_DIM = 128
DIL_ROT_DIM = DIL_HEAD_DIM // 4
DIL_SPAN = 128
DIL_GROUP_COLS = DIL_HEADS_PER_GROUP * DIL_HEAD_DIM

N_EXPERTS = 64
N_EXPERT_GROUPS = 8
TOPK_GROUPS = 4
TOP_K = 6
EXPERT_DIM = 512
SHARED_DIM = 512
ROUTED_SCALE = 2.5
MOE_BLOCK = 256

LANES = 128
NEG_BIG = -1e30
LOG2_E = 1.4426950408889634
ROW_DMA_UNROLL = 4
GMM_SUB = 4
VMEM_LIMIT = 56 * 1024 * 1024


def _cparams(*sem):
    return pltpu.CompilerParams(dimension_semantics=sem, vmem_limit_bytes=VMEM_LIMIT)


def _sigmoid(v):
    return 1.0 / (1.0 + jnp.exp(-v))


def _rms(v, g):
    ms = jnp.mean(v * v, axis=-1, keepdims=True)
    return v * lax.rsqrt(ms + NORM_EPS) * g


def _ada_kernel(c_ref, w_ref, b_ref, o_ref):
    c = c_ref[...]
    a = (c * _sigmoid(c)).astype(BF16)
    o_ref[...] = jnp.dot(a, w_ref[...].astype(BF16), preferred_element_type=F32) + b_ref[...]


def _ada(c8, w_ada, b_ada, tn=1536):
    d, n = w_ada.shape
    return pl.pallas_call(
        _ada_kernel,
        grid=(n // tn,),
        in_specs=[pl.BlockSpec((8, d), lambda j: (0, 0)),
                  pl.BlockSpec((d, tn), lambda j: (0, j)),
                  pl.BlockSpec((1, tn), lambda j: (0, j))],
        out_specs=pl.BlockSpec((8, tn), lambda j: (0, j)),
        out_shape=jax.ShapeDtypeStruct((8, n), F32),
        compiler_params=_cparams("arbitrary"),
        name="ada_mod",
    )(c8, w_ada, b_ada)


def _wprep_source(j, segments):
    src = j * 0
    valid = j * 0
    for dst, s0, width in segments:
        t0, t1 = dst // LANES, (dst + width + LANES - 1) // LANES
        inside = jnp.logical_and(j >= t0, j < t1)
        src = jnp.where(inside, s0 + (j - t0) * LANES, src)
        valid = jnp.where(inside, jnp.minimum(dst + width - j * LANES, LANES), valid)
    return src, valid


def _wprep_kernel(*refs, segments, tiles):
    o_ref = refs[-1]
    for t in range(tiles):
        _, valid = _wprep_source(pl.program_id(0) * tiles + t, segments)
        blk = jnp.concatenate([refs[2 * t][...], refs[2 * t + 1][...]], axis=0).T
        lane = lax.broadcasted_iota(jnp.int32, blk.shape, 1)
        o_ref[:, t * LANES:(t + 1) * LANES] = jnp.where(lane < valid, blk, 0.0).astype(BF16)


def _wprep(w_t, segments, cols, tiles=3):
    _, k = w_t.shape
    half = LANES // 2
    assert all(dst % LANES == 0 and s0 % half == 0 and width % half == 0 for dst, s0, width in segments)
    assert cols % (tiles * LANES) == 0
    part = lambda t, which: pl.BlockSpec(
        (half, k), lambda j: (_wprep_source(j * tiles + t, segments)[0] // half + which, 0))
    return pl.pallas_call(
        functools.partial(_wprep_kernel, segments=segments, tiles=tiles),
        grid=(cols // (tiles * LANES),),
        in_specs=[part(t, which) for t in range(tiles) for which in range(2)],
        out_specs=pl.BlockSpec((k, tiles * LANES), lambda j: (0, j)),
        out_shape=jax.ShapeDtypeStruct((k, cols), BF16),
        compiler_params=_cparams("arbitrary"),
        name="w_in_prep",
    )(*([w_t] * (2 * tiles)))


def _rope_lanes(t, c_tab, s_fwd, s_bwd, half):
    return t * c_tab + pltpu.roll(t, half, 1) * s_fwd + pltpu.roll(t, LANES - half, 1) * s_bwd


def _rope_tables(pos, freq, dim, passthrough):
    ang = pos * freq
    cos, sin = jnp.cos(ang), jnp.sin(ang)
    lane = lax.broadcasted_iota(jnp.int32, ang.shape, 1)
    half = dim // 2
    c_tab = jnp.where(lane < dim, cos, 1.0 if passthrough else 0.0)
    s_fwd = jnp.where(jnp.logical_and(lane >= half, lane < dim), sin, 0.0)
    s_bwd = jnp.where(lane < half, -sin, 0.0)
    return c_tab, s_fwd, s_bwd


def _front_kernel(x_ref, g_ref, scale_ref, shift_ref, wa_ref, gq_ref, gkv_ref, wq_ref, wkv_ref,
                  pos_ref, freq_ref, h_ref, q_ref, k_ref, v_ref):
    h = (_rms(x_ref[...], g_ref[...]) * (1.0 + scale_ref[0]) + shift_ref[0]).astype(BF16)
    h_ref[...] = h
    a = jnp.dot(h, wa_ref[...], preferred_element_type=F32)
    qa = a[:, :Q_LORA_RANK]
    ckv = a[:, Q_LORA_RANK:Q_LORA_RANK + KV_LORA_RANK]
    kr = a[:, Q_LORA_RANK + KV_LORA_RANK:]
    c_tab, s_fwd, s_bwd = _rope_tables(pos_ref[...], freq_ref[...], QK_ROPE_DIM, passthrough=False)
    half = QK_ROPE_DIM // 2
    q = jnp.dot(_rms(qa, gq_ref[...]).astype(BF16), wq_ref[...], preferred_element_type=F32)
    q = q * (MLA_QK_DIM ** -0.5 * LOG2_E)
    kv = jnp.dot(_rms(ckv, gkv_ref[...]).astype(BF16), wkv_ref[...], preferred_element_type=F32)
    k_rot = _rope_lanes(kr, c_tab, s_fwd, s_bwd, half).astype(BF16)
    lane = lax.broadcasted_iota(jnp.int32, (a.shape[0], LANES), 1)
    ones_col = jnp.where(lane == 0, 1.0, 0.0).astype(BF16)
    for hh in range(MLA_HEADS):
        base = hh * MLA_QK_PAD
        q_ref[0, hh, :, :LANES] = q[:, base:base + LANES].astype(BF16)
        q_ref[0, hh, :, LANES:] = _rope_lanes(q[:, base + LANES:base + 2 * LANES],
                                              c_tab, s_fwd, s_bwd, half).astype(BF16)
        k_ref[0, hh, :, :LANES] = kv[:, hh * LANES:(hh + 1) * LANES].astype(BF16)
        k_ref[0, hh, :, LANES:] = k_rot
        v_off = MLA_HEADS * LANES + hh * LANES
        v_ref[0, hh, :, :LANES] = kv[:, v_off:v_off + LANES].astype(BF16)
        v_ref[0, hh, :, LANES:] = ones_col


def _front(x2, g, mod3, w_full, col_a, n_a, gq, gkv, wq, wkv, pos_col, freq, batch, seq, tm=512):
    n, d = x2.shape
    per_b = seq // tm
    const = lambda a: pl.BlockSpec(a.shape, lambda i: (0,) * a.ndim, pipeline_mode=pl.Buffered(1))
    head_major = lambda w: pl.BlockSpec((1, MLA_HEADS, tm, w), lambda i: (i // per_b, 0, i % per_b, 0))
    return pl.pallas_call(
        _front_kernel,
        grid=(n // tm,),
        in_specs=[pl.BlockSpec((tm, d), lambda i: (i, 0)),
                  pl.BlockSpec((1, d), lambda i: (0, 0)),
                  pl.BlockSpec((1, 1, d), lambda i: ((i // per_b) * ADALN_CHUNKS + 1, 0, 0)),
                  pl.BlockSpec((1, 1, d), lambda i: ((i // per_b) * ADALN_CHUNKS + 0, 0, 0)),
                  pl.BlockSpec((d, n_a), lambda i: (0, col_a // n_a), pipeline_mode=pl.Buffered(1)),
                  const(gq), const(gkv), const(wq), const(wkv),
                  pl.BlockSpec((tm, 1), lambda i: (i, 0)), pl.BlockSpec((1, LANES), lambda i: (0, 0))],
        out_specs=[pl.BlockSpec((tm, d), lambda i: (i, 0)),
                   head_major(MLA_QK_PAD), head_major(MLA_QK_PAD), head_major(2 * V_HEAD_DIM)],
        out_shape=[jax.ShapeDtypeStruct((n, d), BF16),
                   jax.ShapeDtypeStruct((batch, MLA_HEADS, seq, MLA_QK_PAD), BF16),
                   jax.ShapeDtypeStruct((batch, MLA_HEADS, seq, MLA_QK_PAD), BF16),
                   jax.ShapeDtypeStruct((batch, MLA_HEADS, seq, 2 * V_HEAD_DIM), BF16)],
        compiler_params=_cparams("arbitrary"),
        name="front_mla_prep",
    )(x2, g, mod3, mod3, w_full, gq, gkv, wq, wkv, pos_col, freq)


def _mm_kernel(h_ref, w_ref, o_ref, *, act):
    y = jnp.dot(h_ref[...], w_ref[...], preferred_element_type=F32)
    if act == "sigmoid":
        y = _sigmoid(y)
    o_ref[...] = y.astype(o_ref.dtype)


def _mm(h, w, col0, cols, act=None, tm=1024, tn=1024):
    n, k = h.shape
    tn = min(tn, cols)
    j0 = col0 // tn
    return pl.pallas_call(
        functools.partial(_mm_kernel, act=act),
        grid=(cols // tn, n // tm),
        in_specs=[pl.BlockSpec((tm, k), lambda j, i: (i, 0)),
                  pl.BlockSpec((k, tn), lambda j, i: (0, j0 + j))],
        out_specs=pl.BlockSpec((tm, tn), lambda j, i: (i, j)),
        out_shape=jax.ShapeDtypeStruct((n, cols), BF16),
        compiler_params=_cparams("arbitrary", "arbitrary"),
        name="in_proj_" + (act or "plain"),
    )(h, w)


def _dilproj_kernel(h_ref, wq_ref, wk_ref, wv_ref, *rest, dilation, make_tables):
    hb = h_ref[...]
    y = jnp.concatenate([jnp.dot(hb, w_ref[...], preferred_element_type=F32) for w_ref in (wq_ref, wk_ref, wv_ref)],
                        axis=1)
    if make_tables:
        pos_ref, freq_ref, o_ref, c_out, sf_out, sb_out, y_sc = rest
        c_tab, s_fwd, s_bwd = _rope_tables(pos_ref[...], freq_ref[...], DIL_ROT_DIM, passthrough=True)
        c_out[...], sf_out[...], sb_out[...] = c_tab, s_fwd, s_bwd
    else:
        c_ref, sf_ref, sb_ref, o_ref, y_sc = rest
        c_tab, s_fwd, s_bwd = c_ref[...], sf_ref[...], sb_ref[...]
    cols = 3 * DIL_GROUP_COLS
    rows = h_ref.shape[0] // dilation
    n_rot = 2 * DIL_HEADS_PER_GROUP
    for hh in range(3 * DIL_HEADS_PER_GROUP):
        t = y[:, hh * LANES:(hh + 1) * LANES]
        if hh < n_rot:
            t = _rope_lanes(t, c_tab, s_fwd, s_bwd, DIL_ROT_DIM // 2)
        if dilation == 1:
            o_ref[0, :, hh * LANES:(hh + 1) * LANES] = t.astype(BF16)
        else:
            y_sc[hh] = t
    if dilation > 1:
        for r in range(dilation):
            for hh in range(3 * DIL_HEADS_PER_GROUP):
                c0 = r * cols + hh * LANES
                o_ref[0, :, c0:c0 + LANES] = y_sc.at[hh][pl.ds(r, rows, stride=dilation), :].astype(BF16)


def _dilproj(h, w, col0, group, rope_in, batch, seq, dilation, tm=1024):
    n, k = h.shape
    gc = DIL_GROUP_COLS
    cols = 3 * gc
    per_b = seq // tm
    make_tables = len(rope_in) == 2
    tab = pl.BlockSpec((tm, LANES), lambda i: (i, 0))
    part = lambda which: pl.BlockSpec((k, gc), lambda i: (0, col0 // gc + which * DIL_GROUPS + group))
    rope_specs = ([pl.BlockSpec((tm, 1), lambda i: (i, 0)), pl.BlockSpec((1, LANES), lambda i: (0, 0))]
                  if make_tables else [tab, tab, tab])
    out_specs = [pl.BlockSpec((1, tm // dilation, dilation * cols), lambda i: (i // per_b, i % per_b, 0))]
    out_shape = [jax.ShapeDtypeStruct((batch, seq // dilation, dilation * cols), BF16)]
    if make_tables:
        out_specs += [tab, tab, tab]
        out_shape += [jax.ShapeDtypeStruct((n, LANES), F32)] * 3
    out = pl.pallas_call(
        functools.partial(_dilproj_kernel, dilation=dilation, make_tables=make_tables),
        grid=(n // tm,),
        in_specs=[pl.BlockSpec((tm, k), lambda i: (i, 0)), part(0), part(1), part(2)] + rope_specs,
        out_specs=out_specs,
        out_shape=out_shape,
        scratch_shapes=[pltpu.VMEM((cols // LANES, tm, LANES), F32)],
        compiler_params=_cparams("arbitrary"),
        name="dil_proj",
    )(h, w, w, w, *rope_in)
    return out[0], tuple(out[1:]) if make_tables else rope_in


def _mla_attn_kernel(q_ref, k_ref, v_ref, o_ref, *, tq, nh):
    i = pl.program_id(2)
    qs = [q_ref[0, hh] for hh in range(nh)]

    def step(c, carry, masked):
        base = pl.multiple_of(c * tq, tq)
        ss = []
        for hh in range(nh):
            k = k_ref[0, hh, pl.ds(base, tq), :]
            s = lax.dot_general(qs[hh], k, (((1,), (1,)), ((), ())), preferred_element_type=F32)
            if masked:
                row = lax.broadcasted_iota(jnp.int32, (tq, tq), 0)
                col = lax.broadcasted_iota(jnp.int32, (tq, tq), 1)
                s = jnp.where(col <= row, s, NEG_BIG)
            ss.append(s)
        out = []
        for hh in range(nh):
            m, l, acc = carry[hh]
            v = v_ref[0, hh, pl.ds(base, tq), :]
            m_new = jnp.maximum(m, jnp.max(ss[hh], axis=-1, keepdims=True))
            alpha = jnp.exp2(m - m_new)
            pv = jnp.dot(jnp.exp2((ss[hh] - m_new).astype(BF16)), v, preferred_element_type=F32)
            out.append((m_new, alpha * l + pv[:, V_HEAD_DIM:V_HEAD_DIM + 1], alpha * acc + pv[:, :V_HEAD_DIM]))
        return tuple(out)

    init = tuple((jnp.full((tq, 1), NEG_BIG, F32), jnp.zeros((tq, 1), F32), jnp.zeros((tq, V_HEAD_DIM), F32))
                 for _ in range(nh))
    carry = lax.fori_loop(0, i, lambda c, cr: step(c, cr, False), init)
    carry = step(i, carry, True)
    for hh in range(nh):
        _, l, acc = carry[hh]
        o_ref[0, :, hh * V_HEAD_DIM:(hh + 1) * V_HEAD_DIM] = (acc / l).astype(BF16)


def _mla_attn(q, k, v, tq=512, nh=4):
    b, h, s, dk = q.shape
    dv = v.shape[-1]
    return pl.pallas_call(
        functools.partial(_mla_attn_kernel, tq=tq, nh=nh),
        grid=(b, h // nh, s // tq),
        in_specs=[pl.BlockSpec((1, nh, tq, dk), lambda bi, hi, i: (bi, hi, i, 0)),
                  pl.BlockSpec((1, nh, s, dk), lambda bi, hi, i: (bi, hi, 0, 0)),
                  pl.BlockSpec((1, nh, s, dv), lambda bi, hi, i: (bi, hi, 0, 0))],
        out_specs=pl.BlockSpec((1, tq, nh * V_HEAD_DIM), lambda bi, hi, i: (bi, i, hi)),
        out_shape=jax.ShapeDtypeStruct((b, s, h * V_HEAD_DIM), BF16),
        compiler_params=_cparams("arbitrary", "arbitrary", "arbitrary"),
        name="mla_attn",
    )(q, k, v)


def _dil_attn_kernel(q_ref, kc_ref, kp_ref, vc_ref, vp_ref, o_ref, lse_ref, *, tq):
    i = pl.program_id(2)
    sub = DIL_SPAN
    row = lax.broadcasted_iota(jnp.int32, (sub, 2 * sub), 0)
    col = lax.broadcasted_iota(jnp.int32, (sub, 2 * sub), 1)
    band = jnp.logical_and(col >= row, col <= row + sub)
    first = jnp.logical_and(band, col >= jnp.where(i > 0, 0, sub))
    lane = lax.broadcasted_iota(jnp.int32, (sub, LANES), 1)
    scale = DIL_HEAD_DIM ** -0.5
    dn = (((1,), (1,)), ((), ()))
    chains = [(j, hh) for j in range(tq // sub) for hh in range(DIL_HEADS_PER_GROUP)]

    def window(cur_ref, prev_ref, j, cs):
        if j == 0:
            return jnp.concatenate([prev_ref[0, :, cs], cur_ref[0, :sub, cs]], axis=0)
        return cur_ref[0, (j - 1) * sub:(j + 1) * sub, cs]

    scores = []
    for j, hh in chains:
        cs = slice(hh * LANES, (hh + 1) * LANES)
        s = lax.dot_general(q_ref[0, j * sub:(j + 1) * sub, cs], window(kc_ref, kp_ref, j, cs), dn,
                            preferred_element_type=F32) * scale
        scores.append(jnp.where(first if j == 0 else band, s, NEG_BIG))
    lse_blk = [jnp.zeros((sub, LANES), F32) for _ in range(tq // sub)]
    for (j, hh), s in zip(chains, scores):
        cs = slice(hh * LANES, (hh + 1) * LANES)
        m = jnp.max(s, axis=-1, keepdims=True)
        p = jnp.exp(s - m)
        l = jnp.sum(p, axis=-1, keepdims=True)
        acc = jnp.dot(p.astype(BF16), window(vc_ref, vp_ref, j, cs), preferred_element_type=F32)
        o_ref[0, j * sub:(j + 1) * sub, cs] = (acc * (1.0 / l)).astype(BF16)
        lse_blk[j] = jnp.where(lane == hh, m + jnp.log(l), lse_blk[j])
    for j in range(tq // sub):
        lse_ref[0, j * sub:(j + 1) * sub, :] = lse_blk[j]


def _dil_attn(t, batch, seq, dilation):
    ln = seq // dilation
    tq = min(ln, 4 * DIL_SPAN)
    gc = DIL_GROUP_COLS
    ratio = tq // DIL_SPAN
    cur = lambda which: pl.BlockSpec((1, tq, gc), lambda b, r, i: (b, i, r * 3 + which))
    prev = lambda which: pl.BlockSpec(
        (1, DIL_SPAN, gc), lambda b, r, i: (b, jnp.maximum(i * ratio - 1, 0), r * 3 + which))
    o, lse = pl.pallas_call(
        functools.partial(_dil_attn_kernel, tq=tq),
        grid=(batch, dilation, ln // tq),
        in_specs=[cur(0), cur(1), prev(1), cur(2), prev(2)],
        out_specs=[pl.BlockSpec((1, tq, gc), lambda b, r, i: (b, i, r)),
                   pl.BlockSpec((1, tq, LANES), lambda b, r, i: (b, i, r))],
        out_shape=[jax.ShapeDtypeStruct((batch, ln, dilation * gc), BF16),
                   jax.ShapeDtypeStruct((batch, ln, dilation * LANES), F32)],
        compiler_params=_cparams("arbitrary", "arbitrary", "arbitrary"),
        name=f"dil_attn_d{dilation}",
    )(t, t, t, t, t)
    return o, lse


def _pack_halves(v):
    w = v.shape[1] // 2
    lo = lax.bitcast_convert_type(v[:, :w].astype(BF16).astype(F32), jnp.uint32)
    hi = lax.bitcast_convert_type(v[:, w:].astype(BF16).astype(F32), jnp.uint32)
    return (lo >> 16) | (hi & jnp.uint32(0xFFFF0000))


ROW_TILE = 8


def _row_tile(p):
    return (pl.ds(pl.multiple_of(p * ROW_TILE, ROW_TILE), ROW_TILE), slice(None))


def _store_row_tiles(ref, pk, base=0):
    rows = pk.shape[0]
    for c in range(ROW_TILE):
        ref[pl.ds(base + c, rows, stride=ROW_TILE), :] = pk[:, c * LANES:(c + 1) * LANES]


def _load_row_tiles(ref, rows, base=0):
    return jnp.concatenate([ref[pl.ds(base + c, rows, stride=ROW_TILE), :] for c in range(ROW_TILE)], axis=1)


def _unpack_halves(pk):
    lo = lax.bitcast_convert_type(pk << 16, F32)
    hi = lax.bitcast_convert_type(pk & jnp.uint32(0xFFFF0000), F32)
    return lo, hi


def _merge_kernel(oa_ref, o0_ref, o1_ref, o2_ref, l0_ref, l1_ref, l2_ref, ga_ref, gb_ref, x_ref,
                  gate_ref, shift_ref, scale_ref, gpost_ref, gpre_ref,
                  wa_ref, wb_ref, wo_ref, wrh_ref, wrl_ref,
                  x1_ref, h2_ref, h2pk_ref, logit_ref, o_sc, l_sc):
    tm = x_ref.shape[0]

    def natural(ref, sc, gi, chunks):
        dil = DIL_PATTERNS[gi][1]
        if dil == 1:
            return [ref[0, :, c * LANES:(c + 1) * LANES].astype(F32) for c in range(chunks)]
        for r in range(dil):
            for c in range(chunks):
                c0 = (r * chunks + c) * LANES
                sc.at[gi, c][pl.ds(r, tm // dil, stride=dil), :] = ref[0, :, c0:c0 + LANES].astype(F32)
        return [sc[gi, c] for c in range(chunks)]

    (l0,), (l1,), (l2,) = [natural(ref, l_sc, gi, 1) for gi, ref in enumerate((l0_ref, l1_ref, l2_ref))]
    o0, o1, o2 = [natural(ref, o_sc, gi, DIL_HEADS_PER_GROUP) for gi, ref in enumerate((o0_ref, o1_ref, o2_ref))]
    m = jnp.maximum(jnp.maximum(l0, l1), l2)
    e0, e1, e2 = jnp.exp(l0 - m), jnp.exp(l1 - m), jnp.exp(l2 - m)
    inv = 1.0 / (e0 + e1 + e2)
    w0, w1, w2 = e0 * inv, e1 * inv, e2 * inv
    parts = []
    for hh in range(DIL_HEADS_PER_GROUP):
        parts.append(w0[:, hh:hh + 1] * o0[hh] + w1[:, hh:hh + 1] * o1[hh] + w2[:, hh:hh + 1] * o2[hh])
    o_dil = jnp.concatenate(parts, axis=1).astype(BF16)
    y_a = jnp.dot(oa_ref[...], wa_ref[...], preferred_element_type=F32)
    y_b = jnp.dot(o_dil, wb_ref[...], preferred_element_type=F32)
    merged = ga_ref[...].astype(F32) * y_a + gb_ref[...].astype(F32) * y_b
    y = jnp.dot(merged.astype(BF16), wo_ref[...], preferred_element_type=F32)
    x1 = x_ref[...] + gate_ref[0] * _rms(y, gpost_ref[...])
    x1_ref[...] = x1
    h2 = _rms(x1, gpre_ref[...]) * (1.0 + scale_ref[0]) + shift_ref[0]
    _store_row_tiles(h2pk_ref, _pack_halves(h2))
    h2_hi = h2.astype(BF16)
    h2_ref[...] = h2_hi
    h2_lo = (h2 - h2_hi.astype(F32)).astype(BF16)
    both = jnp.dot(h2_hi, wrh_ref[...], preferred_element_type=F32)
    logits = both[:, :LANES] + both[:, LANES:] + jnp.dot(h2_lo, wrl_ref[...], preferred_element_type=F32)
    logit_ref[...] = logits.T


def _merge(oa, dil_o, dil_lse, gates, x2, mod3, gpost, gpre, wa, wb, wo, wr_hi, wr_lo, seq, tm=256):
    n, d = x2.shape
    per_b = seq // tm
    row = lambda w: pl.BlockSpec((tm, w), lambda i: (i, 0))
    const = lambda a: pl.BlockSpec(a.shape, lambda i: (0,) * a.ndim, pipeline_mode=pl.Buffered(1))
    modspec = lambda ch: pl.BlockSpec((1, 1, d), lambda i: ((i // per_b) * ADALN_CHUNKS + ch, 0, 0))
    strided = lambda gi, w: pl.BlockSpec((1, tm // DIL_PATTERNS[gi][1], DIL_PATTERNS[gi][1] * w),
                                         lambda i: (i // per_b, i % per_b, 0))
    return pl.pallas_call(
        _merge_kernel,
        grid=(n // tm,),
        in_specs=[row(oa.shape[1]),
                  strided(0, DIL_GROUP_COLS), strided(1, DIL_GROUP_COLS), strided(2, DIL_GROUP_COLS),
                  strided(0, LANES), strided(1, LANES), strided(2, LANES),
                  pl.BlockSpec((tm, d), lambda i: (i, 0)), pl.BlockSpec((tm, d), lambda i: (i, 1)),
                  row(d),
                  modspec(2), modspec(3), modspec(4),
                  const(gpost), const(gpre),
                  const(wa), const(wb), const(wo), const(wr_hi), const(wr_lo)],
        out_specs=[row(d), row(d), pl.BlockSpec((tm * ROW_TILE, LANES), lambda i: (i, 0)),
                   pl.BlockSpec((LANES, tm), lambda i: (0, i))],
        out_shape=[jax.ShapeDtypeStruct((n, d), F32),
                   jax.ShapeDtypeStruct((n, d), BF16),
                   jax.ShapeDtypeStruct((n * ROW_TILE, LANES), jnp.uint32),
                   jax.ShapeDtypeStruct((LANES, n), F32)],
        scratch_shapes=[pltpu.VMEM((DIL_GROUPS, DIL_HEADS_PER_GROUP, tm, LANES), F32),
                        pltpu.VMEM((DIL_GROUPS, 1, tm, LANES), F32)],
        compiler_params=_cparams("arbitrary"),
        name="merge_outproj",
    )(oa, *dil_o, *dil_lse, gates, gates, x2, mod3, mod3, mod3, gpost, gpre, wa, wb, wo, wr_hi, wr_lo)


def _route_kernel(lg_ref, bias_ref, pos_ref, w_ref, meta_ref, blke_ref, cnt_sc, base_sc, rank_sc, score_sc, *, tt):
    ps = pl.program_id(0)
    i = pl.program_id(1)
    per_group = N_EXPERTS // N_EXPERT_GROUPS
    neg_inf = -jnp.inf
    tile = pl.ds(pl.multiple_of(i * tt, tt), tt)

    @pl.when(jnp.logical_and(ps == 0, i == 0))
    def _():
        cnt_sc[...] = jnp.zeros_like(cnt_sc)

    @pl.when(ps == 0)
    def _():
        scores = _sigmoid(lg_ref[...])
        biased = scores + bias_ref[...]
        b3 = biased.reshape(N_EXPERT_GROUPS, per_group, tt)
        mem = lax.broadcasted_iota(jnp.int32, b3.shape, 1)
        m1 = jnp.max(b3, axis=1, keepdims=True)
        first = jnp.min(jnp.where(b3 == m1, mem, per_group), axis=1, keepdims=True)
        m2 = jnp.max(jnp.where(mem == first, neg_inf, b3), axis=1, keepdims=True)
        gs = m1 + m2
        gidx = lax.broadcasted_iota(jnp.int32, gs.shape, 0)
        grank = jnp.zeros(gs.shape, jnp.int32)
        for g2 in range(N_EXPERT_GROUPS):
            r = gs[g2:g2 + 1]
            beats = jnp.logical_or(r > gs, jnp.logical_and(r == gs, g2 < gidx))
            grank = grank + jnp.where(beats, 1, 0)
        sel = jnp.where(grank < TOPK_GROUPS, b3, neg_inf).reshape(N_EXPERTS, tt)
        eidx = lax.broadcasted_iota(jnp.int32, sel.shape, 0)
        erank = jnp.full(sel.shape, TOP_K, jnp.int32)
        remaining = sel
        for kk in range(TOP_K):
            top = jnp.max(remaining, axis=0, keepdims=True)
            first = jnp.min(jnp.where(remaining == top, eidx, N_EXPERTS), axis=0, keepdims=True)
            hit = eidx == first
            erank = jnp.where(hit, kk, erank)
            remaining = jnp.where(hit, neg_inf, remaining)
        rank_sc[:, tile] = erank
        score_sc[:, tile] = scores
        cnt_sc[...] = cnt_sc[...] + jnp.sum(jnp.where(erank < TOP_K, 1.0, 0.0), axis=1,
                                            keepdims=True).astype(jnp.int32)

    @pl.when(jnp.logical_and(ps == 1, i == 0))
    def _():
        cnt = cnt_sc[...]
        pc = ((cnt + (MOE_BLOCK - 1)) // MOE_BLOCK) * MOE_BLOCK
        pcb = jnp.broadcast_to(pc, (N_EXPERTS, LANES))
        eid = lax.broadcasted_iota(jnp.int32, (N_EXPERTS, LANES), 0)
        pends = jnp.zeros((N_EXPERTS, LANES), jnp.int32)
        for e2 in range(N_EXPERTS):
            pends = pends + jnp.where(eid >= e2, pcb[e2:e2 + 1, :], 0)
        pst = pends - pcb
        base_sc[...] = pst[:, 0:1]
        meta_ref[0] = jnp.broadcast_to(cnt, (N_EXPERTS, LANES))
        meta_ref[1] = pst
        meta_ref[2] = pends
        nbl = blke_ref.shape[1]
        blk_start = lax.broadcasted_iota(jnp.int32, (N_EXPERTS, nbl), 1) * MOE_BLOCK
        pend_b = jnp.broadcast_to(pends[:, 0:1], (N_EXPERTS, nbl))
        be = jnp.sum(jnp.where(pend_b <= blk_start, 1, 0), axis=0, keepdims=True)
        blke_ref[...] = jnp.broadcast_to(jnp.minimum(be, N_EXPERTS - 1), blke_ref.shape)

    @pl.when(ps == 1)
    def _():
        erank = rank_sc[:, tile]
        scores = score_sc[:, tile]
        esel = erank < TOP_K
        mask_f = jnp.where(esel, 1.0, 0.0)
        tile_cnt = jnp.sum(mask_f, axis=1, keepdims=True).astype(jnp.int32)
        rr = lax.broadcasted_iota(jnp.int32, (tt, tt), 0)
        cc = lax.broadcasted_iota(jnp.int32, (tt, tt), 1)
        upper = jnp.where(rr < cc, 1.0, 0.0).astype(BF16)
        prefix = jnp.dot(mask_f.astype(BF16), upper, preferred_element_type=F32)
        posd = base_sc[...] + prefix.astype(jnp.int32)
        base_sc[...] = base_sc[...] + tile_cnt
        wsel = jnp.where(esel, scores, 0.0)
        denom = jnp.sum(wsel, axis=0, keepdims=True)
        wn = wsel / (denom + 1e-20) * ROUTED_SCALE
        prow, wrow = [], []
        for kk in range(TOP_K):
            hit = erank == kk
            prow.append(jnp.sum(jnp.where(hit, posd, 0), axis=0, keepdims=True))
            wrow.append(jnp.sum(jnp.where(hit, wn, 0.0), axis=0, keepdims=True))
        pad = pos_ref.shape[0] - TOP_K
        pos_ref[...] = jnp.concatenate(prow + [jnp.zeros((pad, tt), jnp.int32)], axis=0)
        w_ref[...] = jnp.concatenate(wrow + [jnp.zeros((pad, tt), F32)], axis=0)


def _route(logits_t, bias_col, nb, tt=512):
    n = logits_t.shape[1]
    nbl = -(-nb // LANES) * LANES
    return pl.pallas_call(
        functools.partial(_route_kernel, tt=tt),
        grid=(2, n // tt),
        in_specs=[pl.BlockSpec((N_EXPERTS, tt), lambda ps, i: (0, i * (1 - ps))),
                  pl.BlockSpec((N_EXPERTS, 1), lambda ps, i: (0, 0))],
        out_specs=[pl.BlockSpec((8, tt), lambda ps, i: (0, ps * i)),
                   pl.BlockSpec((8, tt), lambda ps, i: (0, ps * i)),
                   pl.BlockSpec((3, N_EXPERTS, LANES), lambda ps, i: (0, 0, 0)),
                   pl.BlockSpec((8, nbl), lambda ps, i: (0, 0))],
        out_shape=[jax.ShapeDtypeStruct((8, n), jnp.int32),
                   jax.ShapeDtypeStruct((8, n), F32),
                   jax.ShapeDtypeStruct((3, N_EXPERTS, LANES), jnp.int32),
                   jax.ShapeDtypeStruct((8, nbl), jnp.int32)],
        scratch_shapes=[pltpu.VMEM((N_EXPERTS, 1), jnp.int32), pltpu.VMEM((N_EXPERTS, 1), jnp.int32),
                        pltpu.VMEM((N_EXPERTS, n), jnp.int32), pltpu.VMEM((N_EXPERTS, n), F32)],
        compiler_params=_cparams("arbitrary", "arbitrary"),
        name="moe_route",
    )(logits_t, bias_col)


def _dispatch_kernel(pos_ref, cnt_ref, pst_ref, h_hbm, hb_ref, wsg_ref, wsu_ref, wsd_ref, xs_hbm, ysh_ref,
                     hbuf, zrow, in_sem, sc_sem, sem, *, tm, n):
    i = pl.program_id(0)
    nt = pl.num_programs(0)
    tile_rows = tm * ROW_TILE

    def fetch(tile, slot):
        return pltpu.make_async_copy(h_hbm.at[pl.ds(pl.multiple_of(tile * tile_rows, tile_rows), tile_rows), :],
                                     hbuf.at[slot], in_sem.at[slot])

    def wait_rows(slot):
        for kk in range(TOP_K):
            pltpu.make_async_copy(hbuf.at[slot], xs_hbm.at[pl.ds(0, tile_rows), :], sc_sem.at[slot]).wait()

    @pl.when(i == 0)
    def _():
        fetch(0, 0).start()
        fetch(1, 1).start()

    slot = i % 3
    fetch(i, slot).wait()
    src = hbuf.at[slot]

    def body(g, carry):
        for u in range(ROW_DMA_UNROLL):
            r = g * ROW_DMA_UNROLL + u
            for kk in range(TOP_K):
                p = pos_ref[kk * n + i * tm + r]
                pltpu.make_async_copy(src.at[_row_tile(r)], xs_hbm.at[_row_tile(p)],
                                      sc_sem.at[slot]).start(priority=kk % 2)
        return carry
    lax.fori_loop(0, tm // ROW_DMA_UNROLL, body, 0)

    hb = hb_ref[...]
    g = jnp.dot(hb, wsg_ref[...], preferred_element_type=F32)
    u = jnp.dot(hb, wsu_ref[...], preferred_element_type=F32)
    ysh_ref[...] = jnp.dot((g * _sigmoid(g) * u).astype(BF16), wsd_ref[...],
                           preferred_element_type=F32).astype(BF16)

    @pl.when(i >= 1)
    def _():
        wait_rows((i + 2) % 3)

    @pl.when(i + 2 < nt)
    def _():
        fetch(i + 2, (i + 2) % 3).start()

    @pl.when(i == nt - 1)
    def _():
        wait_rows(slot)

    @pl.when(i == nt - 1)
    def _():
        zrow[...] = jnp.zeros_like(zrow)

        def per_expert(e, carry):
            cnt = cnt_ref[e]
            first = pst_ref[e] + cnt
            npad = ((cnt + (MOE_BLOCK - 1)) // MOE_BLOCK) * MOE_BLOCK - cnt

            def start(s, c2):
                pltpu.make_async_copy(zrow.at[_row_tile(0)], xs_hbm.at[_row_tile(first + s)], sem.at[1]).start()
                return c2

            def wait(s, c2):
                pltpu.make_async_copy(zrow.at[_row_tile(0)], xs_hbm.at[_row_tile(0)], sem.at[1]).wait()
                return c2
            lax.fori_loop(0, npad, start, 0)
            lax.fori_loop(0, npad, wait, 0)
            return carry
        lax.fori_loop(0, N_EXPERTS, per_expert, 0)

        last = N_EXPERTS - 1
        used = (pst_ref[last] + cnt_ref[last] + (MOE_BLOCK - 1)) // MOE_BLOCK
        blk_rows = MOE_BLOCK * ROW_TILE

        def tail(b, carry):
            cp = pltpu.make_async_copy(zrow, xs_hbm.at[pl.ds(pl.multiple_of(b * blk_rows, blk_rows), blk_rows), :],
                                       sem.at[1])
            cp.start()
            cp.wait()
            return carry
        lax.fori_loop(used, xs_hbm.shape[0] // blk_rows, tail, 0)


def _dispatch(pos_flat, cnt, pst, h2pk, h2, wsg, wsu, wsd, nb, tm=256):
    n, d = h2.shape
    assert n // tm >= 2
    const = lambda a: pl.BlockSpec(a.shape, lambda i, *_: (0,) * a.ndim)
    grid_spec = pltpu.PrefetchScalarGridSpec(
        num_scalar_prefetch=3,
        grid=(n // tm,),
        in_specs=[pl.BlockSpec(memory_space=pl.ANY),
                  pl.BlockSpec((tm, d), lambda i, *_: (i, 0)),
                  const(wsg), const(wsu), const(wsd)],
        out_specs=[pl.BlockSpec(memory_space=pl.ANY), pl.BlockSpec((tm, d), lambda i, *_: (i, 0))],
        scratch_shapes=[pltpu.VMEM((3, tm * ROW_TILE, LANES), jnp.uint32),
                        pltpu.VMEM((MOE_BLOCK * ROW_TILE, LANES), jnp.uint32),
                        pltpu.SemaphoreType.DMA((3,)), pltpu.SemaphoreType.DMA((3,)),
                        pltpu.SemaphoreType.DMA((2,))],
    )
    return pl.pallas_call(
        functools.partial(_dispatch_kernel, tm=tm, n=n),
        grid_spec=grid_spec,
        out_shape=[jax.ShapeDtypeStruct((nb * MOE_BLOCK * ROW_TILE, LANES), jnp.uint32),
                   jax.ShapeDtypeStruct((n, d), BF16)],
        compiler_params=_cparams("arbitrary"),
        name="moe_dispatch",
    )(pos_flat, cnt, pst, h2pk, h2, wsg, wsu, wsd)


def _gmm_kernel(nact_ref, blke_ref, xs_ref, wg_hbm, wu_hbm, wd_hbm, o_ref,
                wgf, wuf, wdf, wgb, wub, wdb, sem):
    nact = nact_ref[0]

    def fetch(e):
        return (pltpu.make_async_copy(wg_hbm.at[e], wgf, sem.at[0]),
                pltpu.make_async_copy(wu_hbm.at[e], wuf, sem.at[1]),
                pltpu.make_async_copy(wd_hbm.at[e], wdf, sem.at[2]))

    @pl.when(pl.program_id(0) == 0)
    def _():
        for cp in fetch(blke_ref[0]):
            cp.start()

    def switch_weights(i):
        e = blke_ref[i]
        changed = jnp.logical_or(i == 0, e != blke_ref[jnp.maximum(i - 1, 0)])

        @pl.when(changed)
        def _():
            for cp in fetch(e):
                cp.wait()
            wgb[...] = wgf[...].astype(BF16)
            wub[...] = wuf[...].astype(BF16)
            wdb[...] = wdf[...].astype(BF16)
            nxt = lax.while_loop(lambda j: jnp.logical_and(j < nact, blke_ref[jnp.minimum(j, nact - 1)] == e),
                                 lambda j: j + 1, i + 1)

            @pl.when(nxt < nact)
            def _():
                for cp in fetch(blke_ref[jnp.minimum(nxt, nact - 1)]):
                    cp.start(priority=1)

    def swiglu(block, nblocks):
        base = block * MOE_BLOCK * ROW_TILE
        lo, hi = _unpack_halves(_load_row_tiles(xs_ref, nblocks * MOE_BLOCK, base))
        lo, hi = lo.astype(BF16), hi.astype(BF16)
        half = lo.shape[1]
        g = (jnp.dot(lo, wgb[:half, :], preferred_element_type=F32)
             + jnp.dot(hi, wgb[half:, :], preferred_element_type=F32))
        u = (jnp.dot(lo, wub[:half, :], preferred_element_type=F32)
             + jnp.dot(hi, wub[half:, :], preferred_element_type=F32))
        a = (g * _sigmoid(g) * u).astype(BF16)
        _store_row_tiles(o_ref, _pack_halves(jnp.dot(a, wdb[...], preferred_element_type=F32)), base)

    for pair in range(GMM_SUB // 2):
        sb0 = 2 * pair
        i0 = pl.program_id(0) * GMM_SUB + sb0
        i1 = i0 + 1
        paired = jnp.logical_and(i1 < nact, blke_ref[jnp.minimum(i0, nact - 1)] == blke_ref[jnp.minimum(i1, nact - 1)])

        @pl.when(i0 < nact)
        def _(i0=i0):
            switch_weights(i0)

        @pl.when(paired)
        def _(sb0=sb0):
            swiglu(sb0, 2)

        @pl.when(jnp.logical_and(jnp.logical_not(paired), i0 < nact))
        def _(sb0=sb0):
            swiglu(sb0, 1)

        @pl.when(jnp.logical_and(jnp.logical_not(paired), i1 < nact))
        def _(i1=i1, sb0=sb0):
            switch_weights(i1)
            swiglu(sb0 + 1, 1)

        for sb, i in ((sb0, i0), (sb0 + 1, i1)):
            @pl.when(i >= nact)
            def _(sb=sb):
                o_ref[pl.ds(sb * MOE_BLOCK * ROW_TILE, MOE_BLOCK * ROW_TILE), :] = jnp.zeros(
                    (MOE_BLOCK * ROW_TILE, LANES), o_ref.dtype)


def _gmm(nact, blk_e, xs, w_gate, w_up, w_down, nb):
    d, f = w_gate.shape[1:]
    assert GMM_SUB % 2 == 0 and nb % GMM_SUB == 0
    rows = GMM_SUB * MOE_BLOCK * ROW_TILE
    blk = lambda i, na: jnp.minimum(i, (na[0] - 1) // GMM_SUB)
    hbm = pl.BlockSpec(memory_space=pl.ANY)
    grid_spec = pltpu.PrefetchScalarGridSpec(
        num_scalar_prefetch=2,
        grid=(nb // GMM_SUB,),
        in_specs=[pl.BlockSpec((rows, LANES), lambda i, na, be: (blk(i, na), 0)), hbm, hbm, hbm],
        out_specs=pl.BlockSpec((rows, LANES), lambda i, na, be: (i, 0)),
        scratch_shapes=[pltpu.VMEM((d, f), F32), pltpu.VMEM((d, f), F32), pltpu.VMEM((f, d), F32),
                        pltpu.VMEM((d, f), BF16), pltpu.VMEM((d, f), BF16), pltpu.VMEM((f, d), BF16),
                        pltpu.SemaphoreType.DMA((3,))],
    )
    return pl.pallas_call(
        _gmm_kernel,
        grid_spec=grid_spec,
        out_shape=jax.ShapeDtypeStruct((nb * MOE_BLOCK * ROW_TILE, LANES), jnp.uint32),
        compiler_params=_cparams("arbitrary"),
        name="moe_experts",
    )(nact, blk_e, xs, w_gate, w_up, w_down)


def _final_kernel(pos_ref, tw_ref, ysh_ref, x1_ref, gate_ref, gpost_ref, yb_hbm, o_ref, rbuf, sem, *, tm, n):
    i = pl.program_id(0)
    slot = i % 2

    def gather(tile, sl):
        def body(g, carry):
            for u in range(ROW_DMA_UNROLL):
                r = g * ROW_DMA_UNROLL + u
                for kk in range(TOP_K):
                    p = pos_ref[kk * n + tile * tm + r]
                    pltpu.make_async_copy(yb_hbm.at[_row_tile(p)], rbuf.at[sl, kk].at[_row_tile(r)],
                                          sem.at[sl]).start(priority=kk % 2)
            return carry
        lax.fori_loop(0, tm // ROW_DMA_UNROLL, body, 0)

    @pl.when(i == 0)
    def _():
        gather(0, 0)

    @pl.when(i + 1 < pl.num_programs(0))
    def _():
        gather(i + 1, 1 - slot)

    for kk in range(TOP_K):
        pltpu.make_async_copy(yb_hbm.at[pl.ds(0, tm * ROW_TILE), :], rbuf.at[slot, kk], sem.at[slot]).wait()
    tw = tw_ref[...]
    half = ROW_TILE * LANES
    r_lo = jnp.zeros((tm, half), F32)
    r_hi = jnp.zeros((tm, half), F32)
    for kk in range(TOP_K):
        lo, hi = _unpack_halves(_load_row_tiles(rbuf.at[slot, kk], tm))
        wk = tw[:, kk:kk + 1]
        r_lo = r_lo + wk * lo
        r_hi = r_hi + wk * hi
    y = ysh_ref[...].astype(F32) + jnp.concatenate([r_lo, r_hi], axis=1)
    o_ref[...] = x1_ref[...] + gate_ref[0] * _rms(y, gpost_ref[...])


def _final(pos_flat, top_w8, ysh, x1, mod3, gpost, yb, seq, tm=256):
    n, d = x1.shape
    per_b = seq // tm
    const = lambda a: pl.BlockSpec(a.shape, lambda i, ps: (0,) * a.ndim)
    row = lambda w: pl.BlockSpec((tm, w), lambda i, ps: (i, 0))
    grid_spec = pltpu.PrefetchScalarGridSpec(
        num_scalar_prefetch=1,
        grid=(n // tm,),
        in_specs=[row(top_w8.shape[1]), row(d), row(d),
                  pl.BlockSpec((1, 1, d), lambda i, ps: ((i // per_b) * ADALN_CHUNKS + 5, 0, 0)),
                  const(gpost),
                  pl.BlockSpec(memory_space=pl.ANY)],
        out_specs=row(d),
        scratch_shapes=[pltpu.VMEM((2, TOP_K, tm * ROW_TILE, LANES), jnp.uint32), pltpu.SemaphoreType.DMA((2,))],
    )
    return pl.pallas_call(
        functools.partial(_final_kernel, tm=tm, n=n),
        grid_spec=grid_spec,
        out_shape=jax.ShapeDtypeStruct((n, d), F32),
        compiler_params=_cparams("arbitrary"),
        name="moe_combine_final",
    )(pos_flat, top_w8, ysh, x1, mod3, gpost, yb)


def _rope_freq_row(dim):
    inv_freq = 1.0 / (ROPE_THETA ** (jnp.arange(0, dim, 2, dtype=F32) / dim))
    return jnp.concatenate([inv_freq, inv_freq, jnp.zeros((LANES - dim,), F32)]).reshape(1, LANES)


def kernel(x, c, positions, w_ada, b_ada, attn_pre_g, w_in, q_a_norm_g, w_q_up, kv_a_norm_g, w_kv_up, w_mla_o, w_dil_o, w_out, attn_post_g, ffn_pre_g, w_router, router_bias, w_exp_gate, w_exp_up, w_exp_down, w_sh_gate, w_sh_up, w_sh_down, ffn_post_g):
    batch, seq, d = x.shape
    n = batch * seq
    depth = w_ada.shape[0]

    pos_col = positions.astype(F32).reshape(n, 1)
    freq_mla, freq_dil = _rope_freq_row(QK_ROPE_DIM), _rope_freq_row(DIL_ROT_DIM)

    x2 = x.reshape(n, d)
    c8 = jnp.pad(c, ((0, 8 - batch), (0, 0)))
    for l in range(depth):
        mod = _ada(c8, w_ada[l], b_ada[l].reshape(1, -1))
        mod3 = mod[:batch].reshape(batch * ADALN_CHUNKS, 1, d)

        wi = w_in[l]
        o_dil = Q_LORA_RANK + KV_LORA_RANK + QK_ROPE_DIM
        o_ga = o_dil + 3 * DIL_HEADS * DIL_HEAD_DIM
        n_gate, n_dil = wi.shape[1] - o_ga, o_ga - o_dil
        n_a = o_dil + LANES - QK_ROPE_DIM
        c_dil, c_a = n_gate, -(-(n_gate + n_dil) // n_a) * n_a
        w_full = _wprep(wi.T, ((0, o_ga, n_gate), (c_dil, o_dil, n_dil), (c_a, 0, o_dil)), c_a + n_a)
        wq3 = w_q_up[l].reshape(Q_LORA_RANK, MLA_HEADS, MLA_QK_DIM)
        wq = jnp.concatenate([wq3, jnp.zeros((Q_LORA_RANK, MLA_HEADS, MLA_QK_PAD - MLA_QK_DIM), F32)],
                             axis=2).reshape(Q_LORA_RANK, MLA_HEADS * MLA_QK_PAD).astype(BF16)
        wkv3 = w_kv_up[l].reshape(KV_LORA_RANK, MLA_HEADS, QK_NOPE_DIM + V_HEAD_DIM)
        wkv = jnp.concatenate([wkv3[:, :, :QK_NOPE_DIM].reshape(KV_LORA_RANK, -1),
                               wkv3[:, :, QK_NOPE_DIM:].reshape(KV_LORA_RANK, -1)], axis=1).astype(BF16)

        h, q, k, v = _front(x2, attn_pre_g[l].reshape(1, d), mod3, w_full, c_a, n_a,
                            q_a_norm_g[l].reshape(1, -1), kv_a_norm_g[l].reshape(1, -1), wq, wkv,
                            pos_col, freq_mla, batch, seq)
        gates = _mm(h, w_full, 0, n_gate, act="sigmoid", tn=2048)
        o_mla = _mla_attn(q, k, v).reshape(n, MLA_HEADS * V_HEAD_DIM)
        dil_o, dil_lse = [], []
        rope_dil = (pos_col, freq_dil)
        for g, (_, dilation) in enumerate(DIL_PATTERNS):
            qkv, rope_dil = _dilproj(h, w_full, c_dil, g, rope_dil, batch, seq, dilation)
            o_g, lse_g = _dil_attn(qkv, batch, seq, dilation)
            dil_o.append(o_g)
            dil_lse.append(lse_g)

        wr = jnp.pad(w_router[l], ((0, 0), (0, LANES - N_EXPERTS)))
        wr_hi = wr.astype(BF16)
        wr_lo = (wr - wr_hi.astype(F32)).astype(BF16)
        x1, h2, h2pk, logits_t = _merge(o_mla, dil_o, dil_lse, gates, x2, mod3,
                                        attn_post_g[l].reshape(1, d), ffn_pre_g[l].reshape(1, d),
                                        w_mla_o[l].astype(BF16), w_dil_o[l].astype(BF16), w_out[l].astype(BF16),
                                        jnp.concatenate([wr_hi, wr_lo], axis=1), wr_hi, seq)

        nb = -(-(n * TOP_K + N_EXPERTS * (MOE_BLOCK - 1)) // MOE_BLOCK)
        pos_t, w_t, meta, blk_e = _route(logits_t, router_bias[l].astype(F32).reshape(N_EXPERTS, 1), nb)
        pos_flat = pos_t.reshape(-1)
        nact = meta[2, N_EXPERTS - 1, :1] // MOE_BLOCK
        xs, ysh = _dispatch(pos_flat, meta[0, :, 0], meta[1, :, 0], h2pk, h2,
                            w_sh_gate[l].astype(BF16), w_sh_up[l].astype(BF16), w_sh_down[l].astype(BF16), nb)
        yb = _gmm(nact, blk_e[0], xs, w_exp_gate[l], w_exp_up[l], w_exp_down[l], nb)
        x2 = _final(pos_flat, w_t.T, ysh, x1, mod3, ffn_post_g[l].reshape(1, d), yb, seq)
    return x2.reshape(batch, seq, d)
```

```python
import functools

import jax
import jax.numpy as jnp
from jax import lax
from jax.experimental import pallas as pl
from jax.experimental.pallas import tpu as pltpu

F32 = jnp.float32
BF16 = jnp.bfloat16

D_MODEL = 2048
NORM_EPS = 1e-6
ROPE_THETA = 500000.0
ADALN_CHUNKS = 6

MLA_HEADS = 8
Q_LORA_RANK = 512
KV_LORA_RANK = 512
QK_NOPE_DIM = 128
QK_ROPE_DIM = 64
V_HEAD_DIM = 128
MLA_QK_DIM = QK_NOPE_DIM + QK_ROPE_DIM
MLA_QK_PAD = 256

DIL_PATTERNS = ((128, 1), (512, 4), (2048, 16))
DIL_GROUPS = len(DIL_PATTERNS)
DIL_HEADS_PER_GROUP = 4
DIL_HEADS = DIL_GROUPS * DIL_HEADS_PER_GROUP
DIL_HEAD_DIM = 128
DIL_ROT_DIM = DIL_HEAD_DIM // 4
DIL_SPAN = 128
DIL_GROUP_COLS = DIL_HEADS_PER_GROUP * DIL_HEAD_DIM

N_EXPERTS = 64
N_EXPERT_GROUPS = 8
TOPK_GROUPS = 4
TOP_K = 6
EXPERT_DIM = 512
SHARED_DIM = 512
ROUTED_SCALE = 2.5
MOE_BLOCK = 256

LANES = 128
NEG_BIG = -1e30
LOG2_E = 1.4426950408889634
ROW_DMA_UNROLL = 4
GMM_SUB = 4
VMEM_LIMIT = 56 * 1024 * 1024


def _cparams(*sem):
    return pltpu.CompilerParams(dimension_semantics=sem, vmem_limit_bytes=VMEM_LIMIT)


def _sigmoid(v):
    return 1.0 / (1.0 + jnp.exp(-v))


def _rms(v, g):
    ms = jnp.mean(v * v, axis=-1, keepdims=True)
    return v * lax.rsqrt(ms + NORM_EPS) * g


def _ada_kernel(c_ref, w_ref, b_ref, o_ref):
    c = c_ref[...]
    a = (c * _sigmoid(c)).astype(BF16)
    o_ref[...] = jnp.dot(a, w_ref[...].astype(BF16), preferred_element_type=F32) + b_ref[...]


def _ada(c8, w_ada, b_ada, tn=1536):
    d, n = w_ada.shape
    return pl.pallas_call(
        _ada_kernel,
        grid=(n // tn,),
        in_specs=[pl.BlockSpec((8, d), lambda j: (0, 0)),
                  pl.BlockSpec((d, tn), lambda j: (0, j)),
                  pl.BlockSpec((1, tn), lambda j: (0, j))],
        out_specs=pl.BlockSpec((8, tn), lambda j: (0, j)),
        out_shape=jax.ShapeDtypeStruct((8, n), F32),
        compiler_params=_cparams("arbitrary"),
        name="ada_mod",
    )(c8, w_ada, b_ada)


def _wprep_source(j, segments):
    src = j * 0
    valid = j * 0
    for dst, s0, width in segments:
        t0, t1 = dst // LANES, (dst + width + LANES - 1) // LANES
        inside = jnp.logical_and(j >= t0, j < t1)
        src = jnp.where(inside, s0 + (j - t0) * LANES, src)
        valid = jnp.where(inside, jnp.minimum(dst + width - j * LANES, LANES), valid)
    return src, valid


def _wprep_kernel(*refs, segments, tiles):
    o_ref = refs[-1]
    for t in range(tiles):
        _, valid = _wprep_source(pl.program_id(0) * tiles + t, segments)
        blk = jnp.concatenate([refs[2 * t][...], refs[2 * t + 1][...]], axis=0).T
        lane = lax.broadcasted_iota(jnp.int32, blk.shape, 1)
        o_ref[:, t * LANES:(t + 1) * LANES] = jnp.where(lane < valid, blk, 0.0).astype(BF16)


def _wprep(w_t, segments, cols, tiles=3):
    _, k = w_t.shape
    half = LANES // 2
    assert all(dst % LANES == 0 and s0 % half == 0 and width % half == 0 for dst, s0, width in segments)
    assert cols % (tiles * LANES) == 0
    part = lambda t, which: pl.BlockSpec(
        (half, k), lambda j: (_wprep_source(j * tiles + t, segments)[0] // half + which, 0))
    return pl.pallas_call(
        functools.partial(_wprep_kernel, segments=segments, tiles=tiles),
        grid=(cols // (tiles * LANES),),
        in_specs=[part(t, which) for t in range(tiles) for which in range(2)],
        out_specs=pl.BlockSpec((k, tiles * LANES), lambda j: (0, j)),
        out_shape=jax.ShapeDtypeStruct((k, cols), BF16),
        compiler_params=_cparams("arbitrary"),
        name="w_in_prep",
    )(*([w_t] * (2 * tiles)))


def _rope_lanes(t, c_tab, s_fwd, s_bwd, half):
    return t * c_tab + pltpu.roll(t, half, 1) * s_fwd + pltpu.roll(t, LANES - half, 1) * s_bwd


def _rope_tables(pos, freq, dim, passthrough):
    rows = pos.shape[0]
    groups = LANES // dim
    sub = rows // groups
    lane = lax.broadcasted_iota(jnp.int32, (sub, LANES), 1)
    packed = jnp.zeros((sub, LANES), F32)
    for gi in range(groups):
        packed = jnp.where(lane // dim == gi, pos[gi * sub:(gi + 1) * sub], packed)
    ang = packed * freq
    cos_p, sin_p = jnp.cos(ang), jnp.sin(ang)
    half = dim // 2
    tabs = []
    for gi in range(groups):
        cos = cos_p if gi == 0 else pltpu.roll(cos_p, LANES - gi * dim, 1)
        sin = sin_p if gi == 0 else pltpu.roll(sin_p, LANES - gi * dim, 1)
        tabs.append((jnp.where(lane < dim, cos, 1.0 if passthrough else 0.0),
                     jnp.where(jnp.logical_and(lane >= half, lane < dim), sin, 0.0),
                     jnp.where(lane < half, -sin, 0.0)))
    return tuple(jnp.concatenate([t[j] for t in tabs], axis=0) for j in range(3))


def _front_kernel(x_ref, g_ref, scale_ref, shift_ref, wa_ref, gq_ref, gkv_ref, wq_ref, wkv_ref,
                  pos_ref, freq_ref, h_ref, q_ref, k_ref, v_ref):
    h = (_rms(x_ref[...], g_ref[...]) * (1.0 + scale_ref[0]) + shift_ref[0]).astype(BF16)
    h_ref[...] = h
    a = jnp.dot(h, wa_ref[...], preferred_element_type=F32)
    qa = a[:, :Q_LORA_RANK]
    ckv = a[:, Q_LORA_RANK:Q_LORA_RANK + KV_LORA_RANK]
    kr = a[:, Q_LORA_RANK + KV_LORA_RANK:]
    c_tab, s_fwd, s_bwd = _rope_tables(pos_ref[...], freq_ref[...], QK_ROPE_DIM, passthrough=False)
    half = QK_ROPE_DIM // 2
    q = jnp.dot(_rms(qa, gq_ref[...]).astype(BF16), wq_ref[...], preferred_element_type=F32)
    q = q * (MLA_QK_DIM ** -0.5 * LOG2_E)
    kv = jnp.dot(_rms(ckv, gkv_ref[...]).astype(BF16), wkv_ref[...], preferred_element_type=F32)
    k_rot = _rope_lanes(kr, c_tab, s_fwd, s_bwd, half).astype(BF16)
    lane = lax.broadcasted_iota(jnp.int32, (a.shape[0], LANES), 1)
    ones_col = jnp.where(lane == 0, 1.0, 0.0).astype(BF16)
    for hh in range(MLA_HEADS):
        base = hh * MLA_QK_PAD
        q_ref[0, hh, :, :LANES] = q[:, base:base + LANES].astype(BF16)
        q_ref[0, hh, :, LANES:] = _rope_lanes(q[:, base + LANES:base + 2 * LANES],
                                              c_tab, s_fwd, s_bwd, half).astype(BF16)
        k_ref[0, hh, :, :LANES] = kv[:, hh * LANES:(hh + 1) * LANES].astype(BF16)
        k_ref[0, hh, :, LANES:] = k_rot
        v_off = MLA_HEADS * LANES + hh * LANES
        v_ref[0, hh, :, :LANES] = kv[:, v_off:v_off + LANES].astype(BF16)
        v_ref[0, hh, :, LANES:] = ones_col


def _front(x2, g, mod3, w_full, col_a, n_a, gq, gkv, wq, wkv, pos_col, freq, batch, seq, tm=512):
    n, d = x2.shape
    per_b = seq // tm
    const = lambda a: pl.BlockSpec(a.shape, lambda i: (0,) * a.ndim, pipeline_mode=pl.Buffered(1))
    head_major = lambda w: pl.BlockSpec((1, MLA_HEADS, tm, w), lambda i: (i // per_b, 0, i % per_b, 0))
    return pl.pallas_call(
        _front_kernel,
        grid=(n // tm,),
        in_specs=[pl.BlockSpec((tm, d), lambda i: (i, 0)),
                  pl.BlockSpec((1, d), lambda i: (0, 0)),
                  pl.BlockSpec((1, 1, d), lambda i: ((i // per_b) * ADALN_CHUNKS + 1, 0, 0)),
                  pl.BlockSpec((1, 1, d), lambda i: ((i // per_b) * ADALN_CHUNKS + 0, 0, 0)),
                  pl.BlockSpec((d, n_a), lambda i: (0, col_a // n_a), pipeline_mode=pl.Buffered(1)),
                  const(gq), const(gkv), const(wq), const(wkv),
                  pl.BlockSpec((tm, 1), lambda i: (i, 0)), pl.BlockSpec((1, LANES), lambda i: (0, 0))],
        out_specs=[pl.BlockSpec((tm, d), lambda i: (i, 0)),
                   head_major(MLA_QK_PAD), head_major(MLA_QK_PAD), head_major(2 * V_HEAD_DIM)],
        out_shape=[jax.ShapeDtypeStruct((n, d), BF16),
                   jax.ShapeDtypeStruct((batch, MLA_HEADS, seq, MLA_QK_PAD), BF16),
                   jax.ShapeDtypeStruct((batch, MLA_HEADS, seq, MLA_QK_PAD), BF16),
                   jax.ShapeDtypeStruct((batch, MLA_HEADS, seq, 2 * V_HEAD_DIM), BF16)],
        compiler_params=_cparams("arbitrary"),
        name="front_mla_prep",
    )(x2, g, mod3, mod3, w_full, gq, gkv, wq, wkv, pos_col, freq)


def _mm_kernel(h_ref, w_ref, o_ref, *, act):
    y = jnp.dot(h_ref[...], w_ref[...], preferred_element_type=F32)
    if act == "sigmoid":
        y = _sigmoid(y)
    o_ref[...] = y.astype(o_ref.dtype)


def _mm(h, w, col0, cols, act=None, tm=1024, tn=1024):
    n, k = h.shape
    tn = min(tn, cols)
    j0 = col0 // tn
    return pl.pallas_call(
        functools.partial(_mm_kernel, act=act),
        grid=(cols // tn, n // tm),
        in_specs=[pl.BlockSpec((tm, k), lambda j, i: (i, 0)),
                  pl.BlockSpec((k, tn), lambda j, i: (0, j0 + j))],
        out_specs=pl.BlockSpec((tm, tn), lambda j, i: (i, j)),
        out_shape=jax.ShapeDtypeStruct((n, cols), BF16),
        compiler_params=_cparams("arbitrary", "arbitrary"),
        name="in_proj_" + (act or "plain"),
    )(h, w)


def _dilproj_kernel(h_ref, wq_ref, wk_ref, wv_ref, *rest, dilation, make_tables):
    hb = h_ref[...]
    y = jnp.concatenate([jnp.dot(hb, w_ref[...], preferred_element_type=F32) for w_ref in (wq_ref, wk_ref, wv_ref)],
                        axis=1)
    if make_tables:
        pos_ref, freq_ref, o_ref, c_out, sf_out, sb_out, y_sc = rest
        c_tab, s_fwd, s_bwd = _rope_tables(pos_ref[...], freq_ref[...], DIL_ROT_DIM, passthrough=True)
        c_out[...], sf_out[...], sb_out[...] = c_tab, s_fwd, s_bwd
    else:
        c_ref, sf_ref, sb_ref, o_ref, y_sc = rest
        c_tab, s_fwd, s_bwd = c_ref[...], sf_ref[...], sb_ref[...]
    cols = 3 * DIL_GROUP_COLS
    rows = h_ref.shape[0] // dilation
    n_rot = 2 * DIL_HEADS_PER_GROUP
    for hh in range(3 * DIL_HEADS_PER_GROUP):
        t = y[:, hh * LANES:(hh + 1) * LANES]
        if hh < n_rot:
            t = _rope_lanes(t, c_tab, s_fwd, s_bwd, DIL_ROT_DIM // 2)
        if dilation == 1:
            o_ref[0, :, hh * LANES:(hh + 1) * LANES] = t.astype(BF16)
        else:
            y_sc[hh] = t
    if dilation > 1:
        for r in range(dilation):
            for hh in range(3 * DIL_HEADS_PER_GROUP):
                c0 = r * cols + hh * LANES
                o_ref[0, :, c0:c0 + LANES] = y_sc.at[hh][pl.ds(r, rows, stride=dilation), :].astype(BF16)


def _dilproj(h, w, col0, group, rope_in, batch, seq, dilation, tm=1024):
    n, k = h.shape
    gc = DIL_GROUP_COLS
    cols = 3 * gc
    per_b = seq // tm
    make_tables = len(rope_in) == 2
    tab = pl.BlockSpec((tm, LANES), lambda i: (i, 0))
    part = lambda which: pl.BlockSpec((k, gc), lambda i: (0, col0 // gc + which * DIL_GROUPS + group))
    rope_specs = ([pl.BlockSpec((tm, 1), lambda i: (i, 0)), pl.BlockSpec((1, LANES), lambda i: (0, 0))]
                  if make_tables else [tab, tab, tab])
    out_specs = [pl.BlockSpec((1, tm // dilation, dilation * cols), lambda i: (i // per_b, i % per_b, 0))]
    out_shape = [jax.ShapeDtypeStruct((batch, seq // dilation, dilation * cols), BF16)]
    if make_tables:
        out_specs += [tab, tab, tab]
        out_shape += [jax.ShapeDtypeStruct((n, LANES), F32)] * 3
    out = pl.pallas_call(
        functools.partial(_dilproj_kernel, dilation=dilation, make_tables=make_tables),
        grid=(n // tm,),
        in_specs=[pl.BlockSpec((tm, k), lambda i: (i, 0)), part(0), part(1), part(2)] + rope_specs,
        out_specs=out_specs,
        out_shape=out_shape,
        scratch_shapes=[pltpu.VMEM((cols // LANES, tm, LANES), F32)],
        compiler_params=_cparams("arbitrary"),
        name="dil_proj",
    )(h, w, w, w, *rope_in)
    return out[0], tuple(out[1:]) if make_tables else rope_in


def _mla_attn_kernel(q_ref, k_ref, v_ref, o_ref, *, tq, nh):
    i = pl.program_id(2)
    qs = [q_ref[0, hh] for hh in range(nh)]

    def step(c, carry, masked):
        base = pl.multiple_of(c * tq, tq)
        ss = []
        for hh in range(nh):
            k = k_ref[0, hh, pl.ds(base, tq), :]
            s = lax.dot_general(qs[hh], k, (((1,), (1,)), ((), ())), preferred_element_type=F32)
            if masked:
                row = lax.broadcasted_iota(jnp.int32, (tq, tq), 0)
                col = lax.broadcasted_iota(jnp.int32, (tq, tq), 1)
                s = jnp.where(col <= row, s, NEG_BIG)
            ss.append(s)
        out = []
        for hh in range(nh):
            m, l, acc = carry[hh]
            v = v_ref[0, hh, pl.ds(base, tq), :]
            m_new = jnp.maximum(m, jnp.max(ss[hh], axis=-1, keepdims=True))
            alpha = jnp.exp2(m - m_new)
            pv = jnp.dot(jnp.exp2((ss[hh] - m_new).astype(BF16)), v, preferred_element_type=F32)
            out.append((m_new, alpha * l + pv[:, V_HEAD_DIM:V_HEAD_DIM + 1], alpha * acc + pv[:, :V_HEAD_DIM]))
        return tuple(out)

    init = tuple((jnp.full((tq, 1), NEG_BIG, F32), jnp.zeros((tq, 1), F32), jnp.zeros((tq, V_HEAD_DIM), F32))
                 for _ in range(nh))
    carry = lax.fori_loop(0, i, lambda c, cr: step(c, cr, False), init)
    carry = step(i, carry, True)
    for hh in range(nh):
        _, l, acc = carry[hh]
        o_ref[0, :, hh * V_HEAD_DIM:(hh + 1) * V_HEAD_DIM] = (acc / l).astype(BF16)


def _mla_attn(q, k, v, tq=512, nh=4):
    b, h, s, dk = q.shape
    dv = v.shape[-1]
    return pl.pallas_call(
        functools.partial(_mla_attn_kernel, tq=tq, nh=nh),
        grid=(b, h // nh, s // tq),
        in_specs=[pl.BlockSpec((1, nh, tq, dk), lambda bi, hi, i: (bi, hi, i, 0)),
                  pl.BlockSpec((1, nh, s, dk), lambda bi, hi, i: (bi, hi, 0, 0)),
                  pl.BlockSpec((1, nh, s, dv), lambda bi, hi, i: (bi, hi, 0, 0))],
        out_specs=pl.BlockSpec((1, tq, nh * V_HEAD_DIM), lambda bi, hi, i: (bi, i, hi)),
        out_shape=jax.ShapeDtypeStruct((b, s, h * V_HEAD_DIM), BF16),
        compiler_params=_cparams("arbitrary", "arbitrary", "arbitrary"),
        name="mla_attn",
    )(q, k, v)


def _dil_attn_kernel(q_ref, kc_ref, kp_ref, vc_ref, vp_ref, o_ref, lse_ref, *, tq):
    i = pl.program_id(2)
    sub = DIL_SPAN
    row = lax.broadcasted_iota(jnp.int32, (sub, 2 * sub), 0)
    col = lax.broadcasted_iota(jnp.int32, (sub, 2 * sub), 1)
    band = jnp.logical_and(col >= row, col <= row + sub)
    first = jnp.logical_and(band, col >= jnp.where(i > 0, 0, sub))
    lane = lax.broadcasted_iota(jnp.int32, (sub, LANES), 1)
    scale = DIL_HEAD_DIM ** -0.5
    dn = (((1,), (1,)), ((), ()))
    chains = [(j, hh) for j in range(tq // sub) for hh in range(DIL_HEADS_PER_GROUP)]

    def window(cur_ref, prev_ref, j, cs):
        if j == 0:
            return jnp.concatenate([prev_ref[0, :, cs], cur_ref[0, :sub, cs]], axis=0)
        return cur_ref[0, (j - 1) * sub:(j + 1) * sub, cs]

    scores = []
    for j, hh in chains:
        cs = slice(hh * LANES, (hh + 1) * LANES)
        s = lax.dot_general(q_ref[0, j * sub:(j + 1) * sub, cs], window(kc_ref, kp_ref, j, cs), dn,
                            preferred_element_type=F32) * scale
        scores.append(jnp.where(first if j == 0 else band, s, NEG_BIG))
    lse_blk = [jnp.zeros((sub, LANES), F32) for _ in range(tq // sub)]
    for (j, hh), s in zip(chains, scores):
        cs = slice(hh * LANES, (hh + 1) * LANES)
        m = jnp.max(s, axis=-1, keepdims=True)
        p = jnp.exp(s - m)
        l = jnp.sum(p, axis=-1, keepdims=True)
        acc = jnp.dot(p.astype(BF16), window(vc_ref, vp_ref, j, cs), preferred_element_type=F32)
        o_ref[0, j * sub:(j + 1) * sub, cs] = (acc * (1.0 / l)).astype(BF16)
        lse_blk[j] = jnp.where(lane == hh, m + jnp.log(l), lse_blk[j])
    for j in range(tq // sub):
        lse_ref[0, j * sub:(j + 1) * sub, :] = lse_blk[j]


def _dil_attn(t, batch, seq, dilation):
    ln = seq // dilation
    tq = min(ln, 4 * DIL_SPAN)
    gc = DIL_GROUP_COLS
    ratio = tq // DIL_SPAN
    cur = lambda which: pl.BlockSpec((1, tq, gc), lambda b, r, i: (b, i, r * 3 + which))
    prev = lambda which: pl.BlockSpec(
        (1, DIL_SPAN, gc), lambda b, r, i: (b, jnp.maximum(i * ratio - 1, 0), r * 3 + which))
    o, lse = pl.pallas_call(
        functools.partial(_dil_attn_kernel, tq=tq),
        grid=(batch, dilation, ln // tq),
        in_specs=[cur(0), cur(1), prev(1), cur(2), prev(2)],
        out_specs=[pl.BlockSpec((1, tq, gc), lambda b, r, i: (b, i, r)),
                   pl.BlockSpec((1, tq, LANES), lambda b, r, i: (b, i, r))],
        out_shape=[jax.ShapeDtypeStruct((batch, ln, dilation * gc), BF16),
                   jax.ShapeDtypeStruct((batch, ln, dilation * LANES), F32)],
        compiler_params=_cparams("arbitrary", "arbitrary", "arbitrary"),
        name=f"dil_attn_d{dilation}",
    )(t, t, t, t, t)
    return o, lse


def _pack_halves(v):
    w = v.shape[1] // 2
    lo = lax.bitcast_convert_type(v[:, :w].astype(BF16).astype(F32), jnp.uint32)
    hi = lax.bitcast_convert_type(v[:, w:].astype(BF16).astype(F32), jnp.uint32)
    return (lo >> 16) | (hi & jnp.uint32(0xFFFF0000))


ROW_TILE = 8


def _row_tile(p):
    return (pl.ds(pl.multiple_of(p * ROW_TILE, ROW_TILE), ROW_TILE), slice(None))


def _store_row_tiles(ref, pk, base=0):
    rows = pk.shape[0]
    for c in range(ROW_TILE):
        ref[pl.ds(base + c, rows, stride=ROW_TILE), :] = pk[:, c * LANES:(c + 1) * LANES]


def _load_row_tiles(ref, rows, base=0):
    return jnp.concatenate([ref[pl.ds(base + c, rows, stride=ROW_TILE), :] for c in range(ROW_TILE)], axis=1)


def _unpack_halves(pk):
    lo = lax.bitcast_convert_type(pk << 16, F32)
    hi = lax.bitcast_convert_type(pk & jnp.uint32(0xFFFF0000), F32)
    return lo, hi


def _merge_kernel(oa_ref, o0_ref, o1_ref, o2_ref, l0_ref, l1_ref, l2_ref, ga_ref, gb_ref, x_ref,
                  gate_ref, shift_ref, scale_ref, gpost_ref, gpre_ref,
                  wa_ref, wb_ref, wo_ref, wrh_ref, wrl_ref,
                  x1_ref, h2_ref, h2pk_ref, logit_ref, o_sc, l_sc):
    tm = x_ref.shape[0]

    def natural(ref, sc, gi, chunks):
        dil = DIL_PATTERNS[gi][1]
        if dil == 1:
            return [ref[0, :, c * LANES:(c + 1) * LANES].astype(F32) for c in range(chunks)]
        for r in range(dil):
            for c in range(chunks):
                c0 = (r * chunks + c) * LANES
                sc.at[gi, c][pl.ds(r, tm // dil, stride=dil), :] = ref[0, :, c0:c0 + LANES].astype(F32)
        return [sc[gi, c] for c in range(chunks)]

    (l0,), (l1,), (l2,) = [natural(ref, l_sc, gi, 1) for gi, ref in enumerate((l0_ref, l1_ref, l2_ref))]
    o0, o1, o2 = [natural(ref, o_sc, gi, DIL_HEADS_PER_GROUP) for gi, ref in enumerate((o0_ref, o1_ref, o2_ref))]
    m = jnp.maximum(jnp.maximum(l0, l1), l2)
    e0, e1, e2 = jnp.exp(l0 - m), jnp.exp(l1 - m), jnp.exp(l2 - m)
    inv = 1.0 / (e0 + e1 + e2)
    w0, w1, w2 = e0 * inv, e1 * inv, e2 * inv
    parts = []
    for hh in range(DIL_HEADS_PER_GROUP):
        parts.append(w0[:, hh:hh + 1] * o0[hh] + w1[:, hh:hh + 1] * o1[hh] + w2[:, hh:hh + 1] * o2[hh])
    o_dil = jnp.concatenate(parts, axis=1).astype(BF16)
    y_a = jnp.dot(oa_ref[...], wa_ref[...], preferred_element_type=F32)
    y_b = jnp.dot(o_dil, wb_ref[...], preferred_element_type=F32)
    merged = ga_ref[...].astype(F32) * y_a + gb_ref[...].astype(F32) * y_b
    y = jnp.dot(merged.astype(BF16), wo_ref[...], preferred_element_type=F32)
    x1 = x_ref[...] + gate_ref[0] * _rms(y, gpost_ref[...])
    x1_ref[...] = x1
    h2 = _rms(x1, gpre_ref[...]) * (1.0 + scale_ref[0]) + shift_ref[0]
    _store_row_tiles(h2pk_ref, _pack_halves(h2))
    h2_hi = h2.astype(BF16)
    h2_ref[...] = h2_hi
    h2_lo = (h2 - h2_hi.astype(F32)).astype(BF16)
    both = jnp.dot(h2_hi, wrh_ref[...], preferred_element_type=F32)
    logits = both[:, :LANES] + both[:, LANES:] + jnp.dot(h2_lo, wrl_ref[...], preferred_element_type=F32)
    logit_ref[...] = logits.T


def _merge(oa, dil_o, dil_lse, gates, x2, mod3, gpost, gpre, wa, wb, wo, wr_hi, wr_lo, seq, tm=256):
    n, d = x2.shape
    per_b = seq // tm
    row = lambda w: pl.BlockSpec((tm, w), lambda i: (i, 0))
    const = lambda a: pl.BlockSpec(a.shape, lambda i: (0,) * a.ndim, pipeline_mode=pl.Buffered(1))
    modspec = lambda ch: pl.BlockSpec((1, 1, d), lambda i: ((i // per_b) * ADALN_CHUNKS + ch, 0, 0))
    strided = lambda gi, w: pl.BlockSpec((1, tm // DIL_PATTERNS[gi][1], DIL_PATTERNS[gi][1] * w),
                                         lambda i: (i // per_b, i % per_b, 0))
    return pl.pallas_call(
        _merge_kernel,
        grid=(n // tm,),
        in_specs=[row(oa.shape[1]),
                  strided(0, DIL_GROUP_COLS), strided(1, DIL_GROUP_COLS), strided(2, DIL_GROUP_COLS),
                  strided(0, LANES), strided(1, LANES), strided(2, LANES),
                  pl.BlockSpec((tm, d), lambda i: (i, 0)), pl.BlockSpec((tm, d), lambda i: (i, 1)),
                  row(d),
                  modspec(2), modspec(3), modspec(4),
                  const(gpost), const(gpre),
                  const(wa), const(wb), const(wo), const(wr_hi), const(wr_lo)],
        out_specs=[row(d), row(d), pl.BlockSpec((tm * ROW_TILE, LANES), lambda i: (i, 0)),
                   pl.BlockSpec((LANES, tm), lambda i: (0, i))],
        out_shape=[jax.ShapeDtypeStruct((n, d), F32),
                   jax.ShapeDtypeStruct((n, d), BF16),
                   jax.ShapeDtypeStruct((n * ROW_TILE, LANES), jnp.uint32),
                   jax.ShapeDtypeStruct((LANES, n), F32)],
        scratch_shapes=[pltpu.VMEM((DIL_GROUPS, DIL_HEADS_PER_GROUP, tm, LANES), F32),
                        pltpu.VMEM((DIL_GROUPS, 1, tm, LANES), F32)],
        compiler_params=_cparams("arbitrary"),
        name="merge_outproj",
    )(oa, *dil_o, *dil_lse, gates, gates, x2, mod3, mod3, mod3, gpost, gpre, wa, wb, wo, wr_hi, wr_lo)


def _route_kernel(lg_ref, bias_ref, pos_ref, w_ref, meta_ref, blke_ref, cnt_sc, base_sc, rank_sc, score_sc, *, tt):
    ps = pl.program_id(0)
    i = pl.program_id(1)
    per_group = N_EXPERTS // N_EXPERT_GROUPS
    neg_inf = -jnp.inf
    tile = pl.ds(pl.multiple_of(i * tt, tt), tt)

    @pl.when(jnp.logical_and(ps == 0, i == 0))
    def _():
        cnt_sc[...] = jnp.zeros_like(cnt_sc)

    @pl.when(ps == 0)
    def _():
        scores = _sigmoid(lg_ref[...])
        biased = scores + bias_ref[...]
        b3 = biased.reshape(N_EXPERT_GROUPS, per_group, tt)
        mem = lax.broadcasted_iota(jnp.int32, b3.shape, 1)
        m1 = jnp.max(b3, axis=1, keepdims=True)
        first = jnp.min(jnp.where(b3 == m1, mem, per_group), axis=1, keepdims=True)
        m2 = jnp.max(jnp.where(mem == first, neg_inf, b3), axis=1, keepdims=True)
        gs = m1 + m2
        gidx = lax.broadcasted_iota(jnp.int32, gs.shape, 0)
        grank = jnp.zeros(gs.shape, jnp.int32)
        for g2 in range(N_EXPERT_GROUPS):
            r = gs[g2:g2 + 1]
            beats = jnp.logical_or(r > gs, jnp.logical_and(r == gs, g2 < gidx))
            grank = grank + jnp.where(beats, 1, 0)
        sel = jnp.where(grank < TOPK_GROUPS, b3, neg_inf).reshape(N_EXPERTS, tt)
        eidx = lax.broadcasted_iota(jnp.int32, sel.shape, 0)
        erank = jnp.full(sel.shape, TOP_K, jnp.int32)
        remaining = sel
        for kk in range(TOP_K):
            top = jnp.max(remaining, axis=0, keepdims=True)
            first = jnp.min(jnp.where(remaining == top, eidx, N_EXPERTS), axis=0, keepdims=True)
            hit = eidx == first
            erank = jnp.where(hit, kk, erank)
            remaining = jnp.where(hit, neg_inf, remaining)
        rank_sc[:, tile] = erank
        score_sc[:, tile] = scores
        cnt_sc[...] = cnt_sc[...] + jnp.sum(jnp.where(erank < TOP_K, 1.0, 0.0), axis=1,
                                            keepdims=True).astype(jnp.int32)

    @pl.when(jnp.logical_and(ps == 1, i == 0))
    def _():
        cnt = cnt_sc[...]
        pc = ((cnt + (MOE_BLOCK - 1)) // MOE_BLOCK) * MOE_BLOCK
        pcb = jnp.broadcast_to(pc, (N_EXPERTS, LANES))
        eid = lax.broadcasted_iota(jnp.int32, (N_EXPERTS, LANES), 0)
        pends = jnp.zeros((N_EXPERTS, LANES), jnp.int32)
        for e2 in range(N_EXPERTS):
            pends = pends + jnp.where(eid >= e2, pcb[e2:e2 + 1, :], 0)
        pst = pends - pcb
        base_sc[...] = pst[:, 0:1]
        meta_ref[0] = jnp.broadcast_to(cnt, (N_EXPERTS, LANES))
        meta_ref[1] = pst
        meta_ref[2] = pends
        nbl = blke_ref.shape[1]
        blk_start = lax.broadcasted_iota(jnp.int32, (N_EXPERTS, nbl), 1) * MOE_BLOCK
        pend_b = jnp.broadcast_to(pends[:, 0:1], (N_EXPERTS, nbl))
        be = jnp.sum(jnp.where(pend_b <= blk_start, 1, 0), axis=0, keepdims=True)
        blke_ref[...] = jnp.broadcast_to(jnp.minimum(be, N_EXPERTS - 1), blke_ref.shape)

    @pl.when(ps == 1)
    def _():
        erank = rank_sc[:, tile]
        scores = score_sc[:, tile]
        esel = erank < TOP_K
        mask_f = jnp.where(esel, 1.0, 0.0)
        tile_cnt = jnp.sum(mask_f, axis=1, keepdims=True).astype(jnp.int32)
        rr = lax.broadcasted_iota(jnp.int32, (tt, tt), 0)
        cc = lax.broadcasted_iota(jnp.int32, (tt, tt), 1)
        upper = jnp.where(rr < cc, 1.0, 0.0).astype(BF16)
        prefix = jnp.dot(mask_f.astype(BF16), upper, preferred_element_type=F32)
        posd = base_sc[...] + prefix.astype(jnp.int32)
        base_sc[...] = base_sc[...] + tile_cnt
        wsel = jnp.where(esel, scores, 0.0)
        denom = jnp.sum(wsel, axis=0, keepdims=True)
        wn = wsel / (denom + 1e-20) * ROUTED_SCALE
        prow, wrow = [], []
        for kk in range(TOP_K):
            hit = erank == kk
            prow.append(jnp.sum(jnp.where(hit, posd, 0), axis=0, keepdims=True))
            wrow.append(jnp.sum(jnp.where(hit, wn, 0.0), axis=0, keepdims=True))
        pad = pos_ref.shape[0] - TOP_K
        pos_ref[...] = jnp.concatenate(prow + [jnp.zeros((pad, tt), jnp.int32)], axis=0)
        w_ref[...] = jnp.concatenate(wrow + [jnp.zeros((pad, tt), F32)], axis=0)


def _route(logits_t, bias_col, nb, tt=512):
    n = logits_t.shape[1]
    nbl = -(-nb // LANES) * LANES
    return pl.pallas_call(
        functools.partial(_route_kernel, tt=tt),
        grid=(2, n // tt),
        in_specs=[pl.BlockSpec((N_EXPERTS, tt), lambda ps, i: (0, i * (1 - ps))),
                  pl.BlockSpec((N_EXPERTS, 1), lambda ps, i: (0, 0))],
        out_specs=[pl.BlockSpec((8, tt), lambda ps, i: (0, ps * i)),
                   pl.BlockSpec((8, tt), lambda ps, i: (0, ps * i)),
                   pl.BlockSpec((3, N_EXPERTS, LANES), lambda ps, i: (0, 0, 0)),
                   pl.BlockSpec((8, nbl), lambda ps, i: (0, 0))],
        out_shape=[jax.ShapeDtypeStruct((8, n), jnp.int32),
                   jax.ShapeDtypeStruct((8, n), F32),
                   jax.ShapeDtypeStruct((3, N_EXPERTS, LANES), jnp.int32),
                   jax.ShapeDtypeStruct((8, nbl), jnp.int32)],
        scratch_shapes=[pltpu.VMEM((N_EXPERTS, 1), jnp.int32), pltpu.VMEM((N_EXPERTS, 1), jnp.int32),
                        pltpu.VMEM((N_EXPERTS, n), jnp.int32), pltpu.VMEM((N_EXPERTS, n), F32)],
        compiler_params=_cparams("arbitrary", "arbitrary"),
        name="moe_route",
    )(logits_t, bias_col)


def _dispatch_kernel(pos_ref, cnt_ref, pst_ref, h_hbm, hb_ref, wsg_ref, wsu_ref, wsd_ref, xs_hbm, ysh_ref,
                     hbuf, zrow, in_sem, sc_sem, sem, *, tm, n):
    i = pl.program_id(0)
    nt = pl.num_programs(0)
    tile_rows = tm * ROW_TILE

    def fetch(tile, slot):
        return pltpu.make_async_copy(h_hbm.at[pl.ds(pl.multiple_of(tile * tile_rows, tile_rows), tile_rows), :],
                                     hbuf.at[slot], in_sem.at[slot])

    def wait_rows(slot):
        for kk in range(TOP_K):
            pltpu.make_async_copy(hbuf.at[slot], xs_hbm.at[pl.ds(0, tile_rows), :], sc_sem.at[slot]).wait()

    @pl.when(i == 0)
    def _():
        fetch(0, 0).start()
        fetch(1, 1).start()

    slot = i % 3
    fetch(i, slot).wait()
    src = hbuf.at[slot]

    def body(g, carry):
        for u in range(ROW_DMA_UNROLL):
            r = g * ROW_DMA_UNROLL + u
            for kk in range(TOP_K):
                p = pos_ref[kk * n + i * tm + r]
                pltpu.make_async_copy(src.at[_row_tile(r)], xs_hbm.at[_row_tile(p)],
                                      sc_sem.at[slot]).start(priority=kk % 2)
        return carry
    lax.fori_loop(0, tm // ROW_DMA_UNROLL, body, 0)

    hb = hb_ref[...]
    g = jnp.dot(hb, wsg_ref[...], preferred_element_type=F32)
    u = jnp.dot(hb, wsu_ref[...], preferred_element_type=F32)
    ysh_ref[...] = jnp.dot((g * _sigmoid(g) * u).astype(BF16), wsd_ref[...],
                           preferred_element_type=F32).astype(BF16)

    @pl.when(i >= 1)
    def _():
        wait_rows((i + 2) % 3)

    @pl.when(i + 2 < nt)
    def _():
        fetch(i + 2, (i + 2) % 3).start()

    @pl.when(i == nt - 1)
    def _():
        wait_rows(slot)

    @pl.when(i == nt - 1)
    def _():
        zrow[...] = jnp.zeros_like(zrow)

        def per_expert(e, carry):
            cnt = cnt_ref[e]
            first = pst_ref[e] + cnt
            npad = ((cnt + (MOE_BLOCK - 1)) // MOE_BLOCK) * MOE_BLOCK - cnt

            def start(s, c2):
                pltpu.make_async_copy(zrow.at[_row_tile(0)], xs_hbm.at[_row_tile(first + s)], sem.at[1]).start()
                return c2

            def wait(s, c2):
                pltpu.make_async_copy(zrow.at[_row_tile(0)], xs_hbm.at[_row_tile(0)], sem.at[1]).wait()
                return c2
            lax.fori_loop(0, npad, start, 0)
            lax.fori_loop(0, npad, wait, 0)
            return carry
        lax.fori_loop(0, N_EXPERTS, per_expert, 0)

        last = N_EXPERTS - 1
        used = (pst_ref[last] + cnt_ref[last] + (MOE_BLOCK - 1)) // MOE_BLOCK
        blk_rows = MOE_BLOCK * ROW_TILE

        def tail(b, carry):
            cp = pltpu.make_async_copy(zrow, xs_hbm.at[pl.ds(pl.multiple_of(b * blk_rows, blk_rows), blk_rows), :],
                                       sem.at[1])
            cp.start()
            cp.wait()
            return carry
        lax.fori_loop(used, xs_hbm.shape[0] // blk_rows, tail, 0)


def _dispatch(pos_flat, cnt, pst, h2pk, h2, wsg, wsu, wsd, nb, tm=256):
    n, d = h2.shape
    assert n // tm >= 2
    const = lambda a: pl.BlockSpec(a.shape, lambda i, *_: (0,) * a.ndim)
    grid_spec = pltpu.PrefetchScalarGridSpec(
        num_scalar_prefetch=3,
        grid=(n // tm,),
        in_specs=[pl.BlockSpec(memory_space=pl.ANY),
                  pl.BlockSpec((tm, d), lambda i, *_: (i, 0)),
                  const(wsg), const(wsu), const(wsd)],
        out_specs=[pl.BlockSpec(memory_space=pl.ANY), pl.BlockSpec((tm, d), lambda i, *_: (i, 0))],
        scratch_shapes=[pltpu.VMEM((3, tm * ROW_TILE, LANES), jnp.uint32),
                        pltpu.VMEM((MOE_BLOCK * ROW_TILE, LANES), jnp.uint32),
                        pltpu.SemaphoreType.DMA((3,)), pltpu.SemaphoreType.DMA((3,)),
                        pltpu.SemaphoreType.DMA((2,))],
    )
    return pl.pallas_call(
        functools.partial(_dispatch_kernel, tm=tm, n=n),
        grid_spec=grid_spec,
        out_shape=[jax.ShapeDtypeStruct((nb * MOE_BLOCK * ROW_TILE, LANES), jnp.uint32),
                   jax.ShapeDtypeStruct((n, d), BF16)],
        compiler_params=_cparams("arbitrary"),
        name="moe_dispatch",
    )(pos_flat, cnt, pst, h2pk, h2, wsg, wsu, wsd)


def _gmm_kernel(nact_ref, blke_ref, xs_ref, wg_hbm, wu_hbm, wd_hbm, o_ref,
                wgf, wuf, wdf, wgb, wub, wdb, sem):
    nact = nact_ref[0]

    def fetch(e):
        return (pltpu.make_async_copy(wg_hbm.at[e], wgf, sem.at[0]),
                pltpu.make_async_copy(wu_hbm.at[e], wuf, sem.at[1]),
                pltpu.make_async_copy(wd_hbm.at[e], wdf, sem.at[2]))

    @pl.when(pl.program_id(0) == 0)
    def _():
        for cp in fetch(blke_ref[0]):
            cp.start()

    def switch_weights(i):
        e = blke_ref[i]
        changed = jnp.logical_or(i == 0, e != blke_ref[jnp.maximum(i - 1, 0)])

        @pl.when(changed)
        def _():
            for cp in fetch(e):
                cp.wait()
            wgb[...] = wgf[...].astype(BF16)
            wub[...] = wuf[...].astype(BF16)
            wdb[...] = wdf[...].astype(BF16)
            nxt = lax.while_loop(lambda j: jnp.logical_and(j < nact, blke_ref[jnp.minimum(j, nact - 1)] == e),
                                 lambda j: j + 1, i + 1)

            @pl.when(nxt < nact)
            def _():
                for cp in fetch(blke_ref[jnp.minimum(nxt, nact - 1)]):
                    cp.start(priority=1)

    def swiglu(block, nblocks):
        base = block * MOE_BLOCK * ROW_TILE
        lo, hi = _unpack_halves(_load_row_tiles(xs_ref, nblocks * MOE_BLOCK, base))
        lo, hi = lo.astype(BF16), hi.astype(BF16)
        half = lo.shape[1]
        g = (jnp.dot(lo, wgb[:half, :], preferred_element_type=F32)
             + jnp.dot(hi, wgb[half:, :], preferred_element_type=F32))
        u = (jnp.dot(lo, wub[:half, :], preferred_element_type=F32)
             + jnp.dot(hi, wub[half:, :], preferred_element_type=F32))
        a = (g * _sigmoid(g) * u).astype(BF16)
        _store_row_tiles(o_ref, _pack_halves(jnp.dot(a, wdb[...], preferred_element_type=F32)), base)

    for pair in range(GMM_SUB // 2):
        sb0 = 2 * pair
        i0 = pl.program_id(0) * GMM_SUB + sb0
        i1 = i0 + 1
        paired = jnp.logical_and(i1 < nact, blke_ref[jnp.minimum(i0, nact - 1)] == blke_ref[jnp.minimum(i1, nact - 1)])

        @pl.when(i0 < nact)
        def _(i0=i0):
            switch_weights(i0)

        @pl.when(paired)
        def _(sb0=sb0):
            swiglu(sb0, 2)

        @pl.when(jnp.logical_and(jnp.logical_not(paired), i0 < nact))
        def _(sb0=sb0):
            swiglu(sb0, 1)

        @pl.when(jnp.logical_and(jnp.logical_not(paired), i1 < nact))
        def _(i1=i1, sb0=sb0):
            switch_weights(i1)
            swiglu(sb0 + 1, 1)

        for sb, i in ((sb0, i0), (sb0 + 1, i1)):
            @pl.when(i >= nact)
            def _(sb=sb):
                o_ref[pl.ds(sb * MOE_BLOCK * ROW_TILE, MOE_BLOCK * ROW_TILE), :] = jnp.zeros(
                    (MOE_BLOCK * ROW_TILE, LANES), o_ref.dtype)


def _gmm(nact, blk_e, xs, w_gate, w_up, w_down, nb):
    d, f = w_gate.shape[1:]
    assert GMM_SUB % 2 == 0 and nb % GMM_SUB == 0
    rows = GMM_SUB * MOE_BLOCK * ROW_TILE
    blk = lambda i, na: jnp.minimum(i, (na[0] - 1) // GMM_SUB)
    hbm = pl.BlockSpec(memory_space=pl.ANY)
    grid_spec = pltpu.PrefetchScalarGridSpec(
        num_scalar_prefetch=2,
        grid=(nb // GMM_SUB,),
        in_specs=[pl.BlockSpec((rows, LANES), lambda i, na, be: (blk(i, na), 0)), hbm, hbm, hbm],
        out_specs=pl.BlockSpec((rows, LANES), lambda i, na, be: (i, 0)),
        scratch_shapes=[pltpu.VMEM((d, f), F32), pltpu.VMEM((d, f), F32), pltpu.VMEM((f, d), F32),
                        pltpu.VMEM((d, f), BF16), pltpu.VMEM((d, f), BF16), pltpu.VMEM((f, d), BF16),
                        pltpu.SemaphoreType.DMA((3,))],
    )
    return pl.pallas_call(
        _gmm_kernel,
        grid_spec=grid_spec,
        out_shape=jax.ShapeDtypeStruct((nb * MOE_BLOCK * ROW_TILE, LANES), jnp.uint32),
        compiler_params=_cparams("arbitrary"),
        name="moe_experts",
    )(nact, blk_e, xs, w_gate, w_up, w_down)


def _final_kernel(pos_ref, tw_ref, ysh_ref, x1_ref, gate_ref, gpost_ref, yb_hbm, o_ref, rbuf, sem, *, tm, n):
    i = pl.program_id(0)
    slot = i % 2

    def gather(tile, sl):
        def body(g, carry):
            for u in range(ROW_DMA_UNROLL):
                r = g * ROW_DMA_UNROLL + u
                for kk in range(TOP_K):
                    p = pos_ref[kk * n + tile * tm + r]
                    pltpu.make_async_copy(yb_hbm.at[_row_tile(p)], rbuf.at[sl, kk].at[_row_tile(r)],
                                          sem.at[sl]).start(priority=kk % 2)
            return carry
        lax.fori_loop(0, tm // ROW_DMA_UNROLL, body, 0)

    @pl.when(i == 0)
    def _():
        gather(0, 0)

    @pl.when(i + 1 < pl.num_programs(0))
    def _():
        gather(i + 1, 1 - slot)

    for kk in range(TOP_K):
        pltpu.make_async_copy(yb_hbm.at[pl.ds(0, tm * ROW_TILE), :], rbuf.at[slot, kk], sem.at[slot]).wait()
    tw = tw_ref[...]
    half = ROW_TILE * LANES
    r_lo = jnp.zeros((tm, half), F32)
    r_hi = jnp.zeros((tm, half), F32)
    for kk in range(TOP_K):
        lo, hi = _unpack_halves(_load_row_tiles(rbuf.at[slot, kk], tm))
        wk = tw[:, kk:kk + 1]
        r_lo = r_lo + wk * lo
        r_hi = r_hi + wk * hi
    y = ysh_ref[...].astype(F32) + jnp.concatenate([r_lo, r_hi], axis=1)
    o_ref[...] = x1_ref[...] + gate_ref[0] * _rms(y, gpost_ref[...])


def _final(pos_flat, top_w8, ysh, x1, mod3, gpost, yb, seq, tm=256):
    n, d = x1.shape
    per_b = seq // tm
    const = lambda a: pl.BlockSpec(a.shape, lambda i, ps: (0,) * a.ndim)
    row = lambda w: pl.BlockSpec((tm, w), lambda i, ps: (i, 0))
    grid_spec = pltpu.PrefetchScalarGridSpec(
        num_scalar_prefetch=1,
        grid=(n // tm,),
        in_specs=[row(top_w8.shape[1]), row(d), row(d),
                  pl.BlockSpec((1, 1, d), lambda i, ps: ((i // per_b) * ADALN_CHUNKS + 5, 0, 0)),
                  const(gpost),
                  pl.BlockSpec(memory_space=pl.ANY)],
        out_specs=row(d),
        scratch_shapes=[pltpu.VMEM((2, TOP_K, tm * ROW_TILE, LANES), jnp.uint32), pltpu.SemaphoreType.DMA((2,))],
    )
    return pl.pallas_call(
        functools.partial(_final_kernel, tm=tm, n=n),
        grid_spec=grid_spec,
        out_shape=jax.ShapeDtypeStruct((n, d), F32),
        compiler_params=_cparams("arbitrary"),
        name="moe_combine_final",
    )(pos_flat, top_w8, ysh, x1, mod3, gpost, yb)


def _rope_freq_row(dim):
    inv_freq = 1.0 / (ROPE_THETA ** (jnp.arange(0, dim, 2, dtype=F32) / dim))
    return jnp.tile(jnp.concatenate([inv_freq, inv_freq]), LANES // dim).reshape(1, LANES)


def kernel(x, c, positions, w_ada, b_ada, attn_pre_g, w_in, q_a_norm_g, w_q_up, kv_a_norm_g, w_kv_up, w_mla_o, w_dil_o, w_out, attn_post_g, ffn_pre_g, w_router, router_bias, w_exp_gate, w_exp_up, w_exp_down, w_sh_gate, w_sh_up, w_sh_down, ffn_post_g):
    batch, seq, d = x.shape
    n = batch * seq
    depth = w_ada.shape[0]

    pos_col = positions.astype(F32).reshape(n, 1)
    freq_mla, freq_dil = _rope_freq_row(QK_ROPE_DIM), _rope_freq_row(DIL_ROT_DIM)

    x2 = x.reshape(n, d)
    c8 = jnp.pad(c, ((0, 8 - batch), (0, 0)))
    for l in range(depth):
        mod = _ada(c8, w_ada[l], b_ada[l].reshape(1, -1))
        mod3 = mod[:batch].reshape(batch * ADALN_CHUNKS, 1, d)

        wi = w_in[l]
        o_dil = Q_LORA_RANK + KV_LORA_RANK + QK_ROPE_DIM
        o_ga = o_dil + 3 * DIL_HEADS * DIL_HEAD_DIM
        n_gate, n_dil = wi.shape[1] - o_ga, o_ga - o_dil
        n_a = o_dil + LANES - QK_ROPE_DIM
        c_dil, c_a = n_gate, -(-(n_gate + n_dil) // n_a) * n_a
        w_full = _wprep(wi.T, ((0, o_ga, n_gate), (c_dil, o_dil, n_dil), (c_a, 0, o_dil)), c_a + n_a)
        wq3 = w_q_up[l].reshape(Q_LORA_RANK, MLA_HEADS, MLA_QK_DIM)
        wq = jnp.concatenate([wq3, jnp.zeros((Q_LORA_RANK, MLA_HEADS, MLA_QK_PAD - MLA_QK_DIM), F32)],
                             axis=2).reshape(Q_LORA_RANK, MLA_HEADS * MLA_QK_PAD).astype(BF16)
        wkv3 = w_kv_up[l].reshape(KV_LORA_RANK, MLA_HEADS, QK_NOPE_DIM + V_HEAD_DIM)
        wkv = jnp.concatenate([wkv3[:, :, :QK_NOPE_DIM].reshape(KV_LORA_RANK, -1),
                               wkv3[:, :, QK_NOPE_DIM:].reshape(KV_LORA_RANK, -1)], axis=1).astype(BF16)

        h, q, k, v = _front(x2, attn_pre_g[l].reshape(1, d), mod3, w_full, c_a, n_a,
                            q_a_norm_g[l].reshape(1, -1), kv_a_norm_g[l].reshape(1, -1), wq, wkv,
                            pos_col, freq_mla, batch, seq)
        gates = _mm(h, w_full, 0, n_gate, act="sigmoid", tn=2048)
        o_mla = _mla_attn(q, k, v).reshape(n, MLA_HEADS * V_HEAD_DIM)
        dil_o, dil_lse = [], []
        rope_dil = (pos_col, freq_dil)
        for g, (_, dilation) in enumerate(DIL_PATTERNS):
            qkv, rope_dil = _dilproj(h, w_full, c_dil, g, rope_dil, batch, seq, dilation)
            o_g, lse_g = _dil_attn(qkv, batch, seq, dilation)
            dil_o.append(o_g)
            dil_lse.append(lse_g)

        wr = jnp.pad(w_router[l], ((0, 0), (0, LANES - N_EXPERTS)))
        wr_hi = wr.astype(BF16)
        wr_lo = (wr - wr_hi.astype(F32)).astype(BF16)
        x1, h2, h2pk, logits_t = _merge(o_mla, dil_o, dil_lse, gates, x2, mod3,
                                        attn_post_g[l].reshape(1, d), ffn_pre_g[l].reshape(1, d),
                                        w_mla_o[l].astype(BF16), w_dil_o[l].astype(BF16), w_out[l].astype(BF16),
                                        jnp.concatenate([wr_hi, wr_lo], axis=1), wr_hi, seq)

        nb = -(-(n * TOP_K + N_EXPERTS * (MOE_BLOCK - 1)) // MOE_BLOCK)
        pos_t, w_t, meta, blk_e = _route(logits_t, router_bias[l].astype(F32).reshape(N_EXPERTS, 1), nb)
        pos_flat = pos_t.reshape(-1)
        nact = meta[2, N_EXPERTS - 1, :1] // MOE_BLOCK
        xs, ysh = _dispatch(pos_flat, meta[0, :, 0], meta[1, :, 0], h2pk, h2,
                            w_sh_gate[l].astype(BF16), w_sh_up[l].astype(BF16), w_sh_down[l].astype(BF16), nb)
        yb = _gmm(nact, blk_e[0], xs, w_exp_gate[l], w_exp_up[l], w_exp_down[l], nb)
        x2 = _final(pos_flat, w_t.T, ysh, x1, mod3, ffn_post_g[l].reshape(1, d), yb, seq)
    return x2.reshape(batch, seq, d)
```

```python
import functools

import jax
import jax.numpy as jnp
from jax import lax
from jax.experimental import pallas as pl
from jax.experimental.pallas import tpu as pltpu

F32 = jnp.float32
BF16 = jnp.bfloat16

D_MODEL = 2048
NORM_EPS = 1e-6
ROPE_THETA = 500000.0
ADALN_CHUNKS = 6

MLA_HEADS = 8
Q_LORA_RANK = 512
KV_LORA_RANK = 512
QK_NOPE_DIM = 128
QK_ROPE_DIM = 64
V_HEAD_DIM = 128
MLA_QK_DIM = QK_NOPE_DIM + QK_ROPE_DIM
MLA_QK_PAD = 256

DIL_PATTERNS = ((128, 1), (512, 4), (2048, 16))
DIL_GROUPS = len(DIL_PATTERNS)
DIL_HEADS_PER_GROUP = 4
DIL_HEADS = DIL_GROUPS * DIL_HEADS_PER_GROUP
DIL_HEAD_DIM = 128
DIL_ROT_DIM = DIL_HEAD_DIM // 4
DIL_SPAN = 128
DIL_GROUP_COLS = DIL_HEADS_PER_GROUP * DIL_HEAD_DIM

N_EXPERTS = 64
N_EXPERT_GROUPS = 8
TOPK_GROUPS = 4
TOP_K = 6
EXPERT_DIM = 512
SHARED_DIM = 512
ROUTED_SCALE = 2.5
MOE_BLOCK = 256

LANES = 128
NEG_BIG = -1e30
LOG2_E = 1.4426950408889634
ROW_DMA_UNROLL = 4
GMM_SUB = 4
VMEM_LIMIT = 56 * 1024 * 1024


def _cparams(*sem):
    return pltpu.CompilerParams(dimension_semantics=sem, vmem_limit_bytes=VMEM_LIMIT)


def _sigmoid(v):
    return 1.0 / (1.0 + jnp.exp(-v))


def _rms(v, g):
    ms = jnp.mean(v * v, axis=-1, keepdims=True)
    return v * lax.rsqrt(ms + NORM_EPS) * g


def _ada_kernel(c_ref, w_ref, b_ref, o_ref):
    c = c_ref[...]
    a = (c * _sigmoid(c)).astype(BF16)
    o_ref[...] = jnp.dot(a, w_ref[...].astype(BF16), preferred_element_type=F32) + b_ref[...]


def _ada(c8, w_ada, b_ada, tn=1536):
    d, n = w_ada.shape
    return pl.pallas_call(
        _ada_kernel,
        grid=(n // tn,),
        in_specs=[pl.BlockSpec((8, d), lambda j: (0, 0)),
                  pl.BlockSpec((d, tn), lambda j: (0, j)),
                  pl.BlockSpec((1, tn), lambda j: (0, j))],
        out_specs=pl.BlockSpec((8, tn), lambda j: (0, j)),
        out_shape=jax.ShapeDtypeStruct((8, n), F32),
        compiler_params=_cparams("arbitrary"),
        name="ada_mod",
    )(c8, w_ada, b_ada)


def _wprep_source(j, segments):
    src = j * 0
    valid = j * 0
    for dst, s0, width in segments:
        t0, t1 = dst // LANES, (dst + width + LANES - 1) // LANES
        inside = jnp.logical_and(j >= t0, j < t1)
        src = jnp.where(inside, s0 + (j - t0) * LANES, src)
        valid = jnp.where(inside, jnp.minimum(dst + width - j * LANES, LANES), valid)
    return src, valid


def _wprep_kernel(*refs, segments, tiles):
    o_ref = refs[-1]
    for t in range(tiles):
        _, valid = _wprep_source(pl.program_id(0) * tiles + t, segments)
        blk = jnp.concatenate([refs[2 * t][...], refs[2 * t + 1][...]], axis=0).T
        lane = lax.broadcasted_iota(jnp.int32, blk.shape, 1)
        o_ref[:, t * LANES:(t + 1) * LANES] = jnp.where(lane < valid, blk, 0.0).astype(BF16)


def _wprep(w_t, segments, cols, tiles=3):
    _, k = w_t.shape
    half = LANES // 2
    assert all(dst % LANES == 0 and s0 % half == 0 and width % half == 0 for dst, s0, width in segments)
    assert cols % (tiles * LANES) == 0
    part = lambda t, which: pl.BlockSpec(
        (half, k), lambda j: (_wprep_source(j * tiles + t, segments)[0] // half + which, 0))
    return pl.pallas_call(
        functools.partial(_wprep_kernel, segments=segments, tiles=tiles),
        grid=(cols // (tiles * LANES),),
        in_specs=[part(t, which) for t in range(tiles) for which in range(2)],
        out_specs=pl.BlockSpec((k, tiles * LANES), lambda j: (0, j)),
        out_shape=jax.ShapeDtypeStruct((k, cols), BF16),
        compiler_params=_cparams("arbitrary"),
        name="w_in_prep",
    )(*([w_t] * (2 * tiles)))


def _rope_lanes(t, c_tab, s_fwd, s_bwd, half):
    return t * c_tab + pltpu.roll(t, half, 1) * s_fwd + pltpu.roll(t, LANES - half, 1) * s_bwd


def _rope_tables(pos, freq, dim, passthrough):
    rows = pos.shape[0]
    groups = LANES // dim
    sub = rows // groups
    lane = lax.broadcasted_iota(jnp.int32, (sub, LANES), 1)
    packed = jnp.zeros((sub, LANES), F32)
    for gi in range(groups):
        packed = jnp.where(lane // dim == gi, pos[gi * sub:(gi + 1) * sub], packed)
    ang = packed * freq
    cos_p, sin_p = jnp.cos(ang), jnp.sin(ang)
    half = dim // 2
    tabs = []
    for gi in range(groups):
        cos = cos_p if gi == 0 else pltpu.roll(cos_p, LANES - gi * dim, 1)
        sin = sin_p if gi == 0 else pltpu.roll(sin_p, LANES - gi * dim, 1)
        tabs.append((jnp.where(lane < dim, cos, 1.0 if passthrough else 0.0),
                     jnp.where(jnp.logical_and(lane >= half, lane < dim), sin, 0.0),
                     jnp.where(lane < half, -sin, 0.0)))
    return tuple(jnp.concatenate([t[j] for t in tabs], axis=0) for j in range(3))


def _front_kernel(x_ref, g_ref, scale_ref, shift_ref, wa_ref, gq_ref, gkv_ref, wq_ref, wkv_ref,
                  pos_ref, freq_ref, h_ref, q_ref, k_ref, v_ref):
    h = (_rms(x_ref[...], g_ref[...]) * (1.0 + scale_ref[0]) + shift_ref[0]).astype(BF16)
    h_ref[...] = h
    a = jnp.dot(h, wa_ref[...], preferred_element_type=F32)
    qa = a[:, :Q_LORA_RANK]
    ckv = a[:, Q_LORA_RANK:Q_LORA_RANK + KV_LORA_RANK]
    kr = a[:, Q_LORA_RANK + KV_LORA_RANK:]
    c_tab, s_fwd, s_bwd = _rope_tables(pos_ref[...], freq_ref[...], QK_ROPE_DIM, passthrough=False)
    half = QK_ROPE_DIM // 2
    q = jnp.dot(_rms(qa, gq_ref[...]).astype(BF16), wq_ref[...], preferred_element_type=F32)
    q = q * (MLA_QK_DIM ** -0.5 * LOG2_E)
    kv = jnp.dot(_rms(ckv, gkv_ref[...]).astype(BF16), wkv_ref[...], preferred_element_type=F32)
    k_rot = _rope_lanes(kr, c_tab, s_fwd, s_bwd, half).astype(BF16)
    lane = lax.broadcasted_iota(jnp.int32, (a.shape[0], LANES), 1)
    ones_col = jnp.where(lane == 0, 1.0, 0.0).astype(BF16)
    for hh in range(MLA_HEADS):
        base = hh * MLA_QK_PAD
        q_ref[0, hh, :, :LANES] = q[:, base:base + LANES].astype(BF16)
        q_ref[0, hh, :, LANES:] = _rope_lanes(q[:, base + LANES:base + 2 * LANES],
                                              c_tab, s_fwd, s_bwd, half).astype(BF16)
        k_ref[0, hh, :, :LANES] = kv[:, hh * LANES:(hh + 1) * LANES].astype(BF16)
        k_ref[0, hh, :, LANES:] = k_rot
        v_off = MLA_HEADS * LANES + hh * LANES
        v_ref[0, hh, :, :LANES] = kv[:, v_off:v_off + LANES].astype(BF16)
        v_ref[0, hh, :, LANES:] = ones_col


def _front(x2, g, mod3, w_full, col_a, n_a, gq, gkv, wq, wkv, pos_col, freq, batch, seq, tm=512):
    n, d = x2.shape
    per_b = seq // tm
    const = lambda a: pl.BlockSpec(a.shape, lambda i: (0,) * a.ndim, pipeline_mode=pl.Buffered(1))
    head_major = lambda w: pl.BlockSpec((1, MLA_HEADS, tm, w), lambda i: (i // per_b, 0, i % per_b, 0))
    return pl.pallas_call(
        _front_kernel,
        grid=(n // tm,),
        in_specs=[pl.BlockSpec((tm, d), lambda i: (i, 0)),
                  pl.BlockSpec((1, d), lambda i: (0, 0)),
                  pl.BlockSpec((1, 1, d), lambda i: ((i // per_b) * ADALN_CHUNKS + 1, 0, 0)),
                  pl.BlockSpec((1, 1, d), lambda i: ((i // per_b) * ADALN_CHUNKS + 0, 0, 0)),
                  pl.BlockSpec((d, n_a), lambda i: (0, col_a // n_a), pipeline_mode=pl.Buffered(1)),
                  const(gq), const(gkv), const(wq), const(wkv),
                  pl.BlockSpec((tm, 1), lambda i: (i, 0)), pl.BlockSpec((1, LANES), lambda i: (0, 0))],
        out_specs=[pl.BlockSpec((tm, d), lambda i: (i, 0)),
                   head_major(MLA_QK_PAD), head_major(MLA_QK_PAD), head_major(2 * V_HEAD_DIM)],
        out_shape=[jax.ShapeDtypeStruct((n, d), BF16),
                   jax.ShapeDtypeStruct((batch, MLA_HEADS, seq, MLA_QK_PAD), BF16),
                   jax.ShapeDtypeStruct((batch, MLA_HEADS, seq, MLA_QK_PAD), BF16),
                   jax.ShapeDtypeStruct((batch, MLA_HEADS, seq, 2 * V_HEAD_DIM), BF16)],
        compiler_params=_cparams("arbitrary"),
        name="front_mla_prep",
    )(x2, g, mod3, mod3, w_full, gq, gkv, wq, wkv, pos_col, freq)


def _mm_kernel(h_ref, w_ref, o_ref, *, act):
    y = jnp.dot(h_ref[...], w_ref[...], preferred_element_type=F32)
    if act == "sigmoid":
        y = _sigmoid(y)
    o_ref[...] = y.astype(o_ref.dtype)


def _mm(h, w, col0, cols, act=None, tm=1024, tn=1024):
    n, k = h.shape
    tn = min(tn, cols)
    j0 = col0 // tn
    return pl.pallas_call(
        functools.partial(_mm_kernel, act=act),
        grid=(cols // tn, n // tm),
        in_specs=[pl.BlockSpec((tm, k), lambda j, i: (i, 0)),
                  pl.BlockSpec((k, tn), lambda j, i: (0, j0 + j))],
        out_specs=pl.BlockSpec((tm, tn), lambda j, i: (i, j)),
        out_shape=jax.ShapeDtypeStruct((n, cols), BF16),
        compiler_params=_cparams("arbitrary", "arbitrary"),
        name="in_proj_" + (act or "plain"),
    )(h, w)


def _dilproj_kernel(h_ref, wq_ref, wk_ref, wv_ref, *rest, dilation, make_tables):
    hb = h_ref[...]
    y = jnp.concatenate([jnp.dot(hb, w_ref[...], preferred_element_type=F32) for w_ref in (wq_ref, wk_ref, wv_ref)],
                        axis=1)
    if make_tables:
        pos_ref, freq_ref, o_ref, c_out, sf_out, sb_out, y_sc = rest
        c_tab, s_fwd, s_bwd = _rope_tables(pos_ref[...], freq_ref[...], DIL_ROT_DIM, passthrough=True)
        c_out[...], sf_out[...], sb_out[...] = c_tab, s_fwd, s_bwd
    else:
        c_ref, sf_ref, sb_ref, o_ref, y_sc = rest
        c_tab, s_fwd, s_bwd = c_ref[...], sf_ref[...], sb_ref[...]
    cols = 3 * DIL_GROUP_COLS
    rows = h_ref.shape[0] // dilation
    n_rot = 2 * DIL_HEADS_PER_GROUP
    for hh in range(3 * DIL_HEADS_PER_GROUP):
        t = y[:, hh * LANES:(hh + 1) * LANES]
        if hh < n_rot:
            t = _rope_lanes(t, c_tab, s_fwd, s_bwd, DIL_ROT_DIM // 2)
        if dilation == 1:
            o_ref[0, :, hh * LANES:(hh + 1) * LANES] = t.astype(BF16)
        else:
            y_sc[hh] = t
    if dilation > 1:
        for r in range(dilation):
            for hh in range(3 * DIL_HEADS_PER_GROUP):
                c0 = r * cols + hh * LANES
                o_ref[0, :, c0:c0 + LANES] = y_sc.at[hh][pl.ds(r, rows, stride=dilation), :].astype(BF16)


def _dilproj(h, w, col0, group, rope_in, batch, seq, dilation, tm=1024):
    n, k = h.shape
    gc = DIL_GROUP_COLS
    cols = 3 * gc
    per_b = seq // tm
    make_tables = len(rope_in) == 2
    tab = pl.BlockSpec((tm, LANES), lambda i: (i, 0))
    part = lambda which: pl.BlockSpec((k, gc), lambda i: (0, col0 // gc + which * DIL_GROUPS + group))
    rope_specs = ([pl.BlockSpec((tm, 1), lambda i: (i, 0)), pl.BlockSpec((1, LANES), lambda i: (0, 0))]
                  if make_tables else [tab, tab, tab])
    out_specs = [pl.BlockSpec((1, tm // dilation, dilation * cols), lambda i: (i // per_b, i % per_b, 0))]
    out_shape = [jax.ShapeDtypeStruct((batch, seq // dilation, dilation * cols), BF16)]
    if make_tables:
        out_specs += [tab, tab, tab]
        out_shape += [jax.ShapeDtypeStruct((n, LANES), F32)] * 3
    out = pl.pallas_call(
        functools.partial(_dilproj_kernel, dilation=dilation, make_tables=make_tables),
        grid=(n // tm,),
        in_specs=[pl.BlockSpec((tm, k), lambda i: (i, 0)), part(0), part(1), part(2)] + rope_specs,
        out_specs=out_specs,
        out_shape=out_shape,
        scratch_shapes=[pltpu.VMEM((cols // LANES, tm, LANES), F32)],
        compiler_params=_cparams("arbitrary"),
        name="dil_proj",
    )(h, w, w, w, *rope_in)
    return out[0], tuple(out[1:]) if make_tables else rope_in


def _mla_attn_kernel(q_ref, k_ref, v_ref, o_ref, *, tq, nh):
    i = pl.program_id(2)
    qs = [q_ref[0, hh] for hh in range(nh)]

    def step(c, carry, masked):
        base = pl.multiple_of(c * tq, tq)
        ss = []
        for hh in range(nh):
            k = k_ref[0, hh, pl.ds(base, tq), :]
            s = lax.dot_general(qs[hh], k, (((1,), (1,)), ((), ())), preferred_element_type=F32)
            if masked:
                row = lax.broadcasted_iota(jnp.int32, (tq, tq), 0)
                col = lax.broadcasted_iota(jnp.int32, (tq, tq), 1)
                s = jnp.where(col <= row, s, NEG_BIG)
            ss.append(s)
        out = []
        for hh in range(nh):
            m, l, acc = carry[hh]
            v = v_ref[0, hh, pl.ds(base, tq), :]
            m_new = jnp.maximum(m, jnp.max(ss[hh], axis=-1, keepdims=True))
            alpha = jnp.exp2(m - m_new)
            pv = jnp.dot(jnp.exp2((ss[hh] - m_new).astype(BF16)), v, preferred_element_type=F32)
            out.append((m_new, alpha * l + pv[:, V_HEAD_DIM:V_HEAD_DIM + 1], alpha * acc + pv[:, :V_HEAD_DIM]))
        return tuple(out)

    init = tuple((jnp.full((tq, 1), NEG_BIG, F32), jnp.zeros((tq, 1), F32), jnp.zeros((tq, V_HEAD_DIM), F32))
                 for _ in range(nh))
    carry = lax.fori_loop(0, i, lambda c, cr: step(c, cr, False), init)
    carry = step(i, carry, True)
    for hh in range(nh):
        _, l, acc = carry[hh]
        o_ref[0, :, hh * V_HEAD_DIM:(hh + 1) * V_HEAD_DIM] = (acc / l).astype(BF16)


def _mla_attn(q, k, v, tq=512, nh=4):
    b, h, s, dk = q.shape
    dv = v.shape[-1]
    return pl.pallas_call(
        functools.partial(_mla_attn_kernel, tq=tq, nh=nh),
        grid=(b, h // nh, s // tq),
        in_specs=[pl.BlockSpec((1, nh, tq, dk), lambda bi, hi, i: (bi, hi, i, 0)),
                  pl.BlockSpec((1, nh, s, dk), lambda bi, hi, i: (bi, hi, 0, 0)),
                  pl.BlockSpec((1, nh, s, dv), lambda bi, hi, i: (bi, hi, 0, 0))],
        out_specs=pl.BlockSpec((1, tq, nh * V_HEAD_DIM), lambda bi, hi, i: (bi, i, hi)),
        out_shape=jax.ShapeDtypeStruct((b, s, h * V_HEAD_DIM), BF16),
        compiler_params=_cparams("arbitrary", "arbitrary", "arbitrary"),
        name="mla_attn",
    )(q, k, v)


def _dil_attn_kernel(q_ref, kc_ref, kp_ref, vc_ref, vp_ref, o_ref, lse_ref, *, tq):
    i = pl.program_id(2)
    sub = DIL_SPAN
    row = lax.broadcasted_iota(jnp.int32, (sub, 2 * sub), 0)
    col = lax.broadcasted_iota(jnp.int32, (sub, 2 * sub), 1)
    band = jnp.logical_and(col >= row, col <= row + sub)
    first = jnp.logical_and(band, col >= jnp.where(i > 0, 0, sub))
    lane = lax.broadcasted_iota(jnp.int32, (sub, LANES), 1)
    scale = DIL_HEAD_DIM ** -0.5
    dn = (((1,), (1,)), ((), ()))
    chains = [(j, hh) for j in range(tq // sub) for hh in range(DIL_HEADS_PER_GROUP)]

    def window(cur_ref, prev_ref, j, cs):
        if j == 0:
            return jnp.concatenate([prev_ref[0, :, cs], cur_ref[0, :sub, cs]], axis=0)
        return cur_ref[0, (j - 1) * sub:(j + 1) * sub, cs]

    scores = []
    for j, hh in chains:
        cs = slice(hh * LANES, (hh + 1) * LANES)
        s = lax.dot_general(q_ref[0, j * sub:(j + 1) * sub, cs], window(kc_ref, kp_ref, j, cs), dn,
                            preferred_element_type=F32) * scale
        scores.append(jnp.where(first if j == 0 else band, s, NEG_BIG))
    lse_blk = [jnp.zeros((sub, LANES), F32) for _ in range(tq // sub)]
    for (j, hh), s in zip(chains, scores):
        cs = slice(hh * LANES, (hh + 1) * LANES)
        m = jnp.max(s, axis=-1, keepdims=True)
        p = jnp.exp(s - m)
        l = jnp.sum(p, axis=-1, keepdims=True)
        acc = jnp.dot(p.astype(BF16), window(vc_ref, vp_ref, j, cs), preferred_element_type=F32)
        o_ref[0, j * sub:(j + 1) * sub, cs] = (acc * (1.0 / l)).astype(BF16)
        lse_blk[j] = jnp.where(lane == hh, m + jnp.log(l), lse_blk[j])
    for j in range(tq // sub):
        lse_ref[0, j * sub:(j + 1) * sub, :] = lse_blk[j]


def _dil_attn(t, batch, seq, dilation):
    ln = seq // dilation
    tq = min(ln, 4 * DIL_SPAN)
    gc = DIL_GROUP_COLS
    ratio = tq // DIL_SPAN
    cur = lambda which: pl.BlockSpec((1, tq, gc), lambda b, r, i: (b, i, r * 3 + which))
    prev = lambda which: pl.BlockSpec(
        (1, DIL_SPAN, gc), lambda b, r, i: (b, jnp.maximum(i * ratio - 1, 0), r * 3 + which))
    o, lse = pl.pallas_call(
        functools.partial(_dil_attn_kernel, tq=tq),
        grid=(batch, dilation, ln // tq),
        in_specs=[cur(0), cur(1), prev(1), cur(2), prev(2)],
        out_specs=[pl.BlockSpec((1, tq, gc), lambda b, r, i: (b, i, r)),
                   pl.BlockSpec((1, tq, LANES), lambda b, r, i: (b, i, r))],
        out_shape=[jax.ShapeDtypeStruct((batch, ln, dilation * gc), BF16),
                   jax.ShapeDtypeStruct((batch, ln, dilation * LANES), F32)],
        compiler_params=_cparams("arbitrary", "arbitrary", "arbitrary"),
        name=f"dil_attn_d{dilation}",
    )(t, t, t, t, t)
    return o, lse


def _pack_halves(v):
    w = v.shape[1] // 2
    lo = lax.bitcast_convert_type(v[:, :w].astype(BF16).astype(F32), jnp.uint32)
    hi = lax.bitcast_convert_type(v[:, w:].astype(BF16).astype(F32), jnp.uint32)
    return (lo >> 16) | (hi & jnp.uint32(0xFFFF0000))


ROW_TILE = 8


def _row_tile(p):
    return (pl.ds(pl.multiple_of(p * ROW_TILE, ROW_TILE), ROW_TILE), slice(None))


def _store_row_tiles(ref, pk, base=0):
    rows = pk.shape[0]
    for c in range(ROW_TILE):
        ref[pl.ds(base + c, rows, stride=ROW_TILE), :] = pk[:, c * LANES:(c + 1) * LANES]


def _load_row_tiles(ref, rows, base=0):
    return jnp.concatenate([ref[pl.ds(base + c, rows, stride=ROW_TILE), :] for c in range(ROW_TILE)], axis=1)


def _unpack_halves(pk):
    lo = lax.bitcast_convert_type(pk << 16, F32)
    hi = lax.bitcast_convert_type(pk & jnp.uint32(0xFFFF0000), F32)
    return lo, hi


def _merge_kernel(oa_ref, o0_ref, o1_ref, o2_ref, l0_ref, l1_ref, l2_ref, ga_ref, gb_ref, x_ref,
                  gate_ref, shift_ref, scale_ref, gpost_ref, gpre_ref,
                  wa_ref, wb_ref, wo_ref, wrh_ref, wrl_ref,
                  x1_ref, h2_ref, h2pk_ref, logit_ref, o_sc, l_sc):
    tm = x_ref.shape[0]

    def natural(ref, sc, gi, chunks):
        dil = DIL_PATTERNS[gi][1]
        if dil == 1:
            return [ref[0, :, c * LANES:(c + 1) * LANES].astype(F32) for c in range(chunks)]
        for r in range(dil):
            for c in range(chunks):
                c0 = (r * chunks + c) * LANES
                sc.at[gi, c][pl.ds(r, tm // dil, stride=dil), :] = ref[0, :, c0:c0 + LANES].astype(F32)
        return [sc[gi, c] for c in range(chunks)]

    (l0,), (l1,), (l2,) = [natural(ref, l_sc, gi, 1) for gi, ref in enumerate((l0_ref, l1_ref, l2_ref))]
    o0, o1, o2 = [natural(ref, o_sc, gi, DIL_HEADS_PER_GROUP) for gi, ref in enumerate((o0_ref, o1_ref, o2_ref))]
    m = jnp.maximum(jnp.maximum(l0, l1), l2)
    e0, e1, e2 = jnp.exp(l0 - m), jnp.exp(l1 - m), jnp.exp(l2 - m)
    inv = 1.0 / (e0 + e1 + e2)
    w0, w1, w2 = e0 * inv, e1 * inv, e2 * inv
    parts = []
    for hh in range(DIL_HEADS_PER_GROUP):
        parts.append(w0[:, hh:hh + 1] * o0[hh] + w1[:, hh:hh + 1] * o1[hh] + w2[:, hh:hh + 1] * o2[hh])
    o_dil = jnp.concatenate(parts, axis=1).astype(BF16)
    y_a = jnp.dot(oa_ref[...], wa_ref[...], preferred_element_type=F32)
    y_b = jnp.dot(o_dil, wb_ref[...], preferred_element_type=F32)
    merged = ga_ref[...].astype(F32) * y_a + gb_ref[...].astype(F32) * y_b
    y = jnp.dot(merged.astype(BF16), wo_ref[...], preferred_element_type=F32)
    x1 = x_ref[...] + gate_ref[0] * _rms(y, gpost_ref[...])
    x1_ref[...] = x1
    h2 = _rms(x1, gpre_ref[...]) * (1.0 + scale_ref[0]) + shift_ref[0]
    _store_row_tiles(h2pk_ref, _pack_halves(h2))
    h2_hi = h2.astype(BF16)
    h2_ref[...] = h2_hi
    h2_lo = (h2 - h2_hi.astype(F32)).astype(BF16)
    both = jnp.dot(h2_hi, wrh_ref[...], preferred_element_type=F32)
    logits = both[:, :LANES] + both[:, LANES:] + jnp.dot(h2_lo, wrl_ref[...], preferred_element_type=F32)
    logit_ref[...] = logits.T


def _merge(oa, dil_o, dil_lse, gates, x2, mod3, gpost, gpre, wa, wb, wo, wr_hi, wr_lo, seq, tm=256):
    n, d = x2.shape
    per_b = seq // tm
    row = lambda w: pl.BlockSpec((tm, w), lambda i: (i, 0))
    const = lambda a: pl.BlockSpec(a.shape, lambda i: (0,) * a.ndim, pipeline_mode=pl.Buffered(1))
    modspec = lambda ch: pl.BlockSpec((1, 1, d), lambda i: ((i // per_b) * ADALN_CHUNKS + ch, 0, 0))
    strided = lambda gi, w: pl.BlockSpec((1, tm // DIL_PATTERNS[gi][1], DIL_PATTERNS[gi][1] * w),
                                         lambda i: (i // per_b, i % per_b, 0))
    return pl.pallas_call(
        _merge_kernel,
        grid=(n // tm,),
        in_specs=[row(oa.shape[1]),
                  strided(0, DIL_GROUP_COLS), strided(1, DIL_GROUP_COLS), strided(2, DIL_GROUP_COLS),
                  strided(0, LANES), strided(1, LANES), strided(2, LANES),
                  pl.BlockSpec((tm, d), lambda i: (i, 0)), pl.BlockSpec((tm, d), lambda i: (i, 1)),
                  row(d),
                  modspec(2), modspec(3), modspec(4),
                  const(gpost), const(gpre),
                  const(wa), const(wb), const(wo), const(wr_hi), const(wr_lo)],
        out_specs=[row(d), row(d), pl.BlockSpec((tm * ROW_TILE, LANES), lambda i: (i, 0)),
                   pl.BlockSpec((LANES, tm), lambda i: (0, i))],
        out_shape=[jax.ShapeDtypeStruct((n, d), F32),
                   jax.ShapeDtypeStruct((n, d), BF16),
                   jax.ShapeDtypeStruct((n * ROW_TILE, LANES), jnp.uint32),
                   jax.ShapeDtypeStruct((LANES, n), F32)],
        scratch_shapes=[pltpu.VMEM((DIL_GROUPS, DIL_HEADS_PER_GROUP, tm, LANES), F32),
                        pltpu.VMEM((DIL_GROUPS, 1, tm, LANES), F32)],
        compiler_params=_cparams("arbitrary"),
        name="merge_outproj",
    )(oa, *dil_o, *dil_lse, gates, gates, x2, mod3, mod3, mod3, gpost, gpre, wa, wb, wo, wr_hi, wr_lo)


def _route_kernel(lg_ref, bias_ref, pos_ref, w_ref, meta_ref, blke_ref, cnt_sc, base_sc, rank_sc, score_sc, *, tt):
    ps = pl.program_id(0)
    i = pl.program_id(1)
    per_group = N_EXPERTS // N_EXPERT_GROUPS
    neg_inf = -jnp.inf
    tile = pl.ds(pl.multiple_of(i * tt, tt), tt)

    @pl.when(jnp.logical_and(ps == 0, i == 0))
    def _():
        cnt_sc[...] = jnp.zeros_like(cnt_sc)

    @pl.when(ps == 0)
    def _():
        scores = _sigmoid(lg_ref[...])
        biased = scores + bias_ref[...]
        b3 = biased.reshape(N_EXPERT_GROUPS, per_group, tt)
        mem = lax.broadcasted_iota(jnp.int32, b3.shape, 1)
        m1 = jnp.max(b3, axis=1, keepdims=True)
        first = jnp.min(jnp.where(b3 == m1, mem, per_group), axis=1, keepdims=True)
        m2 = jnp.max(jnp.where(mem == first, neg_inf, b3), axis=1, keepdims=True)
        gs = m1 + m2
        gidx = lax.broadcasted_iota(jnp.int32, gs.shape, 0)
        grank = jnp.zeros(gs.shape, jnp.int32)
        for g2 in range(N_EXPERT_GROUPS):
            r = gs[g2:g2 + 1]
            beats = jnp.logical_or(r > gs, jnp.logical_and(r == gs, g2 < gidx))
            grank = grank + jnp.where(beats, 1, 0)
        sel = jnp.where(grank < TOPK_GROUPS, b3, neg_inf).reshape(N_EXPERTS, tt)
        eidx = lax.broadcasted_iota(jnp.int32, sel.shape, 0)
        erank = jnp.full(sel.shape, TOP_K, jnp.int32)
        remaining = sel
        for kk in range(TOP_K):
            top = jnp.max(remaining, axis=0, keepdims=True)
            first = jnp.min(jnp.where(remaining == top, eidx, N_EXPERTS), axis=0, keepdims=True)
            hit = eidx == first
            erank = jnp.where(hit, kk, erank)
            remaining = jnp.where(hit, neg_inf, remaining)
        rank_sc[:, tile] = erank
        score_sc[:, tile] = scores
        cnt_sc[...] = cnt_sc[...] + jnp.sum(jnp.where(erank < TOP_K, 1.0, 0.0), axis=1,
                                            keepdims=True).astype(jnp.int32)

    @pl.when(jnp.logical_and(ps == 1, i == 0))
    def _():
        cnt = cnt_sc[...]
        pc = ((cnt + (MOE_BLOCK - 1)) // MOE_BLOCK) * MOE_BLOCK
        pcb = jnp.broadcast_to(pc, (N_EXPERTS, LANES))
        eid = lax.broadcasted_iota(jnp.int32, (N_EXPERTS, LANES), 0)
        pends = jnp.zeros((N_EXPERTS, LANES), jnp.int32)
        for e2 in range(N_EXPERTS):
            pends = pends + jnp.where(eid >= e2, pcb[e2:e2 + 1, :], 0)
        pst = pends - pcb
        base_sc[...] = pst[:, 0:1]
        meta_ref[0] = jnp.broadcast_to(cnt, (N_EXPERTS, LANES))
        meta_ref[1] = pst
        meta_ref[2] = pends
        nbl = blke_ref.shape[1]
        blk_start = lax.broadcasted_iota(jnp.int32, (N_EXPERTS, nbl), 1) * MOE_BLOCK
        pend_b = jnp.broadcast_to(pends[:, 0:1], (N_EXPERTS, nbl))
        be = jnp.sum(jnp.where(pend_b <= blk_start, 1, 0), axis=0, keepdims=True)
        blke_ref[...] = jnp.broadcast_to(jnp.minimum(be, N_EXPERTS - 1), blke_ref.shape)

    @pl.when(ps == 1)
    def _():
        erank = rank_sc[:, tile]
        scores = score_sc[:, tile]
        esel = erank < TOP_K
        mask_f = jnp.where(esel, 1.0, 0.0)
        tile_cnt = jnp.sum(mask_f, axis=1, keepdims=True).astype(jnp.int32)
        rr = lax.broadcasted_iota(jnp.int32, (tt, tt), 0)
        cc = lax.broadcasted_iota(jnp.int32, (tt, tt), 1)
        upper = jnp.where(rr < cc, 1.0, 0.0).astype(BF16)
        prefix = jnp.dot(mask_f.astype(BF16), upper, preferred_element_type=F32)
        posd = base_sc[...] + prefix.astype(jnp.int32)
        base_sc[...] = base_sc[...] + tile_cnt
        wsel = jnp.where(esel, scores, 0.0)
        denom = jnp.sum(wsel, axis=0, keepdims=True)
        wn = wsel / (denom + 1e-20) * ROUTED_SCALE
        prow, wrow = [], []
        for kk in range(TOP_K):
            hit = erank == kk
            prow.append(jnp.sum(jnp.where(hit, posd, 0), axis=0, keepdims=True))
            wrow.append(jnp.sum(jnp.where(hit, wn, 0.0), axis=0, keepdims=True))
        pad = pos_ref.shape[0] - TOP_K
        pos_ref[...] = jnp.concatenate(prow + [jnp.zeros((pad, tt), jnp.int32)], axis=0)
        w_ref[...] = jnp.concatenate(wrow + [jnp.zeros((pad, tt), F32)], axis=0)


def _route(logits_t, bias_col, nb, tt=512):
    n = logits_t.shape[1]
    nbl = -(-nb // LANES) * LANES
    return pl.pallas_call(
        functools.partial(_route_kernel, tt=tt),
        grid=(2, n // tt),
        in_specs=[pl.BlockSpec((N_EXPERTS, tt), lambda ps, i: (0, i * (1 - ps))),
                  pl.BlockSpec((N_EXPERTS, 1), lambda ps, i: (0, 0))],
        out_specs=[pl.BlockSpec((8, tt), lambda ps, i: (0, ps * i)),
                   pl.BlockSpec((8, tt), lambda ps, i: (0, ps * i)),
                   pl.BlockSpec((3, N_EXPERTS, LANES), lambda ps, i: (0, 0, 0)),
                   pl.BlockSpec((8, nbl), lambda ps, i: (0, 0))],
        out_shape=[jax.ShapeDtypeStruct((8, n), jnp.int32),
                   jax.ShapeDtypeStruct((8, n), F32),
                   jax.ShapeDtypeStruct((3, N_EXPERTS, LANES), jnp.int32),
                   jax.ShapeDtypeStruct((8, nbl), jnp.int32)],
        scratch_shapes=[pltpu.VMEM((N_EXPERTS, 1), jnp.int32), pltpu.VMEM((N_EXPERTS, 1), jnp.int32),
                        pltpu.VMEM((N_EXPERTS, n), jnp.int32), pltpu.VMEM((N_EXPERTS, n), F32)],
        compiler_params=_cparams("arbitrary", "arbitrary"),
        name="moe_route",
    )(logits_t, bias_col)


def _dispatch_kernel(pos_ref, cnt_ref, pst_ref, h_hbm, hb_ref, wsg_ref, wsu_ref, wsd_ref, xs_hbm, ysh_ref,
                     hbuf, zrow, in_sem, sc_sem, sem, *, tm, n):
    i = pl.program_id(0)
    nt = pl.num_programs(0)
    tile_rows = tm * ROW_TILE

    def fetch(tile, slot):
        return pltpu.make_async_copy(h_hbm.at[pl.ds(pl.multiple_of(tile * tile_rows, tile_rows), tile_rows), :],
                                     hbuf.at[slot], in_sem.at[slot])

    def wait_rows(slot):
        for kk in range(TOP_K):
            pltpu.make_async_copy(hbuf.at[slot], xs_hbm.at[pl.ds(0, tile_rows), :], sc_sem.at[slot]).wait()

    @pl.when(i == 0)
    def _():
        fetch(0, 0).start()
        fetch(1, 1).start()

    slot = i % 3
    fetch(i, slot).wait()
    src = hbuf.at[slot]

    def body(g, carry):
        for u in range(ROW_DMA_UNROLL):
            r = g * ROW_DMA_UNROLL + u
            for kk in range(TOP_K):
                p = pos_ref[kk * n + i * tm + r]
                pltpu.make_async_copy(src.at[_row_tile(r)], xs_hbm.at[_row_tile(p)],
                                      sc_sem.at[slot]).start(priority=kk % 2)
        return carry
    lax.fori_loop(0, tm // ROW_DMA_UNROLL, body, 0)

    hb = hb_ref[...]
    g = jnp.dot(hb, wsg_ref[...], preferred_element_type=F32)
    u = jnp.dot(hb, wsu_ref[...], preferred_element_type=F32)
    ysh_ref[...] = jnp.dot((g * _sigmoid(g) * u).astype(BF16), wsd_ref[...],
                           preferred_element_type=F32).astype(BF16)

    @pl.when(i >= 1)
    def _():
        wait_rows((i + 2) % 3)

    @pl.when(i + 2 < nt)
    def _():
        fetch(i + 2, (i + 2) % 3).start()

    @pl.when(i == nt - 1)
    def _():
        wait_rows(slot)

    @pl.when(i == nt - 1)
    def _():
        zrow[...] = jnp.zeros_like(zrow)

        def per_expert(e, carry):
            cnt = cnt_ref[e]
            first = pst_ref[e] + cnt
            npad = ((cnt + (MOE_BLOCK - 1)) // MOE_BLOCK) * MOE_BLOCK - cnt

            def start(s, c2):
                pltpu.make_async_copy(zrow.at[_row_tile(0)], xs_hbm.at[_row_tile(first + s)], sem.at[1]).start()
                return c2

            def wait(s, c2):
                pltpu.make_async_copy(zrow.at[_row_tile(0)], xs_hbm.at[_row_tile(0)], sem.at[1]).wait()
                return c2
            lax.fori_loop(0, npad, start, 0)
            lax.fori_loop(0, npad, wait, 0)
            return carry
        lax.fori_loop(0, N_EXPERTS, per_expert, 0)

        last = N_EXPERTS - 1
        used = (pst_ref[last] + cnt_ref[last] + (MOE_BLOCK - 1)) // MOE_BLOCK
        blk_rows = MOE_BLOCK * ROW_TILE

        def tail(b, carry):
            cp = pltpu.make_async_copy(zrow, xs_hbm.at[pl.ds(pl.multiple_of(b * blk_rows, blk_rows), blk_rows), :],
                                       sem.at[1])
            cp.start()
            cp.wait()
            return carry
        lax.fori_loop(used, xs_hbm.shape[0] // blk_rows, tail, 0)


def _dispatch(pos_flat, cnt, pst, h2pk, h2, wsg, wsu, wsd, nb, tm=512):
    n, d = h2.shape
    assert n // tm >= 2
    const = lambda a: pl.BlockSpec(a.shape, lambda i, *_: (0,) * a.ndim)
    grid_spec = pltpu.PrefetchScalarGridSpec(
        num_scalar_prefetch=3,
        grid=(n // tm,),
        in_specs=[pl.BlockSpec(memory_space=pl.ANY),
                  pl.BlockSpec((tm, d), lambda i, *_: (i, 0)),
                  const(wsg), const(wsu), const(wsd)],
        out_specs=[pl.BlockSpec(memory_space=pl.ANY), pl.BlockSpec((tm, d), lambda i, *_: (i, 0))],
        scratch_shapes=[pltpu.VMEM((3, tm * ROW_TILE, LANES), jnp.uint32),
                        pltpu.VMEM((MOE_BLOCK * ROW_TILE, LANES), jnp.uint32),
                        pltpu.SemaphoreType.DMA((3,)), pltpu.SemaphoreType.DMA((3,)),
                        pltpu.SemaphoreType.DMA((2,))],
    )
    return pl.pallas_call(
        functools.partial(_dispatch_kernel, tm=tm, n=n),
        grid_spec=grid_spec,
        out_shape=[jax.ShapeDtypeStruct((nb * MOE_BLOCK * ROW_TILE, LANES), jnp.uint32),
                   jax.ShapeDtypeStruct((n, d), BF16)],
        compiler_params=_cparams("arbitrary"),
        name="moe_dispatch",
    )(pos_flat, cnt, pst, h2pk, h2, wsg, wsu, wsd)


def _gmm_kernel(nact_ref, blke_ref, xs_ref, wg_hbm, wu_hbm, wd_hbm, o_ref,
                wgf, wuf, wdf, wgb, wub, wdb, sem):
    nact = nact_ref[0]

    def fetch(e):
        return (pltpu.make_async_copy(wg_hbm.at[e], wgf, sem.at[0]),
                pltpu.make_async_copy(wu_hbm.at[e], wuf, sem.at[1]),
                pltpu.make_async_copy(wd_hbm.at[e], wdf, sem.at[2]))

    @pl.when(pl.program_id(0) == 0)
    def _():
        for cp in fetch(blke_ref[0]):
            cp.start()

    def switch_weights(i):
        e = blke_ref[i]
        changed = jnp.logical_or(i == 0, e != blke_ref[jnp.maximum(i - 1, 0)])

        @pl.when(changed)
        def _():
            for cp in fetch(e):
                cp.wait()
            wgb[...] = wgf[...].astype(BF16)
            wub[...] = wuf[...].astype(BF16)
            wdb[...] = wdf[...].astype(BF16)
            nxt = lax.while_loop(lambda j: jnp.logical_and(j < nact, blke_ref[jnp.minimum(j, nact - 1)] == e),
                                 lambda j: j + 1, i + 1)

            @pl.when(nxt < nact)
            def _():
                for cp in fetch(blke_ref[jnp.minimum(nxt, nact - 1)]):
                    cp.start(priority=1)

    def swiglu(block, nblocks):
        base = block * MOE_BLOCK * ROW_TILE
        lo, hi = _unpack_halves(_load_row_tiles(xs_ref, nblocks * MOE_BLOCK, base))
        lo, hi = lo.astype(BF16), hi.astype(BF16)
        half = lo.shape[1]
        g = (jnp.dot(lo, wgb[:half, :], preferred_element_type=F32)
             + jnp.dot(hi, wgb[half:, :], preferred_element_type=F32))
        u = (jnp.dot(lo, wub[:half, :], preferred_element_type=F32)
             + jnp.dot(hi, wub[half:, :], preferred_element_type=F32))
        a = (g * _sigmoid(g) * u).astype(BF16)
        _store_row_tiles(o_ref, _pack_halves(jnp.dot(a, wdb[...], preferred_element_type=F32)), base)

    for pair in range(GMM_SUB // 2):
        sb0 = 2 * pair
        i0 = pl.program_id(0) * GMM_SUB + sb0
        i1 = i0 + 1
        paired = jnp.logical_and(i1 < nact, blke_ref[jnp.minimum(i0, nact - 1)] == blke_ref[jnp.minimum(i1, nact - 1)])

        @pl.when(i0 < nact)
        def _(i0=i0):
            switch_weights(i0)

        @pl.when(paired)
        def _(sb0=sb0):
            swiglu(sb0, 2)

        @pl.when(jnp.logical_and(jnp.logical_not(paired), i0 < nact))
        def _(sb0=sb0):
            swiglu(sb0, 1)

        @pl.when(jnp.logical_and(jnp.logical_not(paired), i1 < nact))
        def _(i1=i1, sb0=sb0):
            switch_weights(i1)
            swiglu(sb0 + 1, 1)

        for sb, i in ((sb0, i0), (sb0 + 1, i1)):
            @pl.when(i >= nact)
            def _(sb=sb):
                o_ref[pl.ds(sb * MOE_BLOCK * ROW_TILE, MOE_BLOCK * ROW_TILE), :] = jnp.zeros(
                    (MOE_BLOCK * ROW_TILE, LANES), o_ref.dtype)


def _gmm(nact, blk_e, xs, w_gate, w_up, w_down, nb):
    d, f = w_gate.shape[1:]
    assert GMM_SUB % 2 == 0 and nb % GMM_SUB == 0
    rows = GMM_SUB * MOE_BLOCK * ROW_TILE
    blk = lambda i, na: jnp.minimum(i, (na[0] - 1) // GMM_SUB)
    hbm = pl.BlockSpec(memory_space=pl.ANY)
    grid_spec = pltpu.PrefetchScalarGridSpec(
        num_scalar_prefetch=2,
        grid=(nb // GMM_SUB,),
        in_specs=[pl.BlockSpec((rows, LANES), lambda i, na, be: (blk(i, na), 0)), hbm, hbm, hbm],
        out_specs=pl.BlockSpec((rows, LANES), lambda i, na, be: (i, 0)),
        scratch_shapes=[pltpu.VMEM((d, f), F32), pltpu.VMEM((d, f), F32), pltpu.VMEM((f, d), F32),
                        pltpu.VMEM((d, f), BF16), pltpu.VMEM((d, f), BF16), pltpu.VMEM((f, d), BF16),
                        pltpu.SemaphoreType.DMA((3,))],
    )
    return pl.pallas_call(
        _gmm_kernel,
        grid_spec=grid_spec,
        out_shape=jax.ShapeDtypeStruct((nb * MOE_BLOCK * ROW_TILE, LANES), jnp.uint32),
        compiler_params=_cparams("arbitrary"),
        name="moe_experts",
    )(nact, blk_e, xs, w_gate, w_up, w_down)


def _final_kernel(pos_ref, tw_ref, ysh_ref, x1_ref, gate_ref, gpost_ref, yb_hbm, o_ref, rbuf, sem, *, tm, n):
    i = pl.program_id(0)
    slot = i % 2

    def gather(tile, sl):
        def body(g, carry):
            for u in range(ROW_DMA_UNROLL):
                r = g * ROW_DMA_UNROLL + u
                for kk in range(TOP_K):
                    p = pos_ref[kk * n + tile * tm + r]
                    pltpu.make_async_copy(yb_hbm.at[_row_tile(p)], rbuf.at[sl, kk].at[_row_tile(r)],
                                          sem.at[sl]).start(priority=kk % 2)
            return carry
        lax.fori_loop(0, tm // ROW_DMA_UNROLL, body, 0)

    @pl.when(i == 0)
    def _():
        gather(0, 0)

    @pl.when(i + 1 < pl.num_programs(0))
    def _():
        gather(i + 1, 1 - slot)

    for kk in range(TOP_K):
        pltpu.make_async_copy(yb_hbm.at[pl.ds(0, tm * ROW_TILE), :], rbuf.at[slot, kk], sem.at[slot]).wait()
    tw = tw_ref[...]
    half = ROW_TILE * LANES
    r_lo = jnp.zeros((tm, half), F32)
    r_hi = jnp.zeros((tm, half), F32)
    for kk in range(TOP_K):
        lo, hi = _unpack_halves(_load_row_tiles(rbuf.at[slot, kk], tm))
        wk = tw[:, kk:kk + 1]
        r_lo = r_lo + wk * lo
        r_hi = r_hi + wk * hi
    y = ysh_ref[...].astype(F32) + jnp.concatenate([r_lo, r_hi], axis=1)
    o_ref[...] = x1_ref[...] + gate_ref[0] * _rms(y, gpost_ref[...])


def _final(pos_flat, top_w8, ysh, x1, mod3, gpost, yb, seq, tm=256):
    n, d = x1.shape
    per_b = seq // tm
    const = lambda a: pl.BlockSpec(a.shape, lambda i, ps: (0,) * a.ndim)
    row = lambda w: pl.BlockSpec((tm, w), lambda i, ps: (i, 0))
    grid_spec = pltpu.PrefetchScalarGridSpec(
        num_scalar_prefetch=1,
        grid=(n // tm,),
        in_specs=[row(top_w8.shape[1]), row(d), row(d),
                  pl.BlockSpec((1, 1, d), lambda i, ps: ((i // per_b) * ADALN_CHUNKS + 5, 0, 0)),
                  const(gpost),
                  pl.BlockSpec(memory_space=pl.ANY)],
        out_specs=row(d),
        scratch_shapes=[pltpu.VMEM((2, TOP_K, tm * ROW_TILE, LANES), jnp.uint32), pltpu.SemaphoreType.DMA((2,))],
    )
    return pl.pallas_call(
        functools.partial(_final_kernel, tm=tm, n=n),
        grid_spec=grid_spec,
        out_shape=jax.ShapeDtypeStruct((n, d), F32),
        compiler_params=_cparams("arbitrary"),
        name="moe_combine_final",
    )(pos_flat, top_w8, ysh, x1, mod3, gpost, yb)


def _rope_freq_row(dim):
    inv_freq = 1.0 / (ROPE_THETA ** (jnp.arange(0, dim, 2, dtype=F32) / dim))
    return jnp.tile(jnp.concatenate([inv_freq, inv_freq]), LANES // dim).reshape(1, LANES)


def kernel(x, c, positions, w_ada, b_ada, attn_pre_g, w_in, q_a_norm_g, w_q_up, kv_a_norm_g, w_kv_up, w_mla_o, w_dil_o, w_out, attn_post_g, ffn_pre_g, w_router, router_bias, w_exp_gate, w_exp_up, w_exp_down, w_sh_gate, w_sh_up, w_sh_down, ffn_post_g):
    batch, seq, d = x.shape
    n = batch * seq
    depth = w_ada.shape[0]

    pos_col = positions.astype(F32).reshape(n, 1)
    freq_mla, freq_dil = _rope_freq_row(QK_ROPE_DIM), _rope_freq_row(DIL_ROT_DIM)

    x2 = x.reshape(n, d)
    c8 = jnp.pad(c, ((0, 8 - batch), (0, 0)))
    for l in range(depth):
        mod = _ada(c8, w_ada[l], b_ada[l].reshape(1, -1))
        mod3 = mod[:batch].reshape(batch * ADALN_CHUNKS, 1, d)

        wi = w_in[l]
        o_dil = Q_LORA_RANK + KV_LORA_RANK + QK_ROPE_DIM
        o_ga = o_dil + 3 * DIL_HEADS * DIL_HEAD_DIM
        n_gate, n_dil = wi.shape[1] - o_ga, o_ga - o_dil
        n_a = o_dil + LANES - QK_ROPE_DIM
        c_dil, c_a = n_gate, -(-(n_gate + n_dil) // n_a) * n_a
        w_full = _wprep(wi.T, ((0, o_ga, n_gate), (c_dil, o_dil, n_dil), (c_a, 0, o_dil)), c_a + n_a)
        wq3 = w_q_up[l].reshape(Q_LORA_RANK, MLA_HEADS, MLA_QK_DIM)
        wq = jnp.concatenate([wq3, jnp.zeros((Q_LORA_RANK, MLA_HEADS, MLA_QK_PAD - MLA_QK_DIM), F32)],
                             axis=2).reshape(Q_LORA_RANK, MLA_HEADS * MLA_QK_PAD).astype(BF16)
        wkv3 = w_kv_up[l].reshape(KV_LORA_RANK, MLA_HEADS, QK_NOPE_DIM + V_HEAD_DIM)
        wkv = jnp.concatenate([wkv3[:, :, :QK_NOPE_DIM].reshape(KV_LORA_RANK, -1),
                               wkv3[:, :, QK_NOPE_DIM:].reshape(KV_LORA_RANK, -1)], axis=1).astype(BF16)

        h, q, k, v = _front(x2, attn_pre_g[l].reshape(1, d), mod3, w_full, c_a, n_a,
                            q_a_norm_g[l].reshape(1, -1), kv_a_norm_g[l].reshape(1, -1), wq, wkv,
                            pos_col, freq_mla, batch, seq)
        gates = _mm(h, w_full, 0, n_gate, act="sigmoid", tn=2048)
        o_mla = _mla_attn(q, k, v).reshape(n, MLA_HEADS * V_HEAD_DIM)
        dil_o, dil_lse = [], []
        rope_dil = (pos_col, freq_dil)
        for g, (_, dilation) in enumerate(DIL_PATTERNS):
            qkv, rope_dil = _dilproj(h, w_full, c_dil, g, rope_dil, batch, seq, dilation)
            o_g, lse_g = _dil_attn(qkv, batch, seq, dilation)
            dil_o.append(o_g)
            dil_lse.append(lse_g)

        wr = jnp.pad(w_router[l], ((0, 0), (0, LANES - N_EXPERTS)))
        wr_hi = wr.astype(BF16)
        wr_lo = (wr - wr_hi.astype(F32)).astype(BF16)
        x1, h2, h2pk, logits_t = _merge(o_mla, dil_o, dil_lse, gates, x2, mod3,
                                        attn_post_g[l].reshape(1, d), ffn_pre_g[l].reshape(1, d),
                                        w_mla_o[l].astype(BF16), w_dil_o[l].astype(BF16), w_out[l].astype(BF16),
                                        jnp.concatenate([wr_hi, wr_lo], axis=1), wr_hi, seq)

        nb = -(-(n * TOP_K + N_EXPERTS * (MOE_BLOCK - 1)) // MOE_BLOCK)
        pos_t, w_t, meta, blk_e = _route(logits_t, router_bias[l].astype(F32).reshape(N_EXPERTS, 1), nb)
        pos_flat = pos_t.reshape(-1)
        nact = meta[2, N_EXPERTS - 1, :1] // MOE_BLOCK
        xs, ysh = _dispatch(pos_flat, meta[0, :, 0], meta[1, :, 0], h2pk, h2,
                            w_sh_gate[l].astype(BF16), w_sh_up[l].astype(BF16), w_sh_down[l].astype(BF16), nb)
        yb = _gmm(nact, blk_e[0], xs, w_exp_gate[l], w_exp_up[l], w_exp_down[l], nb)
        x2 = _final(pos_flat, w_t.T, ysh, x1, mod3, ffn_post_g[l].reshape(1, d), yb, seq)
    return x2.reshape(batch, seq, d)
```
